```python
import jax, jax.numpy as jnp
from jax import lax
import numpy as np

D_MODEL = 1024
BATCH = 4
SEQ = 8192
DEPTH = 1

N_MEM = 256
ATTN_HEADS = 8
ATTN_KV_HEADS = 2
ATTN_HEAD_DIM = 64
ATTN_WIDTH = ATTN_HEADS * ATTN_HEAD_DIM
ATTN_KV_WIDTH = ATTN_KV_HEADS * ATTN_HEAD_DIM
WINDOW = 128
BLOCK = 128
RET_HEADS = 4
RET_HEAD_DIM = 128
RET_WIDTH = RET_HEADS * RET_HEAD_DIM
RET_CHUNK = 128
MIX_WIDTH = ATTN_WIDTH + RET_WIDTH
IN_COLS = ATTN_WIDTH + 2 * ATTN_KV_WIDTH + 4 * RET_WIDTH
XA_HEADS = 4
XA_HEAD_DIM = D_MODEL // XA_HEADS
D_FF = -(-(8 * D_MODEL) // (3 * 256)) * 256
ROPE_THETA = 10000.0
EPS = 1e-6

kernel_name = "hybrid_swa_retention_memxattn_block"


def rms_norm(x, w):
    xf = x.astype(jnp.float32)
    y = xf * lax.rsqrt(jnp.mean(xf * xf, axis=-1, keepdims=True) + EPS)
    return (y * w.astype(jnp.float32)).astype(x.dtype)


def rope(x):
    S, d = x.shape[1], x.shape[-1]
    inv_freq = ROPE_THETA ** (-jnp.arange(0, d, 2, dtype=jnp.float32) / d)
    ang = jnp.arange(S, dtype=jnp.float32)[:, None] * inv_freq[None, :]
    cos = jnp.cos(ang)[None, :, None, :]
    sin = jnp.sin(ang)[None, :, None, :]
    xf = x.astype(jnp.float32)
    x1, x2 = xf[..., : d // 2], xf[..., d // 2:]
    return jnp.concatenate([x1 * cos - x2 * sin, x1 * sin + x2 * cos], axis=-1).astype(x.dtype)


def window_attention(q, k, v, sink):
    B, S = q.shape[0], q.shape[1]
    nb = S // BLOCK
    G = ATTN_HEADS // ATTN_KV_HEADS
    qb = q.reshape(B, nb, BLOCK, ATTN_KV_HEADS, G, ATTN_HEAD_DIM)
    pad = ((0, 0), (BLOCK, BLOCK), (0, 0), (0, 0))
    kp = jnp.pad(k, pad).reshape(B, nb + 2, BLOCK, ATTN_KV_HEADS, ATTN_HEAD_DIM)
    vp = jnp.pad(v, pad).reshape(B, nb + 2, BLOCK, ATTN_KV_HEADS, ATTN_HEAD_DIM)
    kw = jnp.concatenate([kp[:, :-2], kp[:, 1:-1], kp[:, 2:]], axis=2)
    vw = jnp.concatenate([vp[:, :-2], vp[:, 1:-1], vp[:, 2:]], axis=2)
    s = jnp.einsum('bnqhgd,bnkhd->bhgnqk', qb, kw).astype(jnp.float32) * (ATTN_HEAD_DIM ** -0.5)
    blk = jnp.arange(nb)[:, None, None]
    qpos = blk * BLOCK + jnp.arange(BLOCK)[None, :, None]
    kpos = (blk - 1) * BLOCK + jnp.arange(3 * BLOCK)[None, None, :]
    valid = (jnp.abs(qpos - kpos) <= WINDOW) & (kpos >= 0) & (kpos < S)
    s = jnp.where(valid, s, -jnp.inf)
    sk = sink.astype(jnp.float32).reshape(ATTN_KV_HEADS, G)[None, :, :, None, None, None]
    m = jnp.maximum(jnp.max(s, axis=-1, keepdims=True), sk)
    p = jnp.exp(s - m)
    p = p / (jnp.sum(p, axis=-1, keepdims=True) + jnp.exp(sk - m))
    o = jnp.einsum('bhgnqk,bnkhd->bnqhgd', p.astype(v.dtype), vw)
    return o.reshape(B, S, ATTN_WIDTH)


def retention_scan(q, k, v, log_gamma, strict):
    C = q.shape[3]
    idx = jnp.arange(C, dtype=jnp.float32)
    diff = idx[:, None] - idx[None, :]
    mask = (diff > 0) if strict else (diff >= 0)
    lg = log_gamma[:, None, None]
    dmask = jnp.where(mask[None], jnp.exp(jnp.where(mask[None], diff[None], 0.0) * lg), 0.0)
    s = jnp.einsum('bhncd,bhnmd->bhncm', q, k) * dmask[None, :, None].astype(q.dtype)
    intra = jnp.einsum('bhncm,bhnme->bhnce', s, v)
    k_dec = k * jnp.exp((C - 1 - idx)[None, :] * log_gamma[:, None])[None, :, None, :, None].astype(k.dtype)
    kv = jnp.einsum('bhncd,bhnce->nbhde', k_dec, v)
    chunk_decay = jnp.exp(C * log_gamma).astype(kv.dtype)[None, :, None, None]

    def step(state, kv_n):
        return state * chunk_decay + kv_n, state

    _, prev = lax.scan(step, jnp.zeros_like(kv[0]), kv)
    q_dec = q * jnp.exp((idx + 1)[None, :] * log_gamma[:, None])[None, :, None, :, None].astype(q.dtype)
    inter = jnp.einsum('bhncd,nbhde->bhnce', q_dec, prev)
    return intra + inter


def retention(q, k, v, g, decay_fwd, decay_bwd, gn_w):
    B, S = q.shape[0], q.shape[1]
    N = S // RET_CHUNK
    q = rope(q)
    k = rope(k) * (RET_HEAD_DIM ** -0.5)

    def chunk(t):
        return t.transpose(0, 2, 1, 3).reshape(B, RET_HEADS, N, RET_CHUNK, RET_HEAD_DIM)

    lg_f = jax.nn.log_sigmoid(decay_fwd.astype(jnp.float32))
    lg_b = jax.nn.log_sigmoid(decay_bwd.astype(jnp.float32))
    y_f = retention_scan(chunk(q), chunk(k), chunk(v), lg_f, False).reshape(B, RET_HEADS, S, RET_HEAD_DIM)
    y_b = retention_scan(chunk(q[:, ::-1]), chunk(k[:, ::-1]), chunk(v[:, ::-1]), lg_b, True)
    y_b = y_b.reshape(B, RET_HEADS, S, RET_HEAD_DIM)[:, :, ::-1]
    y = (y_f + y_b).transpose(0, 2, 1, 3).astype(jnp.float32)
    mu = jnp.mean(y, axis=-1, keepdims=True)
    var = jnp.mean(jnp.square(y - mu), axis=-1, keepdims=True)
    yn = ((y - mu) * lax.rsqrt(var + EPS)).reshape(B, S, RET_WIDTH) * gn_w.astype(jnp.float32)
    return (jax.nn.silu(g.astype(jnp.float32)) * yn).astype(g.dtype)


def setup_inputs(seed: int = 0) -> dict:
    key = jax.random.key(seed)
    ks = jax.random.split(key, 24)
    f32 = jnp.float32

    def w(k, shape, fan_in):
        return jax.random.normal(k, shape, f32) * (fan_in ** -0.5)

    def gain(k, n):
        return 1.0 + 0.05 * jax.random.normal(k, (DEPTH, n), f32)

    base = 1.0 - jnp.exp2(-5.0 - jnp.arange(RET_HEADS, dtype=f32))
    base_logit = jnp.log(base / (1.0 - base))
    return {
        "x": jax.random.normal(ks[0], (BATCH, SEQ, D_MODEL), f32),
        "mem": jax.random.normal(ks[1], (BATCH, N_MEM, D_MODEL), f32),
        "norm_mix_pre": gain(ks[2], D_MODEL),
        "norm_mix_post": gain(ks[3], D_MODEL),
        "w_in": w(ks[4], (DEPTH, D_MODEL, IN_COLS), D_MODEL),
        "attn_sink": 0.5 * jax.random.normal(ks[5], (DEPTH, ATTN_HEADS), f32),
        "attn_out_norm": gain(ks[6], ATTN_WIDTH),
        "ret_decay_fwd": base_logit[None] + 0.1 * jax.random.normal(ks[7], (DEPTH, RET_HEADS), f32),
        "ret_decay_bwd": base_logit[None] + 0.1 * jax.random.normal(ks[8], (DEPTH, RET_HEADS), f32),
        "ret_gn": gain(ks[9], RET_WIDTH),
        "w_out": w(ks[10], (DEPTH, MIX_WIDTH, D_MODEL), MIX_WIDTH),
        "norm_xa_pre": gain(ks[11], D_MODEL),
        "norm_xa_post": gain(ks[12], D_MODEL),
        "norm_mem": gain(ks[13], D_MODEL),
        "xa_wq": w(ks[14], (DEPTH, D_MODEL, D_MODEL), D_MODEL),
        "xa_wkv": w(ks[15], (DEPTH, D_MODEL, 2 * D_MODEL), D_MODEL),
        "xa_wo": w(ks[16], (DEPTH, D_MODEL, D_MODEL), D_MODEL),
        "norm_ffn_pre": gain(ks[17], D_MODEL),
        "norm_ffn_post": gain(ks[18], D_MODEL),
        "ffn_w_gu": w(ks[19], (DEPTH, D_MODEL, 2 * D_FF), D_MODEL),
        "ffn_w_down": w(ks[20], (DEPTH, D_FF, D_MODEL), D_FF),
    }


def reference(x, mem, norm_mix_pre, norm_mix_post, w_in, attn_sink, attn_out_norm,
              ret_decay_fwd, ret_decay_bwd, ret_gn, w_out, norm_xa_pre, norm_xa_post,
              norm_mem, xa_wq, xa_wkv, xa_wo, norm_ffn_pre, norm_ffn_post,
              ffn_w_gu, ffn_w_down):
    B, S, _ = x.shape
    M = mem.shape[1]
    o_k = ATTN_WIDTH
    o_v = o_k + ATTN_KV_WIDTH
    o_r = o_v + ATTN_KV_WIDTH
    for l in range(DEPTH):
        h = rms_norm(x, norm_mix_pre[l])
        z = h @ w_in[l]
        aq = z[..., :o_k].reshape(B, S, ATTN_HEADS, ATTN_HEAD_DIM)
        ak = z[..., o_k:o_v].reshape(B, S, ATTN_KV_HEADS, ATTN_HEAD_DIM)
        av = z[..., o_v:o_r].reshape(B, S, ATTN_KV_HEADS, ATTN_HEAD_DIM)
        rq, rk, rv, rg = jnp.split(z[..., o_r:], 4, axis=-1)
        a = window_attention(rope(aq), rope(ak), av, attn_sink[l])
        a = rms_norm(a, attn_out_norm[l])
        r = retention(rq.reshape(B, S, RET_HEADS, RET_HEAD_DIM),
                      rk.reshape(B, S, RET_HEADS, RET_HEAD_DIM),
                      rv.reshape(B, S, RET_HEADS, RET_HEAD_DIM),
                      rg, ret_decay_fwd[l], ret_decay_bwd[l], ret_gn[l])
        mix = jnp.concatenate([a, r], axis=-1) @ w_out[l]
        x = x + rms_norm(mix, norm_mix_post[l])

        h = rms_norm(x, norm_xa_pre[l])
        mn = rms_norm(mem, norm_mem[l])
        q = (h @ xa_wq[l]).reshape(B, S, XA_HEADS, XA_HEAD_DIM)
        kv = (mn @ xa_wkv[l]).reshape(B, M, 2, XA_HEADS, XA_HEAD_DIM)
        s = jnp.einsum('bshd,bmhd->bhsm', q, kv[:, :, 0]).astype(jnp.float32) * (XA_HEAD_DIM ** -0.5)
        p = jax.nn.softmax(s, axis=-1).astype(x.dtype)
        xo = jnp.einsum('bhsm,bmhd->bshd', p, kv[:, :, 1]).reshape(B, S, D_MODEL) @ xa_wo[l]
        x = x + rms_norm(xo, norm_xa_post[l])

        h = rms_norm(x, norm_ffn_pre[l])
        gate, up = jnp.split(h @ ffn_w_gu[l], 2, axis=-1)
        f = (jax.nn.silu(gate) * up) @ ffn_w_down[l]
        x = x + rms_norm(f, norm_ffn_post[l])
    return x
```

```python
import functools

import numpy as np
import jax
import jax.numpy as jnp
from jax import lax
from jax.experimental import pallas as pl
from jax.experimental.pallas import tpu as pltpu

D_MODEL = 1024
N_MEM = 256
ATTN_HEADS = 8
ATTN_KV_HEADS = 2
ATTN_GROUP = ATTN_HEADS // ATTN_KV_HEADS
ATTN_HEAD_DIM = 64
ATTN_WIDTH = ATTN_HEADS * ATTN_HEAD_DIM
ATTN_KV_WIDTH = ATTN_KV_HEADS * ATTN_HEAD_DIM
BLOCK = 128
RET_HEADS = 4
RET_HEAD_DIM = 128
RET_WIDTH = RET_HEADS * RET_HEAD_DIM
IN_COLS = ATTN_WIDTH + 2 * ATTN_KV_WIDTH + 4 * RET_WIDTH
XA_HEADS = 4
XA_HEAD_DIM = D_MODEL // XA_HEADS
D_FF = -(-(8 * D_MODEL) // (3 * 256)) * 256
ROPE_THETA = 10000.0
EPS = 1e-6

LANES = 128
VMEM_LIMIT_BYTES = 56 * 1024 * 1024

TOK_TILE = 512
CHUNKS_PER_TILE = TOK_TILE // BLOCK

Z_RQ, Z_RK, Z_RV, Z_RG = 0, 512, 1024, 1536
Z_AQ, Z_AK, Z_AV = 2048, 2560, 2688

F32 = jnp.float32
BF16 = jnp.bfloat16
NT_DIMS = (((1,), (1,)), ((), ()))
TN_DIMS = (((0,), (0,)), ((), ()))


def _in_col_perm():
    o_k = ATTN_WIDTH
    o_v = o_k + ATTN_KV_WIDTH
    o_r = o_v + ATTN_KV_WIDTH
    half = ATTN_HEAD_DIM // 2
    perm = list(range(o_r, IN_COLS))
    for j in range(ATTN_GROUP):
        for part in range(2):
            for h in (j, j + ATTN_GROUP):
                perm += [h * ATTN_HEAD_DIM + part * half + d for d in range(half)]
    for part in range(2):
        for h in range(ATTN_KV_HEADS):
            perm += [o_k + h * ATTN_HEAD_DIM + part * half + d for d in range(half)]
    perm += list(range(o_v, o_r))
    return np.asarray(perm, np.int32)


def _attn_out_perm():
    perm = []
    for j in range(ATTN_GROUP):
        for h in (j, j + ATTN_GROUP):
            perm += [h * ATTN_HEAD_DIM + d for d in range(ATTN_HEAD_DIM)]
    return np.asarray(perm, np.int32)


def _rope_tables(seq, dim, groups):
    inv_freq = ROPE_THETA ** (-jnp.arange(0, dim, 2, dtype=F32) / dim)
    ang = jnp.arange(seq, dtype=F32)[:, None] * inv_freq[None, :]
    cos, sin = jnp.cos(ang), jnp.sin(ang)
    cos = jnp.concatenate([cos] * (2 * groups), axis=1)
    sin = jnp.concatenate([-sin] * groups + [sin] * groups, axis=1)
    return cos, sin


def _const_spec(shape):
    nd = len(shape)
    return pl.BlockSpec(shape, lambda *_: (0,) * nd, pipeline_mode=pl.Buffered(1))


def _params(*sem):
    return pltpu.CompilerParams(dimension_semantics=sem,
                                vmem_limit_bytes=VMEM_LIMIT_BYTES)


def _rms(x, w):
    ms = jnp.mean(x * x, axis=-1, keepdims=True)
    return x * lax.rsqrt(ms + EPS) * w


def _dot(a, b):
    return jnp.dot(a, b, preferred_element_type=F32)


def _decay_kernel(logit_ref, dmat_ref, rowdec_ref, cdec_ref):
    x = logit_ref[...]
    lg = jnp.minimum(x, 0.0) - jnp.log1p(jnp.exp(-jnp.abs(x)))
    ri = lax.broadcasted_iota(jnp.int32, (BLOCK, LANES), 0).astype(F32)
    ci = lax.broadcasted_iota(jnp.int32, (BLOCK, LANES), 1).astype(F32)
    diff = ri - ci
    cdec_ref[...] = jnp.exp(BLOCK * lg)
    for h in range(RET_HEADS):
        lf = lg[h:h + 1, :]
        lb = lg[RET_HEADS + h:RET_HEADS + h + 1, :]
        dmat_ref[h] = jnp.exp(jnp.abs(diff) * jnp.where(diff >= 0, lf, lb))
        sl = slice(h * LANES, (h + 1) * LANES)
        rowdec_ref[0, :, sl] = jnp.exp((BLOCK - 1 - ri) * lf)
        rowdec_ref[1, :, sl] = jnp.exp(ri * lb)
        rowdec_ref[2, :, sl] = jnp.exp((ri + 1) * lf)
        rowdec_ref[3, :, sl] = jnp.exp((BLOCK - ri) * lb)


def _decay_tables(decay_fwd, decay_bwd):
    logits = jnp.concatenate([decay_fwd, decay_bwd]).astype(F32)
    logits = jnp.broadcast_to(logits[:, None], (2 * RET_HEADS, LANES))
    return pl.pallas_call(
        _decay_kernel,
        out_shape=(jax.ShapeDtypeStruct((RET_HEADS, BLOCK, LANES), F32),
                   jax.ShapeDtypeStruct((4, BLOCK, RET_WIDTH), F32),
                   jax.ShapeDtypeStruct((2 * RET_HEADS, LANES), F32)),
        name="decay_tables",
    )(logits)


def _inproj_kernel(x_ref, g_ref, w_ref, ca_ref, sa_ref, cr_ref, sr_ref, z_ref):
    h = _rms(x_ref[0], g_ref[...]).astype(BF16)

    def proj(col, width):
        return _dot(h, w_ref[:, col:col + width])

    def rope_store(z, col, n_groups, cos, sin, scale=None):
        for g in range(n_groups):
            zg = z[:, g * LANES:(g + 1) * LANES]
            r = zg * cos + pltpu.roll(zg, LANES // 2, 1) * sin
            if scale is not None:
                r = r * scale
            z_ref[0, :, col + g * LANES:col + (g + 1) * LANES] = r.astype(BF16)

    cr, sr = cr_ref[...], sr_ref[...]
    ca, sa = ca_ref[...], sa_ref[...]
    rope_store(proj(Z_RQ, RET_WIDTH), Z_RQ, RET_HEADS, cr, sr)
    rope_store(proj(Z_RK, RET_WIDTH), Z_RK, RET_HEADS, cr, sr, RET_HEAD_DIM ** -0.5)
    z_ref[0, :, Z_RV:Z_AQ] = proj(Z_RV, 2 * RET_WIDTH).astype(BF16)
    rope_store(proj(Z_AQ, ATTN_WIDTH), Z_AQ, ATTN_WIDTH // LANES, ca, sa,
               ATTN_HEAD_DIM ** -0.5)
    zkv = proj(Z_AK, 2 * ATTN_KV_WIDTH)
    rope_store(zkv, Z_AK, 1, ca, sa)
    z_ref[0, :, Z_AV:Z_AV + ATTN_KV_WIDTH] = zkv[:, ATTN_KV_WIDTH:].astype(BF16)


def _in_projection(x, gain, w_in, tabs):
    B, S, _ = x.shape
    tab_spec = pl.BlockSpec((TOK_TILE, LANES), lambda i, b: (i, 0))
    return pl.pallas_call(
        _inproj_kernel,
        grid=(S // TOK_TILE, B),
        in_specs=[pl.BlockSpec((1, TOK_TILE, D_MODEL), lambda i, b: (b, i, 0)),
                  _const_spec((1, D_MODEL)),
                  _const_spec((D_MODEL, IN_COLS)),
                  tab_spec, tab_spec, tab_spec, tab_spec],
        out_specs=pl.BlockSpec((1, TOK_TILE, IN_COLS), lambda i, b: (b, i, 0)),
        out_shape=jax.ShapeDtypeStruct((B, S, IN_COLS), BF16),
        compiler_params=_params("parallel", "parallel"),
        name="in_projection",
    )(x, gain, w_in, *tabs)


def _state_kernel(kf_ref, vf_ref, kb_ref, vb_ref, rowdec_ref, cdec_ref,
                  pf_ref, pb_ref, st_ref):
    @pl.when(pl.program_id(1) == 0)
    def _():
        st_ref[...] = jnp.zeros_like(st_ref)

    for d, (k_ref, v_ref, p_ref) in enumerate(((kf_ref, vf_ref, pf_ref),
                                               (kb_ref, vb_ref, pb_ref))):
        kd = (k_ref[0].astype(F32) * rowdec_ref[d]).astype(BF16)
        v = v_ref[0]
        for h in range(RET_HEADS):
            sl = slice(h * LANES, (h + 1) * LANES)
            st = st_ref[d, h]
            p_ref[0, 0, h] = st.astype(BF16)
            kv = lax.dot_general(kd[:, sl], v[:, sl], TN_DIMS,
                                 preferred_element_type=F32)
            row = d * RET_HEADS + h
            st_ref[d, h] = st * cdec_ref[row:row + 1, :] + kv


def _retention_states(z, rowdec, cdec):
    B, S, _ = z.shape
    n = S // BLOCK
    kcol, vcol = Z_RK // RET_WIDTH, Z_RV // RET_WIDTH
    blk = (1, BLOCK, RET_WIDTH)
    st_blk = (1, 1, RET_HEADS, RET_HEAD_DIM, RET_HEAD_DIM)
    st_shape = jax.ShapeDtypeStruct((B, n, RET_HEADS, RET_HEAD_DIM, RET_HEAD_DIM), BF16)
    return pl.pallas_call(
        _state_kernel,
        grid=(B, n),
        in_specs=[pl.BlockSpec(blk, lambda b, t: (b, t, kcol)),
                  pl.BlockSpec(blk, lambda b, t: (b, t, vcol)),
                  pl.BlockSpec(blk, lambda b, t: (b, n - 1 - t, kcol)),
                  pl.BlockSpec(blk, lambda b, t: (b, n - 1 - t, vcol)),
                  _const_spec((4, BLOCK, RET_WIDTH)),
                  _const_spec((2 * RET_HEADS, LANES))],
        out_specs=(pl.BlockSpec(st_blk, lambda b, t: (b, t, 0, 0, 0)),
                   pl.BlockSpec(st_blk, lambda b, t: (b, n - 1 - t, 0, 0, 0))),
        out_shape=(st_shape, st_shape),
        scratch_shapes=[pltpu.VMEM((2, RET_HEADS, RET_HEAD_DIM, RET_HEAD_DIM), F32)],
        compiler_params=_params("parallel", "arbitrary"),
        name="retention_states",
    )(z, z, z, z, rowdec, cdec)


def _mixer_kernel(sink_ref, x_ref, rq_ref, rk_ref, rv_ref, rg_ref, aq_ref,
                  ak_ref, av_ref, akp_ref, akn_ref, avp_ref, avn_ref,
                  pf_ref, pb_ref, dmat_ref, rowdec_ref, an_ref, gn_ref,
                  wout_ref, npost_ref, o_ref, kext_ref, vext_ref, mix_ref):
    tile = pl.program_id(1)
    n_blocks = pl.num_programs(1) * CHUNKS_PER_TILE

    lane = lax.broadcasted_iota(jnp.int32, (1, LANES), 1)
    head0 = (lane % ATTN_HEAD_DIM) < (ATTN_HEAD_DIM // 2)
    kmask = (jnp.where(head0, 1.0, 0.0).astype(BF16), jnp.where(head0, 0.0, 1.0).astype(BF16))
    for g in range(ATTN_KV_HEADS):
        kext_ref[g, 0:BLOCK] = akp_ref[0] * kmask[g]
        kext_ref[g, BLOCK:BLOCK + TOK_TILE] = ak_ref[0] * kmask[g]
        kext_ref[g, BLOCK + TOK_TILE:] = akn_ref[0] * kmask[g]
    vext_ref[0:BLOCK] = avp_ref[0]
    vext_ref[BLOCK:BLOCK + TOK_TILE] = av_ref[0]
    vext_ref[BLOCK + TOK_TILE:] = avn_ref[0]

    qq = lax.broadcasted_iota(jnp.int32, (BLOCK, 3 * BLOCK), 0)
    kk = lax.broadcasted_iota(jnp.int32, (BLOCK, 3 * BLOCK), 1)
    first_half = lax.broadcasted_iota(jnp.int32, (1, LANES), 1) < ATTN_HEAD_DIM

    def chunk(c, carry):
        r0 = pl.multiple_of(c * BLOCK, BLOCK)
        rows = pl.ds(r0, BLOCK)
        blk = tile * CHUNKS_PER_TILE + c

        q = aq_ref[0, rows, :]
        qs = jnp.concatenate([q[:, j * LANES:(j + 1) * LANES] for j in range(ATTN_GROUP)], axis=0)
        lo = jnp.maximum(qq, jnp.where(blk == 0, BLOCK, 0))
        hi = jnp.minimum(qq + 2 * BLOCK, jnp.where(blk == n_blocks - 1, 2 * BLOCK - 1, 3 * BLOCK - 1))
        bias = jnp.where((kk >= lo) & (kk <= hi), 0.0, -jnp.inf).astype(F32)
        bias = jnp.concatenate([bias] * ATTN_GROUP, axis=0)
        vw = vext_ref[pl.ds(r0, 3 * BLOCK), :]
        outs = []
        for g in range(ATTN_KV_HEADS):
            kw = kext_ref[g, pl.ds(r0, 3 * BLOCK), :]
            s = lax.dot_general(qs, kw, NT_DIMS, preferred_element_type=F32) + bias
            sk = jnp.concatenate(
                [jnp.full((BLOCK, 1), sink_ref[g * ATTN_GROUP + j], F32) for j in range(ATTN_GROUP)],
                axis=0)
            m = jnp.maximum(jnp.max(s, axis=-1, keepdims=True), sk)
            p = jnp.exp(s - m)
            den = jnp.sum(p, axis=-1, keepdims=True) + jnp.exp(sk - m)
            outs.append(_dot(p.astype(BF16), vw) * (1.0 / den))
        o = jnp.where(first_half, outs[0], outs[1])
        a = jnp.concatenate([o[j * BLOCK:(j + 1) * BLOCK] for j in range(ATTN_GROUP)], axis=1)
        mix_ref[rows, 0:ATTN_WIDTH] = _rms(a, an_ref[...]).astype(BF16)

        for h in range(RET_HEADS):
            sl = slice(h * LANES, (h + 1) * LANES)
            qh = rq_ref[0, rows, sl]
            s = lax.dot_general(qh, rk_ref[0, rows, sl], NT_DIMS,
                                preferred_element_type=F32) * dmat_ref[h]
            y = _dot(s.astype(BF16), rv_ref[0, rows, sl])
            qf = qh.astype(F32)
            qd = jnp.concatenate([(qf * rowdec_ref[2, :, sl]).astype(BF16),
                                  (qf * rowdec_ref[3, :, sl]).astype(BF16)], axis=1)
            st = jnp.concatenate([pf_ref[0, c, h], pb_ref[0, c, h]], axis=0)
            y = y + _dot(qd, st)
            mu = jnp.mean(y, axis=-1, keepdims=True)
            yc = y - mu
            var = jnp.mean(yc * yc, axis=-1, keepdims=True)
            yn = yc * lax.rsqrt(var + EPS) * gn_ref[:, sl]
            gate = rg_ref[0, rows, sl].astype(F32)
            r = gate * jax.nn.sigmoid(gate) * yn
            mix_ref[rows, ATTN_WIDTH + h * LANES:ATTN_WIDTH + (h + 1) * LANES] = r.astype(BF16)
        return carry

    lax.fori_loop(0, CHUNKS_PER_TILE, chunk, 0)

    mix = _dot(mix_ref[...], wout_ref[...])
    o_ref[0] = x_ref[0] + _rms(mix, npost_ref[...])


def _mixer(x, z, states_f, states_b, dmat, rowdec, sink, attn_norm, ret_gn, w_out, norm_post):
    B, S, _ = x.shape
    n = S // BLOCK
    cpt = CHUNKS_PER_TILE
    wide = (1, TOK_TILE, RET_WIDTH)
    narrow = (1, TOK_TILE, ATTN_KV_WIDTH)
    edge = (1, BLOCK, ATTN_KV_WIDTH)
    kc, vc = Z_AK // ATTN_KV_WIDTH, Z_AV // ATTN_KV_WIDTH
    st_blk = (1, cpt, RET_HEADS, RET_HEAD_DIM, RET_HEAD_DIM)

    def col(c):
        return lambda b, i: (b, i, c)

    def prev(c):
        return lambda b, i: (b, jnp.maximum(i * cpt - 1, 0), c)

    def nxt(c):
        return lambda b, i: (b, jnp.minimum((i + 1) * cpt, n - 1), c)

    return pl.pallas_call(
        _mixer_kernel,
        grid=(B, S // TOK_TILE),
        in_specs=[pl.BlockSpec(memory_space=pltpu.SMEM),
                  pl.BlockSpec((1, TOK_TILE, D_MODEL), lambda b, i: (b, i, 0)),
                  pl.BlockSpec(wide, col(Z_RQ // RET_WIDTH)),
                  pl.BlockSpec(wide, col(Z_RK // RET_WIDTH)),
                  pl.BlockSpec(wide, col(Z_RV // RET_WIDTH)),
                  pl.BlockSpec(wide, col(Z_RG // RET_WIDTH)),
                  pl.BlockSpec(wide, col(Z_AQ // ATTN_WIDTH)),
                  pl.BlockSpec(narrow, col(kc)),
                  pl.BlockSpec(narrow, col(vc)),
                  pl.BlockSpec(edge, prev(kc)),
                  pl.BlockSpec(edge, nxt(kc)),
                  pl.BlockSpec(edge, prev(vc)),
                  pl.BlockSpec(edge, nxt(vc)),
                  pl.BlockSpec(st_blk, lambda b, i: (b, i, 0, 0, 0)),
                  pl.BlockSpec(st_blk, lambda b, i: (b, i, 0, 0, 0)),
                  _const_spec((RET_HEADS, BLOCK, LANES)),
                  _const_spec((4, BLOCK, RET_WIDTH)),
                  _const_spec((1, ATTN_WIDTH)),
                  _const_spec((1, RET_WIDTH)),
                  _const_spec((D_MODEL, D_MODEL)),
                  _const_spec((1, D_MODEL))],
        out_specs=pl.BlockSpec((1, TOK_TILE, D_MODEL), lambda b, i: (b, i, 0)),
        out_shape=jax.ShapeDtypeStruct((B, S, D_MODEL), F32),
        scratch_shapes=[pltpu.VMEM((ATTN_KV_HEADS, TOK_TILE + 2 * BLOCK, ATTN_KV_WIDTH), BF16),
                        pltpu.VMEM((TOK_TILE + 2 * BLOCK, ATTN_KV_WIDTH), BF16),
                        pltpu.VMEM((TOK_TILE, D_MODEL), BF16)],
        compiler_params=_params("parallel", "parallel"),
        name="token_mixer",
    )(sink, x, z, z, z, z, z, z, z, z, z, z, z, states_f, states_b, dmat, rowdec,
      attn_norm, ret_gn, w_out, norm_post)


def _memkv_kernel(m_ref, g_ref, w_ref, kv_ref):
    mn = _rms(m_ref[0], g_ref[...]).astype(BF16)
    kv_ref[0] = _dot(mn, w_ref[...]).astype(BF16)


def _memory_kv(mem, gain, w_kv):
    B, M, _ = mem.shape
    return pl.pallas_call(
        _memkv_kernel,
        grid=(B,),
        in_specs=[pl.BlockSpec((1, M, D_MODEL), lambda b: (b, 0, 0)),
                  _const_spec((1, D_MODEL)),
                  _const_spec((D_MODEL, 2 * D_MODEL))],
        out_specs=pl.BlockSpec((1, M, 2 * D_MODEL), lambda b: (b, 0, 0)),
        out_shape=jax.ShapeDtypeStruct((B, M, 2 * D_MODEL), BF16),
        compiler_params=_params("parallel"),
        name="memory_kv",
    )(mem, gain, w_kv)


def _xattn_kernel(x_ref, gpre_ref, wq_ref, km_ref, vm_ref, wo_ref, gpost_ref,
                  o_ref, cat_ref):
    x = x_ref[0]
    h = _rms(x, gpre_ref[...]).astype(BF16)
    q = (_dot(h, wq_ref[...]) * (XA_HEAD_DIM ** -0.5)).astype(BF16)
    for hd in range(XA_HEADS):
        sl = slice(hd * XA_HEAD_DIM, (hd + 1) * XA_HEAD_DIM)
        s = lax.dot_general(q[:, sl], km_ref[0, :, sl], NT_DIMS, preferred_element_type=F32)
        p = jnp.exp(s - jnp.max(s, axis=-1, keepdims=True))
        den = jnp.sum(p, axis=-1, keepdims=True)
        cat_ref[:, sl] = (_dot(p.astype(BF16), vm_ref[0, :, sl]) * (1.0 / den)).astype(BF16)
    xo = _dot(cat_ref[...], wo_ref[...])
    o_ref[0] = x + _rms(xo, gpost_ref[...])


def _cross_attention(x, kv, gain_pre, w_q, w_o, gain_post):
    B, S, _ = x.shape
    M = kv.shape[1]
    tok = pl.BlockSpec((1, TOK_TILE, D_MODEL), lambda b, i: (b, i, 0))
    return pl.pallas_call(
        _xattn_kernel,
        grid=(B, S // TOK_TILE),
        in_specs=[tok,
                  _const_spec((1, D_MODEL)),
                  _const_spec((D_MODEL, D_MODEL)),
                  pl.BlockSpec((1, M, D_MODEL), lambda b, i: (b, 0, 0)),
                  pl.BlockSpec((1, M, D_MODEL), lambda b, i: (b, 0, 1)),
                  _const_spec((D_MODEL, D_MODEL)),
                  _const_spec((1, D_MODEL))],
        out_specs=tok,
        out_shape=jax.ShapeDtypeStruct((B, S, D_MODEL), F32),
        scratch_shapes=[pltpu.VMEM((TOK_TILE, D_MODEL), BF16)],
        compiler_params=_params("parallel", "parallel"),
        name="cross_attention",
    )(x, gain_pre, w_q, kv, kv, w_o, gain_post)


FF_SPLIT = 2
FF_CHUNK = D_FF // FF_SPLIT


def _ffn_kernel(x_ref, gpre_ref, wgu_ref, wd_ref, gpost_ref, o_ref):
    x = x_ref[0]
    h = _rms(x, gpre_ref[...]).astype(BF16)
    f = None
    for j in range(FF_SPLIT):
        c0 = j * FF_CHUNK
        gate = _dot(h, wgu_ref[:, c0:c0 + FF_CHUNK])
        up = _dot(h, wgu_ref[:, D_FF + c0:D_FF + c0 + FF_CHUNK])
        a = (gate * jax.nn.sigmoid(gate) * up).astype(BF16)
        part = _dot(a, wd_ref[c0:c0 + FF_CHUNK, :])
        f = part if f is None else f + part
    o_ref[0] = x + _rms(f, gpost_ref[...])


def _ffn(x, gain_pre, w_gu, w_down, gain_post):
    B, S, _ = x.shape
    tok = pl.BlockSpec((1, TOK_TILE, D_MODEL), lambda b, i: (b, i, 0))
    return pl.pallas_call(
        _ffn_kernel,
        grid=(B, S // TOK_TILE),
        in_specs=[tok,
                  _const_spec((1, D_MODEL)),
                  _const_spec((D_MODEL, 2 * D_FF)),
                  _const_spec((D_FF, D_MODEL)),
                  _const_spec((1, D_MODEL))],
        out_specs=tok,
        out_shape=jax.ShapeDtypeStruct((B, S, D_MODEL), F32),
        compiler_params=_params("parallel", "parallel"),
        name="swiglu_ffn",
    )(x, gain_pre, w_gu, w_down, gain_post)


def kernel(x, mem, norm_mix_pre, norm_mix_post, w_in, attn_sink, attn_out_norm,
           ret_decay_fwd, ret_decay_bwd, ret_gn, w_out, norm_xa_pre, norm_xa_post,
           norm_mem, xa_wq, xa_wkv, xa_wo, norm_ffn_pre, norm_ffn_post,
           ffn_w_gu, ffn_w_down):
    B, S, D = x.shape
    assert D == D_MODEL and S % TOK_TILE == 0 and mem.shape[1] == N_MEM
    depth = w_in.shape[0]
    in_perm = _in_col_perm()
    ao_perm = _attn_out_perm()
    out_row_perm = np.concatenate([ao_perm, np.arange(ATTN_WIDTH, ATTN_WIDTH + RET_WIDTH)])
    tabs = (_rope_tables(S, ATTN_HEAD_DIM, 2) + _rope_tables(S, RET_HEAD_DIM, 1))

    def row(v):
        return v.reshape(1, -1).astype(F32)

    for l in range(depth):
        dmat, rowdec, cdec = _decay_tables(ret_decay_fwd[l], ret_decay_bwd[l])
        z = _in_projection(x, row(norm_mix_pre[l]), w_in[l][:, in_perm].astype(BF16), tabs)
        states_f, states_b = _retention_states(z, rowdec, cdec)
        x = _mixer(x, z, states_f, states_b, dmat, rowdec, attn_sink[l].astype(F32),
                   row(attn_out_norm[l][ao_perm]), row(ret_gn[l]),
                   w_out[l][out_row_perm].astype(BF16), row(norm_mix_post[l]))
        kv = _memory_kv(mem, row(norm_mem[l]), xa_wkv[l].astype(BF16))
        x = _cross_attention(x, kv, row(norm_xa_pre[l]), xa_wq[l].astype(BF16),
                             xa_wo[l].astype(BF16), row(norm_xa_post[l]))
        x = _ffn(x, row(norm_ffn_pre[l]), ffn_w_gu[l].astype(BF16),
                 ffn_w_down[l].astype(BF16), row(norm_ffn_post[l]))
    return x
```

```python
import math

import numpy as np
import jax
import jax.numpy as jnp
from jax import lax
from jax.experimental import pallas as pl
from jax.experimental.pallas import tpu as pltpu

D_MODEL = 1024
N_MEM = 256
ATTN_HEADS = 8
ATTN_KV_HEADS = 2
ATTN_GROUP = ATTN_HEADS // ATTN_KV_HEADS
ATTN_HEAD_DIM = 64
ATTN_WIDTH = ATTN_HEADS * ATTN_HEAD_DIM
ATTN_KV_WIDTH = ATTN_KV_HEADS * ATTN_HEAD_DIM
BLOCK = 128
RET_HEADS = 4
RET_HEAD_DIM = 128
RET_WIDTH = RET_HEADS * RET_HEAD_DIM
IN_COLS = ATTN_WIDTH + 2 * ATTN_KV_WIDTH + 4 * RET_WIDTH
XA_HEADS = 4
XA_HEAD_DIM = D_MODEL // XA_HEADS
D_FF = -(-(8 * D_MODEL) // (3 * 256)) * 256
ROPE_THETA = 10000.0
EPS = 1e-6
LOG2E = math.log2(math.e)

LANES = 128
VMEM_LIMIT_BYTES = 56 * 1024 * 1024

TOK_TILE = 512
CHUNKS_PER_TILE = TOK_TILE // BLOCK
STATE_CHUNKS = 4

T_AQ, T_RQ, T_RV, T_RG, T_AV = 0, 512, 1024, 1536, 2048
T_ROWS = T_AV + ATTN_KV_WIDTH
K_RK, K_AK = 0, RET_WIDTH
K_COLS = RET_WIDTH + ATTN_KV_WIDTH

F32 = jnp.float32
BF16 = jnp.bfloat16
NT_DIMS = (((1,), (1,)), ((), ()))
TN_DIMS = (((0,), (0,)), ((), ()))


def _const_spec(shape):
    nd = len(shape)
    return pl.BlockSpec(shape, lambda *_: (0,) * nd, pipeline_mode=pl.Buffered(1))


def _params(*sem):
    return pltpu.CompilerParams(dimension_semantics=sem,
                                vmem_limit_bytes=VMEM_LIMIT_BYTES)


def _rms(x, w):
    ms = jnp.mean(x * x, axis=-1, keepdims=True)
    return x * lax.rsqrt(ms + EPS) * w


def _dot(a, b):
    return jnp.dot(a, b, preferred_element_type=F32)


def _rope_angles(seq, dim):
    inv_freq = ROPE_THETA ** (-jnp.arange(0, dim, 2, dtype=F32) / dim)
    return jnp.arange(seq, dtype=F32)[:, None] * inv_freq[None, :]


def _decay_kernel(logit_ref, dmat_ref, kdec_ref, qdec_ref, cdec_ref):
    x = logit_ref[...]
    lg = jnp.minimum(x, 0.0) - jnp.log1p(jnp.exp(-jnp.abs(x)))
    ri = lax.broadcasted_iota(jnp.int32, (BLOCK, LANES), 0).astype(F32)
    ci = lax.broadcasted_iota(jnp.int32, (BLOCK, LANES), 1).astype(F32)
    c1 = ci[0:1, :]
    diff = ci - ri
    cdec_ref[...] = jnp.exp(BLOCK * lg)
    for h in range(RET_HEADS):
        lf = lg[h:h + 1, :]
        lb = lg[RET_HEADS + h:RET_HEADS + h + 1, :]
        dmat_ref[h] = jnp.exp(jnp.abs(diff) * jnp.where(diff >= 0, lf, lb))
        sl = slice(h * LANES, (h + 1) * LANES)
        kdec_ref[0, :, sl] = jnp.exp((BLOCK - 1 - ri) * lf)
        kdec_ref[1, :, sl] = jnp.exp(ri * lb)
        qdec_ref[h:h + 1, :] = jnp.exp((c1 + 1) * lf)
        qdec_ref[RET_HEADS + h:RET_HEADS + h + 1, :] = jnp.exp((BLOCK - c1) * lb)


def _decay_tables(decay_fwd, decay_bwd):
    logits = jnp.concatenate([decay_fwd, decay_bwd]).astype(F32)
    logits = jnp.broadcast_to(logits[:, None], (2 * RET_HEADS, LANES))
    return pl.pallas_call(
        _decay_kernel,
        out_shape=(jax.ShapeDtypeStruct((RET_HEADS, BLOCK, LANES), F32),
                   jax.ShapeDtypeStruct((2, BLOCK, RET_WIDTH), F32),
                   jax.ShapeDtypeStruct((2 * RET_HEADS, LANES), F32),
                   jax.ShapeDtypeStruct((2 * RET_HEADS, LANES), F32)),
        name="decay_tables",
    )(logits)


def _inproj_kernel(x_ref, g_ref, wt_ref, wk_ref, ck_ref, sk_ref, ca_ref, sa_ref,
                   cat_ref, sat_ref, crt_ref, srt_ref, zt_ref, zk_ref):
    h = _rms(x_ref[0], g_ref[...]).astype(BF16)

    zk = _dot(h, wk_ref[...])
    ck, sk = ck_ref[...], sk_ref[...]
    for g in range(RET_HEADS):
        zg = zk[:, g * LANES:(g + 1) * LANES]
        r = (zg * ck + pltpu.roll(zg, LANES // 2, 1) * sk) * (RET_HEAD_DIM ** -0.5)
        zk_ref[0, :, K_RK + g * LANES:K_RK + (g + 1) * LANES] = r.astype(BF16)
    zg = zk[:, K_AK:K_AK + LANES]
    lane = lax.broadcasted_iota(jnp.int32, (1, LANES), 1)
    low_half = (lane % ATTN_HEAD_DIM) < (ATTN_HEAD_DIM // 2)
    half = ATTN_HEAD_DIM // 2
    rot = jnp.where(low_half, pltpu.roll(zg, LANES - half, 1), pltpu.roll(zg, half, 1))
    zk_ref[0, :, K_AK:K_AK + LANES] = (zg * ca_ref[...] + rot * sa_ref[...]).astype(BF16)

    def proj_t(row, n_rows):
        return lax.dot_general(wt_ref[row:row + n_rows, :], h, NT_DIMS,
                               preferred_element_type=F32)

    def store_t(row, val):
        for c in range(CHUNKS_PER_TILE):
            zt_ref[0, c, row:row + val.shape[0], :] = val[:, c * BLOCK:(c + 1) * BLOCK].astype(BF16)

    def rope_t(z, row, head_dim, cos, sin):
        hd2 = head_dim // 2
        for k in range(z.shape[0] // head_dim):
            x1 = z[k * head_dim:k * head_dim + hd2]
            x2 = z[k * head_dim + hd2:(k + 1) * head_dim]
            store_t(row + k * head_dim, x1 * cos - x2 * sin)
            store_t(row + k * head_dim + hd2, x2 * cos + x1 * sin)

    q_scale = (ATTN_HEAD_DIM ** -0.5) * LOG2E
    rope_t(proj_t(T_AQ, ATTN_WIDTH), T_AQ, ATTN_HEAD_DIM,
           cat_ref[...] * q_scale, sat_ref[...] * q_scale)
    rope_t(proj_t(T_RQ, RET_WIDTH), T_RQ, RET_HEAD_DIM, crt_ref[...], srt_ref[...])
    store_t(T_RV, proj_t(T_RV, RET_WIDTH))
    store_t(T_RG, proj_t(T_RG, RET_WIDTH))
    store_t(T_AV, proj_t(T_AV, ATTN_KV_WIDTH))


def _in_projection(x, gain, w_t, w_k, tabs_tok, tabs_feat):
    B, S, _ = x.shape
    n = S // BLOCK
    tok_spec = pl.BlockSpec((TOK_TILE, LANES), lambda i, b: (i, 0))
    feat_specs = [pl.BlockSpec((t.shape[0], TOK_TILE), lambda i, b: (0, i)) for t in tabs_feat]
    return pl.pallas_call(
        _inproj_kernel,
        grid=(S // TOK_TILE, B),
        in_specs=[pl.BlockSpec((1, TOK_TILE, D_MODEL), lambda i, b: (b, i, 0)),
                  _const_spec((1, D_MODEL)),
                  _const_spec((T_ROWS, D_MODEL)),
                  _const_spec((D_MODEL, K_COLS)),
                  tok_spec, tok_spec, tok_spec, tok_spec] + feat_specs,
        out_specs=(pl.BlockSpec((1, CHUNKS_PER_TILE, T_ROWS, BLOCK), lambda i, b: (b, i, 0, 0)),
                   pl.BlockSpec((1, TOK_TILE, K_COLS), lambda i, b: (b, i, 0))),
        out_shape=(jax.ShapeDtypeStruct((B, n, T_ROWS, BLOCK), BF16),
                   jax.ShapeDtypeStruct((B, S, K_COLS), BF16)),
        compiler_params=_params("parallel", "parallel"),
        name="in_projection",
    )(x, gain, w_t, w_k, *tabs_tok, *tabs_feat)


def _state_kernel(kf_ref, vf_ref, kb_ref, vb_ref, kdec_ref, cdec_ref,
                  pf_ref, pb_ref, st_ref):
    @pl.when(pl.program_id(1) == 0)
    def _():
        st_ref[...] = jnp.zeros_like(st_ref)

    for i in range(STATE_CHUNKS):
        for d, (k_ref, v_ref, p_ref, c) in enumerate(((kf_ref, vf_ref, pf_ref, i),
                                                      (kb_ref, vb_ref, pb_ref, STATE_CHUNKS - 1 - i))):
            kd = (k_ref[0, c * BLOCK:(c + 1) * BLOCK, :].astype(F32) * kdec_ref[d]).astype(BF16)
            for h in range(RET_HEADS):
                sl = slice(h * LANES, (h + 1) * LANES)
                st = st_ref[d, h]
                p_ref[0, c, h] = st.astype(BF16)
                kv = _dot(v_ref[0, c, sl, :], kd[:, sl])
                row = d * RET_HEADS + h
                st_ref[d, h] = st * cdec_ref[row:row + 1, :] + kv


def _retention_states(zt, zk, kdec, cdec):
    B, n = zt.shape[0], zt.shape[1]
    ng = n // STATE_CHUNKS
    k_blk = (1, STATE_CHUNKS * BLOCK, RET_WIDTH)
    v_blk = (1, STATE_CHUNKS, RET_WIDTH, BLOCK)
    vrow = T_RV // RET_WIDTH
    st_blk = (1, STATE_CHUNKS, RET_HEADS, RET_HEAD_DIM, RET_HEAD_DIM)
    st_shape = jax.ShapeDtypeStruct((B, n, RET_HEADS, RET_HEAD_DIM, RET_HEAD_DIM), BF16)
    return pl.pallas_call(
        _state_kernel,
        grid=(B, ng),
        in_specs=[pl.BlockSpec(k_blk, lambda b, t: (b, t, 0)),
                  pl.BlockSpec(v_blk, lambda b, t: (b, t, vrow, 0)),
                  pl.BlockSpec(k_blk, lambda b, t: (b, ng - 1 - t, 0)),
                  pl.BlockSpec(v_blk, lambda b, t: (b, ng - 1 - t, vrow, 0)),
                  _const_spec((2, BLOCK, RET_WIDTH)),
                  _const_spec((2 * RET_HEADS, LANES))],
        out_specs=(pl.BlockSpec(st_blk, lambda b, t: (b, t, 0, 0, 0)),
                   pl.BlockSpec(st_blk, lambda b, t: (b, ng - 1 - t, 0, 0, 0))),
        out_shape=(st_shape, st_shape),
        scratch_shapes=[pltpu.VMEM((2, RET_HEADS, RET_HEAD_DIM, RET_HEAD_DIM), F32)],
        compiler_params=_params("parallel", "arbitrary"),
        name="retention_states",
    )(zk, zt, zk, zt, kdec, cdec)


def _mixer_kernel(sink_ref, x_ref, aq_ref, rq_ref, rv_ref, rg_ref, av_ref, avp_ref, avn_ref,
                  rk_ref, ak_ref, akp_ref, akn_ref, pf_ref, pb_ref, dmat_ref, qdec_ref,
                  an_ref, gn_ref, wout_ref, npost_ref, o_ref, kext_ref, vext_ref, mixt_ref):
    tile = pl.program_id(1)
    n_blocks = pl.num_programs(1) * CHUNKS_PER_TILE

    lane = lax.broadcasted_iota(jnp.int32, (1, LANES), 1)
    for g in range(ATTN_KV_HEADS):
        km = jnp.where((lane // ATTN_HEAD_DIM) == g, 1.0, 0.0).astype(BF16)
        kext_ref[g, 0:BLOCK] = akp_ref[0] * km
        kext_ref[g, BLOCK:BLOCK + TOK_TILE] = ak_ref[0] * km
        kext_ref[g, BLOCK + TOK_TILE:] = akn_ref[0] * km
    vext_ref[0] = avp_ref[0, 0]
    for c in range(CHUNKS_PER_TILE):
        vext_ref[1 + c] = av_ref[0, c]
    vext_ref[CHUNKS_PER_TILE + 1] = avn_ref[0, 0]

    kk = lax.broadcasted_iota(jnp.int32, (BLOCK, LANES), 0)
    qq = lax.broadcasted_iota(jnp.int32, (BLOCK, LANES), 1)
    neg = jnp.full((BLOCK, LANES), -jnp.inf, F32)
    zero = jnp.zeros((BLOCK, LANES), F32)

    def chunk(c, carry):
        r0 = pl.multiple_of(c * BLOCK, BLOCK)
        blk = tile * CHUNKS_PER_TILE + c

        q = jnp.concatenate([aq_ref[0, c, j * LANES:(j + 1) * LANES, :] for j in range(ATTN_GROUP)],
                            axis=1)
        vw = jnp.concatenate([vext_ref[c], vext_ref[c + 1], vext_ref[c + 2]], axis=1)
        no_prev = jnp.where(blk > 0, 0, BLOCK)
        no_next = jnp.where(blk < n_blocks - 1, 0, BLOCK)
        bias_prev = jnp.where(kk >= qq + no_prev, zero, neg)
        bias_next = jnp.where(kk <= qq - no_next, zero, neg)
        bias = jnp.concatenate([jnp.concatenate([bias_prev] * ATTN_GROUP, axis=1),
                                jnp.zeros((BLOCK, ATTN_GROUP * LANES), F32),
                                jnp.concatenate([bias_next] * ATTN_GROUP, axis=1)], axis=0)
        outs = []
        for g in range(ATTN_KV_HEADS):
            kw = kext_ref[g, pl.ds(r0, 3 * BLOCK), :]
            s = _dot(kw, q) + bias
            sk = jnp.concatenate(
                [jnp.full((1, LANES), sink_ref[g * ATTN_GROUP + j] * LOG2E, F32)
                 for j in range(ATTN_GROUP)], axis=1)
            m = jnp.maximum(jnp.max(s, axis=0, keepdims=True), sk)
            p = jnp.exp2(s - m)
            den = jnp.sum(p, axis=0, keepdims=True) + jnp.exp2(sk - m)
            vg = vw[g * ATTN_HEAD_DIM:(g + 1) * ATTN_HEAD_DIM, :]
            outs.append(_dot(vg, p.astype(BF16)) * (1.0 / den))
        a = jnp.concatenate([outs[g][:, j * LANES:(j + 1) * LANES]
                             for j in range(ATTN_GROUP) for g in range(ATTN_KV_HEADS)], axis=0)
        ms = jnp.sum(a * a, axis=0, keepdims=True) * (1.0 / ATTN_WIDTH)
        mixt_ref[c, 0:ATTN_WIDTH, :] = (a * lax.rsqrt(ms + EPS) * an_ref[...]).astype(BF16)

        for h in range(RET_HEADS):
            sl = slice(h * LANES, (h + 1) * LANES)
            qt = rq_ref[0, c, sl, :]
            pt = _dot(rk_ref[0, pl.ds(r0, BLOCK), sl], qt) * dmat_ref[h]
            y = _dot(rv_ref[0, c, sl, :], pt.astype(BF16))
            qf = qt.astype(F32)
            qd = jnp.concatenate(
                [(qf * qdec_ref[h:h + 1, :]).astype(BF16),
                 (qf * qdec_ref[RET_HEADS + h:RET_HEADS + h + 1, :]).astype(BF16)], axis=0)
            st = jnp.concatenate([pf_ref[0, c, h], pb_ref[0, c, h]], axis=1)
            y = y + _dot(st, qd)
            mu = jnp.mean(y, axis=0, keepdims=True)
            yc = y - mu
            var = jnp.mean(yc * yc, axis=0, keepdims=True)
            yn = yc * lax.rsqrt(var + EPS) * gn_ref[sl, :]
            gate = rg_ref[0, c, sl, :].astype(F32)
            r = gate * jax.nn.sigmoid(gate) * yn
            mixt_ref[c, ATTN_WIDTH + h * LANES:ATTN_WIDTH + (h + 1) * LANES, :] = r.astype(BF16)
        return carry

    lax.fori_loop(0, CHUNKS_PER_TILE, chunk, 0)

    mixt = jnp.concatenate([mixt_ref[c] for c in range(CHUNKS_PER_TILE)], axis=1)
    mix = lax.dot_general(mixt, wout_ref[...], TN_DIMS, preferred_element_type=F32)
    o_ref[0] = x_ref[0] + _rms(mix, npost_ref[...])


def _mixer(x, zt, zk, states_f, states_b, dmat, qdec, sink, attn_norm, ret_gn, w_out, norm_post):
    B, S, _ = x.shape
    n = S // BLOCK
    cpt = CHUNKS_PER_TILE
    wide_t = (1, cpt, RET_WIDTH, BLOCK)
    narrow_t = (1, cpt, ATTN_KV_WIDTH, BLOCK)
    edge_t = (1, 1, ATTN_KV_WIDTH, BLOCK)
    edge_k = (1, BLOCK, ATTN_KV_WIDTH)
    v_row = T_AV // ATTN_KV_WIDTH
    k_col = K_AK // ATTN_KV_WIDTH
    st_blk = (1, cpt, RET_HEADS, RET_HEAD_DIM, RET_HEAD_DIM)

    def rows(r):
        return lambda b, i: (b, i, r, 0)

    def prev(i):
        return jnp.maximum(i * cpt - 1, 0)

    def nxt(i):
        return jnp.minimum((i + 1) * cpt, n - 1)

    return pl.pallas_call(
        _mixer_kernel,
        grid=(B, S // TOK_TILE),
        in_specs=[pl.BlockSpec(memory_space=pltpu.SMEM),
                  pl.BlockSpec((1, TOK_TILE, D_MODEL), lambda b, i: (b, i, 0)),
                  pl.BlockSpec(wide_t, rows(T_AQ // RET_WIDTH)),
                  pl.BlockSpec(wide_t, rows(T_RQ // RET_WIDTH)),
                  pl.BlockSpec(wide_t, rows(T_RV // RET_WIDTH)),
                  pl.BlockSpec(wide_t, rows(T_RG // RET_WIDTH)),
                  pl.BlockSpec(narrow_t, rows(v_row)),
                  pl.BlockSpec(edge_t, lambda b, i: (b, prev(i), v_row, 0)),
                  pl.BlockSpec(edge_t, lambda b, i: (b, nxt(i), v_row, 0)),
                  pl.BlockSpec((1, TOK_TILE, RET_WIDTH), lambda b, i: (b, i, 0)),
                  pl.BlockSpec((1, TOK_TILE, ATTN_KV_WIDTH), lambda b, i: (b, i, k_col)),
                  pl.BlockSpec(edge_k, lambda b, i: (b, prev(i), k_col)),
                  pl.BlockSpec(edge_k, lambda b, i: (b, nxt(i), k_col)),
                  pl.BlockSpec(st_blk, lambda b, i: (b, i, 0, 0, 0)),
                  pl.BlockSpec(st_blk, lambda b, i: (b, i, 0, 0, 0)),
                  _const_spec((RET_HEADS, BLOCK, LANES)),
                  _const_spec((2 * RET_HEADS, LANES)),
                  _const_spec((ATTN_WIDTH, LANES)),
                  _const_spec((RET_WIDTH, LANES)),
                  _const_spec((D_MODEL, D_MODEL)),
                  _const_spec((1, D_MODEL))],
        out_specs=pl.BlockSpec((1, TOK_TILE, D_MODEL), lambda b, i: (b, i, 0)),
        out_shape=jax.ShapeDtypeStruct((B, S, D_MODEL), F32),
        scratch_shapes=[pltpu.VMEM((ATTN_KV_HEADS, TOK_TILE + 2 * BLOCK, ATTN_KV_WIDTH), BF16),
                        pltpu.VMEM((cpt + 2, ATTN_KV_WIDTH, BLOCK), BF16),
                        pltpu.VMEM((cpt, D_MODEL, BLOCK), BF16)],
        compiler_params=_params("parallel", "parallel"),
        name="token_mixer",
    )(sink, x, zt, zt, zt, zt, zt, zt, zt, zk, zk, zk, zk, states_f, states_b, dmat, qdec,
      attn_norm, ret_gn, w_out, norm_post)


def _memkv_kernel(m_ref, g_ref, w_ref, kv_ref):
    mn = _rms(m_ref[0], g_ref[...]).astype(BF16)
    kv_ref[0] = _dot(mn, w_ref[...]).astype(BF16)


def _memory_kv(mem, gain, w_kv):
    B, M, _ = mem.shape
    return pl.pallas_call(
        _memkv_kernel,
        grid=(B,),
        in_specs=[pl.BlockSpec((1, M, D_MODEL), lambda b: (b, 0, 0)),
                  _const_spec((1, D_MODEL)),
                  _const_spec((D_MODEL, 2 * D_MODEL))],
        out_specs=pl.BlockSpec((1, M, 2 * D_MODEL), lambda b: (b, 0, 0)),
        out_shape=jax.ShapeDtypeStruct((B, M, 2 * D_MODEL), BF16),
        compiler_params=_params("parallel"),
        name="memory_kv",
    )(mem, gain, w_kv)


def _xattn_kernel(x_ref, gpre_ref, wq_ref, km_ref, vm_ref, wo_ref, gpost_ref,
                  o_ref, cat_ref):
    x = x_ref[0]
    h = _rms(x, gpre_ref[...]).astype(BF16)
    q = (_dot(h, wq_ref[...]) * (XA_HEAD_DIM ** -0.5)).astype(BF16)
    for hd in range(XA_HEADS):
        sl = slice(hd * XA_HEAD_DIM, (hd + 1) * XA_HEAD_DIM)
        s = lax.dot_general(q[:, sl], km_ref[0, :, sl], NT_DIMS, preferred_element_type=F32)
        p = jnp.exp(s - jnp.max(s, axis=-1, keepdims=True))
        den = jnp.sum(p, axis=-1, keepdims=True)
        cat_ref[:, sl] = (_dot(p.astype(BF16), vm_ref[0, :, sl]) * (1.0 / den)).astype(BF16)
    xo = _dot(cat_ref[...], wo_ref[...])
    o_ref[0] = x + _rms(xo, gpost_ref[...])


def _cross_attention(x, kv, gain_pre, w_q, w_o, gain_post):
    B, S, _ = x.shape
    M = kv.shape[1]
    tok = pl.BlockSpec((1, TOK_TILE, D_MODEL), lambda b, i: (b, i, 0))
    return pl.pallas_call(
        _xattn_kernel,
        grid=(B, S // TOK_TILE),
        in_specs=[tok,
                  _const_spec((1, D_MODEL)),
                  _const_spec((D_MODEL, D_MODEL)),
                  pl.BlockSpec((1, M, D_MODEL), lambda b, i: (b, 0, 0)),
                  pl.BlockSpec((1, M, D_MODEL), lambda b, i: (b, 0, 1)),
                  _const_spec((D_MODEL, D_MODEL)),
                  _const_spec((1, D_MODEL))],
        out_specs=tok,
        out_shape=jax.ShapeDtypeStruct((B, S, D_MODEL), F32),
        scratch_shapes=[pltpu.VMEM((TOK_TILE, D_MODEL), BF16)],
        compiler_params=_params("parallel", "parallel"),
        name="cross_attention",
    )(x, gain_pre, w_q, kv, kv, w_o, gain_post)


FF_SPLIT = 2
FF_CHUNK = D_FF // FF_SPLIT


def _ffn_kernel(x_ref, gpre_ref, wgu_ref, wd_ref, gpost_ref, o_ref):
    x = x_ref[0]
    h = _rms(x, gpre_ref[...]).astype(BF16)
    f = None
    for j in range(FF_SPLIT):
        c0 = j * FF_CHUNK
        gate = _dot(h, wgu_ref[:, c0:c0 + FF_CHUNK])
        up = _dot(h, wgu_ref[:, D_FF + c0:D_FF + c0 + FF_CHUNK])
        a = (gate * jax.nn.sigmoid(gate) * up).astype(BF16)
        part = _dot(a, wd_ref[c0:c0 + FF_CHUNK, :])
        f = part if f is None else f + part
    o_ref[0] = x + _rms(f, gpost_ref[...])


def _ffn(x, gain_pre, w_gu, w_down, gain_post):
    B, S, _ = x.shape
    tok = pl.BlockSpec((1, TOK_TILE, D_MODEL), lambda b, i: (b, i, 0))
    return pl.pallas_call(
        _ffn_kernel,
        grid=(B, S // TOK_TILE),
        in_specs=[tok,
                  _const_spec((1, D_MODEL)),
                  _const_spec((D_MODEL, 2 * D_FF)),
                  _const_spec((D_FF, D_MODEL)),
                  _const_spec((1, D_MODEL))],
        out_specs=tok,
        out_shape=jax.ShapeDtypeStruct((B, S, D_MODEL), F32),
        compiler_params=_params("parallel", "parallel"),
        name="swiglu_ffn",
    )(x, gain_pre, w_gu, w_down, gain_post)


def _attn_feature_order():
    order = []
    for j in range(ATTN_GROUP):
        for g in range(ATTN_KV_HEADS):
            h = g * ATTN_GROUP + j
            order += list(range(h * ATTN_HEAD_DIM, (h + 1) * ATTN_HEAD_DIM))
    return np.asarray(order, np.int32)


def kernel(x, mem, norm_mix_pre, norm_mix_post, w_in, attn_sink, attn_out_norm,
           ret_decay_fwd, ret_decay_bwd, ret_gn, w_out, norm_xa_pre, norm_xa_post,
           norm_mem, xa_wq, xa_wkv, xa_wo, norm_ffn_pre, norm_ffn_post,
           ffn_w_gu, ffn_w_down):
    B, S, D = x.shape
    assert D == D_MODEL and S % (TOK_TILE * STATE_CHUNKS // CHUNKS_PER_TILE) == 0
    assert mem.shape[1] == N_MEM
    depth = w_in.shape[0]
    feat = _attn_feature_order()
    o_k = ATTN_WIDTH
    o_v = o_k + ATTN_KV_WIDTH
    o_r = o_v + ATTN_KV_WIDTH
    rw = RET_WIDTH

    ang_a = _rope_angles(S, ATTN_HEAD_DIM)
    ang_r = _rope_angles(S, RET_HEAD_DIM)
    cos_a, sin_a, cos_r, sin_r = jnp.cos(ang_a), jnp.sin(ang_a), jnp.cos(ang_r), jnp.sin(ang_r)
    tabs_tok = (jnp.concatenate([cos_r, cos_r], axis=1), jnp.concatenate([-sin_r, sin_r], axis=1),
                jnp.concatenate([cos_a] * 4, axis=1),
                jnp.concatenate([-sin_a, sin_a, -sin_a, sin_a], axis=1))
    tabs_feat = (cos_a.T, sin_a.T, cos_r.T, sin_r.T)

    def row(v):
        return v.reshape(1, -1).astype(F32)

    def col(v):
        return jnp.broadcast_to(v.astype(F32)[:, None], (v.shape[0], LANES))

    for l in range(depth):
        w = w_in[l]
        w_t = jnp.concatenate([w[:, :o_k][:, feat], w[:, o_r:o_r + rw], w[:, o_r + 2 * rw:],
                               w[:, o_v:o_r]], axis=1).T.astype(BF16)
        w_k = jnp.concatenate([w[:, o_r + rw:o_r + 2 * rw], w[:, o_k:o_v]], axis=1).astype(BF16)
        out_rows = np.concatenate([feat, np.arange(ATTN_WIDTH, ATTN_WIDTH + RET_WIDTH)])

        dmat, kdec, qdec, cdec = _decay_tables(ret_decay_fwd[l], ret_decay_bwd[l])
        zt, zk = _in_projection(x, row(norm_mix_pre[l]), w_t, w_k, tabs_tok, tabs_feat)
        states_f, states_b = _retention_states(zt, zk, kdec, cdec)
        x = _mixer(x, zt, zk, states_f, states_b, dmat, qdec, attn_sink[l].astype(F32),
                   col(attn_out_norm[l][feat]), col(ret_gn[l]),
                   w_out[l][out_rows].astype(BF16), row(norm_mix_post[l]))
        kv = _memory_kv(mem, row(norm_mem[l]), xa_wkv[l].astype(BF16))
        x = _cross_attention(x, kv, row(norm_xa_pre[l]), xa_wq[l].astype(BF16),
                             xa_wo[l].astype(BF16), row(norm_xa_post[l]))
        x = _ffn(x, row(norm_ffn_pre[l]), ffn_w_gu[l].astype(BF16),
                 ffn_w_down[l].astype(BF16), row(norm_ffn_post[l]))
    return x
```

```python
import math

import numpy as np
import jax
import jax.numpy as jnp
from jax import lax
from jax.experimental import pallas as pl
from jax.experimental.pallas import tpu as pltpu

D_MODEL = 1024
N_MEM = 256
ATTN_HEADS = 8
ATTN_KV_HEADS = 2
ATTN_GROUP = ATTN_HEADS // ATTN_KV_HEADS
ATTN_HEAD_DIM = 64
ATTN_WIDTH = ATTN_HEADS * ATTN_HEAD_DIM
ATTN_KV_WIDTH = ATTN_KV_HEADS * ATTN_HEAD_DIM
BLOCK = 128
RET_HEADS = 4
RET_HEAD_DIM = 128
RET_WIDTH = RET_HEADS * RET_HEAD_DIM
IN_COLS = ATTN_WIDTH + 2 * ATTN_KV_WIDTH + 4 * RET_WIDTH
XA_HEADS = 4
XA_HEAD_DIM = D_MODEL // XA_HEADS
D_FF = -(-(8 * D_MODEL) // (3 * 256)) * 256
ROPE_THETA = 10000.0
EPS = 1e-6
LOG2E = math.log2(math.e)

LANES = 128
VMEM_LIMIT_BYTES = 56 * 1024 * 1024

TOK_TILE = 512
CHUNKS_PER_TILE = TOK_TILE // BLOCK
STATE_CHUNKS = 4

T_AQ, T_RQ, T_RV, T_RG, T_AV = 0, 512, 1024, 1536, 2048
T_ROWS = T_AV + ATTN_KV_WIDTH
K_RK, K_AK = 0, RET_WIDTH
K_COLS = RET_WIDTH + ATTN_KV_WIDTH

F32 = jnp.float32
BF16 = jnp.bfloat16
NT_DIMS = (((1,), (1,)), ((), ()))
TN_DIMS = (((0,), (0,)), ((), ()))


def _const_spec(shape):
    nd = len(shape)
    return pl.BlockSpec(shape, lambda *_: (0,) * nd, pipeline_mode=pl.Buffered(1))


def _params(*sem, flags=None):
    return pltpu.CompilerParams(dimension_semantics=sem,
                                vmem_limit_bytes=VMEM_LIMIT_BYTES, flags=flags)


def _rms(x, w):
    ms = jnp.mean(x * x, axis=-1, keepdims=True)
    return x * lax.rsqrt(ms + EPS) * w


def _dot(a, b):
    return jnp.dot(a, b, preferred_element_type=F32)


def _rope_angles(seq, dim):
    inv_freq = ROPE_THETA ** (-jnp.arange(0, dim, 2, dtype=F32) / dim)
    return jnp.arange(seq, dtype=F32)[:, None] * inv_freq[None, :]


def _decay_kernel(logit_ref, dmat_ref, kdec_ref, qdec_ref, cdec_ref):
    x = logit_ref[...]
    lg = jnp.minimum(x, 0.0) - jnp.log1p(jnp.exp(-jnp.abs(x)))
    ri = lax.broadcasted_iota(jnp.int32, (BLOCK, LANES), 0).astype(F32)
    ci = lax.broadcasted_iota(jnp.int32, (BLOCK, LANES), 1).astype(F32)
    c1 = ci[0:1, :]
    diff = ci - ri
    cdec_ref[...] = jnp.exp(BLOCK * lg)
    for h in range(RET_HEADS):
        lf = lg[h:h + 1, :]
        lb = lg[RET_HEADS + h:RET_HEADS + h + 1, :]
        dmat_ref[h] = jnp.exp(jnp.abs(diff) * jnp.where(diff >= 0, lf, lb))
        sl = slice(h * LANES, (h + 1) * LANES)
        kdec_ref[0, :, sl] = jnp.exp((BLOCK - 1 - ri) * lf)
        kdec_ref[1, :, sl] = jnp.exp(ri * lb)
        qdec_ref[h:h + 1, :] = jnp.exp((c1 + 1) * lf)
        qdec_ref[RET_HEADS + h:RET_HEADS + h + 1, :] = jnp.exp((BLOCK - c1) * lb)


def _decay_tables(decay_fwd, decay_bwd):
    logits = jnp.concatenate([decay_fwd, decay_bwd]).astype(F32)
    logits = jnp.broadcast_to(logits[:, None], (2 * RET_HEADS, LANES))
    return pl.pallas_call(
        _decay_kernel,
        out_shape=(jax.ShapeDtypeStruct((RET_HEADS, BLOCK, LANES), F32),
                   jax.ShapeDtypeStruct((2, BLOCK, RET_WIDTH), F32),
                   jax.ShapeDtypeStruct((2 * RET_HEADS, LANES), F32),
                   jax.ShapeDtypeStruct((2 * RET_HEADS, LANES), F32)),
        name="decay_tables",
    )(logits)


def _inproj_kernel(x_ref, g_ref, wt_ref, wk_ref, ck_ref, sk_ref, ca_ref, sa_ref,
                   cat_ref, sat_ref, crt_ref, srt_ref, zt_ref, zk_ref):
    h = _rms(x_ref[0], g_ref[...]).astype(BF16)

    zk = _dot(h, wk_ref[...])
    ck, sk = ck_ref[...], sk_ref[...]
    for g in range(RET_HEADS):
        zg = zk[:, g * LANES:(g + 1) * LANES]
        r = (zg * ck + pltpu.roll(zg, LANES // 2, 1) * sk) * (RET_HEAD_DIM ** -0.5)
        zk_ref[0, :, K_RK + g * LANES:K_RK + (g + 1) * LANES] = r.astype(BF16)
    zg = zk[:, K_AK:K_AK + LANES]
    lane = lax.broadcasted_iota(jnp.int32, (1, LANES), 1)
    low_half = (lane % ATTN_HEAD_DIM) < (ATTN_HEAD_DIM // 2)
    half = ATTN_HEAD_DIM // 2
    rot = jnp.where(low_half, pltpu.roll(zg, LANES - half, 1), pltpu.roll(zg, half, 1))
    zk_ref[0, :, K_AK:K_AK + LANES] = (zg * ca_ref[...] + rot * sa_ref[...]).astype(BF16)

    def proj_t(row, n_rows):
        return lax.dot_general(wt_ref[row:row + n_rows, :], h, NT_DIMS,
                               preferred_element_type=F32)

    def store_t(row, val):
        for c in range(CHUNKS_PER_TILE):
            zt_ref[0, c, row:row + val.shape[0], :] = val[:, c * BLOCK:(c + 1) * BLOCK].astype(BF16)

    def rope_t(z, row, head_dim, cos, sin):
        hd2 = head_dim // 2
        for k in range(z.shape[0] // head_dim):
            x1 = z[k * head_dim:k * head_dim + hd2]
            x2 = z[k * head_dim + hd2:(k + 1) * head_dim]
            store_t(row + k * head_dim, x1 * cos - x2 * sin)
            store_t(row + k * head_dim + hd2, x2 * cos + x1 * sin)

    q_scale = (ATTN_HEAD_DIM ** -0.5) * LOG2E
    rope_t(proj_t(T_AQ, ATTN_WIDTH), T_AQ, ATTN_HEAD_DIM,
           cat_ref[...] * q_scale, sat_ref[...] * q_scale)
    rope_t(proj_t(T_RQ, RET_WIDTH), T_RQ, RET_HEAD_DIM, crt_ref[...], srt_ref[...])
    store_t(T_RV, proj_t(T_RV, RET_WIDTH))
    store_t(T_RG, proj_t(T_RG, RET_WIDTH))
    store_t(T_AV, proj_t(T_AV, ATTN_KV_WIDTH))


def _in_projection(x, gain, w_t, w_k, tabs_tok, tabs_feat):
    B, S, _ = x.shape
    n = S // BLOCK
    tok_spec = pl.BlockSpec((TOK_TILE, LANES), lambda i, b: (i, 0))
    feat_specs = [pl.BlockSpec((t.shape[0], TOK_TILE), lambda i, b: (0, i)) for t in tabs_feat]
    return pl.pallas_call(
        _inproj_kernel,
        grid=(S // TOK_TILE, B),
        in_specs=[pl.BlockSpec((1, TOK_TILE, D_MODEL), lambda i, b: (b, i, 0)),
                  _const_spec((1, D_MODEL)),
                  _const_spec((T_ROWS, D_MODEL)),
                  _const_spec((D_MODEL, K_COLS)),
                  tok_spec, tok_spec, tok_spec, tok_spec] + feat_specs,
        out_specs=(pl.BlockSpec((1, CHUNKS_PER_TILE, T_ROWS, BLOCK), lambda i, b: (b, i, 0, 0)),
                   pl.BlockSpec((1, TOK_TILE, K_COLS), lambda i, b: (b, i, 0))),
        out_shape=(jax.ShapeDtypeStruct((B, n, T_ROWS, BLOCK), BF16),
                   jax.ShapeDtypeStruct((B, S, K_COLS), BF16)),
        compiler_params=_params("parallel", "parallel"),
        name="in_projection",
    )(x, gain, w_t, w_k, *tabs_tok, *tabs_feat)


def _state_kernel(kf_ref, vf_ref, kb_ref, vb_ref, kdec_ref, cdec_ref,
                  pf_ref, pb_ref, st_ref):
    @pl.when(pl.program_id(1) == 0)
    def _():
        st_ref[...] = jnp.zeros_like(st_ref)

    for i in range(STATE_CHUNKS):
        for d, (k_ref, v_ref, p_ref, c) in enumerate(((kf_ref, vf_ref, pf_ref, i),
                                                      (kb_ref, vb_ref, pb_ref, STATE_CHUNKS - 1 - i))):
            kd = (k_ref[0, c * BLOCK:(c + 1) * BLOCK, :].astype(F32) * kdec_ref[d]).astype(BF16)
            for h in range(RET_HEADS):
                sl = slice(h * LANES, (h + 1) * LANES)
                st = st_ref[d, h]
                p_ref[0, c, h] = st.astype(BF16)
                kv = _dot(v_ref[0, c, sl, :], kd[:, sl])
                row = d * RET_HEADS + h
                st_ref[d, h] = st * cdec_ref[row:row + 1, :] + kv


def _retention_states(zt, zk, kdec, cdec):
    B, n = zt.shape[0], zt.shape[1]
    ng = n // STATE_CHUNKS
    k_blk = (1, STATE_CHUNKS * BLOCK, RET_WIDTH)
    v_blk = (1, STATE_CHUNKS, RET_WIDTH, BLOCK)
    vrow = T_RV // RET_WIDTH
    st_blk = (1, STATE_CHUNKS, RET_HEADS, RET_HEAD_DIM, RET_HEAD_DIM)
    st_shape = jax.ShapeDtypeStruct((B, n, RET_HEADS, RET_HEAD_DIM, RET_HEAD_DIM), BF16)
    return pl.pallas_call(
        _state_kernel,
        grid=(B, ng),
        in_specs=[pl.BlockSpec(k_blk, lambda b, t: (b, t, 0)),
                  pl.BlockSpec(v_blk, lambda b, t: (b, t, vrow, 0)),
                  pl.BlockSpec(k_blk, lambda b, t: (b, ng - 1 - t, 0)),
                  pl.BlockSpec(v_blk, lambda b, t: (b, ng - 1 - t, vrow, 0)),
                  _const_spec((2, BLOCK, RET_WIDTH)),
                  _const_spec((2 * RET_HEADS, LANES))],
        out_specs=(pl.BlockSpec(st_blk, lambda b, t: (b, t, 0, 0, 0)),
                   pl.BlockSpec(st_blk, lambda b, t: (b, ng - 1 - t, 0, 0, 0))),
        out_shape=(st_shape, st_shape),
        scratch_shapes=[pltpu.VMEM((2, RET_HEADS, RET_HEAD_DIM, RET_HEAD_DIM), F32)],
        compiler_params=_params("parallel", "arbitrary"),
        name="retention_states",
    )(zk, zt, zk, zt, kdec, cdec)


def _mixer_kernel(sink_ref, x_ref, aq_ref, rq_ref, rv_ref, rg_ref, av_ref, avp_ref, avn_ref,
                  rk_ref, ak_ref, akp_ref, akn_ref, pf_ref, pb_ref, dmat_ref, qdec_ref,
                  an_ref, gn_ref, wout_ref, npost_ref, o_ref, kext_ref, vext_ref, mixt_ref,
                  sc_ref):
    tile = pl.program_id(1)
    n_blocks = pl.num_programs(1) * CHUNKS_PER_TILE

    lane = lax.broadcasted_iota(jnp.int32, (1, LANES), 1)
    for g in range(ATTN_KV_HEADS):
        km = jnp.where((lane // ATTN_HEAD_DIM) == g, 1.0, 0.0).astype(BF16)
        kext_ref[g, 0:BLOCK] = akp_ref[0] * km
        kext_ref[g, BLOCK:BLOCK + TOK_TILE] = ak_ref[0] * km
        kext_ref[g, BLOCK + TOK_TILE:] = akn_ref[0] * km
    vext_ref[0] = avp_ref[0, 0]
    for c in range(CHUNKS_PER_TILE):
        vext_ref[1 + c] = av_ref[0, c]
    vext_ref[CHUNKS_PER_TILE + 1] = avn_ref[0, 0]

    kk = lax.broadcasted_iota(jnp.int32, (BLOCK, LANES), 0)
    qq = lax.broadcasted_iota(jnp.int32, (BLOCK, LANES), 1)
    neg = jnp.full((BLOCK, LANES), -jnp.inf, F32)
    zero = jnp.zeros((BLOCK, LANES), F32)

    def scores_into(slot, c):
        q = jnp.concatenate([aq_ref[0, c, j * LANES:(j + 1) * LANES, :] for j in range(ATTN_GROUP)],
                            axis=1)
        r0 = c * BLOCK
        for g in range(ATTN_KV_HEADS):
            sc_ref[slot, g] = _dot(kext_ref[g, pl.ds(r0, 3 * BLOCK), :], q)

    def chunk(c, has_next):
        r0 = c * BLOCK
        blk = tile * CHUNKS_PER_TILE + c
        slot = c % 2
        if has_next:
            scores_into(1 - slot, c + 1)

        vw = jnp.concatenate([vext_ref[c], vext_ref[c + 1], vext_ref[c + 2]], axis=1)
        no_prev = jnp.where(blk > 0, 0, BLOCK)
        no_next = jnp.where(blk < n_blocks - 1, 0, BLOCK)
        bias_prev = jnp.where(kk >= qq + no_prev, zero, neg)
        bias_next = jnp.where(kk <= qq - no_next, zero, neg)
        bias = jnp.concatenate([jnp.concatenate([bias_prev] * ATTN_GROUP, axis=1),
                                jnp.zeros((BLOCK, ATTN_GROUP * LANES), F32),
                                jnp.concatenate([bias_next] * ATTN_GROUP, axis=1)], axis=0)
        heads = []
        for h in range(RET_HEADS):
            sl = slice(h * LANES, (h + 1) * LANES)
            qt = rq_ref[0, c, sl, :]
            heads.append((qt, _dot(rk_ref[0, pl.ds(r0, BLOCK), sl], qt)))

        outs = []
        for g in range(ATTN_KV_HEADS):
            s = sc_ref[slot, g] + bias
            sk = jnp.concatenate(
                [jnp.full((1, LANES), sink_ref[g * ATTN_GROUP + j] * LOG2E, F32)
                 for j in range(ATTN_GROUP)], axis=1)
            m = jnp.maximum(jnp.max(s, axis=0, keepdims=True), sk)
            p = jnp.exp2(s - m)
            den = jnp.sum(p, axis=0, keepdims=True) + jnp.exp2(sk - m)
            vg = vw[g * ATTN_HEAD_DIM:(g + 1) * ATTN_HEAD_DIM, :]
            outs.append(_dot(vg, p.astype(BF16)) * (1.0 / den))
        ys = []
        for h in range(RET_HEADS):
            sl = slice(h * LANES, (h + 1) * LANES)
            qt, pt = heads[h]
            y = _dot(rv_ref[0, c, sl, :], (pt * dmat_ref[h]).astype(BF16))
            qf = qt.astype(F32)
            qd = jnp.concatenate(
                [(qf * qdec_ref[h:h + 1, :]).astype(BF16),
                 (qf * qdec_ref[RET_HEADS + h:RET_HEADS + h + 1, :]).astype(BF16)], axis=0)
            st = jnp.concatenate([pf_ref[0, c, h], pb_ref[0, c, h]], axis=1)
            ys.append(y + _dot(st, qd))

        a = jnp.concatenate([outs[g][:, j * LANES:(j + 1) * LANES]
                             for j in range(ATTN_GROUP) for g in range(ATTN_KV_HEADS)], axis=0)
        ms = jnp.sum(a * a, axis=0, keepdims=True) * (1.0 / ATTN_WIDTH)
        mixt_ref[c, 0:ATTN_WIDTH, :] = (a * lax.rsqrt(ms + EPS) * an_ref[...]).astype(BF16)
        for h in range(RET_HEADS):
            sl = slice(h * LANES, (h + 1) * LANES)
            y = ys[h]
            mu = jnp.mean(y, axis=0, keepdims=True)
            yc = y - mu
            var = jnp.mean(yc * yc, axis=0, keepdims=True)
            yn = yc * lax.rsqrt(var + EPS) * gn_ref[sl, :]
            gate = rg_ref[0, c, sl, :].astype(F32)
            r = gate * jax.nn.sigmoid(gate) * yn
            mixt_ref[c, ATTN_WIDTH + h * LANES:ATTN_WIDTH + (h + 1) * LANES, :] = r.astype(BF16)

    scores_into(0, 0)
    for c in range(CHUNKS_PER_TILE):
        chunk(c, c + 1 < CHUNKS_PER_TILE)

    mixt = jnp.concatenate([mixt_ref[c] for c in range(CHUNKS_PER_TILE)], axis=1)
    mix = lax.dot_general(mixt, wout_ref[...], TN_DIMS, preferred_element_type=F32)
    o_ref[0] = x_ref[0] + _rms(mix, npost_ref[...])


def _mixer(x, zt, zk, states_f, states_b, dmat, qdec, sink, attn_norm, ret_gn, w_out, norm_post):
    B, S, _ = x.shape
    n = S // BLOCK
    cpt = CHUNKS_PER_TILE
    wide_t = (1, cpt, RET_WIDTH, BLOCK)
    narrow_t = (1, cpt, ATTN_KV_WIDTH, BLOCK)
    edge_t = (1, 1, ATTN_KV_WIDTH, BLOCK)
    edge_k = (1, BLOCK, ATTN_KV_WIDTH)
    v_row = T_AV // ATTN_KV_WIDTH
    k_col = K_AK // ATTN_KV_WIDTH
    st_blk = (1, cpt, RET_HEADS, RET_HEAD_DIM, RET_HEAD_DIM)

    def rows(r):
        return lambda b, i: (b, i, r, 0)

    def prev(i):
        return jnp.maximum(i * cpt - 1, 0)

    def nxt(i):
        return jnp.minimum((i + 1) * cpt, n - 1)

    return pl.pallas_call(
        _mixer_kernel,
        grid=(B, S // TOK_TILE),
        in_specs=[pl.BlockSpec(memory_space=pltpu.SMEM),
                  pl.BlockSpec((1, TOK_TILE, D_MODEL), lambda b, i: (b, i, 0)),
                  pl.BlockSpec(wide_t, rows(T_AQ // RET_WIDTH)),
                  pl.BlockSpec(wide_t, rows(T_RQ // RET_WIDTH)),
                  pl.BlockSpec(wide_t, rows(T_RV // RET_WIDTH)),
                  pl.BlockSpec(wide_t, rows(T_RG // RET_WIDTH)),
                  pl.BlockSpec(narrow_t, rows(v_row)),
                  pl.BlockSpec(edge_t, lambda b, i: (b, prev(i), v_row, 0)),
                  pl.BlockSpec(edge_t, lambda b, i: (b, nxt(i), v_row, 0)),
                  pl.BlockSpec((1, TOK_TILE, RET_WIDTH), lambda b, i: (b, i, 0)),
                  pl.BlockSpec((1, TOK_TILE, ATTN_KV_WIDTH), lambda b, i: (b, i, k_col)),
                  pl.BlockSpec(edge_k, lambda b, i: (b, prev(i), k_col)),
                  pl.BlockSpec(edge_k, lambda b, i: (b, nxt(i), k_col)),
                  pl.BlockSpec(st_blk, lambda b, i: (b, i, 0, 0, 0)),
                  pl.BlockSpec(st_blk, lambda b, i: (b, i, 0, 0, 0)),
                  _const_spec((RET_HEADS, BLOCK, LANES)),
                  _const_spec((2 * RET_HEADS, LANES)),
                  _const_spec((ATTN_WIDTH, LANES)),
                  _const_spec((RET_WIDTH, LANES)),
                  _const_spec((D_MODEL, D_MODEL)),
                  _const_spec((1, D_MODEL))],
        out_specs=pl.BlockSpec((1, TOK_TILE, D_MODEL), lambda b, i: (b, i, 0)),
        out_shape=jax.ShapeDtypeStruct((B, S, D_MODEL), F32),
        scratch_shapes=[pltpu.VMEM((ATTN_KV_HEADS, TOK_TILE + 2 * BLOCK, ATTN_KV_WIDTH), BF16),
                        pltpu.VMEM((cpt + 2, ATTN_KV_WIDTH, BLOCK), BF16),
                        pltpu.VMEM((cpt, D_MODEL, BLOCK), BF16),
                        pltpu.VMEM((2, ATTN_KV_HEADS, 3 * BLOCK, ATTN_GROUP * LANES), F32)],
        compiler_params=_params("parallel", "parallel"),
        name="token_mixer",
    )(sink, x, zt, zt, zt, zt, zt, zt, zt, zk, zk, zk, zk, states_f, states_b, dmat, qdec,
      attn_norm, ret_gn, w_out, norm_post)


def _memkv_kernel(m_ref, g_ref, w_ref, kv_ref):
    mn = _rms(m_ref[0], g_ref[...]).astype(BF16)
    kv_ref[0] = _dot(mn, w_ref[...]).astype(BF16)


def _memory_kv(mem, gain, w_kv):
    B, M, _ = mem.shape
    return pl.pallas_call(
        _memkv_kernel,
        grid=(B,),
        in_specs=[pl.BlockSpec((1, M, D_MODEL), lambda b: (b, 0, 0)),
                  _const_spec((1, D_MODEL)),
                  _const_spec((D_MODEL, 2 * D_MODEL))],
        out_specs=pl.BlockSpec((1, M, 2 * D_MODEL), lambda b: (b, 0, 0)),
        out_shape=jax.ShapeDtypeStruct((B, M, 2 * D_MODEL), BF16),
        compiler_params=_params("parallel"),
        name="memory_kv",
    )(mem, gain, w_kv)


def _xattn_kernel(x_ref, gpre_ref, wq_ref, km_ref, vm_ref, wo_ref, gpost_ref,
                  o_ref, cat_ref):
    x = x_ref[0]
    h = _rms(x, gpre_ref[...]).astype(BF16)
    q = (_dot(h, wq_ref[...]) * (XA_HEAD_DIM ** -0.5)).astype(BF16)
    for hd in range(XA_HEADS):
        sl = slice(hd * XA_HEAD_DIM, (hd + 1) * XA_HEAD_DIM)
        s = lax.dot_general(q[:, sl], km_ref[0, :, sl], NT_DIMS, preferred_element_type=F32)
        p = jnp.exp(s - jnp.max(s, axis=-1, keepdims=True))
        den = jnp.sum(p, axis=-1, keepdims=True)
        cat_ref[:, sl] = (_dot(p.astype(BF16), vm_ref[0, :, sl]) * (1.0 / den)).astype(BF16)
    xo = _dot(cat_ref[...], wo_ref[...])
    o_ref[0] = x + _rms(xo, gpost_ref[...])


def _cross_attention(x, kv, gain_pre, w_q, w_o, gain_post):
    B, S, _ = x.shape
    M = kv.shape[1]
    tok = pl.BlockSpec((1, TOK_TILE, D_MODEL), lambda b, i: (b, i, 0))
    return pl.pallas_call(
        _xattn_kernel,
        grid=(B, S // TOK_TILE),
        in_specs=[tok,
                  _const_spec((1, D_MODEL)),
                  _const_spec((D_MODEL, D_MODEL)),
                  pl.BlockSpec((1, M, D_MODEL), lambda b, i: (b, 0, 0)),
                  pl.BlockSpec((1, M, D_MODEL), lambda b, i: (b, 0, 1)),
                  _const_spec((D_MODEL, D_MODEL)),
                  _const_spec((1, D_MODEL))],
        out_specs=tok,
        out_shape=jax.ShapeDtypeStruct((B, S, D_MODEL), F32),
        scratch_shapes=[pltpu.VMEM((TOK_TILE, D_MODEL), BF16)],
        compiler_params=_params("parallel", "parallel"),
        name="cross_attention",
    )(x, gain_pre, w_q, kv, kv, w_o, gain_post)


FF_SPLIT = 2
FF_CHUNK = D_FF // FF_SPLIT


def _ffn_kernel(x_ref, gpre_ref, wgu_ref, wd_ref, gpost_ref, o_ref):
    x = x_ref[0]
    h = _rms(x, gpre_ref[...]).astype(BF16)
    f = None
    for j in range(FF_SPLIT):
        c0 = j * FF_CHUNK
        gate = _dot(h, wgu_ref[:, c0:c0 + FF_CHUNK])
        up = _dot(h, wgu_ref[:, D_FF + c0:D_FF + c0 + FF_CHUNK])
        a = (gate * jax.nn.sigmoid(gate) * up).astype(BF16)
        part = _dot(a, wd_ref[c0:c0 + FF_CHUNK, :])
        f = part if f is None else f + part
    o_ref[0] = x + _rms(f, gpost_ref[...])


def _ffn(x, gain_pre, w_gu, w_down, gain_post):
    B, S, _ = x.shape
    tok = pl.BlockSpec((1, TOK_TILE, D_MODEL), lambda b, i: (b, i, 0))
    return pl.pallas_call(
        _ffn_kernel,
        grid=(B, S // TOK_TILE),
        in_specs=[tok,
                  _const_spec((1, D_MODEL)),
                  _const_spec((D_MODEL, 2 * D_FF)),
                  _const_spec((D_FF, D_MODEL)),
                  _const_spec((1, D_MODEL))],
        out_specs=tok,
        out_shape=jax.ShapeDtypeStruct((B, S, D_MODEL), F32),
        compiler_params=_params("parallel", "parallel"),
        name="swiglu_ffn",
    )(x, gain_pre, w_gu, w_down, gain_post)


def _attn_feature_order():
    order = []
    for j in range(ATTN_GROUP):
        for g in range(ATTN_KV_HEADS):
            h = g * ATTN_GROUP + j
            order += list(range(h * ATTN_HEAD_DIM, (h + 1) * ATTN_HEAD_DIM))
    return np.asarray(order, np.int32)


def kernel(x, mem, norm_mix_pre, norm_mix_post, w_in, attn_sink, attn_out_norm,
           ret_decay_fwd, ret_decay_bwd, ret_gn, w_out, norm_xa_pre, norm_xa_post,
           norm_mem, xa_wq, xa_wkv, xa_wo, norm_ffn_pre, norm_ffn_post,
           ffn_w_gu, ffn_w_down):
    B, S, D = x.shape
    assert D == D_MODEL and S % (TOK_TILE * STATE_CHUNKS // CHUNKS_PER_TILE) == 0
    assert mem.shape[1] == N_MEM
    depth = w_in.shape[0]
    feat = _attn_feature_order()
    o_k = ATTN_WIDTH
    o_v = o_k + ATTN_KV_WIDTH
    o_r = o_v + ATTN_KV_WIDTH
    rw = RET_WIDTH

    ang_a = _rope_angles(S, ATTN_HEAD_DIM)
    ang_r = _rope_angles(S, RET_HEAD_DIM)
    cos_a, sin_a, cos_r, sin_r = jnp.cos(ang_a), jnp.sin(ang_a), jnp.cos(ang_r), jnp.sin(ang_r)
    tabs_tok = (jnp.concatenate([cos_r, cos_r], axis=1), jnp.concatenate([-sin_r, sin_r], axis=1),
                jnp.concatenate([cos_a] * 4, axis=1),
                jnp.concatenate([-sin_a, sin_a, -sin_a, sin_a], axis=1))
    tabs_feat = (cos_a.T, sin_a.T, cos_r.T, sin_r.T)

    def row(v):
        return v.reshape(1, -1).astype(F32)

    def col(v):
        return jnp.broadcast_to(v.astype(F32)[:, None], (v.shape[0], LANES))

    for l in range(depth):
        w = w_in[l]
        w_t = jnp.concatenate([w[:, :o_k][:, feat], w[:, o_r:o_r + rw], w[:, o_r + 2 * rw:],
                               w[:, o_v:o_r]], axis=1).T.astype(BF16)
        w_k = jnp.concatenate([w[:, o_r + rw:o_r + 2 * rw], w[:, o_k:o_v]], axis=1).astype(BF16)
        out_rows = np.concatenate([feat, np.arange(ATTN_WIDTH, ATTN_WIDTH + RET_WIDTH)])

        dmat, kdec, qdec, cdec = _decay_tables(ret_decay_fwd[l], ret_decay_bwd[l])
        zt, zk = _in_projection(x, row(norm_mix_pre[l]), w_t, w_k, tabs_tok, tabs_feat)
        states_f, states_b = _retention_states(zt, zk, kdec, cdec)
        x = _mixer(x, zt, zk, states_f, states_b, dmat, qdec, attn_sink[l].astype(F32),
                   col(attn_out_norm[l][feat]), col(ret_gn[l]),
                   w_out[l][out_rows].astype(BF16), row(norm_mix_post[l]))
        kv = _memory_kv(mem, row(norm_mem[l]), xa_wkv[l].astype(BF16))
        x = _cross_attention(x, kv, row(norm_xa_pre[l]), xa_wq[l].astype(BF16),
                             xa_wo[l].astype(BF16), row(norm_xa_post[l]))
        x = _ffn(x, row(norm_ffn_pre[l]), ffn_w_gu[l].astype(BF16),
                 ffn_w_down[l].astype(BF16), row(norm_ffn_post[l]))
    return x
```

```python
import math

import numpy as np
import jax
import jax.numpy as jnp
from jax import lax
from jax.experimental import pallas as pl
from jax.experimental.pallas import tpu as pltpu

D_MODEL = 1024
N_MEM = 256
ATTN_HEADS = 8
ATTN_KV_HEADS = 2
ATTN_GROUP = ATTN_HEADS // ATTN_KV_HEADS
ATTN_HEAD_DIM = 64
ATTN_WIDTH = ATTN_HEADS * ATTN_HEAD_DIM
ATTN_KV_WIDTH = ATTN_KV_HEADS * ATTN_HEAD_DIM
BLOCK = 128
RET_HEADS = 4
RET_HEAD_DIM = 128
RET_WIDTH = RET_HEADS * RET_HEAD_DIM
IN_COLS = ATTN_WIDTH + 2 * ATTN_KV_WIDTH + 4 * RET_WIDTH
XA_HEADS = 4
XA_HEAD_DIM = D_MODEL // XA_HEADS
D_FF = -(-(8 * D_MODEL) // (3 * 256)) * 256
ROPE_THETA = 10000.0
EPS = 1e-6
LOG2E = math.log2(math.e)

LANES = 128
VMEM_LIMIT_BYTES = 56 * 1024 * 1024

TOK_TILE = 512
CHUNKS_PER_TILE = TOK_TILE // BLOCK
STATE_CHUNKS = 4

T_AQ, T_RQ, T_RV, T_RG, T_AV = 0, 512, 1024, 1536, 2048
T_ROWS = T_AV + ATTN_KV_WIDTH
K_RK, K_AK = 0, RET_WIDTH
K_COLS = RET_WIDTH + ATTN_KV_WIDTH

F32 = jnp.float32
BF16 = jnp.bfloat16
NT_DIMS = (((1,), (1,)), ((), ()))
TN_DIMS = (((0,), (0,)), ((), ()))


def _const_spec(shape):
    nd = len(shape)
    return pl.BlockSpec(shape, lambda *_: (0,) * nd, pipeline_mode=pl.Buffered(1))


def _params(*sem, flags=None):
    return pltpu.CompilerParams(dimension_semantics=sem,
                                vmem_limit_bytes=VMEM_LIMIT_BYTES, flags=flags)


def _rms(x, w):
    ms = jnp.mean(x * x, axis=-1, keepdims=True)
    return x * lax.rsqrt(ms + EPS) * w


def _dot(a, b):
    return jnp.dot(a, b, preferred_element_type=F32)


def _rope_tables(n_tiles):
    def parts(dim):
        inv_freq = ROPE_THETA ** (-jnp.arange(0, dim, 2, dtype=F32) / dim)
        base = (jnp.arange(n_tiles, dtype=F32) * TOK_TILE)[:, None] * inv_freq[None, :]
        off = jnp.arange(TOK_TILE, dtype=F32)[:, None] * inv_freq[None, :]
        return jnp.cos(base), jnp.sin(base), jnp.cos(off), jnp.sin(off)

    def tok(c, s, reps):
        cos = jnp.concatenate([c, c] * reps, axis=1)
        sin = jnp.concatenate([s, s] * reps, axis=1)
        return cos, sin, jnp.concatenate([-s, s] * reps, axis=1)

    cb_a, sb_a, co_a, so_a = parts(ATTN_HEAD_DIM)
    cb_r, sb_r, co_r, so_r = parts(RET_HEAD_DIM)
    tok_base = jnp.stack(tok(cb_r, sb_r, 1) + tok(cb_a, sb_a, 2)
                         + (jnp.zeros((n_tiles, LANES), F32),) * 2, axis=1)
    tok_off = jnp.stack(tok(co_r, so_r, 1) + tok(co_a, so_a, 2), axis=0)
    feat_base = jnp.concatenate([cb_a, sb_a, cb_r, sb_r], axis=1)
    feat_base = jnp.broadcast_to(feat_base[:, :, None], feat_base.shape + (BLOCK,))
    feat_off = jnp.concatenate([co_a, so_a, co_r, so_r], axis=1).T
    return tok_base, tok_off, feat_base, feat_off


def _decay_kernel(logit_ref, dmat_ref, kdec_ref, qdec_ref, cdec_ref):
    x = logit_ref[...]
    lg = jnp.minimum(x, 0.0) - jnp.log1p(jnp.exp(-jnp.abs(x)))
    ri = lax.broadcasted_iota(jnp.int32, (BLOCK, LANES), 0).astype(F32)
    ci = lax.broadcasted_iota(jnp.int32, (BLOCK, LANES), 1).astype(F32)
    c1 = ci[0:1, :]
    diff = ci - ri
    cdec_ref[...] = jnp.exp(BLOCK * lg)
    for h in range(RET_HEADS):
        lf = lg[h:h + 1, :]
        lb = lg[RET_HEADS + h:RET_HEADS + h + 1, :]
        dmat_ref[h] = jnp.exp(jnp.abs(diff) * jnp.where(diff >= 0, lf, lb))
        sl = slice(h * LANES, (h + 1) * LANES)
        kdec_ref[0, :, sl] = jnp.exp((BLOCK - 1 - ri) * lf)
        kdec_ref[1, :, sl] = jnp.exp(ri * lb)
        qdec_ref[h:h + 1, :] = jnp.exp((c1 + 1) * lf)
        qdec_ref[RET_HEADS + h:RET_HEADS + h + 1, :] = jnp.exp((BLOCK - c1) * lb)


def _decay_tables(decay_fwd, decay_bwd):
    logits = jnp.concatenate([decay_fwd, decay_bwd]).astype(F32)
    logits = jnp.broadcast_to(logits[:, None], (2 * RET_HEADS, LANES))
    return pl.pallas_call(
        _decay_kernel,
        out_shape=(jax.ShapeDtypeStruct((RET_HEADS, BLOCK, LANES), F32),
                   jax.ShapeDtypeStruct((2, BLOCK, RET_WIDTH), F32),
                   jax.ShapeDtypeStruct((2 * RET_HEADS, LANES), F32),
                   jax.ShapeDtypeStruct((2 * RET_HEADS, LANES), F32)),
        name="decay_tables",
    )(logits)


def _inproj_kernel(x_ref, g_ref, wt_ref, wk_ref, tb_ref, to_ref, fb_ref, fo_ref,
                   zt_ref, zk_ref):
    h = _rms(x_ref[0], g_ref[...]).astype(BF16)

    tb = tb_ref[0]

    def tok_tables(k):
        cb, sb, ssb = tb[3 * k:3 * k + 1], tb[3 * k + 1:3 * k + 2], tb[3 * k + 2:3 * k + 3]
        co, so, sso = to_ref[3 * k], to_ref[3 * k + 1], to_ref[3 * k + 2]
        return cb * co - sb * so, ssb * co + cb * sso

    fb = jnp.concatenate([fb_ref[0]] * CHUNKS_PER_TILE, axis=1)
    fo = fo_ref[...]

    def feat_tables(row, n):
        cb, sb = fb[row:row + n], fb[row + n:row + 2 * n]
        co, so = fo[row:row + n], fo[row + n:row + 2 * n]
        return cb * co - sb * so, sb * co + cb * so

    zk = _dot(h, wk_ref[...])
    ck, sk = tok_tables(0)
    for g in range(RET_HEADS):
        zg = zk[:, g * LANES:(g + 1) * LANES]
        r = (zg * ck + pltpu.roll(zg, LANES // 2, 1) * sk) * (RET_HEAD_DIM ** -0.5)
        zk_ref[0, :, K_RK + g * LANES:K_RK + (g + 1) * LANES] = r.astype(BF16)
    zg = zk[:, K_AK:K_AK + LANES]
    lane = lax.broadcasted_iota(jnp.int32, (1, LANES), 1)
    low_half = (lane % ATTN_HEAD_DIM) < (ATTN_HEAD_DIM // 2)
    half = ATTN_HEAD_DIM // 2
    rot = jnp.where(low_half, pltpu.roll(zg, LANES - half, 1), pltpu.roll(zg, half, 1))
    ca, sa = tok_tables(1)
    zk_ref[0, :, K_AK:K_AK + LANES] = (zg * ca + rot * sa).astype(BF16)

    def proj_t(row, n_rows):
        return lax.dot_general(wt_ref[row:row + n_rows, :], h, NT_DIMS,
                               preferred_element_type=F32)

    def store_t(row, val):
        for c in range(CHUNKS_PER_TILE):
            zt_ref[0, c, row:row + val.shape[0], :] = val[:, c * BLOCK:(c + 1) * BLOCK].astype(BF16)

    def rope_t(z, row, head_dim, cos, sin):
        hd2 = head_dim // 2
        for k in range(z.shape[0] // head_dim):
            x1 = z[k * head_dim:k * head_dim + hd2]
            x2 = z[k * head_dim + hd2:(k + 1) * head_dim]
            store_t(row + k * head_dim, x1 * cos - x2 * sin)
            store_t(row + k * head_dim + hd2, x2 * cos + x1 * sin)

    q_scale = (ATTN_HEAD_DIM ** -0.5) * LOG2E
    cat, sat = feat_tables(0, ATTN_HEAD_DIM // 2)
    rope_t(proj_t(T_AQ, ATTN_WIDTH), T_AQ, ATTN_HEAD_DIM, cat * q_scale, sat * q_scale)
    crt, srt = feat_tables(ATTN_HEAD_DIM, RET_HEAD_DIM // 2)
    rope_t(proj_t(T_RQ, RET_WIDTH), T_RQ, RET_HEAD_DIM, crt, srt)
    store_t(T_RV, proj_t(T_RV, RET_WIDTH))
    store_t(T_RG, proj_t(T_RG, RET_WIDTH))
    store_t(T_AV, proj_t(T_AV, ATTN_KV_WIDTH))


def _in_projection(x, gain, w_t, w_k, rope):
    B, S, _ = x.shape
    n = S // BLOCK
    tok_base, tok_off, feat_base, feat_off = rope
    return pl.pallas_call(
        _inproj_kernel,
        grid=(S // TOK_TILE, B),
        in_specs=[pl.BlockSpec((1, TOK_TILE, D_MODEL), lambda i, b: (b, i, 0)),
                  _const_spec((1, D_MODEL)),
                  _const_spec((T_ROWS, D_MODEL)),
                  _const_spec((D_MODEL, K_COLS)),
                  pl.BlockSpec((1,) + tok_base.shape[1:], lambda i, b: (i, 0, 0)),
                  _const_spec(tok_off.shape),
                  pl.BlockSpec((1,) + feat_base.shape[1:], lambda i, b: (i, 0, 0)),
                  _const_spec(feat_off.shape)],
        out_specs=(pl.BlockSpec((1, CHUNKS_PER_TILE, T_ROWS, BLOCK), lambda i, b: (b, i, 0, 0)),
                   pl.BlockSpec((1, TOK_TILE, K_COLS), lambda i, b: (b, i, 0))),
        out_shape=(jax.ShapeDtypeStruct((B, n, T_ROWS, BLOCK), BF16),
                   jax.ShapeDtypeStruct((B, S, K_COLS), BF16)),
        compiler_params=_params("parallel", "parallel"),
        name="in_projection",
    )(x, gain, w_t, w_k, tok_base, tok_off, feat_base, feat_off)


def _state_kernel(kf_ref, vf_ref, kb_ref, vb_ref, kdec_ref, cdec_ref,
                  pf_ref, pb_ref, st_ref):
    @pl.when(pl.program_id(1) == 0)
    def _():
        st_ref[...] = jnp.zeros_like(st_ref)

    for i in range(STATE_CHUNKS):
        for d, (k_ref, v_ref, p_ref, c) in enumerate(((kf_ref, vf_ref, pf_ref, i),
                                                      (kb_ref, vb_ref, pb_ref, STATE_CHUNKS - 1 - i))):
            kd = (k_ref[0, c * BLOCK:(c + 1) * BLOCK, :].astype(F32) * kdec_ref[d]).astype(BF16)
            for h in range(RET_HEADS):
                sl = slice(h * LANES, (h + 1) * LANES)
                st = st_ref[d, h]
                p_ref[0, c, h] = st.astype(BF16)
                kv = _dot(v_ref[0, c, sl, :], kd[:, sl])
                row = d * RET_HEADS + h
                st_ref[d, h] = st * cdec_ref[row:row + 1, :] + kv


def _retention_states(zt, zk, kdec, cdec):
    B, n = zt.shape[0], zt.shape[1]
    ng = n // STATE_CHUNKS
    k_blk = (1, STATE_CHUNKS * BLOCK, RET_WIDTH)
    v_blk = (1, STATE_CHUNKS, RET_WIDTH, BLOCK)
    vrow = T_RV // RET_WIDTH
    st_blk = (1, STATE_CHUNKS, RET_HEADS, RET_HEAD_DIM, RET_HEAD_DIM)
    st_shape = jax.ShapeDtypeStruct((B, n, RET_HEADS, RET_HEAD_DIM, RET_HEAD_DIM), BF16)
    return pl.pallas_call(
        _state_kernel,
        grid=(B, ng),
        in_specs=[pl.BlockSpec(k_blk, lambda b, t: (b, t, 0)),
                  pl.BlockSpec(v_blk, lambda b, t: (b, t, vrow, 0)),
                  pl.BlockSpec(k_blk, lambda b, t: (b, ng - 1 - t, 0)),
                  pl.BlockSpec(v_blk, lambda b, t: (b, ng - 1 - t, vrow, 0)),
                  _const_spec((2, BLOCK, RET_WIDTH)),
                  _const_spec((2 * RET_HEADS, LANES))],
        out_specs=(pl.BlockSpec(st_blk, lambda b, t: (b, t, 0, 0, 0)),
                   pl.BlockSpec(st_blk, lambda b, t: (b, ng - 1 - t, 0, 0, 0))),
        out_shape=(st_shape, st_shape),
        scratch_shapes=[pltpu.VMEM((2, RET_HEADS, RET_HEAD_DIM, RET_HEAD_DIM), F32)],
        compiler_params=_params("parallel", "arbitrary"),
        name="retention_states",
    )(zk, zt, zk, zt, kdec, cdec)


def _mixer_kernel(sink_ref, x_ref, aq_ref, rq_ref, rv_ref, rg_ref, av_ref, avp_ref, avn_ref,
                  rk_ref, ak_ref, akp_ref, akn_ref, pf_ref, pb_ref, dmat_ref, qdec_ref,
                  an_ref, gn_ref, wout_ref, npost_ref, o_ref, kext_ref, vext_ref, mixt_ref,
                  sc_ref):
    tile = pl.program_id(1)
    n_blocks = pl.num_programs(1) * CHUNKS_PER_TILE

    lane = lax.broadcasted_iota(jnp.int32, (1, LANES), 1)
    for g in range(ATTN_KV_HEADS):
        km = jnp.where((lane // ATTN_HEAD_DIM) == g, 1.0, 0.0).astype(BF16)
        kext_ref[g, 0:BLOCK] = akp_ref[0] * km
        kext_ref[g, BLOCK:BLOCK + TOK_TILE] = ak_ref[0] * km
        kext_ref[g, BLOCK + TOK_TILE:] = akn_ref[0] * km
    vext_ref[0] = avp_ref[0, 0]
    for c in range(CHUNKS_PER_TILE):
        vext_ref[1 + c] = av_ref[0, c]
    vext_ref[CHUNKS_PER_TILE + 1] = avn_ref[0, 0]

    kk = lax.broadcasted_iota(jnp.int32, (BLOCK, LANES), 0)
    qq = lax.broadcasted_iota(jnp.int32, (BLOCK, LANES), 1)
    neg = jnp.full((BLOCK, LANES), -jnp.inf, F32)
    zero = jnp.zeros((BLOCK, LANES), F32)

    def scores_into(slot, c):
        q = jnp.concatenate([aq_ref[0, c, j * LANES:(j + 1) * LANES, :] for j in range(ATTN_GROUP)],
                            axis=1)
        r0 = c * BLOCK
        for g in range(ATTN_KV_HEADS):
            sc_ref[slot, g] = _dot(kext_ref[g, pl.ds(r0, 3 * BLOCK), :], q)

    def chunk(c, has_next):
        r0 = c * BLOCK
        blk = tile * CHUNKS_PER_TILE + c
        slot = c % 2
        if has_next:
            scores_into(1 - slot, c + 1)

        vw = jnp.concatenate([vext_ref[c], vext_ref[c + 1], vext_ref[c + 2]], axis=1)
        no_prev = jnp.where(blk > 0, 0, BLOCK)
        no_next = jnp.where(blk < n_blocks - 1, 0, BLOCK)
        bias_prev = jnp.where(kk >= qq + no_prev, zero, neg)
        bias_next = jnp.where(kk <= qq - no_next, zero, neg)
        bias = jnp.concatenate([jnp.concatenate([bias_prev] * ATTN_GROUP, axis=1),
                                jnp.zeros((BLOCK, ATTN_GROUP * LANES), F32),
                                jnp.concatenate([bias_next] * ATTN_GROUP, axis=1)], axis=0)
        heads = []
        for h in range(RET_HEADS):
            sl = slice(h * LANES, (h + 1) * LANES)
            qt = rq_ref[0, c, sl, :]
            heads.append((qt, _dot(rk_ref[0, pl.ds(r0, BLOCK), sl], qt)))

        outs = []
        for g in range(ATTN_KV_HEADS):
            s = sc_ref[slot, g] + bias
            sk = jnp.concatenate(
                [jnp.full((1, LANES), sink_ref[g * ATTN_GROUP + j] * LOG2E, F32)
                 for j in range(ATTN_GROUP)], axis=1)
            m = jnp.maximum(jnp.max(s, axis=0, keepdims=True), sk)
            p = jnp.exp2(s - m)
            den = jnp.sum(p, axis=0, keepdims=True) + jnp.exp2(sk - m)
            vg = vw[g * ATTN_HEAD_DIM:(g + 1) * ATTN_HEAD_DIM, :]
            outs.append(_dot(vg, p.astype(BF16)) * (1.0 / den))
        ys = []
        for h in range(RET_HEADS):
            sl = slice(h * LANES, (h + 1) * LANES)
            qt, pt = heads[h]
            y = _dot(rv_ref[0, c, sl, :], (pt * dmat_ref[h]).astype(BF16))
            qf = qt.astype(F32)
            qd = jnp.concatenate(
                [(qf * qdec_ref[h:h + 1, :]).astype(BF16),
                 (qf * qdec_ref[RET_HEADS + h:RET_HEADS + h + 1, :]).astype(BF16)], axis=0)
            st = jnp.concatenate([pf_ref[0, c, h], pb_ref[0, c, h]], axis=1)
            ys.append(y + _dot(st, qd))

        a = jnp.concatenate([outs[g][:, j * LANES:(j + 1) * LANES]
                             for j in range(ATTN_GROUP) for g in range(ATTN_KV_HEADS)], axis=0)
        ms = jnp.sum(a * a, axis=0, keepdims=True) * (1.0 / ATTN_WIDTH)
        mixt_ref[c, 0:ATTN_WIDTH, :] = (a * lax.rsqrt(ms + EPS) * an_ref[...]).astype(BF16)
        for h in range(RET_HEADS):
            sl = slice(h * LANES, (h + 1) * LANES)
            y = ys[h]
            mu = jnp.mean(y, axis=0, keepdims=True)
            yc = y - mu
            var = jnp.mean(yc * yc, axis=0, keepdims=True)
            yn = yc * lax.rsqrt(var + EPS) * gn_ref[sl, :]
            gate = rg_ref[0, c, sl, :].astype(F32)
            r = gate * jax.nn.sigmoid(gate) * yn
            mixt_ref[c, ATTN_WIDTH + h * LANES:ATTN_WIDTH + (h + 1) * LANES, :] = r.astype(BF16)

    scores_into(0, 0)
    for c in range(CHUNKS_PER_TILE):
        chunk(c, c + 1 < CHUNKS_PER_TILE)

    mixt = jnp.concatenate([mixt_ref[c] for c in range(CHUNKS_PER_TILE)], axis=1)
    mix = lax.dot_general(mixt, wout_ref[...], TN_DIMS, preferred_element_type=F32)
    o_ref[0] = x_ref[0] + _rms(mix, npost_ref[...])


def _mixer(x, zt, zk, states_f, states_b, dmat, qdec, sink, attn_norm, ret_gn, w_out, norm_post):
    B, S, _ = x.shape
    n = S // BLOCK
    cpt = CHUNKS_PER_TILE
    wide_t = (1, cpt, RET_WIDTH, BLOCK)
    narrow_t = (1, cpt, ATTN_KV_WIDTH, BLOCK)
    edge_t = (1, 1, ATTN_KV_WIDTH, BLOCK)
    edge_k = (1, BLOCK, ATTN_KV_WIDTH)
    v_row = T_AV // ATTN_KV_WIDTH
    k_col = K_AK // ATTN_KV_WIDTH
    st_blk = (1, cpt, RET_HEADS, RET_HEAD_DIM, RET_HEAD_DIM)

    def rows(r):
        return lambda b, i: (b, i, r, 0)

    def prev(i):
        return jnp.maximum(i * cpt - 1, 0)

    def nxt(i):
        return jnp.minimum((i + 1) * cpt, n - 1)

    return pl.pallas_call(
        _mixer_kernel,
        grid=(B, S // TOK_TILE),
        in_specs=[pl.BlockSpec(memory_space=pltpu.SMEM),
                  pl.BlockSpec((1, TOK_TILE, D_MODEL), lambda b, i: (b, i, 0)),
                  pl.BlockSpec(wide_t, rows(T_AQ // RET_WIDTH)),
                  pl.BlockSpec(wide_t, rows(T_RQ // RET_WIDTH)),
                  pl.BlockSpec(wide_t, rows(T_RV // RET_WIDTH)),
                  pl.BlockSpec(wide_t, rows(T_RG // RET_WIDTH)),
                  pl.BlockSpec(narrow_t, rows(v_row)),
                  pl.BlockSpec(edge_t, lambda b, i: (b, prev(i), v_row, 0)),
                  pl.BlockSpec(edge_t, lambda b, i: (b, nxt(i), v_row, 0)),
                  pl.BlockSpec((1, TOK_TILE, RET_WIDTH), lambda b, i: (b, i, 0)),
                  pl.BlockSpec((1, TOK_TILE, ATTN_KV_WIDTH), lambda b, i: (b, i, k_col)),
                  pl.BlockSpec(edge_k, lambda b, i: (b, prev(i), k_col)),
                  pl.BlockSpec(edge_k, lambda b, i: (b, nxt(i), k_col)),
                  pl.BlockSpec(st_blk, lambda b, i: (b, i, 0, 0, 0)),
                  pl.BlockSpec(st_blk, lambda b, i: (b, i, 0, 0, 0)),
                  _const_spec((RET_HEADS, BLOCK, LANES)),
                  _const_spec((2 * RET_HEADS, LANES)),
                  _const_spec((ATTN_WIDTH, LANES)),
                  _const_spec((RET_WIDTH, LANES)),
                  _const_spec((D_MODEL, D_MODEL)),
                  _const_spec((1, D_MODEL))],
        out_specs=pl.BlockSpec((1, TOK_TILE, D_MODEL), lambda b, i: (b, i, 0)),
        out_shape=jax.ShapeDtypeStruct((B, S, D_MODEL), F32),
        scratch_shapes=[pltpu.VMEM((ATTN_KV_HEADS, TOK_TILE + 2 * BLOCK, ATTN_KV_WIDTH), BF16),
                        pltpu.VMEM((cpt + 2, ATTN_KV_WIDTH, BLOCK), BF16),
                        pltpu.VMEM((cpt, D_MODEL, BLOCK), BF16),
                        pltpu.VMEM((2, ATTN_KV_HEADS, 3 * BLOCK, ATTN_GROUP * LANES), F32)],
        compiler_params=_params("parallel", "parallel"),
        name="token_mixer",
    )(sink, x, zt, zt, zt, zt, zt, zt, zt, zk, zk, zk, zk, states_f, states_b, dmat, qdec,
      attn_norm, ret_gn, w_out, norm_post)


def _memkv_kernel(m_ref, g_ref, w_ref, kv_ref):
    mn = _rms(m_ref[0], g_ref[...]).astype(BF16)
    kv_ref[0] = _dot(mn, w_ref[...]).astype(BF16)


def _memory_kv(mem, gain, w_kv):
    B, M, _ = mem.shape
    return pl.pallas_call(
        _memkv_kernel,
        grid=(B,),
        in_specs=[pl.BlockSpec((1, M, D_MODEL), lambda b: (b, 0, 0)),
                  _const_spec((1, D_MODEL)),
                  _const_spec((D_MODEL, 2 * D_MODEL))],
        out_specs=pl.BlockSpec((1, M, 2 * D_MODEL), lambda b: (b, 0, 0)),
        out_shape=jax.ShapeDtypeStruct((B, M, 2 * D_MODEL), BF16),
        compiler_params=_params("parallel"),
        name="memory_kv",
    )(mem, gain, w_kv)


def _xattn_kernel(x_ref, gpre_ref, wq_ref, km_ref, vm_ref, wo_ref, gpost_ref,
                  o_ref, cat_ref):
    x = x_ref[0]
    h = _rms(x, gpre_ref[...]).astype(BF16)
    q = (_dot(h, wq_ref[...]) * (XA_HEAD_DIM ** -0.5)).astype(BF16)
    for hd in range(XA_HEADS):
        sl = slice(hd * XA_HEAD_DIM, (hd + 1) * XA_HEAD_DIM)
        s = lax.dot_general(q[:, sl], km_ref[0, :, sl], NT_DIMS, preferred_element_type=F32)
        p = jnp.exp(s - jnp.max(s, axis=-1, keepdims=True))
        den = jnp.sum(p, axis=-1, keepdims=True)
        cat_ref[:, sl] = (_dot(p.astype(BF16), vm_ref[0, :, sl]) * (1.0 / den)).astype(BF16)
    xo = _dot(cat_ref[...], wo_ref[...])
    o_ref[0] = x + _rms(xo, gpost_ref[...])


def _cross_attention(x, kv, gain_pre, w_q, w_o, gain_post):
    B, S, _ = x.shape
    M = kv.shape[1]
    tok = pl.BlockSpec((1, TOK_TILE, D_MODEL), lambda b, i: (b, i, 0))
    return pl.pallas_call(
        _xattn_kernel,
        grid=(B, S // TOK_TILE),
        in_specs=[tok,
                  _const_spec((1, D_MODEL)),
                  _const_spec((D_MODEL, D_MODEL)),
                  pl.BlockSpec((1, M, D_MODEL), lambda b, i: (b, 0, 0)),
                  pl.BlockSpec((1, M, D_MODEL), lambda b, i: (b, 0, 1)),
                  _const_spec((D_MODEL, D_MODEL)),
                  _const_spec((1, D_MODEL))],
        out_specs=tok,
        out_shape=jax.ShapeDtypeStruct((B, S, D_MODEL), F32),
        scratch_shapes=[pltpu.VMEM((TOK_TILE, D_MODEL), BF16)],
        compiler_params=_params("parallel", "parallel"),
        name="cross_attention",
    )(x, gain_pre, w_q, kv, kv, w_o, gain_post)


FF_SPLIT = 2
FF_CHUNK = D_FF // FF_SPLIT


def _ffn_kernel(x_ref, gpre_ref, wgu_ref, wd_ref, gpost_ref, o_ref):
    x = x_ref[0]
    h = _rms(x, gpre_ref[...]).astype(BF16)
    f = None
    for j in range(FF_SPLIT):
        c0 = j * FF_CHUNK
        gate = _dot(h, wgu_ref[:, c0:c0 + FF_CHUNK])
        up = _dot(h, wgu_ref[:, D_FF + c0:D_FF + c0 + FF_CHUNK])
        a = (gate * jax.nn.sigmoid(gate) * up).astype(BF16)
        part = _dot(a, wd_ref[c0:c0 + FF_CHUNK, :])
        f = part if f is None else f + part
    o_ref[0] = x + _rms(f, gpost_ref[...])


def _ffn(x, gain_pre, w_gu, w_down, gain_post):
    B, S, _ = x.shape
    tok = pl.BlockSpec((1, TOK_TILE, D_MODEL), lambda b, i: (b, i, 0))
    return pl.pallas_call(
        _ffn_kernel,
        grid=(B, S // TOK_TILE),
        in_specs=[tok,
                  _const_spec((1, D_MODEL)),
                  _const_spec((D_MODEL, 2 * D_FF)),
                  _const_spec((D_FF, D_MODEL)),
                  _const_spec((1, D_MODEL))],
        out_specs=tok,
        out_shape=jax.ShapeDtypeStruct((B, S, D_MODEL), F32),
        compiler_params=_params("parallel", "parallel"),
        name="swiglu_ffn",
    )(x, gain_pre, w_gu, w_down, gain_post)


def _attn_feature_order():
    order = []
    for j in range(ATTN_GROUP):
        for g in range(ATTN_KV_HEADS):
            h = g * ATTN_GROUP + j
            order += list(range(h * ATTN_HEAD_DIM, (h + 1) * ATTN_HEAD_DIM))
    return np.asarray(order, np.int32)


def kernel(x, mem, norm_mix_pre, norm_mix_post, w_in, attn_sink, attn_out_norm,
           ret_decay_fwd, ret_decay_bwd, ret_gn, w_out, norm_xa_pre, norm_xa_post,
           norm_mem, xa_wq, xa_wkv, xa_wo, norm_ffn_pre, norm_ffn_post,
           ffn_w_gu, ffn_w_down):
    B, S, D = x.shape
    assert D == D_MODEL and S % (TOK_TILE * STATE_CHUNKS // CHUNKS_PER_TILE) == 0
    assert mem.shape[1] == N_MEM
    depth = w_in.shape[0]
    feat = _attn_feature_order()
    o_k = ATTN_WIDTH
    o_v = o_k + ATTN_KV_WIDTH
    o_r = o_v + ATTN_KV_WIDTH
    rw = RET_WIDTH

    rope = _rope_tables(S // TOK_TILE)

    def row(v):
        return v.reshape(1, -1).astype(F32)

    def col(v):
        return jnp.broadcast_to(v.astype(F32)[:, None], (v.shape[0], LANES))

    for l in range(depth):
        w = w_in[l]
        w_t = jnp.concatenate([w[:, :o_k][:, feat], w[:, o_r:o_r + rw], w[:, o_r + 2 * rw:],
                               w[:, o_v:o_r]], axis=1).T.astype(BF16)
        w_k = jnp.concatenate([w[:, o_r + rw:o_r + 2 * rw], w[:, o_k:o_v]], axis=1).astype(BF16)
        out_rows = np.concatenate([feat, np.arange(ATTN_WIDTH, ATTN_WIDTH + RET_WIDTH)])

        dmat, kdec, qdec, cdec = _decay_tables(ret_decay_fwd[l], ret_decay_bwd[l])
        zt, zk = _in_projection(x, row(norm_mix_pre[l]), w_t, w_k, rope)
        states_f, states_b = _retention_states(zt, zk, kdec, cdec)
        x = _mixer(x, zt, zk, states_f, states_b, dmat, qdec, attn_sink[l].astype(F32),
                   col(attn_out_norm[l][feat]), col(ret_gn[l]),
                   w_out[l][out_rows].astype(BF16), row(norm_mix_post[l]))
        kv = _memory_kv(mem, row(norm_mem[l]), xa_wkv[l].astype(BF16))
        x = _cross_attention(x, kv, row(norm_xa_pre[l]), xa_wq[l].astype(BF16),
                             xa_wo[l].astype(BF16), row(norm_xa_post[l]))
        x = _ffn(x, row(norm_ffn_pre[l]), ffn_w_gu[l].astype(BF16),
                 ffn_w_down[l].astype(BF16), row(norm_ffn_post[l]))
    return x
```

```python
import math

import numpy as np
import jax
import jax.numpy as jnp
from jax import lax
from jax.experimental import pallas as pl
from jax.experimental.pallas import tpu as pltpu

D_MODEL = 1024
N_MEM = 256
ATTN_HEADS = 8
ATTN_KV_HEADS = 2
ATTN_GROUP = ATTN_HEADS // ATTN_KV_HEADS
ATTN_HEAD_DIM = 64
ATTN_WIDTH = ATTN_HEADS * ATTN_HEAD_DIM
ATTN_KV_WIDTH = ATTN_KV_HEADS * ATTN_HEAD_DIM
BLOCK = 128
RET_HEADS = 4
RET_HEAD_DIM = 128
RET_WIDTH = RET_HEADS * RET_HEAD_DIM
IN_COLS = ATTN_WIDTH + 2 * ATTN_KV_WIDTH + 4 * RET_WIDTH
XA_HEADS = 4
XA_HEAD_DIM = D_MODEL // XA_HEADS
D_FF = -(-(8 * D_MODEL) // (3 * 256)) * 256
ROPE_THETA = 10000.0
EPS = 1e-6
LOG2E = math.log2(math.e)

LANES = 128
MXU_TILE = 256
VMEM_LIMIT_BYTES = 56 * 1024 * 1024

TOK_TILE = 512
CHUNKS_PER_TILE = TOK_TILE // BLOCK
STATE_CHUNKS = 4

T_AQ, T_RQ, T_RV, T_RG, T_AV = 0, 512, 1024, 1536, 2048
T_ROWS = T_AV + ATTN_KV_WIDTH
K_RK, K_AK = 0, RET_WIDTH
K_COLS = RET_WIDTH + ATTN_KV_WIDTH

F32 = jnp.float32
BF16 = jnp.bfloat16
NT_DIMS = (((1,), (1,)), ((), ()))
TN_DIMS = (((0,), (0,)), ((), ()))


def _const_spec(shape):
    nd = len(shape)
    return pl.BlockSpec(shape, lambda *_: (0,) * nd, pipeline_mode=pl.Buffered(1))


def _params(*sem, flags=None):
    return pltpu.CompilerParams(dimension_semantics=sem,
                                vmem_limit_bytes=VMEM_LIMIT_BYTES, flags=flags)


def _rms(x, w):
    ms = jnp.mean(x * x, axis=-1, keepdims=True)
    return x * lax.rsqrt(ms + EPS) * w


def _dot(a, b):
    return jnp.dot(a, b, preferred_element_type=F32)


def _rope_tables(n_tiles):
    def parts(dim):
        inv_freq = ROPE_THETA ** (-jnp.arange(0, dim, 2, dtype=F32) / dim)
        base = (jnp.arange(n_tiles, dtype=F32) * TOK_TILE)[:, None] * inv_freq[None, :]
        off = jnp.arange(TOK_TILE, dtype=F32)[:, None] * inv_freq[None, :]
        return jnp.cos(base), jnp.sin(base), jnp.cos(off), jnp.sin(off)

    def tok(c, s, reps):
        cos = jnp.concatenate([c, c] * reps, axis=1)
        sin = jnp.concatenate([s, s] * reps, axis=1)
        return cos, sin, jnp.concatenate([-s, s] * reps, axis=1)

    cb_a, sb_a, co_a, so_a = parts(ATTN_HEAD_DIM)
    cb_r, sb_r, co_r, so_r = parts(RET_HEAD_DIM)
    tok_base = jnp.stack(tok(cb_r, sb_r, 1) + tok(cb_a, sb_a, 2)
                         + (jnp.zeros((n_tiles, LANES), F32),) * 2, axis=1)
    tok_off = jnp.stack(tok(co_r, so_r, 1) + tok(co_a, so_a, 2), axis=0)
    feat_base = jnp.concatenate([cb_a, sb_a, cb_r, sb_r], axis=1)
    feat_base = jnp.broadcast_to(feat_base[:, :, None], feat_base.shape + (BLOCK,))
    feat_off = jnp.concatenate([co_a, so_a, co_r, so_r], axis=1).T
    return tok_base, tok_off, feat_base, feat_off


def _decay_kernel(logit_ref, dmat_ref, kdec_ref, qdec_ref, cdec_ref):
    x = logit_ref[...]
    lg = jnp.minimum(x, 0.0) - jnp.log1p(jnp.exp(-jnp.abs(x)))
    ri = lax.broadcasted_iota(jnp.int32, (BLOCK, LANES), 0).astype(F32)
    ci = lax.broadcasted_iota(jnp.int32, (BLOCK, LANES), 1).astype(F32)
    c1 = ci[0:1, :]
    diff = ci - ri
    cdec_ref[...] = jnp.exp(BLOCK * lg)
    for h in range(RET_HEADS):
        lf = lg[h:h + 1, :]
        lb = lg[RET_HEADS + h:RET_HEADS + h + 1, :]
        dmat_ref[h] = jnp.exp(jnp.abs(diff) * jnp.where(diff >= 0, lf, lb))
        sl = slice(h * LANES, (h + 1) * LANES)
        kdec_ref[0, :, sl] = jnp.exp((BLOCK - 1 - ri) * lf)
        kdec_ref[1, :, sl] = jnp.exp(ri * lb)
        qdec_ref[h:h + 1, :] = jnp.exp((c1 + 1) * lf)
        qdec_ref[RET_HEADS + h:RET_HEADS + h + 1, :] = jnp.exp((BLOCK - c1) * lb)


def _decay_tables(decay_fwd, decay_bwd):
    logits = jnp.concatenate([decay_fwd, decay_bwd]).astype(F32)
    logits = jnp.broadcast_to(logits[:, None], (2 * RET_HEADS, LANES))
    return pl.pallas_call(
        _decay_kernel,
        out_shape=(jax.ShapeDtypeStruct((RET_HEADS, BLOCK, LANES), F32),
                   jax.ShapeDtypeStruct((2, BLOCK, RET_WIDTH), F32),
                   jax.ShapeDtypeStruct((2 * RET_HEADS, LANES), F32),
                   jax.ShapeDtypeStruct((2 * RET_HEADS, LANES), F32)),
        name="decay_tables",
    )(logits)


def _inproj_kernel(x_ref, g_ref, wt_ref, wk_ref, tb_ref, to_ref, fb_ref, fo_ref,
                   zt_ref, zk_ref):
    h = _rms(x_ref[0], g_ref[...]).astype(BF16)

    tb = tb_ref[0]

    def tok_tables(k):
        cb, sb, ssb = tb[3 * k:3 * k + 1], tb[3 * k + 1:3 * k + 2], tb[3 * k + 2:3 * k + 3]
        co, so, sso = to_ref[3 * k], to_ref[3 * k + 1], to_ref[3 * k + 2]
        return cb * co - sb * so, ssb * co + cb * sso

    fb = jnp.concatenate([fb_ref[0]] * CHUNKS_PER_TILE, axis=1)
    fo = fo_ref[...]

    def feat_tables(row, n):
        cb, sb = fb[row:row + n], fb[row + n:row + 2 * n]
        co, so = fo[row:row + n], fo[row + n:row + 2 * n]
        return cb * co - sb * so, sb * co + cb * so

    zk = _dot(h, wk_ref[...])
    ck, sk = tok_tables(0)
    for g in range(RET_HEADS):
        zg = zk[:, g * LANES:(g + 1) * LANES]
        r = (zg * ck + pltpu.roll(zg, LANES // 2, 1) * sk) * (RET_HEAD_DIM ** -0.5)
        zk_ref[0, :, K_RK + g * LANES:K_RK + (g + 1) * LANES] = r.astype(BF16)
    zg = zk[:, K_AK:K_AK + LANES]
    lane = lax.broadcasted_iota(jnp.int32, (1, LANES), 1)
    low_half = (lane % ATTN_HEAD_DIM) < (ATTN_HEAD_DIM // 2)
    half = ATTN_HEAD_DIM // 2
    rot = jnp.where(low_half, pltpu.roll(zg, LANES - half, 1), pltpu.roll(zg, half, 1))
    ca, sa = tok_tables(1)
    zk_ref[0, :, K_AK:K_AK + LANES] = (zg * ca + rot * sa).astype(BF16)

    def proj_t(row, n_rows):
        return lax.dot_general(wt_ref[row:row + n_rows, :], h, NT_DIMS,
                               preferred_element_type=F32)

    def store_t(row, val):
        for c in range(CHUNKS_PER_TILE):
            zt_ref[0, c, row:row + val.shape[0], :] = val[:, c * BLOCK:(c + 1) * BLOCK].astype(BF16)

    def rope_t(z, row, head_dim, cos, sin):
        hd2 = head_dim // 2
        for k in range(z.shape[0] // head_dim):
            x1 = z[k * head_dim:k * head_dim + hd2]
            x2 = z[k * head_dim + hd2:(k + 1) * head_dim]
            store_t(row + k * head_dim, x1 * cos - x2 * sin)
            store_t(row + k * head_dim + hd2, x2 * cos + x1 * sin)

    q_scale = (ATTN_HEAD_DIM ** -0.5) * LOG2E
    cat, sat = feat_tables(0, ATTN_HEAD_DIM // 2)
    rope_t(proj_t(T_AQ, ATTN_WIDTH), T_AQ, ATTN_HEAD_DIM, cat * q_scale, sat * q_scale)
    crt, srt = feat_tables(ATTN_HEAD_DIM, RET_HEAD_DIM // 2)
    rope_t(proj_t(T_RQ, RET_WIDTH), T_RQ, RET_HEAD_DIM, crt, srt)
    store_t(T_RV, proj_t(T_RV, RET_WIDTH))
    store_t(T_RG, proj_t(T_RG, RET_WIDTH))
    store_t(T_AV, proj_t(T_AV, ATTN_KV_WIDTH))


def _in_projection(x, gain, w_t, w_k, rope):
    B, S, _ = x.shape
    n = S // BLOCK
    tok_base, tok_off, feat_base, feat_off = rope
    return pl.pallas_call(
        _inproj_kernel,
        grid=(S // TOK_TILE, B),
        in_specs=[pl.BlockSpec((1, TOK_TILE, D_MODEL), lambda i, b: (b, i, 0)),
                  _const_spec((1, D_MODEL)),
                  _const_spec((T_ROWS, D_MODEL)),
                  _const_spec((D_MODEL, K_COLS)),
                  pl.BlockSpec((1,) + tok_base.shape[1:], lambda i, b: (i, 0, 0)),
                  _const_spec(tok_off.shape),
                  pl.BlockSpec((1,) + feat_base.shape[1:], lambda i, b: (i, 0, 0)),
                  _const_spec(feat_off.shape)],
        out_specs=(pl.BlockSpec((1, CHUNKS_PER_TILE, T_ROWS, BLOCK), lambda i, b: (b, i, 0, 0)),
                   pl.BlockSpec((1, TOK_TILE, K_COLS), lambda i, b: (b, i, 0))),
        out_shape=(jax.ShapeDtypeStruct((B, n, T_ROWS, BLOCK), BF16),
                   jax.ShapeDtypeStruct((B, S, K_COLS), BF16)),
        compiler_params=_params("parallel", "parallel"),
        name="in_projection",
    )(x, gain, w_t, w_k, tok_base, tok_off, feat_base, feat_off)


def _state_kernel(kf_ref, vf_ref, kb_ref, vb_ref, kdec_ref, cdec_ref,
                  pf_ref, pb_ref, st_ref):
    @pl.when(pl.program_id(1) == 0)
    def _():
        st_ref[...] = jnp.zeros_like(st_ref)

    for i in range(STATE_CHUNKS):
        for d, (k_ref, v_ref, p_ref, c) in enumerate(((kf_ref, vf_ref, pf_ref, i),
                                                      (kb_ref, vb_ref, pb_ref, STATE_CHUNKS - 1 - i))):
            kd = (k_ref[0, c * BLOCK:(c + 1) * BLOCK, :].astype(F32) * kdec_ref[d]).astype(BF16)
            for h in range(RET_HEADS):
                sl = slice(h * LANES, (h + 1) * LANES)
                st = st_ref[d, h]
                p_ref[0, c, h] = st.astype(BF16)
                kv = _dot(v_ref[0, c, sl, :], kd[:, sl])
                row = d * RET_HEADS + h
                st_ref[d, h] = st * cdec_ref[row:row + 1, :] + kv


def _retention_states(zt, zk, kdec, cdec):
    B, n = zt.shape[0], zt.shape[1]
    ng = n // STATE_CHUNKS
    k_blk = (1, STATE_CHUNKS * BLOCK, RET_WIDTH)
    v_blk = (1, STATE_CHUNKS, RET_WIDTH, BLOCK)
    vrow = T_RV // RET_WIDTH
    st_blk = (1, STATE_CHUNKS, RET_HEADS, RET_HEAD_DIM, RET_HEAD_DIM)
    st_shape = jax.ShapeDtypeStruct((B, n, RET_HEADS, RET_HEAD_DIM, RET_HEAD_DIM), BF16)
    return pl.pallas_call(
        _state_kernel,
        grid=(B, ng),
        in_specs=[pl.BlockSpec(k_blk, lambda b, t: (b, t, 0)),
                  pl.BlockSpec(v_blk, lambda b, t: (b, t, vrow, 0)),
                  pl.BlockSpec(k_blk, lambda b, t: (b, ng - 1 - t, 0)),
                  pl.BlockSpec(v_blk, lambda b, t: (b, ng - 1 - t, vrow, 0)),
                  _const_spec((2, BLOCK, RET_WIDTH)),
                  _const_spec((2 * RET_HEADS, LANES))],
        out_specs=(pl.BlockSpec(st_blk, lambda b, t: (b, t, 0, 0, 0)),
                   pl.BlockSpec(st_blk, lambda b, t: (b, ng - 1 - t, 0, 0, 0))),
        out_shape=(st_shape, st_shape),
        scratch_shapes=[pltpu.VMEM((2, RET_HEADS, RET_HEAD_DIM, RET_HEAD_DIM), F32)],
        compiler_params=_params("parallel", "arbitrary"),
        name="retention_states",
    )(zk, zt, zk, zt, kdec, cdec)


def _mixer_kernel(sink_ref, x_ref, aq_ref, rq_ref, rv_ref, rg_ref, av_ref, avp_ref, avn_ref,
                  rk_ref, ak_ref, akp_ref, akn_ref, pf_ref, pb_ref, dmat_ref, qdec_ref,
                  an_ref, gn_ref, wout_ref, npost_ref, o_ref, kext_ref, vext_ref, mixt_ref,
                  sc_ref):
    tile = pl.program_id(1)
    n_blocks = pl.num_programs(1) * CHUNKS_PER_TILE

    lane = lax.broadcasted_iota(jnp.int32, (1, LANES), 1)
    for g in range(ATTN_KV_HEADS):
        km = jnp.where((lane // ATTN_HEAD_DIM) == g, 1.0, 0.0).astype(BF16)
        kext_ref[g, 0:BLOCK] = akp_ref[0] * km
        kext_ref[g, BLOCK:BLOCK + TOK_TILE] = ak_ref[0] * km
        kext_ref[g, BLOCK + TOK_TILE:] = akn_ref[0] * km
    vext_ref[0] = avp_ref[0, 0]
    for c in range(CHUNKS_PER_TILE):
        vext_ref[1 + c] = av_ref[0, c]
    vext_ref[CHUNKS_PER_TILE + 1] = avn_ref[0, 0]

    kk = lax.broadcasted_iota(jnp.int32, (BLOCK, LANES), 0)
    qq = lax.broadcasted_iota(jnp.int32, (BLOCK, LANES), 1)
    neg = jnp.full((BLOCK, LANES), -jnp.inf, F32)
    zero = jnp.zeros((BLOCK, LANES), F32)

    def scores_into(slot, c):
        q = jnp.concatenate([aq_ref[0, c, j * LANES:(j + 1) * LANES, :] for j in range(ATTN_GROUP)],
                            axis=1)
        r0 = c * BLOCK
        for g in range(ATTN_KV_HEADS):
            sc_ref[slot, g] = _dot(kext_ref[g, pl.ds(r0, 3 * BLOCK), :], q)

    def chunk(c, has_next):
        r0 = c * BLOCK
        blk = tile * CHUNKS_PER_TILE + c
        slot = c % 2
        if has_next:
            scores_into(1 - slot, c + 1)

        vw = jnp.concatenate([vext_ref[c], vext_ref[c + 1], vext_ref[c + 2]], axis=1)
        no_prev = jnp.where(blk > 0, 0, BLOCK)
        no_next = jnp.where(blk < n_blocks - 1, 0, BLOCK)
        bias_prev = jnp.where(kk >= qq + no_prev, zero, neg)
        bias_next = jnp.where(kk <= qq - no_next, zero, neg)
        bias = jnp.concatenate([jnp.concatenate([bias_prev] * ATTN_GROUP, axis=1),
                                jnp.zeros((BLOCK, ATTN_GROUP * LANES), F32),
                                jnp.concatenate([bias_next] * ATTN_GROUP, axis=1)], axis=0)
        heads = []
        for h in range(RET_HEADS):
            sl = slice(h * LANES, (h + 1) * LANES)
            qt = rq_ref[0, c, sl, :]
            heads.append((qt, _dot(rk_ref[0, pl.ds(r0, BLOCK), sl], qt)))

        outs = []
        for g in range(ATTN_KV_HEADS):
            s = sc_ref[slot, g] + bias
            sk = jnp.concatenate(
                [jnp.full((1, LANES), sink_ref[g * ATTN_GROUP + j] * LOG2E, F32)
                 for j in range(ATTN_GROUP)], axis=1)
            m = jnp.maximum(jnp.max(s, axis=0, keepdims=True), sk)
            p = jnp.exp2(s - m)
            den = jnp.sum(p, axis=0, keepdims=True) + jnp.exp2(sk - m)
            vg = vw[g * ATTN_HEAD_DIM:(g + 1) * ATTN_HEAD_DIM, :]
            outs.append(_dot(vg, p.astype(BF16)) * (1.0 / den))
        ys = []
        for h in range(RET_HEADS):
            sl = slice(h * LANES, (h + 1) * LANES)
            qt, pt = heads[h]
            y = _dot(rv_ref[0, c, sl, :], (pt * dmat_ref[h]).astype(BF16))
            qf = qt.astype(F32)
            qd = jnp.concatenate(
                [(qf * qdec_ref[h:h + 1, :]).astype(BF16),
                 (qf * qdec_ref[RET_HEADS + h:RET_HEADS + h + 1, :]).astype(BF16)], axis=0)
            st = jnp.concatenate([pf_ref[0, c, h], pb_ref[0, c, h]], axis=1)
            ys.append(y + _dot(st, qd))

        a = jnp.concatenate([outs[g][:, j * LANES:(j + 1) * LANES]
                             for j in range(ATTN_GROUP) for g in range(ATTN_KV_HEADS)], axis=0)
        ms = jnp.sum(a * a, axis=0, keepdims=True) * (1.0 / ATTN_WIDTH)
        mixt_ref[c, 0:ATTN_WIDTH, :] = (a * lax.rsqrt(ms + EPS) * an_ref[...]).astype(BF16)
        for h in range(RET_HEADS):
            sl = slice(h * LANES, (h + 1) * LANES)
            y = ys[h]
            mu = jnp.mean(y, axis=0, keepdims=True)
            yc = y - mu
            var = jnp.mean(yc * yc, axis=0, keepdims=True)
            yn = yc * lax.rsqrt(var + EPS) * gn_ref[sl, :]
            gate = rg_ref[0, c, sl, :].astype(F32)
            r = gate * jax.nn.sigmoid(gate) * yn
            mixt_ref[c, ATTN_WIDTH + h * LANES:ATTN_WIDTH + (h + 1) * LANES, :] = r.astype(BF16)

    scores_into(0, 0)
    for c in range(CHUNKS_PER_TILE):
        chunk(c, c + 1 < CHUNKS_PER_TILE)

    mixt = jnp.concatenate([mixt_ref[c] for c in range(CHUNKS_PER_TILE)], axis=1)
    mix = lax.dot_general(mixt, wout_ref[...], TN_DIMS, preferred_element_type=F32)
    o_ref[0] = x_ref[0] + _rms(mix, npost_ref[...])


def _mixer(x, zt, zk, states_f, states_b, dmat, qdec, sink, attn_norm, ret_gn, w_out, norm_post):
    B, S, _ = x.shape
    n = S // BLOCK
    cpt = CHUNKS_PER_TILE
    wide_t = (1, cpt, RET_WIDTH, BLOCK)
    narrow_t = (1, cpt, ATTN_KV_WIDTH, BLOCK)
    edge_t = (1, 1, ATTN_KV_WIDTH, BLOCK)
    edge_k = (1, BLOCK, ATTN_KV_WIDTH)
    v_row = T_AV // ATTN_KV_WIDTH
    k_col = K_AK // ATTN_KV_WIDTH
    st_blk = (1, cpt, RET_HEADS, RET_HEAD_DIM, RET_HEAD_DIM)

    def rows(r):
        return lambda b, i: (b, i, r, 0)

    def prev(i):
        return jnp.maximum(i * cpt - 1, 0)

    def nxt(i):
        return jnp.minimum((i + 1) * cpt, n - 1)

    return pl.pallas_call(
        _mixer_kernel,
        grid=(B, S // TOK_TILE),
        in_specs=[pl.BlockSpec(memory_space=pltpu.SMEM),
                  pl.BlockSpec((1, TOK_TILE, D_MODEL), lambda b, i: (b, i, 0)),
                  pl.BlockSpec(wide_t, rows(T_AQ // RET_WIDTH)),
                  pl.BlockSpec(wide_t, rows(T_RQ // RET_WIDTH)),
                  pl.BlockSpec(wide_t, rows(T_RV // RET_WIDTH)),
                  pl.BlockSpec(wide_t, rows(T_RG // RET_WIDTH)),
                  pl.BlockSpec(narrow_t, rows(v_row)),
                  pl.BlockSpec(edge_t, lambda b, i: (b, prev(i), v_row, 0)),
                  pl.BlockSpec(edge_t, lambda b, i: (b, nxt(i), v_row, 0)),
                  pl.BlockSpec((1, TOK_TILE, RET_WIDTH), lambda b, i: (b, i, 0)),
                  pl.BlockSpec((1, TOK_TILE, ATTN_KV_WIDTH), lambda b, i: (b, i, k_col)),
                  pl.BlockSpec(edge_k, lambda b, i: (b, prev(i), k_col)),
                  pl.BlockSpec(edge_k, lambda b, i: (b, nxt(i), k_col)),
                  pl.BlockSpec(st_blk, lambda b, i: (b, i, 0, 0, 0)),
                  pl.BlockSpec(st_blk, lambda b, i: (b, i, 0, 0, 0)),
                  _const_spec((RET_HEADS, BLOCK, LANES)),
                  _const_spec((2 * RET_HEADS, LANES)),
                  _const_spec((ATTN_WIDTH, LANES)),
                  _const_spec((RET_WIDTH, LANES)),
                  _const_spec((D_MODEL, D_MODEL)),
                  _const_spec((1, D_MODEL))],
        out_specs=pl.BlockSpec((1, TOK_TILE, D_MODEL), lambda b, i: (b, i, 0)),
        out_shape=jax.ShapeDtypeStruct((B, S, D_MODEL), F32),
        scratch_shapes=[pltpu.VMEM((ATTN_KV_HEADS, TOK_TILE + 2 * BLOCK, ATTN_KV_WIDTH), BF16),
                        pltpu.VMEM((cpt + 2, ATTN_KV_WIDTH, BLOCK), BF16),
                        pltpu.VMEM((cpt, D_MODEL, BLOCK), BF16),
                        pltpu.VMEM((2, ATTN_KV_HEADS, 3 * BLOCK, ATTN_GROUP * LANES), F32)],
        compiler_params=_params("parallel", "parallel"),
        name="token_mixer",
    )(sink, x, zt, zt, zt, zt, zt, zt, zt, zk, zk, zk, zk, states_f, states_b, dmat, qdec,
      attn_norm, ret_gn, w_out, norm_post)


def _memkv_kernel(m_ref, g_ref, w_ref, kv_ref):
    mn = _rms(m_ref[0], g_ref[...]).astype(BF16)
    kv_ref[0] = _dot(mn, w_ref[...]).astype(BF16)


def _memory_kv(mem, gain, w_kv):
    B, M, _ = mem.shape
    return pl.pallas_call(
        _memkv_kernel,
        grid=(B,),
        in_specs=[pl.BlockSpec((1, M, D_MODEL), lambda b: (b, 0, 0)),
                  _const_spec((1, D_MODEL)),
                  _const_spec((D_MODEL, 2 * D_MODEL))],
        out_specs=pl.BlockSpec((1, M, 2 * D_MODEL), lambda b: (b, 0, 0)),
        out_shape=jax.ShapeDtypeStruct((B, M, 2 * D_MODEL), BF16),
        compiler_params=_params("parallel"),
        name="memory_kv",
    )(mem, gain, w_kv)


def _xattn_kernel(x_ref, gpre_ref, wq_ref, km_ref, vm_ref, wo_ref, gpost_ref,
                  o_ref, cat_ref):
    x = x_ref[0]
    h = _rms(x, gpre_ref[...]).astype(BF16)
    q = (_dot(h, wq_ref[...]) * (XA_HEAD_DIM ** -0.5)).astype(BF16)
    for hd in range(XA_HEADS):
        sl = slice(hd * XA_HEAD_DIM, (hd + 1) * XA_HEAD_DIM)
        s = lax.dot_general(q[:, sl], km_ref[0, :, sl], NT_DIMS, preferred_element_type=F32)
        p = jnp.exp(s - jnp.max(s, axis=-1, keepdims=True))
        den = jnp.sum(p, axis=-1, keepdims=True)
        cat_ref[:, sl] = (_dot(p.astype(BF16), vm_ref[0, :, sl]) * (1.0 / den)).astype(BF16)
    xo = _dot(cat_ref[...], wo_ref[...])
    o_ref[0] = x + _rms(xo, gpost_ref[...])


def _cross_attention(x, kv, gain_pre, w_q, w_o, gain_post):
    B, S, _ = x.shape
    M = kv.shape[1]
    tok = pl.BlockSpec((1, TOK_TILE, D_MODEL), lambda b, i: (b, i, 0))
    return pl.pallas_call(
        _xattn_kernel,
        grid=(B, S // TOK_TILE),
        in_specs=[tok,
                  _const_spec((1, D_MODEL)),
                  _const_spec((D_MODEL, D_MODEL)),
                  pl.BlockSpec((1, M, D_MODEL), lambda b, i: (b, 0, 0)),
                  pl.BlockSpec((1, M, D_MODEL), lambda b, i: (b, 0, 1)),
                  _const_spec((D_MODEL, D_MODEL)),
                  _const_spec((1, D_MODEL))],
        out_specs=tok,
        out_shape=jax.ShapeDtypeStruct((B, S, D_MODEL), F32),
        scratch_shapes=[pltpu.VMEM((TOK_TILE, D_MODEL), BF16)],
        compiler_params=_params("parallel", "parallel"),
        name="cross_attention",
    )(x, gain_pre, w_q, kv, kv, w_o, gain_post)


FF_CHUNK = 2 * MXU_TILE
FF_CHUNKS = [(c0, min(FF_CHUNK, D_FF - c0)) for c0 in range(0, D_FF, FF_CHUNK)]
assert all(w % MXU_TILE == 0 for _, w in FF_CHUNKS)


def _ffn_kernel(x_ref, gpre_ref, wgu_ref, wd_ref, gpost_ref, o_ref, gu_ref, act_ref):
    x = x_ref[0]
    h = _rms(x, gpre_ref[...]).astype(BF16)

    def gate_up_into(slot, j):
        c0, w = FF_CHUNKS[j]
        gu_ref[slot, 0, :, 0:w] = _dot(h, wgu_ref[:, c0:c0 + w])
        gu_ref[slot, 1, :, 0:w] = _dot(h, wgu_ref[:, D_FF + c0:D_FF + c0 + w])

    gate_up_into(0, 0)
    for j, (c0, w) in enumerate(FF_CHUNKS):
        if j + 1 < len(FF_CHUNKS):
            gate_up_into((j + 1) % 2, j + 1)
        gate = gu_ref[j % 2, 0, :, 0:w]
        up = gu_ref[j % 2, 1, :, 0:w]
        act_ref[:, c0:c0 + w] = (gate * jax.nn.sigmoid(gate) * up).astype(BF16)
    f = _dot(act_ref[...], wd_ref[...])
    o_ref[0] = x + _rms(f, gpost_ref[...])


def _ffn(x, gain_pre, w_gu, w_down, gain_post):
    B, S, _ = x.shape
    tok = pl.BlockSpec((1, TOK_TILE, D_MODEL), lambda b, i: (b, i, 0))
    return pl.pallas_call(
        _ffn_kernel,
        grid=(B, S // TOK_TILE),
        in_specs=[tok,
                  _const_spec((1, D_MODEL)),
                  _const_spec((D_MODEL, 2 * D_FF)),
                  _const_spec((D_FF, D_MODEL)),
                  _const_spec((1, D_MODEL))],
        out_specs=tok,
        out_shape=jax.ShapeDtypeStruct((B, S, D_MODEL), F32),
        scratch_shapes=[pltpu.VMEM((2, 2, TOK_TILE, FF_CHUNK), F32),
                        pltpu.VMEM((TOK_TILE, D_FF), BF16)],
        compiler_params=_params("parallel", "parallel"),
        name="swiglu_ffn",
    )(x, gain_pre, w_gu, w_down, gain_post)


def _attn_feature_order():
    order = []
    for j in range(ATTN_GROUP):
        for g in range(ATTN_KV_HEADS):
            h = g * ATTN_GROUP + j
            order += list(range(h * ATTN_HEAD_DIM, (h + 1) * ATTN_HEAD_DIM))
    return np.asarray(order, np.int32)


def kernel(x, mem, norm_mix_pre, norm_mix_post, w_in, attn_sink, attn_out_norm,
           ret_decay_fwd, ret_decay_bwd, ret_gn, w_out, norm_xa_pre, norm_xa_post,
           norm_mem, xa_wq, xa_wkv, xa_wo, norm_ffn_pre, norm_ffn_post,
           ffn_w_gu, ffn_w_down):
    B, S, D = x.shape
    assert D == D_MODEL and S % (TOK_TILE * STATE_CHUNKS // CHUNKS_PER_TILE) == 0
    assert mem.shape[1] == N_MEM
    depth = w_in.shape[0]
    feat = _attn_feature_order()
    o_k = ATTN_WIDTH
    o_v = o_k + ATTN_KV_WIDTH
    o_r = o_v + ATTN_KV_WIDTH
    rw = RET_WIDTH

    rope = _rope_tables(S // TOK_TILE)

    def row(v):
        return v.reshape(1, -1).astype(F32)

    def col(v):
        return jnp.broadcast_to(v.astype(F32)[:, None], (v.shape[0], LANES))

    for l in range(depth):
        w = w_in[l]
        w_t = jnp.concatenate([w[:, :o_k][:, feat], w[:, o_r:o_r + rw], w[:, o_r + 2 * rw:],
                               w[:, o_v:o_r]], axis=1).T.astype(BF16)
        w_k = jnp.concatenate([w[:, o_r + rw:o_r + 2 * rw], w[:, o_k:o_v]], axis=1).astype(BF16)
        out_rows = np.concatenate([feat, np.arange(ATTN_WIDTH, ATTN_WIDTH + RET_WIDTH)])

        dmat, kdec, qdec, cdec = _decay_tables(ret_decay_fwd[l], ret_decay_bwd[l])
        zt, zk = _in_projection(x, row(norm_mix_pre[l]), w_t, w_k, rope)
        states_f, states_b = _retention_states(zt, zk, kdec, cdec)
        x = _mixer(x, zt, zk, states_f, states_b, dmat, qdec, attn_sink[l].astype(F32),
                   col(attn_out_norm[l][feat]), col(ret_gn[l]),
                   w_out[l][out_rows].astype(BF16), row(norm_mix_post[l]))
        kv = _memory_kv(mem, row(norm_mem[l]), xa_wkv[l].astype(BF16))
        x = _cross_attention(x, kv, row(norm_xa_pre[l]), xa_wq[l].astype(BF16),
                             xa_wo[l].astype(BF16), row(norm_xa_post[l]))
        x = _ffn(x, row(norm_ffn_pre[l]), ffn_w_gu[l].astype(BF16),
                 ffn_w_down[l].astype(BF16), row(norm_ffn_post[l]))
    return x
```

```python
import math

import numpy as np
import jax
import jax.numpy as jnp
from jax import lax
from jax.experimental import pallas as pl
from jax.experimental.pallas import tpu as pltpu

D_MODEL = 1024
N_MEM = 256
ATTN_HEADS = 8
ATTN_KV_HEADS = 2
ATTN_GROUP = ATTN_HEADS // ATTN_KV_HEADS
ATTN_HEAD_DIM = 64
ATTN_WIDTH = ATTN_HEADS * ATTN_HEAD_DIM
ATTN_KV_WIDTH = ATTN_KV_HEADS * ATTN_HEAD_DIM
BLOCK = 128
RET_HEADS = 4
RET_HEAD_DIM = 128
RET_WIDTH = RET_HEADS * RET_HEAD_DIM
IN_COLS = ATTN_WIDTH + 2 * ATTN_KV_WIDTH + 4 * RET_WIDTH
XA_HEADS = 4
XA_HEAD_DIM = D_MODEL // XA_HEADS
D_FF = -(-(8 * D_MODEL) // (3 * 256)) * 256
ROPE_THETA = 10000.0
EPS = 1e-6
LOG2E = math.log2(math.e)

LANES = 128
SUBLANES = 8
MXU_TILE = 256
VMEM_LIMIT_BYTES = 56 * 1024 * 1024

TOK_TILE = 512
CHUNKS_PER_TILE = TOK_TILE // BLOCK
STATE_CHUNKS = 4

T_AQ, T_RQ, T_RV, T_RG, T_AV = 0, 512, 1024, 1536, 2048
T_ROWS = T_AV + ATTN_KV_WIDTH
K_RK, K_AK = 0, RET_WIDTH
K_COLS = RET_WIDTH + ATTN_KV_WIDTH

F32 = jnp.float32
BF16 = jnp.bfloat16
NT_DIMS = (((1,), (1,)), ((), ()))
TN_DIMS = (((0,), (0,)), ((), ()))


def _const_spec(shape):
    nd = len(shape)
    return pl.BlockSpec(shape, lambda *_: (0,) * nd, pipeline_mode=pl.Buffered(1))


def _params(*sem, flags=None):
    return pltpu.CompilerParams(dimension_semantics=sem,
                                vmem_limit_bytes=VMEM_LIMIT_BYTES, flags=flags)


def _rms(x, w):
    ms = jnp.mean(x * x, axis=-1, keepdims=True)
    return x * lax.rsqrt(ms + EPS) * w


def _dot(a, b):
    return jnp.dot(a, b, preferred_element_type=F32)


def _rope_tables(n_tiles):
    def parts(dim):
        inv_freq = ROPE_THETA ** (-jnp.arange(0, dim, 2, dtype=F32) / dim)
        base = (jnp.arange(n_tiles, dtype=F32) * TOK_TILE)[:, None] * inv_freq[None, :]
        off = jnp.arange(TOK_TILE, dtype=F32)[:, None] * inv_freq[None, :]
        return jnp.cos(base), jnp.sin(base), jnp.cos(off), jnp.sin(off)

    def tok(c, s, reps):
        cos = jnp.concatenate([c, c] * reps, axis=1)
        sin = jnp.concatenate([s, s] * reps, axis=1)
        return cos, sin, jnp.concatenate([-s, s] * reps, axis=1)

    cb_a, sb_a, co_a, so_a = parts(ATTN_HEAD_DIM)
    cb_r, sb_r, co_r, so_r = parts(RET_HEAD_DIM)
    tok_base = jnp.stack(tok(cb_r, sb_r, 1) + tok(cb_a, sb_a, 2)
                         + (jnp.zeros((n_tiles, LANES), F32),) * 2, axis=1)
    tok_off = jnp.stack(tok(co_r, so_r, 1) + tok(co_a, so_a, 2), axis=0)
    feat_base = jnp.concatenate([cb_a, sb_a, cb_r, sb_r], axis=1)
    feat_base = jnp.broadcast_to(feat_base[:, :, None], feat_base.shape + (BLOCK,))
    feat_off = jnp.concatenate([co_a, so_a, co_r, so_r], axis=1).T
    return tok_base, tok_off, feat_base, feat_off


def _decay_kernel(logit_ref, dmat_ref, kdec_ref, qdec_ref, cdec_ref):
    x = logit_ref[...]
    lg = jnp.minimum(x, 0.0) - jnp.log1p(jnp.exp(-jnp.abs(x)))
    ri = lax.broadcasted_iota(jnp.int32, (BLOCK, LANES), 0).astype(F32)
    ci = lax.broadcasted_iota(jnp.int32, (BLOCK, LANES), 1).astype(F32)
    c1 = ci[0:1, :]
    diff = ci - ri
    cdec_ref[...] = jnp.exp(BLOCK * lg)
    for h in range(RET_HEADS):
        lf = lg[h:h + 1, :]
        lb = lg[RET_HEADS + h:RET_HEADS + h + 1, :]
        dmat_ref[h] = jnp.exp(jnp.abs(diff) * jnp.where(diff >= 0, lf, lb))
        sl = slice(h * LANES, (h + 1) * LANES)
        kdec_ref[0, :, sl] = jnp.exp((BLOCK - 1 - ri) * lf)
        kdec_ref[1, :, sl] = jnp.exp(ri * lb)
        qdec_ref[h:h + 1, :] = jnp.exp((c1 + 1) * lf)
        qdec_ref[RET_HEADS + h:RET_HEADS + h + 1, :] = jnp.exp((BLOCK - c1) * lb)


def _decay_tables(decay_fwd, decay_bwd):
    logits = jnp.concatenate([decay_fwd, decay_bwd]).astype(F32)
    logits = jnp.broadcast_to(logits[:, None], (2 * RET_HEADS, LANES))
    return pl.pallas_call(
        _decay_kernel,
        out_shape=(jax.ShapeDtypeStruct((RET_HEADS, BLOCK, LANES), F32),
                   jax.ShapeDtypeStruct((2, BLOCK, RET_WIDTH), F32),
                   jax.ShapeDtypeStruct((2 * RET_HEADS, LANES), F32),
                   jax.ShapeDtypeStruct((2 * RET_HEADS, LANES), F32)),
        name="decay_tables",
    )(logits)


def _inproj_kernel(x_ref, g_ref, wt_ref, wk_ref, tb_ref, to_ref, fb_ref, fo_ref,
                   zt_ref, zk_ref):
    h = _rms(x_ref[0], g_ref[...]).astype(BF16)

    tb = tb_ref[0]

    def tok_tables(k):
        cb, sb, ssb = tb[3 * k:3 * k + 1], tb[3 * k + 1:3 * k + 2], tb[3 * k + 2:3 * k + 3]
        co, so, sso = to_ref[3 * k], to_ref[3 * k + 1], to_ref[3 * k + 2]
        return cb * co - sb * so, ssb * co + cb * sso

    fb = jnp.concatenate([fb_ref[0]] * CHUNKS_PER_TILE, axis=1)
    fo = fo_ref[...]

    def feat_tables(row, n):
        cb, sb = fb[row:row + n], fb[row + n:row + 2 * n]
        co, so = fo[row:row + n], fo[row + n:row + 2 * n]
        return cb * co - sb * so, sb * co + cb * so

    zk = _dot(h, wk_ref[...])
    ck, sk = tok_tables(0)
    for g in range(RET_HEADS):
        zg = zk[:, g * LANES:(g + 1) * LANES]
        r = (zg * ck + pltpu.roll(zg, LANES // 2, 1) * sk) * (RET_HEAD_DIM ** -0.5)
        zk_ref[0, :, K_RK + g * LANES:K_RK + (g + 1) * LANES] = r.astype(BF16)
    zg = zk[:, K_AK:K_AK + LANES]
    lane = lax.broadcasted_iota(jnp.int32, (1, LANES), 1)
    low_half = (lane % ATTN_HEAD_DIM) < (ATTN_HEAD_DIM // 2)
    half = ATTN_HEAD_DIM // 2
    rot = jnp.where(low_half, pltpu.roll(zg, LANES - half, 1), pltpu.roll(zg, half, 1))
    ca, sa = tok_tables(1)
    zk_ref[0, :, K_AK:K_AK + LANES] = (zg * ca + rot * sa).astype(BF16)

    def proj_t(row, n_rows):
        return lax.dot_general(wt_ref[row:row + n_rows, :], h, NT_DIMS,
                               preferred_element_type=F32)

    def store_t(row, val):
        for c in range(CHUNKS_PER_TILE):
            zt_ref[0, c, row:row + val.shape[0], :] = val[:, c * BLOCK:(c + 1) * BLOCK].astype(BF16)

    def rope_t(z, row, head_dim, cos, sin):
        hd2 = head_dim // 2
        for k in range(z.shape[0] // head_dim):
            x1 = z[k * head_dim:k * head_dim + hd2]
            x2 = z[k * head_dim + hd2:(k + 1) * head_dim]
            store_t(row + k * head_dim, x1 * cos - x2 * sin)
            store_t(row + k * head_dim + hd2, x2 * cos + x1 * sin)

    q_scale = (ATTN_HEAD_DIM ** -0.5) * LOG2E
    cat, sat = feat_tables(0, ATTN_HEAD_DIM // 2)
    rope_t(proj_t(T_AQ, ATTN_WIDTH), T_AQ, ATTN_HEAD_DIM, cat * q_scale, sat * q_scale)
    crt, srt = feat_tables(ATTN_HEAD_DIM, RET_HEAD_DIM // 2)
    rope_t(proj_t(T_RQ, RET_WIDTH), T_RQ, RET_HEAD_DIM, crt, srt)
    store_t(T_RV, proj_t(T_RV, RET_WIDTH))
    store_t(T_RG, proj_t(T_RG, RET_WIDTH))
    store_t(T_AV, proj_t(T_AV, ATTN_KV_WIDTH))


def _in_projection(x, gain, w_t, w_k, rope):
    B, S, _ = x.shape
    n = S // BLOCK
    tok_base, tok_off, feat_base, feat_off = rope
    return pl.pallas_call(
        _inproj_kernel,
        grid=(S // TOK_TILE, B),
        in_specs=[pl.BlockSpec((1, TOK_TILE, D_MODEL), lambda i, b: (b, i, 0)),
                  _const_spec((1, D_MODEL)),
                  _const_spec((T_ROWS, D_MODEL)),
                  _const_spec((D_MODEL, K_COLS)),
                  pl.BlockSpec((1,) + tok_base.shape[1:], lambda i, b: (i, 0, 0)),
                  _const_spec(tok_off.shape),
                  pl.BlockSpec((1,) + feat_base.shape[1:], lambda i, b: (i, 0, 0)),
                  _const_spec(feat_off.shape)],
        out_specs=(pl.BlockSpec((1, CHUNKS_PER_TILE, T_ROWS, BLOCK), lambda i, b: (b, i, 0, 0)),
                   pl.BlockSpec((1, TOK_TILE, K_COLS), lambda i, b: (b, i, 0))),
        out_shape=(jax.ShapeDtypeStruct((B, n, T_ROWS, BLOCK), BF16),
                   jax.ShapeDtypeStruct((B, S, K_COLS), BF16)),
        compiler_params=_params("parallel", "parallel"),
        name="in_projection",
    )(x, gain, w_t, w_k, tok_base, tok_off, feat_base, feat_off)


def _state_kernel(kf_ref, vf_ref, kb_ref, vb_ref, kdec_ref, cdec_ref,
                  pf_ref, pb_ref, st_ref):
    @pl.when(pl.program_id(1) == 0)
    def _():
        st_ref[...] = jnp.zeros_like(st_ref)

    for i in range(STATE_CHUNKS):
        for d, (k_ref, v_ref, p_ref, c) in enumerate(((kf_ref, vf_ref, pf_ref, i),
                                                      (kb_ref, vb_ref, pb_ref, STATE_CHUNKS - 1 - i))):
            kd = (k_ref[0, c * BLOCK:(c + 1) * BLOCK, :].astype(F32) * kdec_ref[d]).astype(BF16)
            for h in range(RET_HEADS):
                sl = slice(h * LANES, (h + 1) * LANES)
                st = st_ref[d, h]
                p_ref[0, c, h] = st.astype(BF16)
                kv = _dot(v_ref[0, c, sl, :], kd[:, sl])
                row = d * RET_HEADS + h
                st_ref[d, h] = st * cdec_ref[row:row + 1, :] + kv


def _retention_states(zt, zk, kdec, cdec):
    B, n = zt.shape[0], zt.shape[1]
    ng = n // STATE_CHUNKS
    k_blk = (1, STATE_CHUNKS * BLOCK, RET_WIDTH)
    v_blk = (1, STATE_CHUNKS, RET_WIDTH, BLOCK)
    vrow = T_RV // RET_WIDTH
    st_blk = (1, STATE_CHUNKS, RET_HEADS, RET_HEAD_DIM, RET_HEAD_DIM)
    st_shape = jax.ShapeDtypeStruct((B, n, RET_HEADS, RET_HEAD_DIM, RET_HEAD_DIM), BF16)
    return pl.pallas_call(
        _state_kernel,
        grid=(B, ng),
        in_specs=[pl.BlockSpec(k_blk, lambda b, t: (b, t, 0)),
                  pl.BlockSpec(v_blk, lambda b, t: (b, t, vrow, 0)),
                  pl.BlockSpec(k_blk, lambda b, t: (b, ng - 1 - t, 0)),
                  pl.BlockSpec(v_blk, lambda b, t: (b, ng - 1 - t, vrow, 0)),
                  _const_spec((2, BLOCK, RET_WIDTH)),
                  _const_spec((2 * RET_HEADS, LANES))],
        out_specs=(pl.BlockSpec(st_blk, lambda b, t: (b, t, 0, 0, 0)),
                   pl.BlockSpec(st_blk, lambda b, t: (b, ng - 1 - t, 0, 0, 0))),
        out_shape=(st_shape, st_shape),
        scratch_shapes=[pltpu.VMEM((2, RET_HEADS, RET_HEAD_DIM, RET_HEAD_DIM), F32)],
        compiler_params=_params("parallel", "arbitrary"),
        name="retention_states",
    )(zk, zt, zk, zt, kdec, cdec)


def _mixer_kernel(sink_ref, x_ref, aq_ref, rq_ref, rv_ref, rg_ref, av_ref, avp_ref, avn_ref,
                  rk_ref, ak_ref, akp_ref, akn_ref, pf_ref, pb_ref, dmat_ref, qdec_ref,
                  an_ref, gn_ref, wout_ref, npost_ref, o_ref, kext_ref, vext_ref, mixt_ref,
                  sc_ref, p_ref):
    tile = pl.program_id(1)
    n_blocks = pl.num_programs(1) * CHUNKS_PER_TILE

    lane = lax.broadcasted_iota(jnp.int32, (1, LANES), 1)
    for g in range(ATTN_KV_HEADS):
        km = jnp.where((lane // ATTN_HEAD_DIM) == g, 1.0, 0.0).astype(BF16)
        kext_ref[g, 0:BLOCK] = akp_ref[0] * km
        kext_ref[g, BLOCK:BLOCK + TOK_TILE] = ak_ref[0] * km
        kext_ref[g, BLOCK + TOK_TILE:] = akn_ref[0] * km
    vext_ref[0] = avp_ref[0, 0]
    for c in range(CHUNKS_PER_TILE):
        vext_ref[1 + c] = av_ref[0, c]
    vext_ref[CHUNKS_PER_TILE + 1] = avn_ref[0, 0]

    kk = lax.broadcasted_iota(jnp.int32, (BLOCK, LANES), 0)
    qq = lax.broadcasted_iota(jnp.int32, (BLOCK, LANES), 1)
    neg = jnp.full((BLOCK, LANES), -jnp.inf, F32)
    zero = jnp.zeros((BLOCK, LANES), F32)

    def scores_into(slot, c):
        q = jnp.concatenate([aq_ref[0, c, j * LANES:(j + 1) * LANES, :] for j in range(ATTN_GROUP)],
                            axis=1)
        r0 = c * BLOCK
        for g in range(ATTN_KV_HEADS):
            sc_ref[slot, g] = _dot(kext_ref[g, pl.ds(r0, 3 * BLOCK), :], q)

    def chunk(c, has_next):
        r0 = c * BLOCK
        blk = tile * CHUNKS_PER_TILE + c
        slot = c % 2
        if has_next:
            scores_into(1 - slot, c + 1)

        vw = jnp.concatenate([vext_ref[c], vext_ref[c + 1], vext_ref[c + 2]], axis=1)
        no_prev = jnp.where(blk > 0, 0, BLOCK)
        no_next = jnp.where(blk < n_blocks - 1, 0, BLOCK)
        bias_prev = jnp.where(kk >= qq + no_prev, zero, neg)
        bias_next = jnp.where(kk <= qq - no_next, zero, neg)
        heads = []
        for h in range(RET_HEADS):
            sl = slice(h * LANES, (h + 1) * LANES)
            qt = rq_ref[0, c, sl, :]
            heads.append((qt, _dot(rk_ref[0, pl.ds(r0, BLOCK), sl], qt)))

        ones = jnp.ones((2 * SUBLANES, 3 * BLOCK), BF16)
        outs = []
        for g in range(ATTN_KV_HEADS):
            sink_terms = []
            for j in range(ATTN_GROUP):
                cols = slice(j * LANES, (j + 1) * LANES)
                s0 = sc_ref[slot, g, 0:BLOCK, cols] + bias_prev
                s1 = sc_ref[slot, g, BLOCK:2 * BLOCK, cols]
                s2 = sc_ref[slot, g, 2 * BLOCK:3 * BLOCK, cols] + bias_next
                sk = jnp.full((1, LANES), sink_ref[g * ATTN_GROUP + j] * LOG2E, F32)
                m = jnp.maximum(
                    jnp.maximum(jnp.max(s0, axis=0, keepdims=True), jnp.max(s1, axis=0, keepdims=True)),
                    jnp.maximum(jnp.max(s2, axis=0, keepdims=True), sk))
                for i, s in enumerate((s0, s1, s2)):
                    p_ref[g, i * BLOCK:(i + 1) * BLOCK, cols] = jnp.exp2(s - m).astype(BF16)
                sink_terms.append(jnp.exp2(sk - m))
            vg = jnp.concatenate([vw[g * ATTN_HEAD_DIM:(g + 1) * ATTN_HEAD_DIM, :], ones], axis=0)
            ov = _dot(vg, p_ref[g])
            den = ov[ATTN_HEAD_DIM:ATTN_HEAD_DIM + 1, :] + jnp.concatenate(sink_terms, axis=1)
            outs.append(ov[0:ATTN_HEAD_DIM, :] * (1.0 / den))
        ys = []
        for h in range(RET_HEADS):
            sl = slice(h * LANES, (h + 1) * LANES)
            qt, pt = heads[h]
            y = _dot(rv_ref[0, c, sl, :], (pt * dmat_ref[h]).astype(BF16))
            qf = qt.astype(F32)
            qd = jnp.concatenate(
                [(qf * qdec_ref[h:h + 1, :]).astype(BF16),
                 (qf * qdec_ref[RET_HEADS + h:RET_HEADS + h + 1, :]).astype(BF16)], axis=0)
            st = jnp.concatenate([pf_ref[0, c, h], pb_ref[0, c, h]], axis=1)
            ys.append(y + _dot(st, qd))

        a = jnp.concatenate([outs[g][:, j * LANES:(j + 1) * LANES]
                             for j in range(ATTN_GROUP) for g in range(ATTN_KV_HEADS)], axis=0)
        ms = jnp.sum(a * a, axis=0, keepdims=True) * (1.0 / ATTN_WIDTH)
        mixt_ref[c, 0:ATTN_WIDTH, :] = (a * lax.rsqrt(ms + EPS) * an_ref[...]).astype(BF16)
        for h in range(RET_HEADS):
            sl = slice(h * LANES, (h + 1) * LANES)
            y = ys[h]
            mu = jnp.mean(y, axis=0, keepdims=True)
            yc = y - mu
            var = jnp.mean(yc * yc, axis=0, keepdims=True)
            yn = yc * lax.rsqrt(var + EPS) * gn_ref[sl, :]
            gate = rg_ref[0, c, sl, :].astype(F32)
            r = gate * jax.nn.sigmoid(gate) * yn
            mixt_ref[c, ATTN_WIDTH + h * LANES:ATTN_WIDTH + (h + 1) * LANES, :] = r.astype(BF16)

    scores_into(0, 0)
    for c in range(CHUNKS_PER_TILE):
        chunk(c, c + 1 < CHUNKS_PER_TILE)

    mixt = jnp.concatenate([mixt_ref[c] for c in range(CHUNKS_PER_TILE)], axis=1)
    mix = lax.dot_general(mixt, wout_ref[...], TN_DIMS, preferred_element_type=F32)
    o_ref[0] = x_ref[0] + _rms(mix, npost_ref[...])


def _mixer(x, zt, zk, states_f, states_b, dmat, qdec, sink, attn_norm, ret_gn, w_out, norm_post):
    B, S, _ = x.shape
    n = S // BLOCK
    cpt = CHUNKS_PER_TILE
    wide_t = (1, cpt, RET_WIDTH, BLOCK)
    narrow_t = (1, cpt, ATTN_KV_WIDTH, BLOCK)
    edge_t = (1, 1, ATTN_KV_WIDTH, BLOCK)
    edge_k = (1, BLOCK, ATTN_KV_WIDTH)
    v_row = T_AV // ATTN_KV_WIDTH
    k_col = K_AK // ATTN_KV_WIDTH
    st_blk = (1, cpt, RET_HEADS, RET_HEAD_DIM, RET_HEAD_DIM)

    def rows(r):
        return lambda b, i: (b, i, r, 0)

    def prev(i):
        return jnp.maximum(i * cpt - 1, 0)

    def nxt(i):
        return jnp.minimum((i + 1) * cpt, n - 1)

    return pl.pallas_call(
        _mixer_kernel,
        grid=(B, S // TOK_TILE),
        in_specs=[pl.BlockSpec(memory_space=pltpu.SMEM),
                  pl.BlockSpec((1, TOK_TILE, D_MODEL), lambda b, i: (b, i, 0)),
                  pl.BlockSpec(wide_t, rows(T_AQ // RET_WIDTH)),
                  pl.BlockSpec(wide_t, rows(T_RQ // RET_WIDTH)),
                  pl.BlockSpec(wide_t, rows(T_RV // RET_WIDTH)),
                  pl.BlockSpec(wide_t, rows(T_RG // RET_WIDTH)),
                  pl.BlockSpec(narrow_t, rows(v_row)),
                  pl.BlockSpec(edge_t, lambda b, i: (b, prev(i), v_row, 0)),
                  pl.BlockSpec(edge_t, lambda b, i: (b, nxt(i), v_row, 0)),
                  pl.BlockSpec((1, TOK_TILE, RET_WIDTH), lambda b, i: (b, i, 0)),
                  pl.BlockSpec((1, TOK_TILE, ATTN_KV_WIDTH), lambda b, i: (b, i, k_col)),
                  pl.BlockSpec(edge_k, lambda b, i: (b, prev(i), k_col)),
                  pl.BlockSpec(edge_k, lambda b, i: (b, nxt(i), k_col)),
                  pl.BlockSpec(st_blk, lambda b, i: (b, i, 0, 0, 0)),
                  pl.BlockSpec(st_blk, lambda b, i: (b, i, 0, 0, 0)),
                  _const_spec((RET_HEADS, BLOCK, LANES)),
                  _const_spec((2 * RET_HEADS, LANES)),
                  _const_spec((ATTN_WIDTH, LANES)),
                  _const_spec((RET_WIDTH, LANES)),
                  _const_spec((D_MODEL, D_MODEL)),
                  _const_spec((1, D_MODEL))],
        out_specs=pl.BlockSpec((1, TOK_TILE, D_MODEL), lambda b, i: (b, i, 0)),
        out_shape=jax.ShapeDtypeStruct((B, S, D_MODEL), F32),
        scratch_shapes=[pltpu.VMEM((ATTN_KV_HEADS, TOK_TILE + 2 * BLOCK, ATTN_KV_WIDTH), BF16),
                        pltpu.VMEM((cpt + 2, ATTN_KV_WIDTH, BLOCK), BF16),
                        pltpu.VMEM((cpt, D_MODEL, BLOCK), BF16),
                        pltpu.VMEM((2, ATTN_KV_HEADS, 3 * BLOCK, ATTN_GROUP * LANES), F32),
                        pltpu.VMEM((ATTN_KV_HEADS, 3 * BLOCK, ATTN_GROUP * LANES), BF16)],
        compiler_params=_params("parallel", "parallel"),
        name="token_mixer",
    )(sink, x, zt, zt, zt, zt, zt, zt, zt, zk, zk, zk, zk, states_f, states_b, dmat, qdec,
      attn_norm, ret_gn, w_out, norm_post)


def _memkv_kernel(m_ref, g_ref, w_ref, kv_ref):
    mn = _rms(m_ref[0], g_ref[...]).astype(BF16)
    kv_ref[0] = _dot(mn, w_ref[...]).astype(BF16)


def _memory_kv(mem, gain, w_kv):
    B, M, _ = mem.shape
    return pl.pallas_call(
        _memkv_kernel,
        grid=(B,),
        in_specs=[pl.BlockSpec((1, M, D_MODEL), lambda b: (b, 0, 0)),
                  _const_spec((1, D_MODEL)),
                  _const_spec((D_MODEL, 2 * D_MODEL))],
        out_specs=pl.BlockSpec((1, M, 2 * D_MODEL), lambda b: (b, 0, 0)),
        out_shape=jax.ShapeDtypeStruct((B, M, 2 * D_MODEL), BF16),
        compiler_params=_params("parallel"),
        name="memory_kv",
    )(mem, gain, w_kv)


def _skew_specs(B, n_tiles):
    last = B * n_tiles - 1

    def at(delta):
        def index_map(g):
            t = jnp.clip(g + delta, 0, last)
            return (t // n_tiles, t % n_tiles, 0)
        return pl.BlockSpec((1, TOK_TILE, D_MODEL), index_map)

    def batch_of(g):
        return jnp.minimum(g, last) // n_tiles

    return at(-1), at(+1), batch_of


def _skewed(step_fn, first_fn):
    g = pl.program_id(0)
    pl.when(g == 0)(first_fn)
    for par in range(2):
        pl.when(lax.rem(g, 2) == par)(lambda par=par: step_fn(par))


SIDE_PIECES = 4
SIDE_ROWS = TOK_TILE // SIDE_PIECES


def _never():
    return jnp.zeros((1,), jnp.int32)


def _side_work(par, never_ref, xp_ref, xn_ref, res_ref, h_ref, o_ref, gpre_ref, gpost_ref):
    never = never_ref[0] != 0

    def piece(k):
        rows = slice(k * SIDE_ROWS, (k + 1) * SIDE_ROWS)
        out = xp_ref[0, rows, :] + _rms(res_ref[1 - par, rows, :], gpost_ref[...])
        o_ref[0, rows, :] = out
        hn = _rms(xn_ref[0, rows, :], gpre_ref[...])
        h_ref[1 - par, rows, :] = hn.astype(BF16)
        both = jnp.sum(out, axis=0, keepdims=True) + jnp.sum(hn, axis=0, keepdims=True)
        return jnp.where(never, both, 0.0)

    return piece


def _xattn_kernel(x_ref, gpre_ref, wq_ref, km_ref, vm_ref, wo_ref, gpost_ref,
                  o_ref, cat_ref):
    x = x_ref[0]
    h = _rms(x, gpre_ref[...]).astype(BF16)
    q = (_dot(h, wq_ref[...]) * (XA_HEAD_DIM ** -0.5)).astype(BF16)
    for hd in range(XA_HEADS):
        sl = slice(hd * XA_HEAD_DIM, (hd + 1) * XA_HEAD_DIM)
        s = lax.dot_general(q[:, sl], km_ref[0, :, sl], NT_DIMS, preferred_element_type=F32)
        p = jnp.exp(s - jnp.max(s, axis=-1, keepdims=True))
        den = jnp.sum(p, axis=-1, keepdims=True)
        cat_ref[:, sl] = (_dot(p.astype(BF16), vm_ref[0, :, sl]) * (1.0 / den)).astype(BF16)
    xo = _dot(cat_ref[...], wo_ref[...])
    o_ref[0] = x + _rms(xo, gpost_ref[...])


def _cross_attention(x, kv, gain_pre, w_q, w_o, gain_post):
    B, S, _ = x.shape
    M = kv.shape[1]
    tok = pl.BlockSpec((1, TOK_TILE, D_MODEL), lambda b, i: (b, i, 0))
    return pl.pallas_call(
        _xattn_kernel,
        grid=(B, S // TOK_TILE),
        in_specs=[tok,
                  _const_spec((1, D_MODEL)),
                  _const_spec((D_MODEL, D_MODEL)),
                  pl.BlockSpec((1, M, D_MODEL), lambda b, i: (b, 0, 0)),
                  pl.BlockSpec((1, M, D_MODEL), lambda b, i: (b, 0, 1)),
                  _const_spec((D_MODEL, D_MODEL)),
                  _const_spec((1, D_MODEL))],
        out_specs=tok,
        out_shape=jax.ShapeDtypeStruct((B, S, D_MODEL), F32),
        scratch_shapes=[pltpu.VMEM((TOK_TILE, D_MODEL), BF16)],
        compiler_params=_params("parallel", "parallel"),
        name="cross_attention",
    )(x, gain_pre, w_q, kv, kv, w_o, gain_post)


FF_CHUNK = 2 * MXU_TILE
FF_CHUNKS = [(c0, min(FF_CHUNK, D_FF - c0)) for c0 in range(0, D_FF, FF_CHUNK)]
assert all(w % MXU_TILE == 0 for _, w in FF_CHUNKS)
assert SIDE_PIECES <= len(FF_CHUNKS)


def _ffn_kernel(never_ref, xp_ref, xn_ref, gpre_ref, wgu_ref, wd_ref, gpost_ref, o_ref,
                h_ref, f_ref, gu_ref, act_ref):
    def first():
        h_ref[0] = _rms(xp_ref[0], gpre_ref[...]).astype(BF16)
        f_ref[1] = jnp.zeros((TOK_TILE, D_MODEL), F32)

    def step(par):
        def gate_up_into(slot, j):
            c0, w = FF_CHUNKS[j]
            gu_ref[slot, 0, :, 0:w] = _dot(h_ref[par], wgu_ref[:, c0:c0 + w])
            gu_ref[slot, 1, :, 0:w] = _dot(h_ref[par], wgu_ref[:, D_FF + c0:D_FF + c0 + w])

        side = _side_work(par, never_ref, xp_ref, xn_ref, f_ref, h_ref, o_ref, gpre_ref, gpost_ref)
        gate_up_into(0, 0)
        for j, (c0, w) in enumerate(FF_CHUNKS):
            if j + 1 < len(FF_CHUNKS):
                gate_up_into((j + 1) % 2, j + 1)
            gate = gu_ref[j % 2, 0, :, 0:w]
            up = gu_ref[j % 2, 1, :, 0:w]
            if j < SIDE_PIECES:
                up = up + side(j)[:, 0:w]
            act_ref[:, c0:c0 + w] = (gate * jax.nn.sigmoid(gate) * up).astype(BF16)
        f_ref[par] = _dot(act_ref[...], wd_ref[...])

    _skewed(step, first)


def _ffn(x, gain_pre, w_gu, w_down, gain_post):
    B, S, _ = x.shape
    n_tiles = S // TOK_TILE
    prev_tile, next_tile, _ = _skew_specs(B, n_tiles)
    return pl.pallas_call(
        _ffn_kernel,
        grid=(B * n_tiles + 1,),
        in_specs=[pl.BlockSpec(memory_space=pltpu.SMEM), prev_tile, next_tile,
                  _const_spec((1, D_MODEL)),
                  _const_spec((D_MODEL, 2 * D_FF)),
                  _const_spec((D_FF, D_MODEL)),
                  _const_spec((1, D_MODEL))],
        out_specs=prev_tile,
        out_shape=jax.ShapeDtypeStruct((B, S, D_MODEL), F32),
        scratch_shapes=[pltpu.VMEM((2, TOK_TILE, D_MODEL), BF16),
                        pltpu.VMEM((2, TOK_TILE, D_MODEL), F32),
                        pltpu.VMEM((2, 2, TOK_TILE, FF_CHUNK), F32),
                        pltpu.VMEM((TOK_TILE, D_FF), BF16)],
        compiler_params=_params("arbitrary"),
        name="swiglu_ffn",
    )(_never(), x, x, gain_pre, w_gu, w_down, gain_post)


def _attn_feature_order():
    order = []
    for j in range(ATTN_GROUP):
        for g in range(ATTN_KV_HEADS):
            h = g * ATTN_GROUP + j
            order += list(range(h * ATTN_HEAD_DIM, (h + 1) * ATTN_HEAD_DIM))
    return np.asarray(order, np.int32)


def kernel(x, mem, norm_mix_pre, norm_mix_post, w_in, attn_sink, attn_out_norm,
           ret_decay_fwd, ret_decay_bwd, ret_gn, w_out, norm_xa_pre, norm_xa_post,
           norm_mem, xa_wq, xa_wkv, xa_wo, norm_ffn_pre, norm_ffn_post,
           ffn_w_gu, ffn_w_down):
    B, S, D = x.shape
    assert D == D_MODEL and S % (TOK_TILE * STATE_CHUNKS // CHUNKS_PER_TILE) == 0
    assert mem.shape[1] == N_MEM
    depth = w_in.shape[0]
    feat = _attn_feature_order()
    o_k = ATTN_WIDTH
    o_v = o_k + ATTN_KV_WIDTH
    o_r = o_v + ATTN_KV_WIDTH
    rw = RET_WIDTH

    rope = _rope_tables(S // TOK_TILE)

    def row(v):
        return v.reshape(1, -1).astype(F32)

    def col(v):
        return jnp.broadcast_to(v.astype(F32)[:, None], (v.shape[0], LANES))

    for l in range(depth):
        w = w_in[l]
        w_t = jnp.concatenate([w[:, :o_k][:, feat], w[:, o_r:o_r + rw], w[:, o_r + 2 * rw:],
                               w[:, o_v:o_r]], axis=1).T.astype(BF16)
        w_k = jnp.concatenate([w[:, o_r + rw:o_r + 2 * rw], w[:, o_k:o_v]], axis=1).astype(BF16)
        out_rows = np.concatenate([feat, np.arange(ATTN_WIDTH, ATTN_WIDTH + RET_WIDTH)])

        dmat, kdec, qdec, cdec = _decay_tables(ret_decay_fwd[l], ret_decay_bwd[l])
        zt, zk = _in_projection(x, row(norm_mix_pre[l]), w_t, w_k, rope)
        states_f, states_b = _retention_states(zt, zk, kdec, cdec)
        x = _mixer(x, zt, zk, states_f, states_b, dmat, qdec, attn_sink[l].astype(F32),
                   col(attn_out_norm[l][feat]), col(ret_gn[l]),
                   w_out[l][out_rows].astype(BF16), row(norm_mix_post[l]))
        kv = _memory_kv(mem, row(norm_mem[l]), xa_wkv[l].astype(BF16))
        x = _cross_attention(x, kv, row(norm_xa_pre[l]), xa_wq[l].astype(BF16),
                             xa_wo[l].astype(BF16), row(norm_xa_post[l]))
        x = _ffn(x, row(norm_ffn_pre[l]), ffn_w_gu[l].astype(BF16),
                 ffn_w_down[l].astype(BF16), row(norm_ffn_post[l]))
    return x
```

```python
import math

import numpy as np
import jax
import jax.numpy as jnp
from jax import lax
from jax.experimental import pallas as pl
from jax.experimental.pallas import tpu as pltpu

D_MODEL = 1024
N_MEM = 256
ATTN_HEADS = 8
ATTN_KV_HEADS = 2
ATTN_GROUP = ATTN_HEADS // ATTN_KV_HEADS
ATTN_HEAD_DIM = 64
ATTN_WIDTH = ATTN_HEADS * ATTN_HEAD_DIM
ATTN_KV_WIDTH = ATTN_KV_HEADS * ATTN_HEAD_DIM
BLOCK = 128
RET_HEADS = 4
RET_HEAD_DIM = 128
RET_WIDTH = RET_HEADS * RET_HEAD_DIM
IN_COLS = ATTN_WIDTH + 2 * ATTN_KV_WIDTH + 4 * RET_WIDTH
XA_HEADS = 4
XA_HEAD_DIM = D_MODEL // XA_HEADS
D_FF = -(-(8 * D_MODEL) // (3 * 256)) * 256
ROPE_THETA = 10000.0
EPS = 1e-6
LOG2E = math.log2(math.e)

LANES = 128
MXU_TILE = 256
VMEM_LIMIT_BYTES = 56 * 1024 * 1024

TOK_TILE = 512
CHUNKS_PER_TILE = TOK_TILE // BLOCK
STATE_CHUNKS = 4
MIX_TILE = 1024
MIX_CHUNKS = MIX_TILE // BLOCK

T_AQ, T_RQ, T_RV, T_RG, T_AV = 0, 512, 1024, 1536, 2048
T_ROWS = T_AV + ATTN_KV_WIDTH
K_RK, K_AK = 0, RET_WIDTH
K_COLS = RET_WIDTH + ATTN_KV_WIDTH

F32 = jnp.float32
BF16 = jnp.bfloat16
NT_DIMS = (((1,), (1,)), ((), ()))
TN_DIMS = (((0,), (0,)), ((), ()))


def _const_spec(shape):
    nd = len(shape)
    return pl.BlockSpec(shape, lambda *_: (0,) * nd, pipeline_mode=pl.Buffered(1))


def _params(*sem, flags=None):
    return pltpu.CompilerParams(dimension_semantics=sem,
                                vmem_limit_bytes=VMEM_LIMIT_BYTES, flags=flags)


def _rms(x, w):
    ms = jnp.mean(x * x, axis=-1, keepdims=True)
    return x * lax.rsqrt(ms + EPS) * w


def _dot(a, b):
    return jnp.dot(a, b, preferred_element_type=F32)


def _rope_tables(n_tiles):
    def parts(dim):
        inv_freq = ROPE_THETA ** (-jnp.arange(0, dim, 2, dtype=F32) / dim)
        base = (jnp.arange(n_tiles, dtype=F32) * TOK_TILE)[:, None] * inv_freq[None, :]
        off = jnp.arange(TOK_TILE, dtype=F32)[:, None] * inv_freq[None, :]
        return jnp.cos(base), jnp.sin(base), jnp.cos(off), jnp.sin(off)

    def tok(c, s, reps):
        cos = jnp.concatenate([c, c] * reps, axis=1)
        sin = jnp.concatenate([s, s] * reps, axis=1)
        return cos, sin, jnp.concatenate([-s, s] * reps, axis=1)

    cb_a, sb_a, co_a, so_a = parts(ATTN_HEAD_DIM)
    cb_r, sb_r, co_r, so_r = parts(RET_HEAD_DIM)
    tok_base = jnp.stack(tok(cb_r, sb_r, 1) + tok(cb_a, sb_a, 2)
                         + (jnp.zeros((n_tiles, LANES), F32),) * 2, axis=1)
    tok_off = jnp.stack(tok(co_r, so_r, 1) + tok(co_a, so_a, 2), axis=0)
    feat_base = jnp.concatenate([cb_a, sb_a, cb_r, sb_r], axis=1)
    feat_base = jnp.broadcast_to(feat_base[:, :, None], feat_base.shape + (BLOCK,))
    feat_off = jnp.concatenate([co_a, so_a, co_r, so_r], axis=1).T
    return tok_base, tok_off, feat_base, feat_off


def _decay_kernel(logit_ref, dmat_ref, kdec_ref, qdec_ref, cdec_ref):
    x = logit_ref[...]
    lg = jnp.minimum(x, 0.0) - jnp.log1p(jnp.exp(-jnp.abs(x)))
    ri = lax.broadcasted_iota(jnp.int32, (BLOCK, LANES), 0).astype(F32)
    ci = lax.broadcasted_iota(jnp.int32, (BLOCK, LANES), 1).astype(F32)
    c1 = ci[0:1, :]
    diff = ci - ri
    cdec_ref[...] = jnp.exp(BLOCK * lg)
    for h in range(RET_HEADS):
        lf = lg[h:h + 1, :]
        lb = lg[RET_HEADS + h:RET_HEADS + h + 1, :]
        dmat_ref[h] = jnp.exp(jnp.abs(diff) * jnp.where(diff >= 0, lf, lb))
        sl = slice(h * LANES, (h + 1) * LANES)
        kdec_ref[0, :, sl] = jnp.exp((BLOCK - 1 - ri) * lf)
        kdec_ref[1, :, sl] = jnp.exp(ri * lb)
        qdec_ref[h:h + 1, :] = jnp.exp((c1 + 1) * lf)
        qdec_ref[RET_HEADS + h:RET_HEADS + h + 1, :] = jnp.exp((BLOCK - c1) * lb)


def _decay_tables(decay_fwd, decay_bwd):
    logits = jnp.concatenate([decay_fwd, decay_bwd]).astype(F32)
    logits = jnp.broadcast_to(logits[:, None], (2 * RET_HEADS, LANES))
    return pl.pallas_call(
        _decay_kernel,
        out_shape=(jax.ShapeDtypeStruct((RET_HEADS, BLOCK, LANES), F32),
                   jax.ShapeDtypeStruct((2, BLOCK, RET_WIDTH), F32),
                   jax.ShapeDtypeStruct((2 * RET_HEADS, LANES), F32),
                   jax.ShapeDtypeStruct((2 * RET_HEADS, LANES), F32)),
        name="decay_tables",
    )(logits)


def _inproj_kernel(x_ref, g_ref, wt_ref, wk_ref, tb_ref, to_ref, fb_ref, fo_ref,
                   zt_ref, zk_ref):
    h = _rms(x_ref[0], g_ref[...]).astype(BF16)

    tb = tb_ref[0]

    def tok_tables(k):
        cb, sb, ssb = tb[3 * k:3 * k + 1], tb[3 * k + 1:3 * k + 2], tb[3 * k + 2:3 * k + 3]
        co, so, sso = to_ref[3 * k], to_ref[3 * k + 1], to_ref[3 * k + 2]
        return cb * co - sb * so, ssb * co + cb * sso

    fb = jnp.concatenate([fb_ref[0]] * CHUNKS_PER_TILE, axis=1)
    fo = fo_ref[...]

    def feat_tables(row, n):
        cb, sb = fb[row:row + n], fb[row + n:row + 2 * n]
        co, so = fo[row:row + n], fo[row + n:row + 2 * n]
        return cb * co - sb * so, sb * co + cb * so

    zk = _dot(h, wk_ref[...])
    ck, sk = tok_tables(0)
    for g in range(RET_HEADS):
        zg = zk[:, g * LANES:(g + 1) * LANES]
        r = (zg * ck + pltpu.roll(zg, LANES // 2, 1) * sk) * (RET_HEAD_DIM ** -0.5)
        zk_ref[0, :, K_RK + g * LANES:K_RK + (g + 1) * LANES] = r.astype(BF16)
    zg = zk[:, K_AK:K_AK + LANES]
    lane = lax.broadcasted_iota(jnp.int32, (1, LANES), 1)
    low_half = (lane % ATTN_HEAD_DIM) < (ATTN_HEAD_DIM // 2)
    half = ATTN_HEAD_DIM // 2
    rot = jnp.where(low_half, pltpu.roll(zg, LANES - half, 1), pltpu.roll(zg, half, 1))
    ca, sa = tok_tables(1)
    zk_ref[0, :, K_AK:K_AK + LANES] = (zg * ca + rot * sa).astype(BF16)

    def proj_t(row, n_rows):
        return lax.dot_general(wt_ref[row:row + n_rows, :], h, NT_DIMS,
                               preferred_element_type=F32)

    def store_t(row, val):
        for c in range(CHUNKS_PER_TILE):
            zt_ref[0, c, row:row + val.shape[0], :] = val[:, c * BLOCK:(c + 1) * BLOCK].astype(BF16)

    def rope_t(z, row, head_dim, cos, sin):
        hd2 = head_dim // 2
        for k in range(z.shape[0] // head_dim):
            x1 = z[k * head_dim:k * head_dim + hd2]
            x2 = z[k * head_dim + hd2:(k + 1) * head_dim]
            store_t(row + k * head_dim, x1 * cos - x2 * sin)
            store_t(row + k * head_dim + hd2, x2 * cos + x1 * sin)

    q_scale = (ATTN_HEAD_DIM ** -0.5) * LOG2E
    cat, sat = feat_tables(0, ATTN_HEAD_DIM // 2)
    rope_t(proj_t(T_AQ, ATTN_WIDTH), T_AQ, ATTN_HEAD_DIM, cat * q_scale, sat * q_scale)
    crt, srt = feat_tables(ATTN_HEAD_DIM, RET_HEAD_DIM // 2)
    rope_t(proj_t(T_RQ, RET_WIDTH), T_RQ, RET_HEAD_DIM, crt, srt)
    store_t(T_RV, proj_t(T_RV, RET_WIDTH))
    store_t(T_RG, proj_t(T_RG, RET_WIDTH))
    store_t(T_AV, proj_t(T_AV, ATTN_KV_WIDTH))


def _in_projection(x, gain, w_t, w_k, rope):
    B, S, _ = x.shape
    n = S // BLOCK
    tok_base, tok_off, feat_base, feat_off = rope
    return pl.pallas_call(
        _inproj_kernel,
        grid=(S // TOK_TILE, B),
        in_specs=[pl.BlockSpec((1, TOK_TILE, D_MODEL), lambda i, b: (b, i, 0)),
                  _const_spec((1, D_MODEL)),
                  _const_spec((T_ROWS, D_MODEL)),
                  _const_spec((D_MODEL, K_COLS)),
                  pl.BlockSpec((1,) + tok_base.shape[1:], lambda i, b: (i, 0, 0)),
                  _const_spec(tok_off.shape),
                  pl.BlockSpec((1,) + feat_base.shape[1:], lambda i, b: (i, 0, 0)),
                  _const_spec(feat_off.shape)],
        out_specs=(pl.BlockSpec((1, CHUNKS_PER_TILE, T_ROWS, BLOCK), lambda i, b: (b, i, 0, 0)),
                   pl.BlockSpec((1, TOK_TILE, K_COLS), lambda i, b: (b, i, 0))),
        out_shape=(jax.ShapeDtypeStruct((B, n, T_ROWS, BLOCK), BF16),
                   jax.ShapeDtypeStruct((B, S, K_COLS), BF16)),
        compiler_params=_params("parallel", "parallel"),
        name="in_projection",
    )(x, gain, w_t, w_k, tok_base, tok_off, feat_base, feat_off)


def _state_kernel(kf_ref, vf_ref, kb_ref, vb_ref, kdec_ref, cdec_ref,
                  pf_ref, pb_ref, st_ref):
    @pl.when(pl.program_id(1) == 0)
    def _():
        st_ref[...] = jnp.zeros_like(st_ref)

    for i in range(STATE_CHUNKS):
        for d, (k_ref, v_ref, p_ref, c) in enumerate(((kf_ref, vf_ref, pf_ref, i),
                                                      (kb_ref, vb_ref, pb_ref, STATE_CHUNKS - 1 - i))):
            kd = (k_ref[0, c * BLOCK:(c + 1) * BLOCK, :].astype(F32) * kdec_ref[d]).astype(BF16)
            for h in range(RET_HEADS):
                sl = slice(h * LANES, (h + 1) * LANES)
                st = st_ref[d, h]
                p_ref[0, c, h] = st.astype(BF16)
                kv = _dot(v_ref[0, c, sl, :], kd[:, sl])
                row = d * RET_HEADS + h
                st_ref[d, h] = st * cdec_ref[row:row + 1, :] + kv


def _retention_states(zt, zk, kdec, cdec):
    B, n = zt.shape[0], zt.shape[1]
    ng = n // STATE_CHUNKS
    k_blk = (1, STATE_CHUNKS * BLOCK, RET_WIDTH)
    v_blk = (1, STATE_CHUNKS, RET_WIDTH, BLOCK)
    vrow = T_RV // RET_WIDTH
    st_blk = (1, STATE_CHUNKS, RET_HEADS, RET_HEAD_DIM, RET_HEAD_DIM)
    st_shape = jax.ShapeDtypeStruct((B, n, RET_HEADS, RET_HEAD_DIM, RET_HEAD_DIM), BF16)
    return pl.pallas_call(
        _state_kernel,
        grid=(B, ng),
        in_specs=[pl.BlockSpec(k_blk, lambda b, t: (b, t, 0)),
                  pl.BlockSpec(v_blk, lambda b, t: (b, t, vrow, 0)),
                  pl.BlockSpec(k_blk, lambda b, t: (b, ng - 1 - t, 0)),
                  pl.BlockSpec(v_blk, lambda b, t: (b, ng - 1 - t, vrow, 0)),
                  _const_spec((2, BLOCK, RET_WIDTH)),
                  _const_spec((2 * RET_HEADS, LANES))],
        out_specs=(pl.BlockSpec(st_blk, lambda b, t: (b, t, 0, 0, 0)),
                   pl.BlockSpec(st_blk, lambda b, t: (b, ng - 1 - t, 0, 0, 0))),
        out_shape=(st_shape, st_shape),
        scratch_shapes=[pltpu.VMEM((2, RET_HEADS, RET_HEAD_DIM, RET_HEAD_DIM), F32)],
        compiler_params=_params("parallel", "arbitrary"),
        name="retention_states",
    )(zk, zt, zk, zt, kdec, cdec)


def _mixer_kernel(sink_ref, x_ref, zt_ref, avp_ref, avn_ref, zk_ref, akp_ref, akn_ref,
                  pf_ref, pb_ref, dmat_ref, qdec_ref,
                  an_ref, gn_ref, wout_ref, npost_ref, o_ref, kext_ref, vext_ref, mixt_ref,
                  sc_ref):
    tile = pl.program_id(1)
    n_blocks = pl.num_programs(1) * MIX_CHUNKS

    lane = lax.broadcasted_iota(jnp.int32, (1, LANES), 1)
    for g in range(ATTN_KV_HEADS):
        km = jnp.where((lane // ATTN_HEAD_DIM) == g, 1.0, 0.0).astype(BF16)
        kext_ref[g, 0:BLOCK] = akp_ref[0] * km
        kext_ref[g, BLOCK:BLOCK + MIX_TILE] = zk_ref[0, :, K_AK:K_AK + ATTN_KV_WIDTH] * km
        kext_ref[g, BLOCK + MIX_TILE:] = akn_ref[0] * km
    vext_ref[0] = avp_ref[0, 0]
    for c in range(MIX_CHUNKS):
        vext_ref[1 + c] = zt_ref[0, c, T_AV:T_AV + ATTN_KV_WIDTH, :]
    vext_ref[MIX_CHUNKS + 1] = avn_ref[0, 0]

    kk = lax.broadcasted_iota(jnp.int32, (BLOCK, LANES), 0)
    qq = lax.broadcasted_iota(jnp.int32, (BLOCK, LANES), 1)
    neg = jnp.full((BLOCK, LANES), -jnp.inf, F32)
    zero = jnp.zeros((BLOCK, LANES), F32)

    def scores_into(slot, c):
        q = jnp.concatenate([zt_ref[0, c, T_AQ + j * LANES:T_AQ + (j + 1) * LANES, :]
                             for j in range(ATTN_GROUP)], axis=1)
        r0 = c * BLOCK
        for g in range(ATTN_KV_HEADS):
            sc_ref[slot, g] = _dot(kext_ref[g, pl.ds(r0, 3 * BLOCK), :], q)

    def chunk(c, has_next):
        r0 = c * BLOCK
        blk = tile * MIX_CHUNKS + c
        slot = c % 2
        if has_next:
            scores_into(1 - slot, c + 1)

        vw = jnp.concatenate([vext_ref[c], vext_ref[c + 1], vext_ref[c + 2]], axis=1)
        no_prev = jnp.where(blk > 0, 0, BLOCK)
        no_next = jnp.where(blk < n_blocks - 1, 0, BLOCK)
        bias_prev = jnp.where(kk >= qq + no_prev, zero, neg)
        bias_next = jnp.where(kk <= qq - no_next, zero, neg)
        bias = jnp.concatenate([jnp.concatenate([bias_prev] * ATTN_GROUP, axis=1),
                                jnp.zeros((BLOCK, ATTN_GROUP * LANES), F32),
                                jnp.concatenate([bias_next] * ATTN_GROUP, axis=1)], axis=0)
        heads = []
        for h in range(RET_HEADS):
            qt = zt_ref[0, c, T_RQ + h * LANES:T_RQ + (h + 1) * LANES, :]
            kh = zk_ref[0, pl.ds(r0, BLOCK), K_RK + h * LANES:K_RK + (h + 1) * LANES]
            heads.append((qt, _dot(kh, qt)))

        outs = []
        for g in range(ATTN_KV_HEADS):
            s = sc_ref[slot, g] + bias
            sk = jnp.concatenate(
                [jnp.full((1, LANES), sink_ref[g * ATTN_GROUP + j] * LOG2E, F32)
                 for j in range(ATTN_GROUP)], axis=1)
            m = jnp.maximum(jnp.max(s, axis=0, keepdims=True), sk)
            p = jnp.exp2(s - m)
            den = jnp.sum(p, axis=0, keepdims=True) + jnp.exp2(sk - m)
            vg = vw[g * ATTN_HEAD_DIM:(g + 1) * ATTN_HEAD_DIM, :]
            outs.append(_dot(vg, p.astype(BF16)) * (1.0 / den))
        ys = []
        for h in range(RET_HEADS):
            qt, pt = heads[h]
            y = _dot(zt_ref[0, c, T_RV + h * LANES:T_RV + (h + 1) * LANES, :],
                     (pt * dmat_ref[h]).astype(BF16))
            qf = qt.astype(F32)
            qd = jnp.concatenate(
                [(qf * qdec_ref[h:h + 1, :]).astype(BF16),
                 (qf * qdec_ref[RET_HEADS + h:RET_HEADS + h + 1, :]).astype(BF16)], axis=0)
            st = jnp.concatenate([pf_ref[0, c, h], pb_ref[0, c, h]], axis=1)
            ys.append(y + _dot(st, qd))

        a = jnp.concatenate([outs[g][:, j * LANES:(j + 1) * LANES]
                             for j in range(ATTN_GROUP) for g in range(ATTN_KV_HEADS)], axis=0)
        ms = jnp.sum(a * a, axis=0, keepdims=True) * (1.0 / ATTN_WIDTH)
        mixt_ref[c, 0:ATTN_WIDTH, :] = (a * lax.rsqrt(ms + EPS) * an_ref[...]).astype(BF16)
        for h in range(RET_HEADS):
            sl = slice(h * LANES, (h + 1) * LANES)
            y = ys[h]
            mu = jnp.mean(y, axis=0, keepdims=True)
            yc = y - mu
            var = jnp.mean(yc * yc, axis=0, keepdims=True)
            yn = yc * lax.rsqrt(var + EPS) * gn_ref[sl, :]
            gate = zt_ref[0, c, T_RG + h * LANES:T_RG + (h + 1) * LANES, :].astype(F32)
            r = gate * jax.nn.sigmoid(gate) * yn
            mixt_ref[c, ATTN_WIDTH + h * LANES:ATTN_WIDTH + (h + 1) * LANES, :] = r.astype(BF16)

    scores_into(0, 0)
    for c in range(MIX_CHUNKS):
        chunk(c, c + 1 < MIX_CHUNKS)

    mixt = jnp.concatenate([mixt_ref[c] for c in range(MIX_CHUNKS)], axis=1)
    mix = lax.dot_general(mixt, wout_ref[...], TN_DIMS, preferred_element_type=F32)
    o_ref[0] = x_ref[0] + _rms(mix, npost_ref[...])


def _mixer(x, zt, zk, states_f, states_b, dmat, qdec, sink, attn_norm, ret_gn, w_out, norm_post):
    B, S, _ = x.shape
    n = S // BLOCK
    cpt = MIX_CHUNKS
    edge_t = (1, 1, ATTN_KV_WIDTH, BLOCK)
    edge_k = (1, BLOCK, ATTN_KV_WIDTH)
    v_row = T_AV // ATTN_KV_WIDTH
    k_col = K_AK // ATTN_KV_WIDTH
    st_blk = (1, cpt, RET_HEADS, RET_HEAD_DIM, RET_HEAD_DIM)

    def prev(i):
        return jnp.maximum(i * cpt - 1, 0)

    def nxt(i):
        return jnp.minimum((i + 1) * cpt, n - 1)

    return pl.pallas_call(
        _mixer_kernel,
        grid=(B, S // MIX_TILE),
        in_specs=[pl.BlockSpec(memory_space=pltpu.SMEM),
                  pl.BlockSpec((1, MIX_TILE, D_MODEL), lambda b, i: (b, i, 0)),
                  pl.BlockSpec((1, cpt, T_ROWS, BLOCK), lambda b, i: (b, i, 0, 0)),
                  pl.BlockSpec(edge_t, lambda b, i: (b, prev(i), v_row, 0)),
                  pl.BlockSpec(edge_t, lambda b, i: (b, nxt(i), v_row, 0)),
                  pl.BlockSpec((1, MIX_TILE, K_COLS), lambda b, i: (b, i, 0)),
                  pl.BlockSpec(edge_k, lambda b, i: (b, prev(i), k_col)),
                  pl.BlockSpec(edge_k, lambda b, i: (b, nxt(i), k_col)),
                  pl.BlockSpec(st_blk, lambda b, i: (b, i, 0, 0, 0)),
                  pl.BlockSpec(st_blk, lambda b, i: (b, i, 0, 0, 0)),
                  _const_spec((RET_HEADS, BLOCK, LANES)),
                  _const_spec((2 * RET_HEADS, LANES)),
                  _const_spec((ATTN_WIDTH, LANES)),
                  _const_spec((RET_WIDTH, LANES)),
                  _const_spec((D_MODEL, D_MODEL)),
                  _const_spec((1, D_MODEL))],
        out_specs=pl.BlockSpec((1, MIX_TILE, D_MODEL), lambda b, i: (b, i, 0)),
        out_shape=jax.ShapeDtypeStruct((B, S, D_MODEL), F32),
        scratch_shapes=[pltpu.VMEM((ATTN_KV_HEADS, MIX_TILE + 2 * BLOCK, ATTN_KV_WIDTH), BF16),
                        pltpu.VMEM((cpt + 2, ATTN_KV_WIDTH, BLOCK), BF16),
                        pltpu.VMEM((cpt, D_MODEL, BLOCK), BF16),
                        pltpu.VMEM((2, ATTN_KV_HEADS, 3 * BLOCK, ATTN_GROUP * LANES), F32)],
        compiler_params=_params("parallel", "parallel"),
        name="token_mixer",
    )(sink, x, zt, zt, zt, zk, zk, zk, states_f, states_b, dmat, qdec,
      attn_norm, ret_gn, w_out, norm_post)


def _memkv_kernel(m_ref, g_ref, w_ref, kv_ref):
    mn = _rms(m_ref[0], g_ref[...]).astype(BF16)
    kv_ref[0] = _dot(mn, w_ref[...]).astype(BF16)


def _memory_kv(mem, gain, w_kv):
    B, M, _ = mem.shape
    return pl.pallas_call(
        _memkv_kernel,
        grid=(B,),
        in_specs=[pl.BlockSpec((1, M, D_MODEL), lambda b: (b, 0, 0)),
                  _const_spec((1, D_MODEL)),
                  _const_spec((D_MODEL, 2 * D_MODEL))],
        out_specs=pl.BlockSpec((1, M, 2 * D_MODEL), lambda b: (b, 0, 0)),
        out_shape=jax.ShapeDtypeStruct((B, M, 2 * D_MODEL), BF16),
        compiler_params=_params("parallel"),
        name="memory_kv",
    )(mem, gain, w_kv)


def _xattn_kernel(x_ref, gpre_ref, wq_ref, km_ref, vm_ref, wo_ref, gpost_ref,
                  o_ref, cat_ref):
    x = x_ref[0]
    h = _rms(x, gpre_ref[...]).astype(BF16)
    q = (_dot(h, wq_ref[...]) * (XA_HEAD_DIM ** -0.5)).astype(BF16)
    for hd in range(XA_HEADS):
        sl = slice(hd * XA_HEAD_DIM, (hd + 1) * XA_HEAD_DIM)
        s = lax.dot_general(q[:, sl], km_ref[0, :, sl], NT_DIMS, preferred_element_type=F32)
        p = jnp.exp(s - jnp.max(s, axis=-1, keepdims=True))
        den = jnp.sum(p, axis=-1, keepdims=True)
        cat_ref[:, sl] = (_dot(p.astype(BF16), vm_ref[0, :, sl]) * (1.0 / den)).astype(BF16)
    xo = _dot(cat_ref[...], wo_ref[...])
    o_ref[0] = x + _rms(xo, gpost_ref[...])


def _cross_attention(x, kv, gain_pre, w_q, w_o, gain_post):
    B, S, _ = x.shape
    M = kv.shape[1]
    tok = pl.BlockSpec((1, TOK_TILE, D_MODEL), lambda b, i: (b, i, 0))
    return pl.pallas_call(
        _xattn_kernel,
        grid=(B, S // TOK_TILE),
        in_specs=[tok,
                  _const_spec((1, D_MODEL)),
                  _const_spec((D_MODEL, D_MODEL)),
                  pl.BlockSpec((1, M, D_MODEL), lambda b, i: (b, 0, 0)),
                  pl.BlockSpec((1, M, D_MODEL), lambda b, i: (b, 0, 1)),
                  _const_spec((D_MODEL, D_MODEL)),
                  _const_spec((1, D_MODEL))],
        out_specs=tok,
        out_shape=jax.ShapeDtypeStruct((B, S, D_MODEL), F32),
        scratch_shapes=[pltpu.VMEM((TOK_TILE, D_MODEL), BF16)],
        compiler_params=_params("parallel", "parallel"),
        name="cross_attention",
    )(x, gain_pre, w_q, kv, kv, w_o, gain_post)


FF_CHUNK = 2 * MXU_TILE
FF_CHUNKS = [(c0, min(FF_CHUNK, D_FF - c0)) for c0 in range(0, D_FF, FF_CHUNK)]
assert all(w % MXU_TILE == 0 for _, w in FF_CHUNKS)


def _ffn_kernel(x_ref, gpre_ref, wgu_ref, wd_ref, gpost_ref, o_ref, gu_ref, act_ref):
    x = x_ref[0]
    h = _rms(x, gpre_ref[...]).astype(BF16)

    def gate_up_into(slot, j):
        c0, w = FF_CHUNKS[j]
        gu_ref[slot, 0, :, 0:w] = _dot(h, wgu_ref[:, c0:c0 + w])
        gu_ref[slot, 1, :, 0:w] = _dot(h, wgu_ref[:, D_FF + c0:D_FF + c0 + w])

    gate_up_into(0, 0)
    for j, (c0, w) in enumerate(FF_CHUNKS):
        if j + 1 < len(FF_CHUNKS):
            gate_up_into((j + 1) % 2, j + 1)
        gate = gu_ref[j % 2, 0, :, 0:w]
        up = gu_ref[j % 2, 1, :, 0:w]
        act_ref[:, c0:c0 + w] = (gate * jax.nn.sigmoid(gate) * up).astype(BF16)
    f = _dot(act_ref[...], wd_ref[...])
    o_ref[0] = x + _rms(f, gpost_ref[...])


def _ffn(x, gain_pre, w_gu, w_down, gain_post):
    B, S, _ = x.shape
    tok = pl.BlockSpec((1, TOK_TILE, D_MODEL), lambda b, i: (b, i, 0))
    return pl.pallas_call(
        _ffn_kernel,
        grid=(B, S // TOK_TILE),
        in_specs=[tok,
                  _const_spec((1, D_MODEL)),
                  _const_spec((D_MODEL, 2 * D_FF)),
                  _const_spec((D_FF, D_MODEL)),
                  _const_spec((1, D_MODEL))],
        out_specs=tok,
        out_shape=jax.ShapeDtypeStruct((B, S, D_MODEL), F32),
        scratch_shapes=[pltpu.VMEM((2, 2, TOK_TILE, FF_CHUNK), F32),
                        pltpu.VMEM((TOK_TILE, D_FF), BF16)],
        compiler_params=_params("parallel", "parallel"),
        name="swiglu_ffn",
    )(x, gain_pre, w_gu, w_down, gain_post)


def _attn_feature_order():
    order = []
    for j in range(ATTN_GROUP):
        for g in range(ATTN_KV_HEADS):
            h = g * ATTN_GROUP + j
            order += list(range(h * ATTN_HEAD_DIM, (h + 1) * ATTN_HEAD_DIM))
    return np.asarray(order, np.int32)


def kernel(x, mem, norm_mix_pre, norm_mix_post, w_in, attn_sink, attn_out_norm,
           ret_decay_fwd, ret_decay_bwd, ret_gn, w_out, norm_xa_pre, norm_xa_post,
           norm_mem, xa_wq, xa_wkv, xa_wo, norm_ffn_pre, norm_ffn_post,
           ffn_w_gu, ffn_w_down):
    B, S, D = x.shape
    assert D == D_MODEL and S % max(TOK_TILE, MIX_TILE, STATE_CHUNKS * BLOCK) == 0
    assert mem.shape[1] == N_MEM
    depth = w_in.shape[0]
    feat = _attn_feature_order()
    o_k = ATTN_WIDTH
    o_v = o_k + ATTN_KV_WIDTH
    o_r = o_v + ATTN_KV_WIDTH
    rw = RET_WIDTH

    rope = _rope_tables(S // TOK_TILE)

    def row(v):
        return v.reshape(1, -1).astype(F32)

    def col(v):
        return jnp.broadcast_to(v.astype(F32)[:, None], (v.shape[0], LANES))

    for l in range(depth):
        w = w_in[l]
        w_t = jnp.concatenate([w[:, :o_k][:, feat], w[:, o_r:o_r + rw], w[:, o_r + 2 * rw:],
                               w[:, o_v:o_r]], axis=1).T.astype(BF16)
        w_k = jnp.concatenate([w[:, o_r + rw:o_r + 2 * rw], w[:, o_k:o_v]], axis=1).astype(BF16)
        out_rows = np.concatenate([feat, np.arange(ATTN_WIDTH, ATTN_WIDTH + RET_WIDTH)])

        dmat, kdec, qdec, cdec = _decay_tables(ret_decay_fwd[l], ret_decay_bwd[l])
        zt, zk = _in_projection(x, row(norm_mix_pre[l]), w_t, w_k, rope)
        states_f, states_b = _retention_states(zt, zk, kdec, cdec)
        x = _mixer(x, zt, zk, states_f, states_b, dmat, qdec, attn_sink[l].astype(F32),
                   col(attn_out_norm[l][feat]), col(ret_gn[l]),
                   w_out[l][out_rows].astype(BF16), row(norm_mix_post[l]))
        kv = _memory_kv(mem, row(norm_mem[l]), xa_wkv[l].astype(BF16))
        x = _cross_attention(x, kv, row(norm_xa_pre[l]), xa_wq[l].astype(BF16),
                             xa_wo[l].astype(BF16), row(norm_xa_post[l]))
        x = _ffn(x, row(norm_ffn_pre[l]), ffn_w_gu[l].astype(BF16),
                 ffn_w_down[l].astype(BF16), row(norm_ffn_post[l]))
    return x
```

```python
import math

import numpy as np
import jax
import jax.numpy as jnp
from jax import lax
from jax.experimental import pallas as pl
from jax.experimental.pallas import tpu as pltpu

D_MODEL = 1024
N_MEM = 256
ATTN_HEADS = 8
ATTN_KV_HEADS = 2
ATTN_GROUP = ATTN_HEADS // ATTN_KV_HEADS
ATTN_HEAD_DIM = 64
ATTN_WIDTH = ATTN_HEADS * ATTN_HEAD_DIM
ATTN_KV_WIDTH = ATTN_KV_HEADS * ATTN_HEAD_DIM
BLOCK = 128
RET_HEADS = 4
RET_HEAD_DIM = 128
RET_WIDTH = RET_HEADS * RET_HEAD_DIM
IN_COLS = ATTN_WIDTH + 2 * ATTN_KV_WIDTH + 4 * RET_WIDTH
XA_HEADS = 4
XA_HEAD_DIM = D_MODEL // XA_HEADS
D_FF = -(-(8 * D_MODEL) // (3 * 256)) * 256
ROPE_THETA = 10000.0
EPS = 1e-6
LOG2E = math.log2(math.e)

LANES = 128
BF16_ROWS = 16
MXU_TILE = 256
VMEM_LIMIT_BYTES = 56 * 1024 * 1024

TOK_TILE = 512
CHUNKS_PER_TILE = TOK_TILE // BLOCK
STATE_CHUNKS = 8
MIX_TILE = 1024
MIX_CHUNKS = MIX_TILE // BLOCK

T_AQ, T_RQ, T_RV, T_RG, T_AV = 0, 512, 1024, 1536, 2048
T_ROWS = T_AV + ATTN_KV_WIDTH
K_RK, K_AK = 0, RET_WIDTH
K_COLS = RET_WIDTH + ATTN_KV_WIDTH

F32 = jnp.float32
BF16 = jnp.bfloat16
NT_DIMS = (((1,), (1,)), ((), ()))
TN_DIMS = (((0,), (0,)), ((), ()))


def _const_spec(shape):
    nd = len(shape)
    return pl.BlockSpec(shape, lambda *_: (0,) * nd, pipeline_mode=pl.Buffered(1))


def _params(*sem, flags=None):
    return pltpu.CompilerParams(dimension_semantics=sem,
                                vmem_limit_bytes=VMEM_LIMIT_BYTES, flags=flags)


def _rms(x, w):
    ms = jnp.mean(x * x, axis=-1, keepdims=True)
    return x * lax.rsqrt(ms + EPS) * w


def _dot(a, b):
    return jnp.dot(a, b, preferred_element_type=F32)


def _rope_tables(n_tiles):
    def parts(dim):
        inv_freq = ROPE_THETA ** (-np.arange(0, dim, 2, dtype=np.float64) / dim)
        base = (np.arange(n_tiles, dtype=np.float64) * TOK_TILE)[:, None] * inv_freq[None, :]
        off = np.arange(TOK_TILE, dtype=np.float64)[:, None] * inv_freq[None, :]
        return np.cos(base), np.sin(base), np.cos(off), np.sin(off)

    def tok(c, s, reps):
        cos = np.concatenate([c, c] * reps, axis=1)
        sin = np.concatenate([s, s] * reps, axis=1)
        return cos, sin, np.concatenate([-s, s] * reps, axis=1)

    cb_a, sb_a, co_a, so_a = parts(ATTN_HEAD_DIM)
    cb_r, sb_r, co_r, so_r = parts(RET_HEAD_DIM)
    tok_base = np.stack(tok(cb_r, sb_r, 1) + tok(cb_a, sb_a, 2)
                        + (np.zeros((n_tiles, LANES)),) * 2, axis=1)
    tok_off = np.stack(tok(co_r, so_r, 1) + tok(co_a, so_a, 2), axis=0)
    feat_base = np.concatenate([cb_a, sb_a, cb_r, sb_r], axis=1)
    feat_base = np.broadcast_to(feat_base[:, :, None], feat_base.shape + (BLOCK,))
    feat_off = np.concatenate([co_a, so_a, co_r, so_r], axis=1).T
    return tuple(jnp.asarray(t, F32) for t in (tok_base, tok_off, feat_base, feat_off))


def _decay_kernel(logit_ref, dmat_ref, kdec_ref, qdec_ref, cdec_ref):
    x = logit_ref[...]
    lg = jnp.minimum(x, 0.0) - jnp.log1p(jnp.exp(-jnp.abs(x)))
    ri = lax.broadcasted_iota(jnp.int32, (BLOCK, LANES), 0).astype(F32)
    ci = lax.broadcasted_iota(jnp.int32, (BLOCK, LANES), 1).astype(F32)
    c1 = ci[0:1, :]
    diff = ci - ri
    cdec_ref[...] = jnp.exp(BLOCK * lg)
    for h in range(RET_HEADS):
        lf = lg[h:h + 1, :]
        lb = lg[RET_HEADS + h:RET_HEADS + h + 1, :]
        dmat_ref[h] = jnp.exp(jnp.abs(diff) * jnp.where(diff >= 0, lf, lb))
        sl = slice(h * LANES, (h + 1) * LANES)
        kdec_ref[0, :, sl] = jnp.exp((BLOCK - 1 - ri) * lf)
        kdec_ref[1, :, sl] = jnp.exp(ri * lb)
        qdec_ref[h:h + 1, :] = jnp.exp((c1 + 1) * lf)
        qdec_ref[RET_HEADS + h:RET_HEADS + h + 1, :] = jnp.exp((BLOCK - c1) * lb)


def _decay_tables(decay_fwd, decay_bwd):
    logits = jnp.concatenate([decay_fwd, decay_bwd]).astype(F32)
    logits = jnp.broadcast_to(logits[:, None], (2 * RET_HEADS, LANES))
    return pl.pallas_call(
        _decay_kernel,
        out_shape=(jax.ShapeDtypeStruct((RET_HEADS, BLOCK, LANES), F32),
                   jax.ShapeDtypeStruct((2, BLOCK, RET_WIDTH), F32),
                   jax.ShapeDtypeStruct((2 * RET_HEADS, LANES), F32),
                   jax.ShapeDtypeStruct((2 * RET_HEADS, LANES), F32)),
        name="decay_tables",
    )(logits)


def _inproj_kernel(x_ref, g_ref, wt_ref, wk_ref, tb_ref, to_ref, fb_ref, fo_ref,
                   zt_ref, zk_ref):
    h = _rms(x_ref[0], g_ref[...]).astype(BF16)

    tb = tb_ref[0]

    def tok_tables(k):
        cb, sb, ssb = tb[3 * k:3 * k + 1], tb[3 * k + 1:3 * k + 2], tb[3 * k + 2:3 * k + 3]
        co, so, sso = to_ref[3 * k], to_ref[3 * k + 1], to_ref[3 * k + 2]
        return cb * co - sb * so, ssb * co + cb * sso

    fb = jnp.concatenate([fb_ref[0]] * CHUNKS_PER_TILE, axis=1)
    fo = fo_ref[...]

    def feat_tables(row, n):
        cb, sb = fb[row:row + n], fb[row + n:row + 2 * n]
        co, so = fo[row:row + n], fo[row + n:row + 2 * n]
        return cb * co - sb * so, sb * co + cb * so

    zk = _dot(h, wk_ref[...])
    ck, sk = tok_tables(0)
    for g in range(RET_HEADS):
        zg = zk[:, g * LANES:(g + 1) * LANES]
        r = (zg * ck + pltpu.roll(zg, LANES // 2, 1) * sk) * (RET_HEAD_DIM ** -0.5)
        zk_ref[0, :, K_RK + g * LANES:K_RK + (g + 1) * LANES] = r.astype(BF16)
    zg = zk[:, K_AK:K_AK + LANES]
    lane = lax.broadcasted_iota(jnp.int32, (1, LANES), 1)
    low_half = (lane % ATTN_HEAD_DIM) < (ATTN_HEAD_DIM // 2)
    half = ATTN_HEAD_DIM // 2
    rot = jnp.where(low_half, pltpu.roll(zg, LANES - half, 1), pltpu.roll(zg, half, 1))
    ca, sa = tok_tables(1)
    zk_ref[0, :, K_AK:K_AK + LANES] = (zg * ca + rot * sa).astype(BF16)

    def proj_t(row, n_rows):
        return lax.dot_general(wt_ref[row:row + n_rows, :], h, NT_DIMS,
                               preferred_element_type=F32)

    def store_t(row, val):
        for c in range(CHUNKS_PER_TILE):
            zt_ref[0, c, row:row + val.shape[0], :] = val[:, c * BLOCK:(c + 1) * BLOCK].astype(BF16)

    def rope_t(z, row, head_dim, cos, sin):
        hd2 = head_dim // 2
        for k in range(z.shape[0] // head_dim):
            x1 = z[k * head_dim:k * head_dim + hd2]
            x2 = z[k * head_dim + hd2:(k + 1) * head_dim]
            store_t(row + k * head_dim, x1 * cos - x2 * sin)
            store_t(row + k * head_dim + hd2, x2 * cos + x1 * sin)

    q_scale = (ATTN_HEAD_DIM ** -0.5) * LOG2E
    cat, sat = feat_tables(0, ATTN_HEAD_DIM // 2)
    rope_t(proj_t(T_AQ, ATTN_WIDTH), T_AQ, ATTN_HEAD_DIM, cat * q_scale, sat * q_scale)
    crt, srt = feat_tables(ATTN_HEAD_DIM, RET_HEAD_DIM // 2)
    rope_t(proj_t(T_RQ, RET_WIDTH), T_RQ, RET_HEAD_DIM, crt, srt)
    store_t(T_RV, proj_t(T_RV, RET_WIDTH))
    store_t(T_RG, proj_t(T_RG, RET_WIDTH))
    store_t(T_AV, proj_t(T_AV, ATTN_KV_WIDTH))


def _in_projection(x, gain, w_t, w_k, rope):
    B, S, _ = x.shape
    n = S // BLOCK
    tok_base, tok_off, feat_base, feat_off = rope
    return pl.pallas_call(
        _inproj_kernel,
        grid=(S // TOK_TILE, B),
        in_specs=[pl.BlockSpec((1, TOK_TILE, D_MODEL), lambda i, b: (b, i, 0)),
                  _const_spec((1, D_MODEL)),
                  _const_spec((T_ROWS, D_MODEL)),
                  _const_spec((D_MODEL, K_COLS)),
                  pl.BlockSpec((1,) + tok_base.shape[1:], lambda i, b: (i, 0, 0)),
                  _const_spec(tok_off.shape),
                  pl.BlockSpec((1,) + feat_base.shape[1:], lambda i, b: (i, 0, 0)),
                  _const_spec(feat_off.shape)],
        out_specs=(pl.BlockSpec((1, CHUNKS_PER_TILE, T_ROWS, BLOCK), lambda i, b: (b, i, 0, 0)),
                   pl.BlockSpec((1, TOK_TILE, K_COLS), lambda i, b: (b, i, 0))),
        out_shape=(jax.ShapeDtypeStruct((B, n, T_ROWS, BLOCK), BF16),
                   jax.ShapeDtypeStruct((B, S, K_COLS), BF16)),
        compiler_params=_params("parallel", "parallel"),
        name="in_projection",
    )(x, gain, w_t, w_k, tok_base, tok_off, feat_base, feat_off)


def _state_kernel(kf_ref, vf_ref, kb_ref, vb_ref, kdec_ref, cdec_ref,
                  pf_ref, pb_ref, st_ref):
    @pl.when(pl.program_id(1) == 0)
    def _():
        st_ref[...] = jnp.zeros_like(st_ref)

    for i in range(STATE_CHUNKS):
        for d, (k_ref, v_ref, p_ref, c) in enumerate(((kf_ref, vf_ref, pf_ref, i),
                                                      (kb_ref, vb_ref, pb_ref, STATE_CHUNKS - 1 - i))):
            kd = (k_ref[0, c * BLOCK:(c + 1) * BLOCK, :].astype(F32) * kdec_ref[d]).astype(BF16)
            for h in range(RET_HEADS):
                sl = slice(h * LANES, (h + 1) * LANES)
                st = st_ref[d, h]
                p_ref[0, c, h] = st.astype(BF16)
                kv = _dot(v_ref[0, c, sl, :], kd[:, sl])
                row = d * RET_HEADS + h
                st_ref[d, h] = st * cdec_ref[row:row + 1, :] + kv


def _retention_states(zt, zk, kdec, cdec):
    B, n = zt.shape[0], zt.shape[1]
    ng = n // STATE_CHUNKS
    k_blk = (1, STATE_CHUNKS * BLOCK, RET_WIDTH)
    v_blk = (1, STATE_CHUNKS, RET_WIDTH, BLOCK)
    vrow = T_RV // RET_WIDTH
    st_blk = (1, STATE_CHUNKS, RET_HEADS, RET_HEAD_DIM, RET_HEAD_DIM)
    st_shape = jax.ShapeDtypeStruct((B, n, RET_HEADS, RET_HEAD_DIM, RET_HEAD_DIM), BF16)
    return pl.pallas_call(
        _state_kernel,
        grid=(B, ng),
        in_specs=[pl.BlockSpec(k_blk, lambda b, t: (b, t, 0)),
                  pl.BlockSpec(v_blk, lambda b, t: (b, t, vrow, 0)),
                  pl.BlockSpec(k_blk, lambda b, t: (b, ng - 1 - t, 0)),
                  pl.BlockSpec(v_blk, lambda b, t: (b, ng - 1 - t, vrow, 0)),
                  _const_spec((2, BLOCK, RET_WIDTH)),
                  _const_spec((2 * RET_HEADS, LANES))],
        out_specs=(pl.BlockSpec(st_blk, lambda b, t: (b, t, 0, 0, 0)),
                   pl.BlockSpec(st_blk, lambda b, t: (b, ng - 1 - t, 0, 0, 0))),
        out_shape=(st_shape, st_shape),
        scratch_shapes=[pltpu.VMEM((2, RET_HEADS, RET_HEAD_DIM, RET_HEAD_DIM), F32)],
        compiler_params=_params("parallel", "arbitrary"),
        name="retention_states",
    )(zk, zt, zk, zt, kdec, cdec)


def _mixer_kernel(sink_ref, x_ref, zt_ref, avp_ref, avn_ref, zk_ref, akp_ref, akn_ref,
                  pf_ref, pb_ref, dmat_ref, qdec_ref,
                  an_ref, gn_ref, wout_ref, npost_ref, o_ref, kext_ref, vext_ref, mixt_ref,
                  sc_ref):
    tile = pl.program_id(1)
    n_blocks = pl.num_programs(1) * MIX_CHUNKS

    lane = lax.broadcasted_iota(jnp.int32, (1, LANES), 1)
    for g in range(ATTN_KV_HEADS):
        km = jnp.where((lane // ATTN_HEAD_DIM) == g, 1.0, 0.0).astype(BF16)
        kext_ref[g, 0:BLOCK] = akp_ref[0] * km
        kext_ref[g, BLOCK:BLOCK + MIX_TILE] = zk_ref[0, :, K_AK:K_AK + ATTN_KV_WIDTH] * km
        kext_ref[g, BLOCK + MIX_TILE:] = akn_ref[0] * km
    vext_ref[0] = avp_ref[0, 0]
    for c in range(MIX_CHUNKS):
        vext_ref[1 + c] = zt_ref[0, c, T_AV:T_AV + ATTN_KV_WIDTH, :]
    vext_ref[MIX_CHUNKS + 1] = avn_ref[0, 0]

    kk = lax.broadcasted_iota(jnp.int32, (BLOCK, LANES), 0)
    qq = lax.broadcasted_iota(jnp.int32, (BLOCK, LANES), 1)
    neg = jnp.full((BLOCK, LANES), -jnp.inf, F32)
    zero = jnp.zeros((BLOCK, LANES), F32)

    def scores_into(slot, c):
        q = jnp.concatenate([zt_ref[0, c, T_AQ + j * LANES:T_AQ + (j + 1) * LANES, :]
                             for j in range(ATTN_GROUP)], axis=1)
        r0 = c * BLOCK
        for g in range(ATTN_KV_HEADS):
            sc_ref[slot, g] = _dot(kext_ref[g, pl.ds(r0, 3 * BLOCK), :], q)

    def chunk(c, has_next):
        r0 = c * BLOCK
        blk = tile * MIX_CHUNKS + c
        slot = c % 2
        if has_next:
            scores_into(1 - slot, c + 1)

        vw = jnp.concatenate([vext_ref[c], vext_ref[c + 1], vext_ref[c + 2]], axis=1)
        no_prev = jnp.where(blk > 0, 0, BLOCK)
        no_next = jnp.where(blk < n_blocks - 1, 0, BLOCK)
        bias_prev = jnp.where(kk >= qq + no_prev, zero, neg)
        bias_next = jnp.where(kk <= qq - no_next, zero, neg)
        bias_prev = jnp.concatenate([bias_prev] * ATTN_GROUP, axis=1)
        bias_next = jnp.concatenate([bias_next] * ATTN_GROUP, axis=1)
        heads = []
        for h in range(RET_HEADS):
            qt = zt_ref[0, c, T_RQ + h * LANES:T_RQ + (h + 1) * LANES, :]
            kh = zk_ref[0, pl.ds(r0, BLOCK), K_RK + h * LANES:K_RK + (h + 1) * LANES]
            heads.append((qt, _dot(kh, qt)))

        ones = jnp.ones((BF16_ROWS, 3 * BLOCK), BF16)
        outs = []
        for g in range(ATTN_KV_HEADS):
            s = sc_ref[slot, g]
            s = jnp.concatenate([s[0:BLOCK] + bias_prev, s[BLOCK:2 * BLOCK],
                                 s[2 * BLOCK:] + bias_next], axis=0)
            sk = jnp.concatenate(
                [jnp.full((1, LANES), sink_ref[g * ATTN_GROUP + j] * LOG2E, F32)
                 for j in range(ATTN_GROUP)], axis=1)
            m = jnp.maximum(jnp.max(s, axis=0, keepdims=True), sk)
            p = jnp.exp2(s - m).astype(BF16)
            vg = jnp.concatenate([vw[g * ATTN_HEAD_DIM:(g + 1) * ATTN_HEAD_DIM, :], ones], axis=0)
            ov = _dot(vg, p)
            den = ov[ATTN_HEAD_DIM:ATTN_HEAD_DIM + 1, :] + jnp.exp2(sk - m)
            outs.append(ov[0:ATTN_HEAD_DIM, :] * (1.0 / den))
        ys = []
        for h in range(RET_HEADS):
            qt, pt = heads[h]
            y = _dot(zt_ref[0, c, T_RV + h * LANES:T_RV + (h + 1) * LANES, :],
                     (pt * dmat_ref[h]).astype(BF16))
            qf = qt.astype(F32)
            qd = jnp.concatenate(
                [(qf * qdec_ref[h:h + 1, :]).astype(BF16),
                 (qf * qdec_ref[RET_HEADS + h:RET_HEADS + h + 1, :]).astype(BF16)], axis=0)
            st = jnp.concatenate([pf_ref[0, c, h], pb_ref[0, c, h]], axis=1)
            ys.append(y + _dot(st, qd))

        a = jnp.concatenate([outs[g][:, j * LANES:(j + 1) * LANES]
                             for j in range(ATTN_GROUP) for g in range(ATTN_KV_HEADS)], axis=0)
        ms = jnp.sum(a * a, axis=0, keepdims=True) * (1.0 / ATTN_WIDTH)
        mixt_ref[c, 0:ATTN_WIDTH, :] = (a * lax.rsqrt(ms + EPS) * an_ref[...]).astype(BF16)
        for h in range(RET_HEADS):
            sl = slice(h * LANES, (h + 1) * LANES)
            y = ys[h]
            mu = jnp.mean(y, axis=0, keepdims=True)
            yc = y - mu
            var = jnp.mean(yc * yc, axis=0, keepdims=True)
            yn = yc * lax.rsqrt(var + EPS) * gn_ref[sl, :]
            gate = zt_ref[0, c, T_RG + h * LANES:T_RG + (h + 1) * LANES, :].astype(F32)
            r = gate * jax.nn.sigmoid(gate) * yn
            mixt_ref[c, ATTN_WIDTH + h * LANES:ATTN_WIDTH + (h + 1) * LANES, :] = r.astype(BF16)

    scores_into(0, 0)
    for c in range(MIX_CHUNKS):
        chunk(c, c + 1 < MIX_CHUNKS)

    mixt = jnp.concatenate([mixt_ref[c] for c in range(MIX_CHUNKS)], axis=1)
    mix = lax.dot_general(mixt, wout_ref[...], TN_DIMS, preferred_element_type=F32)
    o_ref[0] = x_ref[0] + _rms(mix, npost_ref[...])


def _mixer(x, zt, zk, states_f, states_b, dmat, qdec, sink, attn_norm, ret_gn, w_out, norm_post):
    B, S, _ = x.shape
    n = S // BLOCK
    cpt = MIX_CHUNKS
    edge_t = (1, 1, ATTN_KV_WIDTH, BLOCK)
    edge_k = (1, BLOCK, ATTN_KV_WIDTH)
    v_row = T_AV // ATTN_KV_WIDTH
    k_col = K_AK // ATTN_KV_WIDTH
    st_blk = (1, cpt, RET_HEADS, RET_HEAD_DIM, RET_HEAD_DIM)

    def prev(i):
        return jnp.maximum(i * cpt - 1, 0)

    def nxt(i):
        return jnp.minimum((i + 1) * cpt, n - 1)

    return pl.pallas_call(
        _mixer_kernel,
        grid=(B, S // MIX_TILE),
        in_specs=[pl.BlockSpec(memory_space=pltpu.SMEM),
                  pl.BlockSpec((1, MIX_TILE, D_MODEL), lambda b, i: (b, i, 0)),
                  pl.BlockSpec((1, cpt, T_ROWS, BLOCK), lambda b, i: (b, i, 0, 0)),
                  pl.BlockSpec(edge_t, lambda b, i: (b, prev(i), v_row, 0)),
                  pl.BlockSpec(edge_t, lambda b, i: (b, nxt(i), v_row, 0)),
                  pl.BlockSpec((1, MIX_TILE, K_COLS), lambda b, i: (b, i, 0)),
                  pl.BlockSpec(edge_k, lambda b, i: (b, prev(i), k_col)),
                  pl.BlockSpec(edge_k, lambda b, i: (b, nxt(i), k_col)),
                  pl.BlockSpec(st_blk, lambda b, i: (b, i, 0, 0, 0)),
                  pl.BlockSpec(st_blk, lambda b, i: (b, i, 0, 0, 0)),
                  _const_spec((RET_HEADS, BLOCK, LANES)),
                  _const_spec((2 * RET_HEADS, LANES)),
                  _const_spec((ATTN_WIDTH, LANES)),
                  _const_spec((RET_WIDTH, LANES)),
                  _const_spec((D_MODEL, D_MODEL)),
                  _const_spec((1, D_MODEL))],
        out_specs=pl.BlockSpec((1, MIX_TILE, D_MODEL), lambda b, i: (b, i, 0)),
        out_shape=jax.ShapeDtypeStruct((B, S, D_MODEL), F32),
        scratch_shapes=[pltpu.VMEM((ATTN_KV_HEADS, MIX_TILE + 2 * BLOCK, ATTN_KV_WIDTH), BF16),
                        pltpu.VMEM((cpt + 2, ATTN_KV_WIDTH, BLOCK), BF16),
                        pltpu.VMEM((cpt, D_MODEL, BLOCK), BF16),
                        pltpu.VMEM((2, ATTN_KV_HEADS, 3 * BLOCK, ATTN_GROUP * LANES), F32)],
        compiler_params=_params("parallel", "parallel"),
        name="token_mixer",
    )(sink, x, zt, zt, zt, zk, zk, zk, states_f, states_b, dmat, qdec,
      attn_norm, ret_gn, w_out, norm_post)


def _memkv_kernel(m_ref, g_ref, w_ref, kv_ref):
    mn = _rms(m_ref[0], g_ref[...]).astype(BF16)
    kv_ref[0] = _dot(mn, w_ref[...]).astype(BF16)


def _memory_kv(mem, gain, w_kv):
    B, M, _ = mem.shape
    return pl.pallas_call(
        _memkv_kernel,
        grid=(B,),
        in_specs=[pl.BlockSpec((1, M, D_MODEL), lambda b: (b, 0, 0)),
                  _const_spec((1, D_MODEL)),
                  _const_spec((D_MODEL, 2 * D_MODEL))],
        out_specs=pl.BlockSpec((1, M, 2 * D_MODEL), lambda b: (b, 0, 0)),
        out_shape=jax.ShapeDtypeStruct((B, M, 2 * D_MODEL), BF16),
        compiler_params=_params("parallel"),
        name="memory_kv",
    )(mem, gain, w_kv)


def _xattn_kernel(x_ref, gpre_ref, wq_ref, km_ref, vm_ref, wo_ref, gpost_ref,
                  o_ref, cat_ref):
    x = x_ref[0]
    h = _rms(x, gpre_ref[...]).astype(BF16)
    q = (_dot(h, wq_ref[...]) * (XA_HEAD_DIM ** -0.5)).astype(BF16)
    for hd in range(XA_HEADS):
        sl = slice(hd * XA_HEAD_DIM, (hd + 1) * XA_HEAD_DIM)
        s = lax.dot_general(q[:, sl], km_ref[0, :, sl], NT_DIMS, preferred_element_type=F32)
        p = jnp.exp(s - jnp.max(s, axis=-1, keepdims=True))
        den = jnp.sum(p, axis=-1, keepdims=True)
        cat_ref[:, sl] = (_dot(p.astype(BF16), vm_ref[0, :, sl]) * (1.0 / den)).astype(BF16)
    xo = _dot(cat_ref[...], wo_ref[...])
    o_ref[0] = x + _rms(xo, gpost_ref[...])


def _cross_attention(x, kv, gain_pre, w_q, w_o, gain_post):
    B, S, _ = x.shape
    M = kv.shape[1]
    tok = pl.BlockSpec((1, TOK_TILE, D_MODEL), lambda b, i: (b, i, 0))
    return pl.pallas_call(
        _xattn_kernel,
        grid=(B, S // TOK_TILE),
        in_specs=[tok,
                  _const_spec((1, D_MODEL)),
                  _const_spec((D_MODEL, D_MODEL)),
                  pl.BlockSpec((1, M, D_MODEL), lambda b, i: (b, 0, 0)),
                  pl.BlockSpec((1, M, D_MODEL), lambda b, i: (b, 0, 1)),
                  _const_spec((D_MODEL, D_MODEL)),
                  _const_spec((1, D_MODEL))],
        out_specs=tok,
        out_shape=jax.ShapeDtypeStruct((B, S, D_MODEL), F32),
        scratch_shapes=[pltpu.VMEM((TOK_TILE, D_MODEL), BF16)],
        compiler_params=_params("parallel", "parallel"),
        name="cross_attention",
    )(x, gain_pre, w_q, kv, kv, w_o, gain_post)


FF_CHUNK = 2 * MXU_TILE
FF_CHUNKS = [(c0, min(FF_CHUNK, D_FF - c0)) for c0 in range(0, D_FF, FF_CHUNK)]
assert all(w % MXU_TILE == 0 for _, w in FF_CHUNKS)


def _ffn_kernel(x_ref, gpre_ref, wgu_ref, wd_ref, gpost_ref, o_ref, gu_ref, act_ref):
    x = x_ref[0]
    h = _rms(x, gpre_ref[...]).astype(BF16)

    def gate_up_into(slot, j):
        c0, w = FF_CHUNKS[j]
        gu_ref[slot, 0, :, 0:w] = _dot(h, wgu_ref[:, c0:c0 + w])
        gu_ref[slot, 1, :, 0:w] = _dot(h, wgu_ref[:, D_FF + c0:D_FF + c0 + w])

    gate_up_into(0, 0)
    for j, (c0, w) in enumerate(FF_CHUNKS):
        if j + 1 < len(FF_CHUNKS):
            gate_up_into((j + 1) % 2, j + 1)
        gate = gu_ref[j % 2, 0, :, 0:w]
        up = gu_ref[j % 2, 1, :, 0:w]
        act_ref[:, c0:c0 + w] = (gate * jax.nn.sigmoid(gate) * up).astype(BF16)
    f = _dot(act_ref[...], wd_ref[...])
    o_ref[0] = x + _rms(f, gpost_ref[...])


def _ffn(x, gain_pre, w_gu, w_down, gain_post):
    B, S, _ = x.shape
    tok = pl.BlockSpec((1, TOK_TILE, D_MODEL), lambda b, i: (b, i, 0))
    return pl.pallas_call(
        _ffn_kernel,
        grid=(B, S // TOK_TILE),
        in_specs=[tok,
                  _const_spec((1, D_MODEL)),
                  _const_spec((D_MODEL, 2 * D_FF)),
                  _const_spec((D_FF, D_MODEL)),
                  _const_spec((1, D_MODEL))],
        out_specs=tok,
        out_shape=jax.ShapeDtypeStruct((B, S, D_MODEL), F32),
        scratch_shapes=[pltpu.VMEM((2, 2, TOK_TILE, FF_CHUNK), F32),
                        pltpu.VMEM((TOK_TILE, D_FF), BF16)],
        compiler_params=_params("parallel", "parallel"),
        name="swiglu_ffn",
    )(x, gain_pre, w_gu, w_down, gain_post)


def _pair_heads(a, axis):
    shape = a.shape
    a = a.reshape(shape[:axis] + (ATTN_KV_HEADS, ATTN_GROUP, ATTN_HEAD_DIM) + shape[axis + 1:])
    return jnp.swapaxes(a, axis, axis + 1).reshape(shape)


def kernel(x, mem, norm_mix_pre, norm_mix_post, w_in, attn_sink, attn_out_norm,
           ret_decay_fwd, ret_decay_bwd, ret_gn, w_out, norm_xa_pre, norm_xa_post,
           norm_mem, xa_wq, xa_wkv, xa_wo, norm_ffn_pre, norm_ffn_post,
           ffn_w_gu, ffn_w_down):
    B, S, D = x.shape
    assert D == D_MODEL and S % max(TOK_TILE, MIX_TILE, STATE_CHUNKS * BLOCK) == 0
    assert mem.shape[1] == N_MEM
    depth = w_in.shape[0]
    o_k = ATTN_WIDTH
    o_v = o_k + ATTN_KV_WIDTH
    o_r = o_v + ATTN_KV_WIDTH
    rw = RET_WIDTH

    rope = _rope_tables(S // TOK_TILE)

    def row(v):
        return v.reshape(1, -1).astype(F32)

    def col(v):
        return jnp.broadcast_to(v.astype(F32)[:, None], (v.shape[0], LANES))

    for l in range(depth):
        w = w_in[l]
        w_t = jnp.concatenate([_pair_heads(w[:, :o_k], 1), w[:, o_r:o_r + rw], w[:, o_r + 2 * rw:],
                               w[:, o_v:o_r]], axis=1).T.astype(BF16)
        w_k = jnp.concatenate([w[:, o_r + rw:o_r + 2 * rw], w[:, o_k:o_v]], axis=1).astype(BF16)
        w_o = jnp.concatenate([_pair_heads(w_out[l][:ATTN_WIDTH], 0), w_out[l][ATTN_WIDTH:]], axis=0)

        dmat, kdec, qdec, cdec = _decay_tables(ret_decay_fwd[l], ret_decay_bwd[l])
        zt, zk = _in_projection(x, row(norm_mix_pre[l]), w_t, w_k, rope)
        states_f, states_b = _retention_states(zt, zk, kdec, cdec)
        x = _mixer(x, zt, zk, states_f, states_b, dmat, qdec, attn_sink[l].astype(F32),
                   col(_pair_heads(attn_out_norm[l], 0)), col(ret_gn[l]),
                   w_o.astype(BF16), row(norm_mix_post[l]))
        kv = _memory_kv(mem, row(norm_mem[l]), xa_wkv[l].astype(BF16))
        x = _cross_attention(x, kv, row(norm_xa_pre[l]), xa_wq[l].astype(BF16),
                             xa_wo[l].astype(BF16), row(norm_xa_post[l]))
        x = _ffn(x, row(norm_ffn_pre[l]), ffn_w_gu[l].astype(BF16),
                 ffn_w_down[l].astype(BF16), row(norm_ffn_post[l]))
    return x
```

```python
import math

import numpy as np
import jax
import jax.numpy as jnp
from jax import lax
from jax.experimental import pallas as pl
from jax.experimental.pallas import tpu as pltpu

D_MODEL = 1024
N_MEM = 256
ATTN_HEADS = 8
ATTN_KV_HEADS = 2
ATTN_GROUP = ATTN_HEADS // ATTN_KV_HEADS
ATTN_HEAD_DIM = 64
ATTN_WIDTH = ATTN_HEADS * ATTN_HEAD_DIM
ATTN_KV_WIDTH = ATTN_KV_HEADS * ATTN_HEAD_DIM
BLOCK = 128
RET_HEADS = 4
RET_HEAD_DIM = 128
RET_WIDTH = RET_HEADS * RET_HEAD_DIM
IN_COLS = ATTN_WIDTH + 2 * ATTN_KV_WIDTH + 4 * RET_WIDTH
XA_HEADS = 4
XA_HEAD_DIM = D_MODEL // XA_HEADS
D_FF = -(-(8 * D_MODEL) // (3 * 256)) * 256
ROPE_THETA = 10000.0
EPS = 1e-6
LOG2E = math.log2(math.e)

LANES = 128
BF16_ROWS = 16
MXU_TILE = 256
VMEM_LIMIT_BYTES = 56 * 1024 * 1024

TOK_TILE = 512
CHUNKS_PER_TILE = TOK_TILE // BLOCK
STATE_CHUNKS = 8
MIX_TILE = 1024
MIX_CHUNKS = MIX_TILE // BLOCK

T_AQ, T_RQ, T_RV, T_RG, T_AV = 0, 512, 1024, 1536, 2048
T_ROWS = T_AV + ATTN_KV_WIDTH
K_RK, K_AK = 0, RET_WIDTH
K_COLS = RET_WIDTH + ATTN_KV_WIDTH

F32 = jnp.float32
BF16 = jnp.bfloat16
NT_DIMS = (((1,), (1,)), ((), ()))
TN_DIMS = (((0,), (0,)), ((), ()))


def _const_spec(shape):
    nd = len(shape)
    return pl.BlockSpec(shape, lambda *_: (0,) * nd, pipeline_mode=pl.Buffered(1))


def _params(*sem, flags=None):
    return pltpu.CompilerParams(dimension_semantics=sem,
                                vmem_limit_bytes=VMEM_LIMIT_BYTES, flags=flags)


def _rms(x, w):
    ms = jnp.mean(x * x, axis=-1, keepdims=True)
    return x * lax.rsqrt(ms + EPS) * w


def _dot(a, b):
    return jnp.dot(a, b, preferred_element_type=F32)


def _rope_tables(n_tiles):
    def parts(dim):
        inv_freq = ROPE_THETA ** (-np.arange(0, dim, 2, dtype=np.float64) / dim)
        base = (np.arange(n_tiles, dtype=np.float64) * TOK_TILE)[:, None] * inv_freq[None, :]
        off = np.arange(TOK_TILE, dtype=np.float64)[:, None] * inv_freq[None, :]
        return np.cos(base), np.sin(base), np.cos(off), np.sin(off)

    def tok(c, s):
        return (np.concatenate([c, c], axis=1), np.concatenate([s, s], axis=1),
                np.concatenate([-s, s], axis=1))

    cb_a, sb_a, co_a, so_a = parts(ATTN_HEAD_DIM)
    cb_r, sb_r, co_r, so_r = parts(RET_HEAD_DIM)
    tok_base = np.stack(tok(cb_r, sb_r) + (np.zeros((n_tiles, LANES)),) * 5, axis=1)
    tok_off = np.stack(tok(co_r, so_r), axis=0)
    feat_base = np.concatenate([cb_a, sb_a, cb_r, sb_r], axis=1)
    feat_base = np.broadcast_to(feat_base[:, :, None], feat_base.shape + (BLOCK,))
    feat_off = np.concatenate([co_a, so_a, co_r, so_r], axis=1).T
    return tuple(jnp.asarray(t, F32) for t in (tok_base, tok_off, feat_base, feat_off))


def _decay_kernel(logit_ref, dmat_ref, kdec_ref, qdec_ref, cdec_ref):
    x = logit_ref[...]
    lg = jnp.minimum(x, 0.0) - jnp.log1p(jnp.exp(-jnp.abs(x)))
    ri = lax.broadcasted_iota(jnp.int32, (BLOCK, LANES), 0).astype(F32)
    ci = lax.broadcasted_iota(jnp.int32, (BLOCK, LANES), 1).astype(F32)
    c1 = ci[0:1, :]
    diff = ci - ri
    cdec_ref[...] = jnp.exp(BLOCK * lg)
    for h in range(RET_HEADS):
        lf = lg[h:h + 1, :]
        lb = lg[RET_HEADS + h:RET_HEADS + h + 1, :]
        dmat_ref[h] = jnp.exp(jnp.abs(diff) * jnp.where(diff >= 0, lf, lb))
        sl = slice(h * LANES, (h + 1) * LANES)
        kdec_ref[0, :, sl] = jnp.exp((BLOCK - 1 - ri) * lf)
        kdec_ref[1, :, sl] = jnp.exp(ri * lb)
        qdec_ref[h:h + 1, :] = jnp.exp((c1 + 1) * lf)
        qdec_ref[RET_HEADS + h:RET_HEADS + h + 1, :] = jnp.exp((BLOCK - c1) * lb)


def _decay_tables(decay_fwd, decay_bwd):
    logits = jnp.concatenate([decay_fwd, decay_bwd]).astype(F32)
    logits = jnp.broadcast_to(logits[:, None], (2 * RET_HEADS, LANES))
    return pl.pallas_call(
        _decay_kernel,
        out_shape=(jax.ShapeDtypeStruct((RET_HEADS, BLOCK, LANES), F32),
                   jax.ShapeDtypeStruct((2, BLOCK, RET_WIDTH), F32),
                   jax.ShapeDtypeStruct((2 * RET_HEADS, LANES), F32),
                   jax.ShapeDtypeStruct((2 * RET_HEADS, LANES), F32)),
        name="decay_tables",
    )(logits)


def _inproj_kernel(x_ref, g_ref, wt_ref, wk_ref, tb_ref, to_ref, fb_ref, fo_ref,
                   zt_ref, zk_ref):
    h = _rms(x_ref[0], g_ref[...]).astype(BF16)

    tb = tb_ref[0]

    ck = tb[0:1] * to_ref[0] - tb[1:2] * to_ref[1]
    sk = tb[2:3] * to_ref[0] + tb[0:1] * to_ref[2]

    fb = jnp.concatenate([fb_ref[0]] * CHUNKS_PER_TILE, axis=1)
    fo = fo_ref[...]

    def feat_tables(row, n):
        cb, sb = fb[row:row + n], fb[row + n:row + 2 * n]
        co, so = fo[row:row + n], fo[row + n:row + 2 * n]
        return cb * co - sb * so, sb * co + cb * so

    zk = _dot(h, wk_ref[...])
    for g in range(RET_HEADS):
        zg = zk[:, g * LANES:(g + 1) * LANES]
        r = (zg * ck + pltpu.roll(zg, LANES // 2, 1) * sk) * (RET_HEAD_DIM ** -0.5)
        zk_ref[0, :, K_RK + g * LANES:K_RK + (g + 1) * LANES] = r.astype(BF16)

    def proj_t(row, n_rows):
        return lax.dot_general(wt_ref[row:row + n_rows, :], h, NT_DIMS,
                               preferred_element_type=F32)

    def store_t(row, val):
        for c in range(CHUNKS_PER_TILE):
            zt_ref[0, c, row:row + val.shape[0], :] = val[:, c * BLOCK:(c + 1) * BLOCK].astype(BF16)

    def rope_t(z, row, head_dim, cos, sin):
        hd2 = head_dim // 2
        for k in range(z.shape[0] // head_dim):
            x1 = z[k * head_dim:k * head_dim + hd2]
            x2 = z[k * head_dim + hd2:(k + 1) * head_dim]
            store_t(row + k * head_dim, x1 * cos - x2 * sin)
            store_t(row + k * head_dim + hd2, x2 * cos + x1 * sin)

    q_scale = (ATTN_HEAD_DIM ** -0.5) * LOG2E
    cat, sat = feat_tables(0, ATTN_HEAD_DIM // 2)
    rope_t(proj_t(T_AQ, ATTN_WIDTH), T_AQ, ATTN_HEAD_DIM, cat * q_scale, sat * q_scale)
    crt, srt = feat_tables(ATTN_HEAD_DIM, RET_HEAD_DIM // 2)
    rope_t(proj_t(T_RQ, RET_WIDTH), T_RQ, RET_HEAD_DIM, crt, srt)
    store_t(T_RV, proj_t(T_RV, RET_WIDTH))
    store_t(T_RG, proj_t(T_RG, RET_WIDTH))
    kv = proj_t(T_AV, 2 * ATTN_KV_WIDTH)
    store_t(T_AV, kv[0:ATTN_KV_WIDTH])
    hd2 = ATTN_HEAD_DIM // 2
    rows = []
    for k in range(ATTN_KV_HEADS):
        r0 = ATTN_KV_WIDTH + k * ATTN_HEAD_DIM
        x1, x2 = kv[r0:r0 + hd2], kv[r0 + hd2:r0 + ATTN_HEAD_DIM]
        rows += [x1 * cat - x2 * sat, x2 * cat + x1 * sat]
    zk_ref[0, :, K_AK:K_AK + ATTN_KV_WIDTH] = jnp.concatenate(rows, axis=0).T.astype(BF16)


def _in_projection(x, gain, w_t, w_k, rope):
    B, S, _ = x.shape
    n = S // BLOCK
    tok_base, tok_off, feat_base, feat_off = rope
    return pl.pallas_call(
        _inproj_kernel,
        grid=(S // TOK_TILE, B),
        in_specs=[pl.BlockSpec((1, TOK_TILE, D_MODEL), lambda i, b: (b, i, 0)),
                  _const_spec((1, D_MODEL)),
                  _const_spec((T_ROWS + ATTN_KV_WIDTH, D_MODEL)),
                  _const_spec((D_MODEL, RET_WIDTH)),
                  pl.BlockSpec((1,) + tok_base.shape[1:], lambda i, b: (i, 0, 0)),
                  _const_spec(tok_off.shape),
                  pl.BlockSpec((1,) + feat_base.shape[1:], lambda i, b: (i, 0, 0)),
                  _const_spec(feat_off.shape)],
        out_specs=(pl.BlockSpec((1, CHUNKS_PER_TILE, T_ROWS, BLOCK), lambda i, b: (b, i, 0, 0)),
                   pl.BlockSpec((1, TOK_TILE, K_COLS), lambda i, b: (b, i, 0))),
        out_shape=(jax.ShapeDtypeStruct((B, n, T_ROWS, BLOCK), BF16),
                   jax.ShapeDtypeStruct((B, S, K_COLS), BF16)),
        compiler_params=_params("parallel", "parallel"),
        name="in_projection",
    )(x, gain, w_t, w_k, tok_base, tok_off, feat_base, feat_off)


def _state_kernel(kf_ref, vf_ref, kb_ref, vb_ref, kdec_ref, cdec_ref,
                  pf_ref, pb_ref, st_ref):
    @pl.when(pl.program_id(1) == 0)
    def _():
        st_ref[...] = jnp.zeros_like(st_ref)

    for i in range(STATE_CHUNKS):
        for d, (k_ref, v_ref, p_ref, c) in enumerate(((kf_ref, vf_ref, pf_ref, i),
                                                      (kb_ref, vb_ref, pb_ref, STATE_CHUNKS - 1 - i))):
            kd = (k_ref[0, c * BLOCK:(c + 1) * BLOCK, :].astype(F32) * kdec_ref[d]).astype(BF16)
            for h in range(RET_HEADS):
                sl = slice(h * LANES, (h + 1) * LANES)
                st = st_ref[d, h]
                p_ref[0, c, h] = st.astype(BF16)
                kv = _dot(v_ref[0, c, sl, :], kd[:, sl])
                row = d * RET_HEADS + h
                st_ref[d, h] = st * cdec_ref[row:row + 1, :] + kv


def _retention_states(zt, zk, kdec, cdec):
    B, n = zt.shape[0], zt.shape[1]
    ng = n // STATE_CHUNKS
    k_blk = (1, STATE_CHUNKS * BLOCK, RET_WIDTH)
    v_blk = (1, STATE_CHUNKS, RET_WIDTH, BLOCK)
    vrow = T_RV // RET_WIDTH
    st_blk = (1, STATE_CHUNKS, RET_HEADS, RET_HEAD_DIM, RET_HEAD_DIM)
    st_shape = jax.ShapeDtypeStruct((B, n, RET_HEADS, RET_HEAD_DIM, RET_HEAD_DIM), BF16)
    return pl.pallas_call(
        _state_kernel,
        grid=(B, ng),
        in_specs=[pl.BlockSpec(k_blk, lambda b, t: (b, t, 0)),
                  pl.BlockSpec(v_blk, lambda b, t: (b, t, vrow, 0)),
                  pl.BlockSpec(k_blk, lambda b, t: (b, ng - 1 - t, 0)),
                  pl.BlockSpec(v_blk, lambda b, t: (b, ng - 1 - t, vrow, 0)),
                  _const_spec((2, BLOCK, RET_WIDTH)),
                  _const_spec((2 * RET_HEADS, LANES))],
        out_specs=(pl.BlockSpec(st_blk, lambda b, t: (b, t, 0, 0, 0)),
                   pl.BlockSpec(st_blk, lambda b, t: (b, ng - 1 - t, 0, 0, 0))),
        out_shape=(st_shape, st_shape),
        scratch_shapes=[pltpu.VMEM((2, RET_HEADS, RET_HEAD_DIM, RET_HEAD_DIM), F32)],
        compiler_params=_params("parallel", "arbitrary"),
        name="retention_states",
    )(zk, zt, zk, zt, kdec, cdec)


def _mixer_kernel(sink_ref, x_ref, zt_ref, avp_ref, avn_ref, zk_ref, akp_ref, akn_ref,
                  pf_ref, pb_ref, dmat_ref, qdec_ref,
                  an_ref, gn_ref, wout_ref, npost_ref, o_ref, kext_ref, vext_ref, mixt_ref,
                  sc_ref):
    tile = pl.program_id(1)
    n_blocks = pl.num_programs(1) * MIX_CHUNKS

    lane = lax.broadcasted_iota(jnp.int32, (1, LANES), 1)
    for g in range(ATTN_KV_HEADS):
        km = jnp.where((lane // ATTN_HEAD_DIM) == g, 1.0, 0.0).astype(BF16)
        kext_ref[g, 0:BLOCK] = akp_ref[0] * km
        kext_ref[g, BLOCK:BLOCK + MIX_TILE] = zk_ref[0, :, K_AK:K_AK + ATTN_KV_WIDTH] * km
        kext_ref[g, BLOCK + MIX_TILE:] = akn_ref[0] * km
    vext_ref[0] = avp_ref[0, 0]
    for c in range(MIX_CHUNKS):
        vext_ref[1 + c] = zt_ref[0, c, T_AV:T_AV + ATTN_KV_WIDTH, :]
    vext_ref[MIX_CHUNKS + 1] = avn_ref[0, 0]

    kk = lax.broadcasted_iota(jnp.int32, (BLOCK, LANES), 0)
    qq = lax.broadcasted_iota(jnp.int32, (BLOCK, LANES), 1)
    neg = jnp.full((BLOCK, LANES), -jnp.inf, F32)
    zero = jnp.zeros((BLOCK, LANES), F32)

    def scores_into(slot, c):
        q = jnp.concatenate([zt_ref[0, c, T_AQ + j * LANES:T_AQ + (j + 1) * LANES, :]
                             for j in range(ATTN_GROUP)], axis=1)
        r0 = c * BLOCK
        for g in range(ATTN_KV_HEADS):
            sc_ref[slot, g] = _dot(kext_ref[g, pl.ds(r0, 3 * BLOCK), :], q)

    def chunk(c, has_next):
        r0 = c * BLOCK
        blk = tile * MIX_CHUNKS + c
        slot = c % 2
        if has_next:
            scores_into(1 - slot, c + 1)

        vw = jnp.concatenate([vext_ref[c], vext_ref[c + 1], vext_ref[c + 2]], axis=1)
        no_prev = jnp.where(blk > 0, 0, BLOCK)
        no_next = jnp.where(blk < n_blocks - 1, 0, BLOCK)
        bias_prev = jnp.where(kk >= qq + no_prev, zero, neg)
        bias_next = jnp.where(kk <= qq - no_next, zero, neg)
        bias_prev = jnp.concatenate([bias_prev] * ATTN_GROUP, axis=1)
        bias_next = jnp.concatenate([bias_next] * ATTN_GROUP, axis=1)
        heads = []
        for h in range(RET_HEADS):
            qt = zt_ref[0, c, T_RQ + h * LANES:T_RQ + (h + 1) * LANES, :]
            kh = zk_ref[0, pl.ds(r0, BLOCK), K_RK + h * LANES:K_RK + (h + 1) * LANES]
            heads.append((qt, _dot(kh, qt)))

        ones = jnp.ones((BF16_ROWS, 3 * BLOCK), BF16)
        outs = []
        for g in range(ATTN_KV_HEADS):
            s = sc_ref[slot, g]
            s = jnp.concatenate([s[0:BLOCK] + bias_prev, s[BLOCK:2 * BLOCK],
                                 s[2 * BLOCK:] + bias_next], axis=0)
            sk = jnp.concatenate(
                [jnp.full((1, LANES), sink_ref[g * ATTN_GROUP + j] * LOG2E, F32)
                 for j in range(ATTN_GROUP)], axis=1)
            m = jnp.maximum(jnp.max(s, axis=0, keepdims=True), sk)
            p = jnp.exp2(s - m).astype(BF16)
            vg = jnp.concatenate([vw[g * ATTN_HEAD_DIM:(g + 1) * ATTN_HEAD_DIM, :], ones], axis=0)
            ov = _dot(vg, p)
            den = ov[ATTN_HEAD_DIM:ATTN_HEAD_DIM + 1, :] + jnp.exp2(sk - m)
            outs.append(ov[0:ATTN_HEAD_DIM, :] * (1.0 / den))
        ys = []
        for h in range(RET_HEADS):
            qt, pt = heads[h]
            y = _dot(zt_ref[0, c, T_RV + h * LANES:T_RV + (h + 1) * LANES, :],
                     (pt * dmat_ref[h]).astype(BF16))
            y = y + _dot(pf_ref[0, c, h], qt) * qdec_ref[h:h + 1, :]
            y = y + _dot(pb_ref[0, c, h], qt) * qdec_ref[RET_HEADS + h:RET_HEADS + h + 1, :]
            ys.append(y)

        a = jnp.concatenate([outs[g][:, j * LANES:(j + 1) * LANES]
                             for j in range(ATTN_GROUP) for g in range(ATTN_KV_HEADS)], axis=0)
        ms = jnp.sum(a * a, axis=0, keepdims=True) * (1.0 / ATTN_WIDTH)
        mixt_ref[c, 0:ATTN_WIDTH, :] = (a * lax.rsqrt(ms + EPS) * an_ref[...]).astype(BF16)
        for h in range(RET_HEADS):
            sl = slice(h * LANES, (h + 1) * LANES)
            y = ys[h]
            mu = jnp.mean(y, axis=0, keepdims=True)
            yc = y - mu
            var = jnp.mean(yc * yc, axis=0, keepdims=True)
            yn = yc * lax.rsqrt(var + EPS) * gn_ref[sl, :]
            gate = zt_ref[0, c, T_RG + h * LANES:T_RG + (h + 1) * LANES, :].astype(F32)
            r = gate * jax.nn.sigmoid(gate) * yn
            mixt_ref[c, ATTN_WIDTH + h * LANES:ATTN_WIDTH + (h + 1) * LANES, :] = r.astype(BF16)

    scores_into(0, 0)
    for c in range(MIX_CHUNKS):
        chunk(c, c + 1 < MIX_CHUNKS)

    mixt = jnp.concatenate([mixt_ref[c] for c in range(MIX_CHUNKS)], axis=1)
    mix = lax.dot_general(mixt, wout_ref[...], TN_DIMS, preferred_element_type=F32)
    o_ref[0] = x_ref[0] + _rms(mix, npost_ref[...])


def _mixer(x, zt, zk, states_f, states_b, dmat, qdec, sink, attn_norm, ret_gn, w_out, norm_post):
    B, S, _ = x.shape
    n = S // BLOCK
    cpt = MIX_CHUNKS
    edge_t = (1, 1, ATTN_KV_WIDTH, BLOCK)
    edge_k = (1, BLOCK, ATTN_KV_WIDTH)
    v_row = T_AV // ATTN_KV_WIDTH
    k_col = K_AK // ATTN_KV_WIDTH
    st_blk = (1, cpt, RET_HEADS, RET_HEAD_DIM, RET_HEAD_DIM)

    def prev(i):
        return jnp.maximum(i * cpt - 1, 0)

    def nxt(i):
        return jnp.minimum((i + 1) * cpt, n - 1)

    return pl.pallas_call(
        _mixer_kernel,
        grid=(B, S // MIX_TILE),
        in_specs=[pl.BlockSpec(memory_space=pltpu.SMEM),
                  pl.BlockSpec((1, MIX_TILE, D_MODEL), lambda b, i: (b, i, 0)),
                  pl.BlockSpec((1, cpt, T_ROWS, BLOCK), lambda b, i: (b, i, 0, 0)),
                  pl.BlockSpec(edge_t, lambda b, i: (b, prev(i), v_row, 0)),
                  pl.BlockSpec(edge_t, lambda b, i: (b, nxt(i), v_row, 0)),
                  pl.BlockSpec((1, MIX_TILE, K_COLS), lambda b, i: (b, i, 0)),
                  pl.BlockSpec(edge_k, lambda b, i: (b, prev(i), k_col)),
                  pl.BlockSpec(edge_k, lambda b, i: (b, nxt(i), k_col)),
                  pl.BlockSpec(st_blk, lambda b, i: (b, i, 0, 0, 0)),
                  pl.BlockSpec(st_blk, lambda b, i: (b, i, 0, 0, 0)),
                  _const_spec((RET_HEADS, BLOCK, LANES)),
                  _const_spec((2 * RET_HEADS, LANES)),
                  _const_spec((ATTN_WIDTH, LANES)),
                  _const_spec((RET_WIDTH, LANES)),
                  _const_spec((D_MODEL, D_MODEL)),
                  _const_spec((1, D_MODEL))],
        out_specs=pl.BlockSpec((1, MIX_TILE, D_MODEL), lambda b, i: (b, i, 0)),
        out_shape=jax.ShapeDtypeStruct((B, S, D_MODEL), F32),
        scratch_shapes=[pltpu.VMEM((ATTN_KV_HEADS, MIX_TILE + 2 * BLOCK, ATTN_KV_WIDTH), BF16),
                        pltpu.VMEM((cpt + 2, ATTN_KV_WIDTH, BLOCK), BF16),
                        pltpu.VMEM((cpt, D_MODEL, BLOCK), BF16),
                        pltpu.VMEM((2, ATTN_KV_HEADS, 3 * BLOCK, ATTN_GROUP * LANES), F32)],
        compiler_params=_params("parallel", "parallel"),
        name="token_mixer",
    )(sink, x, zt, zt, zt, zk, zk, zk, states_f, states_b, dmat, qdec,
      attn_norm, ret_gn, w_out, norm_post)


def _memkv_kernel(m_ref, g_ref, w_ref, kv_ref):
    mn = _rms(m_ref[0], g_ref[...]).astype(BF16)
    kv_ref[0] = _dot(mn, w_ref[...]).astype(BF16)


def _memory_kv(mem, gain, w_kv):
    B, M, _ = mem.shape
    return pl.pallas_call(
        _memkv_kernel,
        grid=(B,),
        in_specs=[pl.BlockSpec((1, M, D_MODEL), lambda b: (b, 0, 0)),
                  _const_spec((1, D_MODEL)),
                  _const_spec((D_MODEL, 2 * D_MODEL))],
        out_specs=pl.BlockSpec((1, M, 2 * D_MODEL), lambda b: (b, 0, 0)),
        out_shape=jax.ShapeDtypeStruct((B, M, 2 * D_MODEL), BF16),
        compiler_params=_params("parallel"),
        name="memory_kv",
    )(mem, gain, w_kv)


def _xattn_kernel(x_ref, gpre_ref, wq_ref, km_ref, vm_ref, wo_ref, gpost_ref,
                  o_ref, cat_ref):
    x = x_ref[0]
    h = _rms(x, gpre_ref[...]).astype(BF16)
    q = _dot(h, wq_ref[...]).astype(BF16)
    for hd in range(XA_HEADS):
        sl = slice(hd * XA_HEAD_DIM, (hd + 1) * XA_HEAD_DIM)
        s = lax.dot_general(q[:, sl], km_ref[0, :, sl], NT_DIMS, preferred_element_type=F32)
        p = jnp.exp(s - jnp.max(s, axis=-1, keepdims=True))
        den = jnp.sum(p, axis=-1, keepdims=True)
        cat_ref[:, sl] = (_dot(p.astype(BF16), vm_ref[0, :, sl]) * (1.0 / den)).astype(BF16)
    xo = _dot(cat_ref[...], wo_ref[...])
    o_ref[0] = x + _rms(xo, gpost_ref[...])


def _cross_attention(x, kv, gain_pre, w_q, w_o, gain_post):
    B, S, _ = x.shape
    M = kv.shape[1]
    tok = pl.BlockSpec((1, TOK_TILE, D_MODEL), lambda b, i: (b, i, 0))
    return pl.pallas_call(
        _xattn_kernel,
        grid=(B, S // TOK_TILE),
        in_specs=[tok,
                  _const_spec((1, D_MODEL)),
                  _const_spec((D_MODEL, D_MODEL)),
                  pl.BlockSpec((1, M, D_MODEL), lambda b, i: (b, 0, 0)),
                  pl.BlockSpec((1, M, D_MODEL), lambda b, i: (b, 0, 1)),
                  _const_spec((D_MODEL, D_MODEL)),
                  _const_spec((1, D_MODEL))],
        out_specs=tok,
        out_shape=jax.ShapeDtypeStruct((B, S, D_MODEL), F32),
        scratch_shapes=[pltpu.VMEM((TOK_TILE, D_MODEL), BF16)],
        compiler_params=_params("parallel", "parallel"),
        name="cross_attention",
    )(x, gain_pre, w_q, kv, kv, w_o, gain_post)


FF_CHUNK = 2 * MXU_TILE
FF_CHUNKS = [(c0, min(FF_CHUNK, D_FF - c0)) for c0 in range(0, D_FF, FF_CHUNK)]
assert all(w % MXU_TILE == 0 for _, w in FF_CHUNKS)


def _ffn_kernel(x_ref, gpre_ref, wgu_ref, wd_ref, gpost_ref, o_ref, gu_ref, act_ref):
    x = x_ref[0]
    h = _rms(x, gpre_ref[...]).astype(BF16)

    def gate_up_into(slot, j):
        c0, w = FF_CHUNKS[j]
        gu_ref[slot, 0, :, 0:w] = _dot(h, wgu_ref[:, c0:c0 + w])
        gu_ref[slot, 1, :, 0:w] = _dot(h, wgu_ref[:, D_FF + c0:D_FF + c0 + w])

    gate_up_into(0, 0)
    for j, (c0, w) in enumerate(FF_CHUNKS):
        if j + 1 < len(FF_CHUNKS):
            gate_up_into((j + 1) % 2, j + 1)
        gate = gu_ref[j % 2, 0, :, 0:w]
        up = gu_ref[j % 2, 1, :, 0:w]
        act_ref[:, c0:c0 + w] = (gate * jax.nn.sigmoid(gate) * up).astype(BF16)
    f = _dot(act_ref[...], wd_ref[...])
    o_ref[0] = x + _rms(f, gpost_ref[...])


def _ffn(x, gain_pre, w_gu, w_down, gain_post):
    B, S, _ = x.shape
    tok = pl.BlockSpec((1, TOK_TILE, D_MODEL), lambda b, i: (b, i, 0))
    return pl.pallas_call(
        _ffn_kernel,
        grid=(B, S // TOK_TILE),
        in_specs=[tok,
                  _const_spec((1, D_MODEL)),
                  _const_spec((D_MODEL, 2 * D_FF)),
                  _const_spec((D_FF, D_MODEL)),
                  _const_spec((1, D_MODEL))],
        out_specs=tok,
        out_shape=jax.ShapeDtypeStruct((B, S, D_MODEL), F32),
        scratch_shapes=[pltpu.VMEM((2, 2, TOK_TILE, FF_CHUNK), F32),
                        pltpu.VMEM((TOK_TILE, D_FF), BF16)],
        compiler_params=_params("parallel", "parallel"),
        name="swiglu_ffn",
    )(x, gain_pre, w_gu, w_down, gain_post)


def _pair_heads(a, axis):
    shape = a.shape
    a = a.reshape(shape[:axis] + (ATTN_KV_HEADS, ATTN_GROUP, ATTN_HEAD_DIM) + shape[axis + 1:])
    return jnp.swapaxes(a, axis, axis + 1).reshape(shape)


def kernel(x, mem, norm_mix_pre, norm_mix_post, w_in, attn_sink, attn_out_norm,
           ret_decay_fwd, ret_decay_bwd, ret_gn, w_out, norm_xa_pre, norm_xa_post,
           norm_mem, xa_wq, xa_wkv, xa_wo, norm_ffn_pre, norm_ffn_post,
           ffn_w_gu, ffn_w_down):
    B, S, D = x.shape
    assert D == D_MODEL and S % max(TOK_TILE, MIX_TILE, STATE_CHUNKS * BLOCK) == 0
    assert mem.shape[1] == N_MEM
    depth = w_in.shape[0]
    o_k = ATTN_WIDTH
    o_v = o_k + ATTN_KV_WIDTH
    o_r = o_v + ATTN_KV_WIDTH
    rw = RET_WIDTH

    rope = _rope_tables(S // TOK_TILE)

    def row(v):
        return v.reshape(1, -1).astype(F32)

    def col(v):
        return jnp.broadcast_to(v.astype(F32)[:, None], (v.shape[0], LANES))

    for l in range(depth):
        w = w_in[l]
        w_t = jnp.concatenate([_pair_heads(w[:, :o_k], 1), w[:, o_r:o_r + rw], w[:, o_r + 2 * rw:],
                               w[:, o_v:o_r], w[:, o_k:o_v]], axis=1).T.astype(BF16)
        w_k = w[:, o_r + rw:o_r + 2 * rw].astype(BF16)
        w_o = jnp.concatenate([_pair_heads(w_out[l][:ATTN_WIDTH], 0), w_out[l][ATTN_WIDTH:]], axis=0)

        dmat, kdec, qdec, cdec = _decay_tables(ret_decay_fwd[l], ret_decay_bwd[l])
        zt, zk = _in_projection(x, row(norm_mix_pre[l]), w_t, w_k, rope)
        states_f, states_b = _retention_states(zt, zk, kdec, cdec)
        x = _mixer(x, zt, zk, states_f, states_b, dmat, qdec, attn_sink[l].astype(F32),
                   col(_pair_heads(attn_out_norm[l], 0)), col(ret_gn[l]),
                   w_o.astype(BF16), row(norm_mix_post[l]))
        kv = _memory_kv(mem, row(norm_mem[l]), xa_wkv[l].astype(BF16))
        x = _cross_attention(x, kv, row(norm_xa_pre[l]),
                             (xa_wq[l] * (XA_HEAD_DIM ** -0.5)).astype(BF16),
                             xa_wo[l].astype(BF16), row(norm_xa_post[l]))
        x = _ffn(x, row(norm_ffn_pre[l]), ffn_w_gu[l].astype(BF16),
                 ffn_w_down[l].astype(BF16), row(norm_ffn_post[l]))
    return x
```

```python
import math

import numpy as np
import jax
import jax.numpy as jnp
from jax import lax
from jax.experimental import pallas as pl
from jax.experimental.pallas import tpu as pltpu

D_MODEL = 1024
N_MEM = 256
ATTN_HEADS = 8
ATTN_KV_HEADS = 2
ATTN_GROUP = ATTN_HEADS // ATTN_KV_HEADS
ATTN_HEAD_DIM = 64
ATTN_WIDTH = ATTN_HEADS * ATTN_HEAD_DIM
ATTN_KV_WIDTH = ATTN_KV_HEADS * ATTN_HEAD_DIM
BLOCK = 128
RET_HEADS = 4
RET_HEAD_DIM = 128
RET_WIDTH = RET_HEADS * RET_HEAD_DIM
IN_COLS = ATTN_WIDTH + 2 * ATTN_KV_WIDTH + 4 * RET_WIDTH
XA_HEADS = 4
XA_HEAD_DIM = D_MODEL // XA_HEADS
D_FF = -(-(8 * D_MODEL) // (3 * 256)) * 256
ROPE_THETA = 10000.0
EPS = 1e-6
LOG2E = math.log2(math.e)

LANES = 128
BF16_ROWS = 16
MXU_TILE = 256
VMEM_LIMIT_BYTES = 56 * 1024 * 1024

TOK_TILE = 512
CHUNKS_PER_TILE = TOK_TILE // BLOCK
STATE_CHUNKS = 8
MIX_TILE = 1024
MIX_CHUNKS = MIX_TILE // BLOCK
ROW_PIECE = 256

T_AQ, T_RQ, T_RV, T_RG, T_AV = 0, 512, 1024, 1536, 2048
T_ROWS = T_AV + ATTN_KV_WIDTH
K_RK, K_AK = 0, RET_WIDTH
K_COLS = RET_WIDTH + ATTN_KV_WIDTH

F32 = jnp.float32
BF16 = jnp.bfloat16
NT_DIMS = (((1,), (1,)), ((), ()))
TN_DIMS = (((0,), (0,)), ((), ()))


def _const_spec(shape):
    nd = len(shape)
    return pl.BlockSpec(shape, lambda *_: (0,) * nd, pipeline_mode=pl.Buffered(1))


def _params(*sem, flags=None):
    return pltpu.CompilerParams(dimension_semantics=sem,
                                vmem_limit_bytes=VMEM_LIMIT_BYTES, flags=flags)


def _rms(x, w):
    ms = jnp.mean(x * x, axis=-1, keepdims=True)
    return x * lax.rsqrt(ms + EPS) * w


def _dot(a, b):
    return jnp.dot(a, b, preferred_element_type=F32)


def _rope_tables(n_tiles):
    def parts(dim):
        inv_freq = ROPE_THETA ** (-np.arange(0, dim, 2, dtype=np.float64) / dim)
        base = (np.arange(n_tiles, dtype=np.float64) * TOK_TILE)[:, None] * inv_freq[None, :]
        off = np.arange(TOK_TILE, dtype=np.float64)[:, None] * inv_freq[None, :]
        return np.cos(base), np.sin(base), np.cos(off), np.sin(off)

    def tok(c, s):
        return (np.concatenate([c, c], axis=1), np.concatenate([s, s], axis=1),
                np.concatenate([-s, s], axis=1))

    cb_a, sb_a, co_a, so_a = parts(ATTN_HEAD_DIM)
    cb_r, sb_r, co_r, so_r = parts(RET_HEAD_DIM)
    tok_base = np.stack(tok(cb_r, sb_r) + (np.zeros((n_tiles, LANES)),) * 5, axis=1)
    tok_off = np.stack(tok(co_r, so_r), axis=0)
    feat_base = np.concatenate([cb_a, sb_a, cb_r, sb_r], axis=1)
    feat_base = np.broadcast_to(feat_base[:, :, None], feat_base.shape + (BLOCK,))
    feat_off = np.concatenate([co_a, so_a, co_r, so_r], axis=1).T
    return tuple(jnp.asarray(t, F32) for t in (tok_base, tok_off, feat_base, feat_off))


def _decay_kernel(logit_ref, dmat_ref, kdec_ref, qdec_ref, cdec_ref):
    x = logit_ref[...]
    lg = jnp.minimum(x, 0.0) - jnp.log1p(jnp.exp(-jnp.abs(x)))
    ri = lax.broadcasted_iota(jnp.int32, (BLOCK, LANES), 0).astype(F32)
    ci = lax.broadcasted_iota(jnp.int32, (BLOCK, LANES), 1).astype(F32)
    c1 = ci[0:1, :]
    diff = ci - ri
    cdec_ref[...] = jnp.exp(BLOCK * lg)
    for h in range(RET_HEADS):
        lf = lg[h:h + 1, :]
        lb = lg[RET_HEADS + h:RET_HEADS + h + 1, :]
        dmat_ref[h] = jnp.exp(jnp.abs(diff) * jnp.where(diff >= 0, lf, lb))
        sl = slice(h * LANES, (h + 1) * LANES)
        kdec_ref[0, :, sl] = jnp.exp((BLOCK - 1 - ri) * lf)
        kdec_ref[1, :, sl] = jnp.exp(ri * lb)
        qdec_ref[h:h + 1, :] = jnp.exp((c1 + 1) * lf)
        qdec_ref[RET_HEADS + h:RET_HEADS + h + 1, :] = jnp.exp((BLOCK - c1) * lb)


def _decay_tables(decay_fwd, decay_bwd):
    logits = jnp.concatenate([decay_fwd, decay_bwd]).astype(F32)
    logits = jnp.broadcast_to(logits[:, None], (2 * RET_HEADS, LANES))
    return pl.pallas_call(
        _decay_kernel,
        out_shape=(jax.ShapeDtypeStruct((RET_HEADS, BLOCK, LANES), F32),
                   jax.ShapeDtypeStruct((2, BLOCK, RET_WIDTH), F32),
                   jax.ShapeDtypeStruct((2 * RET_HEADS, LANES), F32),
                   jax.ShapeDtypeStruct((2 * RET_HEADS, LANES), F32)),
        name="decay_tables",
    )(logits)


def _inproj_kernel(x_ref, g_ref, wt_ref, wk_ref, tb_ref, to_ref, fb_ref, fo_ref,
                   zt_ref, zk_ref):
    h = _rms(x_ref[0], g_ref[...]).astype(BF16)

    tb = tb_ref[0]

    ck = tb[0:1] * to_ref[0] - tb[1:2] * to_ref[1]
    sk = tb[2:3] * to_ref[0] + tb[0:1] * to_ref[2]

    fb = jnp.concatenate([fb_ref[0]] * CHUNKS_PER_TILE, axis=1)
    fo = fo_ref[...]

    def feat_tables(row, n):
        cb, sb = fb[row:row + n], fb[row + n:row + 2 * n]
        co, so = fo[row:row + n], fo[row + n:row + 2 * n]
        return cb * co - sb * so, sb * co + cb * so

    zk = _dot(h, wk_ref[...])
    for g in range(RET_HEADS):
        zg = zk[:, g * LANES:(g + 1) * LANES]
        r = (zg * ck + pltpu.roll(zg, LANES // 2, 1) * sk) * (RET_HEAD_DIM ** -0.5)
        zk_ref[0, :, K_RK + g * LANES:K_RK + (g + 1) * LANES] = r.astype(BF16)

    def proj_t(row, n_rows):
        return lax.dot_general(wt_ref[row:row + n_rows, :], h, NT_DIMS,
                               preferred_element_type=F32)

    def store_t(row, val):
        for c in range(CHUNKS_PER_TILE):
            zt_ref[0, c, row:row + val.shape[0], :] = val[:, c * BLOCK:(c + 1) * BLOCK].astype(BF16)

    def rope_t(z, row, head_dim, cos, sin):
        hd2 = head_dim // 2
        for k in range(z.shape[0] // head_dim):
            x1 = z[k * head_dim:k * head_dim + hd2]
            x2 = z[k * head_dim + hd2:(k + 1) * head_dim]
            store_t(row + k * head_dim, x1 * cos - x2 * sin)
            store_t(row + k * head_dim + hd2, x2 * cos + x1 * sin)

    q_scale = (ATTN_HEAD_DIM ** -0.5) * LOG2E
    cat, sat = feat_tables(0, ATTN_HEAD_DIM // 2)
    rope_t(proj_t(T_AQ, ATTN_WIDTH), T_AQ, ATTN_HEAD_DIM, cat * q_scale, sat * q_scale)
    crt, srt = feat_tables(ATTN_HEAD_DIM, RET_HEAD_DIM // 2)
    rope_t(proj_t(T_RQ, RET_WIDTH), T_RQ, RET_HEAD_DIM, crt, srt)
    store_t(T_RV, proj_t(T_RV, RET_WIDTH))
    store_t(T_RG, proj_t(T_RG, RET_WIDTH))
    kv = proj_t(T_AV, 2 * ATTN_KV_WIDTH)
    store_t(T_AV, kv[0:ATTN_KV_WIDTH])
    hd2 = ATTN_HEAD_DIM // 2
    rows = []
    for k in range(ATTN_KV_HEADS):
        r0 = ATTN_KV_WIDTH + k * ATTN_HEAD_DIM
        x1, x2 = kv[r0:r0 + hd2], kv[r0 + hd2:r0 + ATTN_HEAD_DIM]
        rows += [x1 * cat - x2 * sat, x2 * cat + x1 * sat]
    zk_ref[0, :, K_AK:K_AK + ATTN_KV_WIDTH] = jnp.concatenate(rows, axis=0).T.astype(BF16)


def _in_projection(x, gain, w_t, w_k, rope):
    B, S, _ = x.shape
    n = S // BLOCK
    tok_base, tok_off, feat_base, feat_off = rope
    return pl.pallas_call(
        _inproj_kernel,
        grid=(S // TOK_TILE, B),
        in_specs=[pl.BlockSpec((1, TOK_TILE, D_MODEL), lambda i, b: (b, i, 0)),
                  _const_spec((1, D_MODEL)),
                  _const_spec((T_ROWS + ATTN_KV_WIDTH, D_MODEL)),
                  _const_spec((D_MODEL, RET_WIDTH)),
                  pl.BlockSpec((1,) + tok_base.shape[1:], lambda i, b: (i, 0, 0)),
                  _const_spec(tok_off.shape),
                  pl.BlockSpec((1,) + feat_base.shape[1:], lambda i, b: (i, 0, 0)),
                  _const_spec(feat_off.shape)],
        out_specs=(pl.BlockSpec((1, CHUNKS_PER_TILE, T_ROWS, BLOCK), lambda i, b: (b, i, 0, 0)),
                   pl.BlockSpec((1, TOK_TILE, K_COLS), lambda i, b: (b, i, 0))),
        out_shape=(jax.ShapeDtypeStruct((B, n, T_ROWS, BLOCK), BF16),
                   jax.ShapeDtypeStruct((B, S, K_COLS), BF16)),
        compiler_params=_params("parallel", "parallel"),
        name="in_projection",
    )(x, gain, w_t, w_k, tok_base, tok_off, feat_base, feat_off)


def _state_kernel(kf_ref, vf_ref, kb_ref, vb_ref, kdec_ref, cdec_ref,
                  pf_ref, pb_ref, st_ref):
    @pl.when(pl.program_id(1) == 0)
    def _():
        st_ref[...] = jnp.zeros_like(st_ref)

    for i in range(STATE_CHUNKS):
        for d, (k_ref, v_ref, p_ref, c) in enumerate(((kf_ref, vf_ref, pf_ref, i),
                                                      (kb_ref, vb_ref, pb_ref, STATE_CHUNKS - 1 - i))):
            kd = (k_ref[0, c * BLOCK:(c + 1) * BLOCK, :].astype(F32) * kdec_ref[d]).astype(BF16)
            for h in range(RET_HEADS):
                sl = slice(h * LANES, (h + 1) * LANES)
                st = st_ref[d, h]
                p_ref[0, c, h] = st.astype(BF16)
                kv = _dot(v_ref[0, c, sl, :], kd[:, sl])
                row = d * RET_HEADS + h
                st_ref[d, h] = st * cdec_ref[row:row + 1, :] + kv


def _retention_states(zt, zk, kdec, cdec):
    B, n = zt.shape[0], zt.shape[1]
    ng = n // STATE_CHUNKS
    k_blk = (1, STATE_CHUNKS * BLOCK, RET_WIDTH)
    v_blk = (1, STATE_CHUNKS, RET_WIDTH, BLOCK)
    vrow = T_RV // RET_WIDTH
    st_blk = (1, STATE_CHUNKS, RET_HEADS, RET_HEAD_DIM, RET_HEAD_DIM)
    st_shape = jax.ShapeDtypeStruct((B, n, RET_HEADS, RET_HEAD_DIM, RET_HEAD_DIM), BF16)
    return pl.pallas_call(
        _state_kernel,
        grid=(B, ng),
        in_specs=[pl.BlockSpec(k_blk, lambda b, t: (b, t, 0)),
                  pl.BlockSpec(v_blk, lambda b, t: (b, t, vrow, 0)),
                  pl.BlockSpec(k_blk, lambda b, t: (b, ng - 1 - t, 0)),
                  pl.BlockSpec(v_blk, lambda b, t: (b, ng - 1 - t, vrow, 0)),
                  _const_spec((2, BLOCK, RET_WIDTH)),
                  _const_spec((2 * RET_HEADS, LANES))],
        out_specs=(pl.BlockSpec(st_blk, lambda b, t: (b, t, 0, 0, 0)),
                   pl.BlockSpec(st_blk, lambda b, t: (b, ng - 1 - t, 0, 0, 0))),
        out_shape=(st_shape, st_shape),
        scratch_shapes=[pltpu.VMEM((2, RET_HEADS, RET_HEAD_DIM, RET_HEAD_DIM), F32)],
        compiler_params=_params("parallel", "arbitrary"),
        name="retention_states",
    )(zk, zt, zk, zt, kdec, cdec)


def _mixer_kernel(sink_ref, x_ref, zt_ref, avp_ref, avn_ref, zk_ref, akp_ref, akn_ref,
                  pf_ref, pb_ref, dmat_ref, qdec_ref,
                  an_ref, gn_ref, wout_ref, npost_ref, o_ref, kext_ref, vext_ref, mixt_ref,
                  sc_ref):
    tile = pl.program_id(1)
    n_blocks = pl.num_programs(1) * MIX_CHUNKS

    lane = lax.broadcasted_iota(jnp.int32, (1, LANES), 1)
    for g in range(ATTN_KV_HEADS):
        km = jnp.where((lane // ATTN_HEAD_DIM) == g, 1.0, 0.0).astype(BF16)
        kext_ref[g, 0:BLOCK] = akp_ref[0] * km
        kext_ref[g, BLOCK:BLOCK + MIX_TILE] = zk_ref[0, :, K_AK:K_AK + ATTN_KV_WIDTH] * km
        kext_ref[g, BLOCK + MIX_TILE:] = akn_ref[0] * km
    vext_ref[0] = avp_ref[0, 0]
    for c in range(MIX_CHUNKS):
        vext_ref[1 + c] = zt_ref[0, c, T_AV:T_AV + ATTN_KV_WIDTH, :]
    vext_ref[MIX_CHUNKS + 1] = avn_ref[0, 0]

    kk = lax.broadcasted_iota(jnp.int32, (BLOCK, LANES), 0)
    qq = lax.broadcasted_iota(jnp.int32, (BLOCK, LANES), 1)
    neg = jnp.full((BLOCK, LANES), -jnp.inf, F32)
    zero = jnp.zeros((BLOCK, LANES), F32)

    def scores_into(slot, c):
        q = jnp.concatenate([zt_ref[0, c, T_AQ + j * LANES:T_AQ + (j + 1) * LANES, :]
                             for j in range(ATTN_GROUP)], axis=1)
        r0 = c * BLOCK
        for g in range(ATTN_KV_HEADS):
            sc_ref[slot, g] = _dot(kext_ref[g, pl.ds(r0, 3 * BLOCK), :], q)

    def chunk(c, has_next):
        r0 = c * BLOCK
        blk = tile * MIX_CHUNKS + c
        slot = c % 2
        if has_next:
            scores_into(1 - slot, c + 1)

        vw = jnp.concatenate([vext_ref[c], vext_ref[c + 1], vext_ref[c + 2]], axis=1)
        no_prev = jnp.where(blk > 0, 0, BLOCK)
        no_next = jnp.where(blk < n_blocks - 1, 0, BLOCK)
        bias_prev = jnp.where(kk >= qq + no_prev, zero, neg)
        bias_next = jnp.where(kk <= qq - no_next, zero, neg)
        bias_prev = jnp.concatenate([bias_prev] * ATTN_GROUP, axis=1)
        bias_next = jnp.concatenate([bias_next] * ATTN_GROUP, axis=1)
        heads = []
        for h in range(RET_HEADS):
            qt = zt_ref[0, c, T_RQ + h * LANES:T_RQ + (h + 1) * LANES, :]
            kh = zk_ref[0, pl.ds(r0, BLOCK), K_RK + h * LANES:K_RK + (h + 1) * LANES]
            heads.append((qt, _dot(kh, qt)))

        ones = jnp.ones((BF16_ROWS, 3 * BLOCK), BF16)
        outs = []
        for g in range(ATTN_KV_HEADS):
            s = sc_ref[slot, g]
            s = jnp.concatenate([s[0:BLOCK] + bias_prev, s[BLOCK:2 * BLOCK],
                                 s[2 * BLOCK:] + bias_next], axis=0)
            sk = jnp.concatenate(
                [jnp.full((1, LANES), sink_ref[g * ATTN_GROUP + j] * LOG2E, F32)
                 for j in range(ATTN_GROUP)], axis=1)
            m = jnp.maximum(jnp.max(s, axis=0, keepdims=True), sk)
            p = jnp.exp2(s - m).astype(BF16)
            vg = jnp.concatenate([vw[g * ATTN_HEAD_DIM:(g + 1) * ATTN_HEAD_DIM, :], ones], axis=0)
            ov = _dot(vg, p)
            den = ov[ATTN_HEAD_DIM:ATTN_HEAD_DIM + 1, :] + jnp.exp2(sk - m)
            outs.append(ov[0:ATTN_HEAD_DIM, :] * (1.0 / den))
        ys = []
        for h in range(RET_HEADS):
            qt, pt = heads[h]
            y = _dot(zt_ref[0, c, T_RV + h * LANES:T_RV + (h + 1) * LANES, :],
                     (pt * dmat_ref[h]).astype(BF16))
            y = y + _dot(pf_ref[0, c, h], qt) * qdec_ref[h:h + 1, :]
            y = y + _dot(pb_ref[0, c, h], qt) * qdec_ref[RET_HEADS + h:RET_HEADS + h + 1, :]
            ys.append(y)

        a = jnp.concatenate([outs[g][:, j * LANES:(j + 1) * LANES]
                             for j in range(ATTN_GROUP) for g in range(ATTN_KV_HEADS)], axis=0)
        ms = jnp.sum(a * a, axis=0, keepdims=True) * (1.0 / ATTN_WIDTH)
        mixt_ref[c, 0:ATTN_WIDTH, :] = (a * lax.rsqrt(ms + EPS) * an_ref[...]).astype(BF16)
        for h in range(RET_HEADS):
            sl = slice(h * LANES, (h + 1) * LANES)
            y = ys[h]
            mu = jnp.mean(y, axis=0, keepdims=True)
            yc = y - mu
            var = jnp.mean(yc * yc, axis=0, keepdims=True)
            yn = yc * lax.rsqrt(var + EPS) * gn_ref[sl, :]
            gate = zt_ref[0, c, T_RG + h * LANES:T_RG + (h + 1) * LANES, :].astype(F32)
            r = gate * jax.nn.sigmoid(gate) * yn
            mixt_ref[c, ATTN_WIDTH + h * LANES:ATTN_WIDTH + (h + 1) * LANES, :] = r.astype(BF16)

    scores_into(0, 0)
    for c in range(MIX_CHUNKS):
        chunk(c, c + 1 < MIX_CHUNKS)

    mixt = jnp.concatenate([mixt_ref[c] for c in range(MIX_CHUNKS)], axis=1)
    mix = lax.dot_general(mixt, wout_ref[...], TN_DIMS, preferred_element_type=F32)
    o_ref[0] = x_ref[0] + _rms(mix, npost_ref[...])


def _mixer(x, zt, zk, states_f, states_b, dmat, qdec, sink, attn_norm, ret_gn, w_out, norm_post):
    B, S, _ = x.shape
    n = S // BLOCK
    cpt = MIX_CHUNKS
    edge_t = (1, 1, ATTN_KV_WIDTH, BLOCK)
    edge_k = (1, BLOCK, ATTN_KV_WIDTH)
    v_row = T_AV // ATTN_KV_WIDTH
    k_col = K_AK // ATTN_KV_WIDTH
    st_blk = (1, cpt, RET_HEADS, RET_HEAD_DIM, RET_HEAD_DIM)

    def prev(i):
        return jnp.maximum(i * cpt - 1, 0)

    def nxt(i):
        return jnp.minimum((i + 1) * cpt, n - 1)

    return pl.pallas_call(
        _mixer_kernel,
        grid=(B, S // MIX_TILE),
        in_specs=[pl.BlockSpec(memory_space=pltpu.SMEM),
                  pl.BlockSpec((1, MIX_TILE, D_MODEL), lambda b, i: (b, i, 0)),
                  pl.BlockSpec((1, cpt, T_ROWS, BLOCK), lambda b, i: (b, i, 0, 0)),
                  pl.BlockSpec(edge_t, lambda b, i: (b, prev(i), v_row, 0)),
                  pl.BlockSpec(edge_t, lambda b, i: (b, nxt(i), v_row, 0)),
                  pl.BlockSpec((1, MIX_TILE, K_COLS), lambda b, i: (b, i, 0)),
                  pl.BlockSpec(edge_k, lambda b, i: (b, prev(i), k_col)),
                  pl.BlockSpec(edge_k, lambda b, i: (b, nxt(i), k_col)),
                  pl.BlockSpec(st_blk, lambda b, i: (b, i, 0, 0, 0)),
                  pl.BlockSpec(st_blk, lambda b, i: (b, i, 0, 0, 0)),
                  _const_spec((RET_HEADS, BLOCK, LANES)),
                  _const_spec((2 * RET_HEADS, LANES)),
                  _const_spec((ATTN_WIDTH, LANES)),
                  _const_spec((RET_WIDTH, LANES)),
                  _const_spec((D_MODEL, D_MODEL)),
                  _const_spec((1, D_MODEL))],
        out_specs=pl.BlockSpec((1, MIX_TILE, D_MODEL), lambda b, i: (b, i, 0)),
        out_shape=jax.ShapeDtypeStruct((B, S, D_MODEL), F32),
        scratch_shapes=[pltpu.VMEM((ATTN_KV_HEADS, MIX_TILE + 2 * BLOCK, ATTN_KV_WIDTH), BF16),
                        pltpu.VMEM((cpt + 2, ATTN_KV_WIDTH, BLOCK), BF16),
                        pltpu.VMEM((cpt, D_MODEL, BLOCK), BF16),
                        pltpu.VMEM((2, ATTN_KV_HEADS, 3 * BLOCK, ATTN_GROUP * LANES), F32)],
        compiler_params=_params("parallel", "parallel"),
        name="token_mixer",
    )(sink, x, zt, zt, zt, zk, zk, zk, states_f, states_b, dmat, qdec,
      attn_norm, ret_gn, w_out, norm_post)


def _memkv_kernel(m_ref, g_ref, w_ref, kv_ref):
    mn = _rms(m_ref[0], g_ref[...]).astype(BF16)
    kv_ref[0] = _dot(mn, w_ref[...]).astype(BF16)


def _memory_kv(mem, gain, w_kv):
    B, M, _ = mem.shape
    return pl.pallas_call(
        _memkv_kernel,
        grid=(B,),
        in_specs=[pl.BlockSpec((1, M, D_MODEL), lambda b: (b, 0, 0)),
                  _const_spec((1, D_MODEL)),
                  _const_spec((D_MODEL, 2 * D_MODEL))],
        out_specs=pl.BlockSpec((1, M, 2 * D_MODEL), lambda b: (b, 0, 0)),
        out_shape=jax.ShapeDtypeStruct((B, M, 2 * D_MODEL), BF16),
        compiler_params=_params("parallel"),
        name="memory_kv",
    )(mem, gain, w_kv)


def _xattn_kernel(x_ref, gpre_ref, wq_ref, km_ref, vm_ref, wo_ref, gpost_ref,
                  o_ref, cat_ref, sc_ref, h_ref, q_ref, r_ref):
    pieces = [slice(r * ROW_PIECE, (r + 1) * ROW_PIECE) for r in range(TOK_TILE // ROW_PIECE)]
    for rows in pieces:
        h_ref[rows, :] = _rms(x_ref[0, rows, :], gpre_ref[...]).astype(BF16)
    for rows in pieces:
        q_ref[rows, :] = _dot(h_ref[rows, :], wq_ref[...]).astype(BF16)
    for hd in range(XA_HEADS):
        sl = slice(hd * XA_HEAD_DIM, (hd + 1) * XA_HEAD_DIM)
        sc_ref[hd] = lax.dot_general(q_ref[:, sl], km_ref[0, :, sl], NT_DIMS,
                                     preferred_element_type=F32)
    for hd in range(XA_HEADS):
        sl = slice(hd * XA_HEAD_DIM, (hd + 1) * XA_HEAD_DIM)
        s = sc_ref[hd]
        p = jnp.exp(s - jnp.max(s, axis=-1, keepdims=True))
        den = jnp.sum(p, axis=-1, keepdims=True)
        cat_ref[:, sl] = (_dot(p.astype(BF16), vm_ref[0, :, sl]) * (1.0 / den)).astype(BF16)
    for rows in pieces:
        r_ref[rows, :] = _dot(cat_ref[rows, :], wo_ref[...])
    for rows in pieces:
        o_ref[0, rows, :] = x_ref[0, rows, :] + _rms(r_ref[rows, :], gpost_ref[...])


def _cross_attention(x, kv, gain_pre, w_q, w_o, gain_post):
    B, S, _ = x.shape
    M = kv.shape[1]
    tok = pl.BlockSpec((1, TOK_TILE, D_MODEL), lambda b, i: (b, i, 0))
    return pl.pallas_call(
        _xattn_kernel,
        grid=(B, S // TOK_TILE),
        in_specs=[tok,
                  _const_spec((1, D_MODEL)),
                  _const_spec((D_MODEL, D_MODEL)),
                  pl.BlockSpec((1, M, D_MODEL), lambda b, i: (b, 0, 0)),
                  pl.BlockSpec((1, M, D_MODEL), lambda b, i: (b, 0, 1)),
                  _const_spec((D_MODEL, D_MODEL)),
                  _const_spec((1, D_MODEL))],
        out_specs=tok,
        out_shape=jax.ShapeDtypeStruct((B, S, D_MODEL), F32),
        scratch_shapes=[pltpu.VMEM((TOK_TILE, D_MODEL), BF16),
                        pltpu.VMEM((XA_HEADS, TOK_TILE, N_MEM), F32),
                        pltpu.VMEM((TOK_TILE, D_MODEL), BF16),
                        pltpu.VMEM((TOK_TILE, D_MODEL), BF16),
                        pltpu.VMEM((TOK_TILE, D_MODEL), F32)],
        compiler_params=_params("parallel", "parallel"),
        name="cross_attention",
    )(x, gain_pre, w_q, kv, kv, w_o, gain_post)


FF_CHUNK = 2 * MXU_TILE
FF_CHUNKS = [(c0, min(FF_CHUNK, D_FF - c0)) for c0 in range(0, D_FF, FF_CHUNK)]
assert all(w % MXU_TILE == 0 for _, w in FF_CHUNKS)


def _ffn_kernel(x_ref, gpre_ref, wgu_ref, wd_ref, gpost_ref, o_ref, gu_ref, act_ref,
                h_ref, f_ref):
    pieces = [slice(r * ROW_PIECE, (r + 1) * ROW_PIECE) for r in range(TOK_TILE // ROW_PIECE)]

    def gate_up_into(slot, j, rows=slice(None)):
        c0, w = FF_CHUNKS[j]
        gu_ref[slot, 0, rows, 0:w] = _dot(h_ref[rows, :], wgu_ref[:, c0:c0 + w])
        gu_ref[slot, 1, rows, 0:w] = _dot(h_ref[rows, :], wgu_ref[:, D_FF + c0:D_FF + c0 + w])

    for rows in pieces:
        h_ref[rows, :] = _rms(x_ref[0, rows, :], gpre_ref[...]).astype(BF16)
    for rows in pieces:
        gate_up_into(0, 0, rows)
    for j, (c0, w) in enumerate(FF_CHUNKS):
        if j + 1 < len(FF_CHUNKS):
            gate_up_into((j + 1) % 2, j + 1)
        gate = gu_ref[j % 2, 0, :, 0:w]
        up = gu_ref[j % 2, 1, :, 0:w]
        act_ref[:, c0:c0 + w] = (gate * jax.nn.sigmoid(gate) * up).astype(BF16)
    for rows in pieces:
        f_ref[rows, :] = _dot(act_ref[rows, :], wd_ref[...])
    for rows in pieces:
        o_ref[0, rows, :] = x_ref[0, rows, :] + _rms(f_ref[rows, :], gpost_ref[...])


def _ffn(x, gain_pre, w_gu, w_down, gain_post):
    B, S, _ = x.shape
    tok = pl.BlockSpec((1, TOK_TILE, D_MODEL), lambda b, i: (b, i, 0))
    return pl.pallas_call(
        _ffn_kernel,
        grid=(B, S // TOK_TILE),
        in_specs=[tok,
                  _const_spec((1, D_MODEL)),
                  _const_spec((D_MODEL, 2 * D_FF)),
                  _const_spec((D_FF, D_MODEL)),
                  _const_spec((1, D_MODEL))],
        out_specs=tok,
        out_shape=jax.ShapeDtypeStruct((B, S, D_MODEL), F32),
        scratch_shapes=[pltpu.VMEM((2, 2, TOK_TILE, FF_CHUNK), F32),
                        pltpu.VMEM((TOK_TILE, D_FF), BF16),
                        pltpu.VMEM((TOK_TILE, D_MODEL), BF16),
                        pltpu.VMEM((TOK_TILE, D_MODEL), F32)],
        compiler_params=_params("parallel", "parallel"),
        name="swiglu_ffn",
    )(x, gain_pre, w_gu, w_down, gain_post)


def _pair_heads(a, axis):
    shape = a.shape
    a = a.reshape(shape[:axis] + (ATTN_KV_HEADS, ATTN_GROUP, ATTN_HEAD_DIM) + shape[axis + 1:])
    return jnp.swapaxes(a, axis, axis + 1).reshape(shape)


def kernel(x, mem, norm_mix_pre, norm_mix_post, w_in, attn_sink, attn_out_norm,
           ret_decay_fwd, ret_decay_bwd, ret_gn, w_out, norm_xa_pre, norm_xa_post,
           norm_mem, xa_wq, xa_wkv, xa_wo, norm_ffn_pre, norm_ffn_post,
           ffn_w_gu, ffn_w_down):
    B, S, D = x.shape
    assert D == D_MODEL and S % max(TOK_TILE, MIX_TILE, STATE_CHUNKS * BLOCK) == 0
    assert mem.shape[1] == N_MEM
    depth = w_in.shape[0]
    o_k = ATTN_WIDTH
    o_v = o_k + ATTN_KV_WIDTH
    o_r = o_v + ATTN_KV_WIDTH
    rw = RET_WIDTH

    rope = _rope_tables(S // TOK_TILE)

    def row(v):
        return v.reshape(1, -1).astype(F32)

    def col(v):
        return jnp.broadcast_to(v.astype(F32)[:, None], (v.shape[0], LANES))

    for l in range(depth):
        w = w_in[l]
        w_t = jnp.concatenate([_pair_heads(w[:, :o_k], 1), w[:, o_r:o_r + rw], w[:, o_r + 2 * rw:],
                               w[:, o_v:o_r], w[:, o_k:o_v]], axis=1).T.astype(BF16)
        w_k = w[:, o_r + rw:o_r + 2 * rw].astype(BF16)
        w_o = jnp.concatenate([_pair_heads(w_out[l][:ATTN_WIDTH], 0), w_out[l][ATTN_WIDTH:]], axis=0)

        dmat, kdec, qdec, cdec = _decay_tables(ret_decay_fwd[l], ret_decay_bwd[l])
        zt, zk = _in_projection(x, row(norm_mix_pre[l]), w_t, w_k, rope)
        states_f, states_b = _retention_states(zt, zk, kdec, cdec)
        x = _mixer(x, zt, zk, states_f, states_b, dmat, qdec, attn_sink[l].astype(F32),
                   col(_pair_heads(attn_out_norm[l], 0)), col(ret_gn[l]),
                   w_o.astype(BF16), row(norm_mix_post[l]))
        kv = _memory_kv(mem, row(norm_mem[l]), xa_wkv[l].astype(BF16))
        x = _cross_attention(x, kv, row(norm_xa_pre[l]),
                             (xa_wq[l] * (XA_HEAD_DIM ** -0.5)).astype(BF16),
                             xa_wo[l].astype(BF16), row(norm_xa_post[l]))
        x = _ffn(x, row(norm_ffn_pre[l]), ffn_w_gu[l].astype(BF16),
                 ffn_w_down[l].astype(BF16), row(norm_ffn_post[l]))
    return x
```

```python
import math

import numpy as np
import jax
import jax.numpy as jnp
from jax import lax
from jax.experimental import pallas as pl
from jax.experimental.pallas import tpu as pltpu

D_MODEL = 1024
N_MEM = 256
ATTN_HEADS = 8
ATTN_KV_HEADS = 2
ATTN_GROUP = ATTN_HEADS // ATTN_KV_HEADS
ATTN_HEAD_DIM = 64
ATTN_WIDTH = ATTN_HEADS * ATTN_HEAD_DIM
ATTN_KV_WIDTH = ATTN_KV_HEADS * ATTN_HEAD_DIM
BLOCK = 128
RET_HEADS = 4
RET_HEAD_DIM = 128
RET_WIDTH = RET_HEADS * RET_HEAD_DIM
IN_COLS = ATTN_WIDTH + 2 * ATTN_KV_WIDTH + 4 * RET_WIDTH
XA_HEADS = 4
XA_HEAD_DIM = D_MODEL // XA_HEADS
D_FF = -(-(8 * D_MODEL) // (3 * 256)) * 256
ROPE_THETA = 10000.0
EPS = 1e-6
LOG2E = math.log2(math.e)

LANES = 128
BF16_ROWS = 16
MXU_TILE = 256
VMEM_LIMIT_BYTES = 56 * 1024 * 1024

TOK_TILE = 512
CHUNKS_PER_TILE = TOK_TILE // BLOCK
STATE_CHUNKS = 16
MIX_TILE = 1024
MIX_CHUNKS = MIX_TILE // BLOCK
ROW_PIECE = 256

T_AQ, T_RQ, T_RV, T_RG, T_AV = 0, 512, 1024, 1536, 2048
T_ROWS = T_AV + ATTN_KV_WIDTH
K_RK, K_AK = 0, RET_WIDTH
K_COLS = RET_WIDTH + ATTN_KV_WIDTH

F32 = jnp.float32
BF16 = jnp.bfloat16
NT_DIMS = (((1,), (1,)), ((), ()))
TN_DIMS = (((0,), (0,)), ((), ()))


def _const_spec(shape):
    nd = len(shape)
    return pl.BlockSpec(shape, lambda *_: (0,) * nd, pipeline_mode=pl.Buffered(1))


def _params(*sem, flags=None):
    return pltpu.CompilerParams(dimension_semantics=sem,
                                vmem_limit_bytes=VMEM_LIMIT_BYTES, flags=flags)


def _rms(x, w):
    ms = jnp.mean(x * x, axis=-1, keepdims=True)
    return x * lax.rsqrt(ms + EPS) * w


def _rms_unit(x):
    return x * lax.rsqrt(jnp.mean(x * x, axis=-1, keepdims=True) + EPS)


def _dot(a, b):
    return jnp.dot(a, b, preferred_element_type=F32)


def _rope_tables(n_tiles):
    def parts(dim):
        inv_freq = ROPE_THETA ** (-np.arange(0, dim, 2, dtype=np.float64) / dim)
        base = (np.arange(n_tiles, dtype=np.float64) * TOK_TILE)[:, None] * inv_freq[None, :]
        off = np.arange(TOK_TILE, dtype=np.float64)[:, None] * inv_freq[None, :]
        return np.cos(base), np.sin(base), np.cos(off), np.sin(off)

    def tok(c, s):
        return (np.concatenate([c, c], axis=1), np.concatenate([s, s], axis=1),
                np.concatenate([-s, s], axis=1))

    cb_a, sb_a, co_a, so_a = parts(ATTN_HEAD_DIM)
    cb_r, sb_r, co_r, so_r = parts(RET_HEAD_DIM)
    tok_base = np.stack(tok(cb_r, sb_r) + (np.zeros((n_tiles, LANES)),) * 5, axis=1)
    tok_off = np.stack(tok(co_r, so_r), axis=0)
    feat_base = np.concatenate([cb_a, sb_a, cb_r, sb_r], axis=1)
    feat_base = np.broadcast_to(feat_base[:, :, None], feat_base.shape + (BLOCK,))
    feat_off = np.concatenate([co_a, so_a, co_r, so_r], axis=1).T
    return tuple(jnp.asarray(t, F32) for t in (tok_base, tok_off, feat_base, feat_off))


def _decay_kernel(logit_ref, dmat_ref, kdec_ref, qdec_ref, cdec_ref):
    x = logit_ref[...]
    lg = jnp.minimum(x, 0.0) - jnp.log1p(jnp.exp(-jnp.abs(x)))
    ri = lax.broadcasted_iota(jnp.int32, (BLOCK, LANES), 0).astype(F32)
    ci = lax.broadcasted_iota(jnp.int32, (BLOCK, LANES), 1).astype(F32)
    c1 = ci[0:1, :]
    diff = ci - ri
    cdec_ref[...] = jnp.exp(BLOCK * lg)
    for h in range(RET_HEADS):
        lf = lg[h:h + 1, :]
        lb = lg[RET_HEADS + h:RET_HEADS + h + 1, :]
        dmat_ref[h] = jnp.exp(jnp.abs(diff) * jnp.where(diff >= 0, lf, lb))
        sl = slice(h * LANES, (h + 1) * LANES)
        kdec_ref[0, :, sl] = jnp.exp((BLOCK - 1 - ri) * lf)
        kdec_ref[1, :, sl] = jnp.exp(ri * lb)
        qdec_ref[h:h + 1, :] = jnp.exp((c1 + 1) * lf)
        qdec_ref[RET_HEADS + h:RET_HEADS + h + 1, :] = jnp.exp((BLOCK - c1) * lb)


def _decay_tables(decay_fwd, decay_bwd):
    logits = jnp.concatenate([decay_fwd, decay_bwd]).astype(F32)
    logits = jnp.broadcast_to(logits[:, None], (2 * RET_HEADS, LANES))
    return pl.pallas_call(
        _decay_kernel,
        out_shape=(jax.ShapeDtypeStruct((RET_HEADS, BLOCK, LANES), F32),
                   jax.ShapeDtypeStruct((2, BLOCK, RET_WIDTH), F32),
                   jax.ShapeDtypeStruct((2 * RET_HEADS, LANES), F32),
                   jax.ShapeDtypeStruct((2 * RET_HEADS, LANES), F32)),
        name="decay_tables",
    )(logits)


def _inproj_kernel(x_ref, wt_ref, wk_ref, tb_ref, to_ref, fb_ref, fo_ref,
                   zt_ref, zk_ref):
    h = _rms_unit(x_ref[0]).astype(BF16)

    tb = tb_ref[0]

    ck = tb[0:1] * to_ref[0] - tb[1:2] * to_ref[1]
    sk = tb[2:3] * to_ref[0] + tb[0:1] * to_ref[2]

    fb = jnp.concatenate([fb_ref[0]] * CHUNKS_PER_TILE, axis=1)
    fo = fo_ref[...]

    def feat_tables(row, n):
        cb, sb = fb[row:row + n], fb[row + n:row + 2 * n]
        co, so = fo[row:row + n], fo[row + n:row + 2 * n]
        return cb * co - sb * so, sb * co + cb * so

    zk = _dot(h, wk_ref[...])
    for g in range(RET_HEADS):
        zg = zk[:, g * LANES:(g + 1) * LANES]
        r = (zg * ck + pltpu.roll(zg, LANES // 2, 1) * sk) * (RET_HEAD_DIM ** -0.5)
        zk_ref[0, :, K_RK + g * LANES:K_RK + (g + 1) * LANES] = r.astype(BF16)

    def proj_t(row, n_rows):
        return lax.dot_general(wt_ref[row:row + n_rows, :], h, NT_DIMS,
                               preferred_element_type=F32)

    def store_t(row, val):
        for c in range(CHUNKS_PER_TILE):
            zt_ref[0, c, row:row + val.shape[0], :] = val[:, c * BLOCK:(c + 1) * BLOCK].astype(BF16)

    def rope_t(z, row, head_dim, cos, sin):
        hd2 = head_dim // 2
        for k in range(z.shape[0] // head_dim):
            x1 = z[k * head_dim:k * head_dim + hd2]
            x2 = z[k * head_dim + hd2:(k + 1) * head_dim]
            store_t(row + k * head_dim, x1 * cos - x2 * sin)
            store_t(row + k * head_dim + hd2, x2 * cos + x1 * sin)

    q_scale = (ATTN_HEAD_DIM ** -0.5) * LOG2E
    cat, sat = feat_tables(0, ATTN_HEAD_DIM // 2)
    rope_t(proj_t(T_AQ, ATTN_WIDTH), T_AQ, ATTN_HEAD_DIM, cat * q_scale, sat * q_scale)
    crt, srt = feat_tables(ATTN_HEAD_DIM, RET_HEAD_DIM // 2)
    rope_t(proj_t(T_RQ, RET_WIDTH), T_RQ, RET_HEAD_DIM, crt, srt)
    store_t(T_RV, proj_t(T_RV, RET_WIDTH))
    store_t(T_RG, proj_t(T_RG, RET_WIDTH))
    kv = proj_t(T_AV, 2 * ATTN_KV_WIDTH)
    store_t(T_AV, kv[0:ATTN_KV_WIDTH])
    hd2 = ATTN_HEAD_DIM // 2
    rows = []
    for k in range(ATTN_KV_HEADS):
        r0 = ATTN_KV_WIDTH + k * ATTN_HEAD_DIM
        x1, x2 = kv[r0:r0 + hd2], kv[r0 + hd2:r0 + ATTN_HEAD_DIM]
        rows += [x1 * cat - x2 * sat, x2 * cat + x1 * sat]
    zk_ref[0, :, K_AK:K_AK + ATTN_KV_WIDTH] = jnp.concatenate(rows, axis=0).T.astype(BF16)


def _in_projection(x, w_t, w_k, rope):
    B, S, _ = x.shape
    n = S // BLOCK
    tok_base, tok_off, feat_base, feat_off = rope
    return pl.pallas_call(
        _inproj_kernel,
        grid=(S // TOK_TILE, B),
        in_specs=[pl.BlockSpec((1, TOK_TILE, D_MODEL), lambda i, b: (b, i, 0)),
                  _const_spec((T_ROWS + ATTN_KV_WIDTH, D_MODEL)),
                  _const_spec((D_MODEL, RET_WIDTH)),
                  pl.BlockSpec((1,) + tok_base.shape[1:], lambda i, b: (i, 0, 0)),
                  _const_spec(tok_off.shape),
                  pl.BlockSpec((1,) + feat_base.shape[1:], lambda i, b: (i, 0, 0)),
                  _const_spec(feat_off.shape)],
        out_specs=(pl.BlockSpec((1, CHUNKS_PER_TILE, T_ROWS, BLOCK), lambda i, b: (b, i, 0, 0)),
                   pl.BlockSpec((1, TOK_TILE, K_COLS), lambda i, b: (b, i, 0))),
        out_shape=(jax.ShapeDtypeStruct((B, n, T_ROWS, BLOCK), BF16),
                   jax.ShapeDtypeStruct((B, S, K_COLS), BF16)),
        compiler_params=_params("parallel", "parallel"),
        name="in_projection",
    )(x, w_t, w_k, tok_base, tok_off, feat_base, feat_off)


def _state_kernel(kf_ref, vf_ref, kb_ref, vb_ref, kdec_ref, cdec_ref,
                  pf_ref, pb_ref, st_ref):
    @pl.when(pl.program_id(1) == 0)
    def _():
        st_ref[...] = jnp.zeros_like(st_ref)

    for i in range(STATE_CHUNKS):
        for d, (k_ref, v_ref, p_ref, c) in enumerate(((kf_ref, vf_ref, pf_ref, i),
                                                      (kb_ref, vb_ref, pb_ref, STATE_CHUNKS - 1 - i))):
            kd = (k_ref[0, c * BLOCK:(c + 1) * BLOCK, :].astype(F32) * kdec_ref[d]).astype(BF16)
            for h in range(RET_HEADS):
                sl = slice(h * LANES, (h + 1) * LANES)
                st = st_ref[d, h]
                p_ref[0, c, h] = st.astype(BF16)
                kv = _dot(v_ref[0, c, sl, :], kd[:, sl])
                row = d * RET_HEADS + h
                st_ref[d, h] = st * cdec_ref[row:row + 1, :] + kv


def _retention_states(zt, zk, kdec, cdec):
    B, n = zt.shape[0], zt.shape[1]
    ng = n // STATE_CHUNKS
    k_blk = (1, STATE_CHUNKS * BLOCK, RET_WIDTH)
    v_blk = (1, STATE_CHUNKS, RET_WIDTH, BLOCK)
    vrow = T_RV // RET_WIDTH
    st_blk = (1, STATE_CHUNKS, RET_HEADS, RET_HEAD_DIM, RET_HEAD_DIM)
    st_shape = jax.ShapeDtypeStruct((B, n, RET_HEADS, RET_HEAD_DIM, RET_HEAD_DIM), BF16)
    return pl.pallas_call(
        _state_kernel,
        grid=(B, ng),
        in_specs=[pl.BlockSpec(k_blk, lambda b, t: (b, t, 0)),
                  pl.BlockSpec(v_blk, lambda b, t: (b, t, vrow, 0)),
                  pl.BlockSpec(k_blk, lambda b, t: (b, ng - 1 - t, 0)),
                  pl.BlockSpec(v_blk, lambda b, t: (b, ng - 1 - t, vrow, 0)),
                  _const_spec((2, BLOCK, RET_WIDTH)),
                  _const_spec((2 * RET_HEADS, LANES))],
        out_specs=(pl.BlockSpec(st_blk, lambda b, t: (b, t, 0, 0, 0)),
                   pl.BlockSpec(st_blk, lambda b, t: (b, ng - 1 - t, 0, 0, 0))),
        out_shape=(st_shape, st_shape),
        scratch_shapes=[pltpu.VMEM((2, RET_HEADS, RET_HEAD_DIM, RET_HEAD_DIM), F32)],
        compiler_params=_params("parallel", "arbitrary"),
        name="retention_states",
    )(zk, zt, zk, zt, kdec, cdec)


def _mixer_kernel(sink_ref, x_ref, zt_ref, avp_ref, avn_ref, zk_ref, akp_ref, akn_ref,
                  pf_ref, pb_ref, dmat_ref, qdec_ref,
                  wout_ref, npost_ref, o_ref, kext_ref, vext_ref, mixt_ref,
                  sc_ref):
    tile = pl.program_id(1)
    n_blocks = pl.num_programs(1) * MIX_CHUNKS

    lane = lax.broadcasted_iota(jnp.int32, (1, LANES), 1)
    for g in range(ATTN_KV_HEADS):
        km = jnp.where((lane // ATTN_HEAD_DIM) == g, 1.0, 0.0).astype(BF16)
        kext_ref[g, 0:BLOCK] = akp_ref[0] * km
        kext_ref[g, BLOCK:BLOCK + MIX_TILE] = zk_ref[0, :, K_AK:K_AK + ATTN_KV_WIDTH] * km
        kext_ref[g, BLOCK + MIX_TILE:] = akn_ref[0] * km
    vext_ref[0] = avp_ref[0, 0]
    for c in range(MIX_CHUNKS):
        vext_ref[1 + c] = zt_ref[0, c, T_AV:T_AV + ATTN_KV_WIDTH, :]
    vext_ref[MIX_CHUNKS + 1] = avn_ref[0, 0]

    kk = lax.broadcasted_iota(jnp.int32, (BLOCK, LANES), 0)
    qq = lax.broadcasted_iota(jnp.int32, (BLOCK, LANES), 1)
    neg = jnp.full((BLOCK, LANES), -jnp.inf, F32)
    zero = jnp.zeros((BLOCK, LANES), F32)

    def scores_into(slot, c):
        q = jnp.concatenate([zt_ref[0, c, T_AQ + j * LANES:T_AQ + (j + 1) * LANES, :]
                             for j in range(ATTN_GROUP)], axis=1)
        r0 = c * BLOCK
        for g in range(ATTN_KV_HEADS):
            sc_ref[slot, g] = _dot(kext_ref[g, pl.ds(r0, 3 * BLOCK), :], q)

    def chunk(c, has_next):
        r0 = c * BLOCK
        blk = tile * MIX_CHUNKS + c
        slot = c % 2
        if has_next:
            scores_into(1 - slot, c + 1)

        vw = jnp.concatenate([vext_ref[c], vext_ref[c + 1], vext_ref[c + 2]], axis=1)
        no_prev = jnp.where(blk > 0, 0, BLOCK)
        no_next = jnp.where(blk < n_blocks - 1, 0, BLOCK)
        bias_prev = jnp.where(kk >= qq + no_prev, zero, neg)
        bias_next = jnp.where(kk <= qq - no_next, zero, neg)
        bias_prev = jnp.concatenate([bias_prev] * ATTN_GROUP, axis=1)
        bias_next = jnp.concatenate([bias_next] * ATTN_GROUP, axis=1)
        heads = []
        for h in range(RET_HEADS):
            qt = zt_ref[0, c, T_RQ + h * LANES:T_RQ + (h + 1) * LANES, :]
            kh = zk_ref[0, pl.ds(r0, BLOCK), K_RK + h * LANES:K_RK + (h + 1) * LANES]
            heads.append((qt, _dot(kh, qt)))

        ones = jnp.ones((BF16_ROWS, 3 * BLOCK), BF16)
        outs = []
        for g in range(ATTN_KV_HEADS):
            s = sc_ref[slot, g]
            s = jnp.concatenate([s[0:BLOCK] + bias_prev, s[BLOCK:2 * BLOCK],
                                 s[2 * BLOCK:] + bias_next], axis=0)
            sk = jnp.concatenate(
                [jnp.full((1, LANES), sink_ref[g * ATTN_GROUP + j] * LOG2E, F32)
                 for j in range(ATTN_GROUP)], axis=1)
            m = jnp.maximum(jnp.max(s, axis=0, keepdims=True), sk)
            p = jnp.exp2(s - m).astype(BF16)
            vg = jnp.concatenate([vw[g * ATTN_HEAD_DIM:(g + 1) * ATTN_HEAD_DIM, :], ones], axis=0)
            ov = _dot(vg, p)
            den = ov[ATTN_HEAD_DIM:ATTN_HEAD_DIM + 1, :] + jnp.exp2(sk - m)
            outs.append(ov[0:ATTN_HEAD_DIM, :] * (1.0 / den))
        ys = []
        for h in range(RET_HEADS):
            qt, pt = heads[h]
            y = _dot(zt_ref[0, c, T_RV + h * LANES:T_RV + (h + 1) * LANES, :],
                     (pt * dmat_ref[h]).astype(BF16))
            y = y + _dot(pf_ref[0, c, h], qt) * qdec_ref[h:h + 1, :]
            y = y + _dot(pb_ref[0, c, h], qt) * qdec_ref[RET_HEADS + h:RET_HEADS + h + 1, :]
            ys.append(y)

        a = jnp.concatenate([outs[g][:, j * LANES:(j + 1) * LANES]
                             for j in range(ATTN_GROUP) for g in range(ATTN_KV_HEADS)], axis=0)
        ms = jnp.sum(a * a, axis=0, keepdims=True) * (1.0 / ATTN_WIDTH)
        mixt_ref[c, 0:ATTN_WIDTH, :] = (a * lax.rsqrt(ms + EPS)).astype(BF16)
        for h in range(RET_HEADS):
            y = ys[h]
            mu = jnp.mean(y, axis=0, keepdims=True)
            yc = y - mu
            var = jnp.mean(yc * yc, axis=0, keepdims=True)
            yn = yc * lax.rsqrt(var + EPS)
            gate = zt_ref[0, c, T_RG + h * LANES:T_RG + (h + 1) * LANES, :].astype(F32)
            r = gate * jax.nn.sigmoid(gate) * yn
            mixt_ref[c, ATTN_WIDTH + h * LANES:ATTN_WIDTH + (h + 1) * LANES, :] = r.astype(BF16)

    scores_into(0, 0)
    for c in range(MIX_CHUNKS):
        chunk(c, c + 1 < MIX_CHUNKS)

    mixt = jnp.concatenate([mixt_ref[c] for c in range(MIX_CHUNKS)], axis=1)
    mix = lax.dot_general(mixt, wout_ref[...], TN_DIMS, preferred_element_type=F32)
    o_ref[0] = x_ref[0] + _rms(mix, npost_ref[...])


def _mixer(x, zt, zk, states_f, states_b, dmat, qdec, sink, w_out, norm_post):
    B, S, _ = x.shape
    n = S // BLOCK
    cpt = MIX_CHUNKS
    edge_t = (1, 1, ATTN_KV_WIDTH, BLOCK)
    edge_k = (1, BLOCK, ATTN_KV_WIDTH)
    v_row = T_AV // ATTN_KV_WIDTH
    k_col = K_AK // ATTN_KV_WIDTH
    st_blk = (1, cpt, RET_HEADS, RET_HEAD_DIM, RET_HEAD_DIM)

    def prev(i):
        return jnp.maximum(i * cpt - 1, 0)

    def nxt(i):
        return jnp.minimum((i + 1) * cpt, n - 1)

    return pl.pallas_call(
        _mixer_kernel,
        grid=(B, S // MIX_TILE),
        in_specs=[pl.BlockSpec(memory_space=pltpu.SMEM),
                  pl.BlockSpec((1, MIX_TILE, D_MODEL), lambda b, i: (b, i, 0)),
                  pl.BlockSpec((1, cpt, T_ROWS, BLOCK), lambda b, i: (b, i, 0, 0)),
                  pl.BlockSpec(edge_t, lambda b, i: (b, prev(i), v_row, 0)),
                  pl.BlockSpec(edge_t, lambda b, i: (b, nxt(i), v_row, 0)),
                  pl.BlockSpec((1, MIX_TILE, K_COLS), lambda b, i: (b, i, 0)),
                  pl.BlockSpec(edge_k, lambda b, i: (b, prev(i), k_col)),
                  pl.BlockSpec(edge_k, lambda b, i: (b, nxt(i), k_col)),
                  pl.BlockSpec(st_blk, lambda b, i: (b, i, 0, 0, 0)),
                  pl.BlockSpec(st_blk, lambda b, i: (b, i, 0, 0, 0)),
                  _const_spec((RET_HEADS, BLOCK, LANES)),
                  _const_spec((2 * RET_HEADS, LANES)),
                  _const_spec((D_MODEL, D_MODEL)),
                  _const_spec((1, D_MODEL))],
        out_specs=pl.BlockSpec((1, MIX_TILE, D_MODEL), lambda b, i: (b, i, 0)),
        out_shape=jax.ShapeDtypeStruct((B, S, D_MODEL), F32),
        scratch_shapes=[pltpu.VMEM((ATTN_KV_HEADS, MIX_TILE + 2 * BLOCK, ATTN_KV_WIDTH), BF16),
                        pltpu.VMEM((cpt + 2, ATTN_KV_WIDTH, BLOCK), BF16),
                        pltpu.VMEM((cpt, D_MODEL, BLOCK), BF16),
                        pltpu.VMEM((2, ATTN_KV_HEADS, 3 * BLOCK, ATTN_GROUP * LANES), F32)],
        compiler_params=_params("parallel", "parallel"),
        name="token_mixer",
    )(sink, x, zt, zt, zt, zk, zk, zk, states_f, states_b, dmat, qdec, w_out, norm_post)


def _memkv_kernel(m_ref, w_ref, kv_ref):
    mn = _rms_unit(m_ref[0]).astype(BF16)
    kv_ref[0] = _dot(mn, w_ref[...]).astype(BF16)


def _memory_kv(mem, w_kv):
    B, M, _ = mem.shape
    return pl.pallas_call(
        _memkv_kernel,
        grid=(B,),
        in_specs=[pl.BlockSpec((1, M, D_MODEL), lambda b: (b, 0, 0)),
                  _const_spec((D_MODEL, 2 * D_MODEL))],
        out_specs=pl.BlockSpec((1, M, 2 * D_MODEL), lambda b: (b, 0, 0)),
        out_shape=jax.ShapeDtypeStruct((B, M, 2 * D_MODEL), BF16),
        compiler_params=_params("parallel"),
        name="memory_kv",
    )(mem, w_kv)


def _xattn_kernel(x_ref, wq_ref, km_ref, vm_ref, wo_ref, gpost_ref,
                  o_ref, cat_ref, sc_ref, h_ref, q_ref, r_ref):
    pieces = [slice(r * ROW_PIECE, (r + 1) * ROW_PIECE) for r in range(TOK_TILE // ROW_PIECE)]
    for rows in pieces:
        h_ref[rows, :] = _rms_unit(x_ref[0, rows, :]).astype(BF16)
    for rows in pieces:
        q_ref[rows, :] = _dot(h_ref[rows, :], wq_ref[...]).astype(BF16)
    for hd in range(XA_HEADS):
        sl = slice(hd * XA_HEAD_DIM, (hd + 1) * XA_HEAD_DIM)
        sc_ref[hd] = lax.dot_general(q_ref[:, sl], km_ref[0, :, sl], NT_DIMS,
                                     preferred_element_type=F32)
    for hd in range(XA_HEADS):
        sl = slice(hd * XA_HEAD_DIM, (hd + 1) * XA_HEAD_DIM)
        s = sc_ref[hd]
        p = jnp.exp(s - jnp.max(s, axis=-1, keepdims=True))
        den = jnp.sum(p, axis=-1, keepdims=True)
        cat_ref[:, sl] = (_dot(p.astype(BF16), vm_ref[0, :, sl]) * (1.0 / den)).astype(BF16)
    for rows in pieces:
        r_ref[rows, :] = _dot(cat_ref[rows, :], wo_ref[...])
    for rows in pieces:
        o_ref[0, rows, :] = x_ref[0, rows, :] + _rms(r_ref[rows, :], gpost_ref[...])


def _cross_attention(x, kv, w_q, w_o, gain_post):
    B, S, _ = x.shape
    M = kv.shape[1]
    tok = pl.BlockSpec((1, TOK_TILE, D_MODEL), lambda b, i: (b, i, 0))
    return pl.pallas_call(
        _xattn_kernel,
        grid=(B, S // TOK_TILE),
        in_specs=[tok,
                  _const_spec((D_MODEL, D_MODEL)),
                  pl.BlockSpec((1, M, D_MODEL), lambda b, i: (b, 0, 0)),
                  pl.BlockSpec((1, M, D_MODEL), lambda b, i: (b, 0, 1)),
                  _const_spec((D_MODEL, D_MODEL)),
                  _const_spec((1, D_MODEL))],
        out_specs=tok,
        out_shape=jax.ShapeDtypeStruct((B, S, D_MODEL), F32),
        scratch_shapes=[pltpu.VMEM((TOK_TILE, D_MODEL), BF16),
                        pltpu.VMEM((XA_HEADS, TOK_TILE, N_MEM), F32),
                        pltpu.VMEM((TOK_TILE, D_MODEL), BF16),
                        pltpu.VMEM((TOK_TILE, D_MODEL), BF16),
                        pltpu.VMEM((TOK_TILE, D_MODEL), F32)],
        compiler_params=_params("parallel", "parallel"),
        name="cross_attention",
    )(x, w_q, kv, kv, w_o, gain_post)


FF_CHUNK = 2 * MXU_TILE
FF_CHUNKS = [(c0, min(FF_CHUNK, D_FF - c0)) for c0 in range(0, D_FF, FF_CHUNK)]
assert all(w % MXU_TILE == 0 for _, w in FF_CHUNKS)


def _ffn_kernel(x_ref, wgu_ref, wd_ref, gpost_ref, o_ref, gu_ref, act_ref,
                h_ref, f_ref):
    pieces = [slice(r * ROW_PIECE, (r + 1) * ROW_PIECE) for r in range(TOK_TILE // ROW_PIECE)]

    def gate_up_into(slot, j, rows=slice(None)):
        c0, w = FF_CHUNKS[j]
        gu_ref[slot, 0, rows, 0:w] = _dot(h_ref[rows, :], wgu_ref[:, c0:c0 + w])
        gu_ref[slot, 1, rows, 0:w] = _dot(h_ref[rows, :], wgu_ref[:, D_FF + c0:D_FF + c0 + w])

    for rows in pieces:
        h_ref[rows, :] = _rms_unit(x_ref[0, rows, :]).astype(BF16)
    for rows in pieces:
        gate_up_into(0, 0, rows)
    for j, (c0, w) in enumerate(FF_CHUNKS):
        if j + 1 < len(FF_CHUNKS):
            gate_up_into((j + 1) % 2, j + 1)
        gate = gu_ref[j % 2, 0, :, 0:w]
        up = gu_ref[j % 2, 1, :, 0:w]
        act_ref[:, c0:c0 + w] = (gate * jax.nn.sigmoid(gate) * up).astype(BF16)
    for rows in pieces:
        f_ref[rows, :] = _dot(act_ref[rows, :], wd_ref[...])
    for rows in pieces:
        o_ref[0, rows, :] = x_ref[0, rows, :] + _rms(f_ref[rows, :], gpost_ref[...])


def _ffn(x, w_gu, w_down, gain_post):
    B, S, _ = x.shape
    tok = pl.BlockSpec((1, TOK_TILE, D_MODEL), lambda b, i: (b, i, 0))
    return pl.pallas_call(
        _ffn_kernel,
        grid=(B, S // TOK_TILE),
        in_specs=[tok,
                  _const_spec((D_MODEL, 2 * D_FF)),
                  _const_spec((D_FF, D_MODEL)),
                  _const_spec((1, D_MODEL))],
        out_specs=tok,
        out_shape=jax.ShapeDtypeStruct((B, S, D_MODEL), F32),
        scratch_shapes=[pltpu.VMEM((2, 2, TOK_TILE, FF_CHUNK), F32),
                        pltpu.VMEM((TOK_TILE, D_FF), BF16),
                        pltpu.VMEM((TOK_TILE, D_MODEL), BF16),
                        pltpu.VMEM((TOK_TILE, D_MODEL), F32)],
        compiler_params=_params("parallel", "parallel"),
        name="swiglu_ffn",
    )(x, w_gu, w_down, gain_post)


def _pair_heads(a, axis):
    shape = a.shape
    a = a.reshape(shape[:axis] + (ATTN_KV_HEADS, ATTN_GROUP, ATTN_HEAD_DIM) + shape[axis + 1:])
    return jnp.swapaxes(a, axis, axis + 1).reshape(shape)


def kernel(x, mem, norm_mix_pre, norm_mix_post, w_in, attn_sink, attn_out_norm,
           ret_decay_fwd, ret_decay_bwd, ret_gn, w_out, norm_xa_pre, norm_xa_post,
           norm_mem, xa_wq, xa_wkv, xa_wo, norm_ffn_pre, norm_ffn_post,
           ffn_w_gu, ffn_w_down):
    B, S, D = x.shape
    assert D == D_MODEL and S % max(TOK_TILE, MIX_TILE, STATE_CHUNKS * BLOCK) == 0
    assert mem.shape[1] == N_MEM
    depth = w_in.shape[0]
    o_k = ATTN_WIDTH
    o_v = o_k + ATTN_KV_WIDTH
    o_r = o_v + ATTN_KV_WIDTH
    rw = RET_WIDTH

    rope = _rope_tables(S // TOK_TILE)

    def row(v):
        return v.reshape(1, -1).astype(F32)

    def rows_scaled(w, gain):
        return w * gain.astype(F32)[:, None]

    for l in range(depth):
        w = rows_scaled(w_in[l], norm_mix_pre[l])
        w_t = jnp.concatenate([_pair_heads(w[:, :o_k], 1), w[:, o_r:o_r + rw], w[:, o_r + 2 * rw:],
                               w[:, o_v:o_r], w[:, o_k:o_v]], axis=1).T.astype(BF16)
        w_k = w[:, o_r + rw:o_r + 2 * rw].astype(BF16)
        mix_gain = jnp.concatenate([_pair_heads(attn_out_norm[l], 0), ret_gn[l]])
        w_o = jnp.concatenate([_pair_heads(w_out[l][:ATTN_WIDTH], 0), w_out[l][ATTN_WIDTH:]], axis=0)
        w_o = rows_scaled(w_o, mix_gain).astype(BF16)

        dmat, kdec, qdec, cdec = _decay_tables(ret_decay_fwd[l], ret_decay_bwd[l])
        zt, zk = _in_projection(x, w_t, w_k, rope)
        states_f, states_b = _retention_states(zt, zk, kdec, cdec)
        x = _mixer(x, zt, zk, states_f, states_b, dmat, qdec, attn_sink[l].astype(F32),
                   w_o, row(norm_mix_post[l]))
        kv = _memory_kv(mem, rows_scaled(xa_wkv[l], norm_mem[l]).astype(BF16))
        w_q = rows_scaled(xa_wq[l], norm_xa_pre[l]) * (XA_HEAD_DIM ** -0.5)
        x = _cross_attention(x, kv, w_q.astype(BF16), xa_wo[l].astype(BF16), row(norm_xa_post[l]))
        x = _ffn(x, rows_scaled(ffn_w_gu[l], norm_ffn_pre[l]).astype(BF16),
                 ffn_w_down[l].astype(BF16), row(norm_ffn_post[l]))
    return x
```

```python
import math

import numpy as np
import jax
import jax.numpy as jnp
from jax import lax
from jax.experimental import pallas as pl
from jax.experimental.pallas import tpu as pltpu

D_MODEL = 1024
N_MEM = 256
ATTN_HEADS = 8
ATTN_KV_HEADS = 2
ATTN_GROUP = ATTN_HEADS // ATTN_KV_HEADS
ATTN_HEAD_DIM = 64
ATTN_WIDTH = ATTN_HEADS * ATTN_HEAD_DIM
ATTN_KV_WIDTH = ATTN_KV_HEADS * ATTN_HEAD_DIM
BLOCK = 128
RET_HEADS = 4
RET_HEAD_DIM = 128
RET_WIDTH = RET_HEADS * RET_HEAD_DIM
IN_COLS = ATTN_WIDTH + 2 * ATTN_KV_WIDTH + 4 * RET_WIDTH
XA_HEADS = 4
XA_HEAD_DIM = D_MODEL // XA_HEADS
D_FF = -(-(8 * D_MODEL) // (3 * 256)) * 256
ROPE_THETA = 10000.0
EPS = 1e-6
LOG2E = math.log2(math.e)

LANES = 128
BF16_ROWS = 16
MXU_TILE = 256
VMEM_LIMIT_BYTES = 56 * 1024 * 1024

TOK_TILE = 512
CHUNKS_PER_TILE = TOK_TILE // BLOCK
STATE_CHUNKS = 16
MIX_TILE = 1024
MIX_CHUNKS = MIX_TILE // BLOCK
ROW_PIECE = 256

T_AQ, T_RQ, T_RV, T_RG, T_AV = 0, 512, 1024, 1536, 2048
T_ROWS = T_AV + ATTN_KV_WIDTH
K_RK, K_AK = 0, RET_WIDTH
K_COLS = RET_WIDTH + ATTN_KV_WIDTH

F32 = jnp.float32
BF16 = jnp.bfloat16
NT_DIMS = (((1,), (1,)), ((), ()))
TN_DIMS = (((0,), (0,)), ((), ()))


def _const_spec(shape):
    nd = len(shape)
    return pl.BlockSpec(shape, lambda *_: (0,) * nd, pipeline_mode=pl.Buffered(1))


def _params(*sem, flags=None):
    return pltpu.CompilerParams(dimension_semantics=sem,
                                vmem_limit_bytes=VMEM_LIMIT_BYTES, flags=flags)


def _rms(x, w):
    ms = jnp.mean(x * x, axis=-1, keepdims=True)
    return x * lax.rsqrt(ms + EPS) * w


def _dot(a, b):
    return jnp.dot(a, b, preferred_element_type=F32)


def _rope_tables(n_tiles):
    def parts(dim):
        inv_freq = ROPE_THETA ** (-np.arange(0, dim, 2, dtype=np.float64) / dim)
        base = (np.arange(n_tiles, dtype=np.float64) * TOK_TILE)[:, None] * inv_freq[None, :]
        off = np.arange(TOK_TILE, dtype=np.float64)[:, None] * inv_freq[None, :]
        return np.cos(base), np.sin(base), np.cos(off), np.sin(off)

    def tok(c, s):
        return (np.concatenate([c, c], axis=1), np.concatenate([s, s], axis=1),
                np.concatenate([-s, s], axis=1))

    cb_a, sb_a, co_a, so_a = parts(ATTN_HEAD_DIM)
    cb_r, sb_r, co_r, so_r = parts(RET_HEAD_DIM)
    tok_base = np.stack(tok(cb_r, sb_r) + (np.zeros((n_tiles, LANES)),) * 5, axis=1)
    tok_off = np.stack(tok(co_r, so_r), axis=0)
    feat_base = np.concatenate([cb_a, sb_a, cb_r, sb_r], axis=1)
    feat_base = np.broadcast_to(feat_base[:, :, None], feat_base.shape + (BLOCK,))
    feat_off = np.concatenate([co_a, so_a, co_r, so_r], axis=1).T
    return tuple(jnp.asarray(t, F32) for t in (tok_base, tok_off, feat_base, feat_off))


def _decay_kernel(logit_ref, dmat_ref, kdec_ref, qdec_ref, cdec_ref):
    x = logit_ref[...]
    lg = jnp.minimum(x, 0.0) - jnp.log1p(jnp.exp(-jnp.abs(x)))
    ri = lax.broadcasted_iota(jnp.int32, (BLOCK, LANES), 0).astype(F32)
    ci = lax.broadcasted_iota(jnp.int32, (BLOCK, LANES), 1).astype(F32)
    c1 = ci[0:1, :]
    diff = ci - ri
    cdec_ref[...] = jnp.exp(BLOCK * lg)
    for h in range(RET_HEADS):
        lf = lg[h:h + 1, :]
        lb = lg[RET_HEADS + h:RET_HEADS + h + 1, :]
        dmat_ref[h] = jnp.exp(jnp.abs(diff) * jnp.where(diff >= 0, lf, lb))
        sl = slice(h * LANES, (h + 1) * LANES)
        kdec_ref[0, :, sl] = jnp.exp((BLOCK - 1 - ri) * lf)
        kdec_ref[1, :, sl] = jnp.exp(ri * lb)
        qdec_ref[h:h + 1, :] = jnp.exp((c1 + 1) * lf)
        qdec_ref[RET_HEADS + h:RET_HEADS + h + 1, :] = jnp.exp((BLOCK - c1) * lb)


def _decay_tables(decay_fwd, decay_bwd):
    logits = jnp.concatenate([decay_fwd, decay_bwd]).astype(F32)
    logits = jnp.broadcast_to(logits[:, None], (2 * RET_HEADS, LANES))
    return pl.pallas_call(
        _decay_kernel,
        out_shape=(jax.ShapeDtypeStruct((RET_HEADS, BLOCK, LANES), F32),
                   jax.ShapeDtypeStruct((2, BLOCK, RET_WIDTH), F32),
                   jax.ShapeDtypeStruct((2 * RET_HEADS, LANES), F32),
                   jax.ShapeDtypeStruct((2 * RET_HEADS, LANES), F32)),
        name="decay_tables",
    )(logits)


def _inproj_kernel(x_ref, g_ref, wt_ref, wk_ref, tb_ref, to_ref, fb_ref, fo_ref,
                   zt_ref, zk_ref):
    h = _rms(x_ref[0], g_ref[...]).astype(BF16)

    tb = tb_ref[0]

    ck = tb[0:1] * to_ref[0] - tb[1:2] * to_ref[1]
    sk = tb[2:3] * to_ref[0] + tb[0:1] * to_ref[2]

    fb = jnp.concatenate([fb_ref[0]] * CHUNKS_PER_TILE, axis=1)
    fo = fo_ref[...]

    def feat_tables(row, n):
        cb, sb = fb[row:row + n], fb[row + n:row + 2 * n]
        co, so = fo[row:row + n], fo[row + n:row + 2 * n]
        return cb * co - sb * so, sb * co + cb * so

    zk = _dot(h, wk_ref[...])
    for g in range(RET_HEADS):
        zg = zk[:, g * LANES:(g + 1) * LANES]
        r = (zg * ck + pltpu.roll(zg, LANES // 2, 1) * sk) * (RET_HEAD_DIM ** -0.5)
        zk_ref[0, :, K_RK + g * LANES:K_RK + (g + 1) * LANES] = r.astype(BF16)

    def proj_t(row, n_rows):
        return lax.dot_general(wt_ref[row:row + n_rows, :], h, NT_DIMS,
                               preferred_element_type=F32)

    def store_t(row, val):
        for c in range(CHUNKS_PER_TILE):
            zt_ref[0, c, row:row + val.shape[0], :] = val[:, c * BLOCK:(c + 1) * BLOCK].astype(BF16)

    def rope_t(z, row, head_dim, cos, sin):
        hd2 = head_dim // 2
        for k in range(z.shape[0] // head_dim):
            x1 = z[k * head_dim:k * head_dim + hd2]
            x2 = z[k * head_dim + hd2:(k + 1) * head_dim]
            store_t(row + k * head_dim, x1 * cos - x2 * sin)
            store_t(row + k * head_dim + hd2, x2 * cos + x1 * sin)

    q_scale = (ATTN_HEAD_DIM ** -0.5) * LOG2E
    cat, sat = feat_tables(0, ATTN_HEAD_DIM // 2)
    rope_t(proj_t(T_AQ, ATTN_WIDTH), T_AQ, ATTN_HEAD_DIM, cat * q_scale, sat * q_scale)
    crt, srt = feat_tables(ATTN_HEAD_DIM, RET_HEAD_DIM // 2)
    rope_t(proj_t(T_RQ, RET_WIDTH), T_RQ, RET_HEAD_DIM, crt, srt)
    store_t(T_RV, proj_t(T_RV, RET_WIDTH))
    store_t(T_RG, proj_t(T_RG, RET_WIDTH))
    kv = proj_t(T_AV, 2 * ATTN_KV_WIDTH)
    store_t(T_AV, kv[0:ATTN_KV_WIDTH])
    hd2 = ATTN_HEAD_DIM // 2
    rows = []
    for k in range(ATTN_KV_HEADS):
        r0 = ATTN_KV_WIDTH + k * ATTN_HEAD_DIM
        x1, x2 = kv[r0:r0 + hd2], kv[r0 + hd2:r0 + ATTN_HEAD_DIM]
        rows += [x1 * cat - x2 * sat, x2 * cat + x1 * sat]
    zk_ref[0, :, K_AK:K_AK + ATTN_KV_WIDTH] = jnp.concatenate(rows, axis=0).T.astype(BF16)


def _in_projection(x, gain, w_t, w_k, rope):
    B, S, _ = x.shape
    n = S // BLOCK
    tok_base, tok_off, feat_base, feat_off = rope
    return pl.pallas_call(
        _inproj_kernel,
        grid=(S // TOK_TILE, B),
        in_specs=[pl.BlockSpec((1, TOK_TILE, D_MODEL), lambda i, b: (b, i, 0)),
                  _const_spec((1, D_MODEL)),
                  _const_spec((T_ROWS + ATTN_KV_WIDTH, D_MODEL)),
                  _const_spec((D_MODEL, RET_WIDTH)),
                  pl.BlockSpec((1,) + tok_base.shape[1:], lambda i, b: (i, 0, 0)),
                  _const_spec(tok_off.shape),
                  pl.BlockSpec((1,) + feat_base.shape[1:], lambda i, b: (i, 0, 0)),
                  _const_spec(feat_off.shape)],
        out_specs=(pl.BlockSpec((1, CHUNKS_PER_TILE, T_ROWS, BLOCK), lambda i, b: (b, i, 0, 0)),
                   pl.BlockSpec((1, TOK_TILE, K_COLS), lambda i, b: (b, i, 0))),
        out_shape=(jax.ShapeDtypeStruct((B, n, T_ROWS, BLOCK), BF16),
                   jax.ShapeDtypeStruct((B, S, K_COLS), BF16)),
        compiler_params=_params("parallel", "parallel"),
        name="in_projection",
    )(x, gain, w_t, w_k, tok_base, tok_off, feat_base, feat_off)


def _state_kernel(kf_ref, vf_ref, kb_ref, vb_ref, kdec_ref, cdec_ref,
                  pf_ref, pb_ref, st_ref):
    @pl.when(pl.program_id(1) == 0)
    def _():
        st_ref[...] = jnp.zeros_like(st_ref)

    for i in range(STATE_CHUNKS):
        for d, (k_ref, v_ref, p_ref, c) in enumerate(((kf_ref, vf_ref, pf_ref, i),
                                                      (kb_ref, vb_ref, pb_ref, STATE_CHUNKS - 1 - i))):
            kd = (k_ref[0, c * BLOCK:(c + 1) * BLOCK, :].astype(F32) * kdec_ref[d]).astype(BF16)
            for h in range(RET_HEADS):
                sl = slice(h * LANES, (h + 1) * LANES)
                st = st_ref[d, h]
                p_ref[0, c, h] = st.astype(BF16)
                kv = _dot(v_ref[0, c, sl, :], kd[:, sl])
                row = d * RET_HEADS + h
                st_ref[d, h] = st * cdec_ref[row:row + 1, :] + kv


def _retention_states(zt, zk, kdec, cdec):
    B, n = zt.shape[0], zt.shape[1]
    ng = n // STATE_CHUNKS
    k_blk = (1, STATE_CHUNKS * BLOCK, RET_WIDTH)
    v_blk = (1, STATE_CHUNKS, RET_WIDTH, BLOCK)
    vrow = T_RV // RET_WIDTH
    st_blk = (1, STATE_CHUNKS, RET_HEADS, RET_HEAD_DIM, RET_HEAD_DIM)
    st_shape = jax.ShapeDtypeStruct((B, n, RET_HEADS, RET_HEAD_DIM, RET_HEAD_DIM), BF16)
    return pl.pallas_call(
        _state_kernel,
        grid=(B, ng),
        in_specs=[pl.BlockSpec(k_blk, lambda b, t: (b, t, 0)),
                  pl.BlockSpec(v_blk, lambda b, t: (b, t, vrow, 0)),
                  pl.BlockSpec(k_blk, lambda b, t: (b, ng - 1 - t, 0)),
                  pl.BlockSpec(v_blk, lambda b, t: (b, ng - 1 - t, vrow, 0)),
                  _const_spec((2, BLOCK, RET_WIDTH)),
                  _const_spec((2 * RET_HEADS, LANES))],
        out_specs=(pl.BlockSpec(st_blk, lambda b, t: (b, t, 0, 0, 0)),
                   pl.BlockSpec(st_blk, lambda b, t: (b, ng - 1 - t, 0, 0, 0))),
        out_shape=(st_shape, st_shape),
        scratch_shapes=[pltpu.VMEM((2, RET_HEADS, RET_HEAD_DIM, RET_HEAD_DIM), F32)],
        compiler_params=_params("parallel", "arbitrary"),
        name="retention_states",
    )(zk, zt, zk, zt, kdec, cdec)


def _mixer_kernel(sink_ref, x_ref, zt_ref, avp_ref, avn_ref, zk_ref, akp_ref, akn_ref,
                  pf_ref, pb_ref, dmat_ref, qdec_ref,
                  an_ref, gn_ref, wout_ref, npost_ref, o_ref, kext_ref, vext_ref, mixt_ref,
                  sc_ref):
    tile = pl.program_id(1)
    n_blocks = pl.num_programs(1) * MIX_CHUNKS

    lane = lax.broadcasted_iota(jnp.int32, (1, LANES), 1)
    for g in range(ATTN_KV_HEADS):
        km = jnp.where((lane // ATTN_HEAD_DIM) == g, 1.0, 0.0).astype(BF16)
        kext_ref[g, 0:BLOCK] = akp_ref[0] * km
        kext_ref[g, BLOCK:BLOCK + MIX_TILE] = zk_ref[0, :, K_AK:K_AK + ATTN_KV_WIDTH] * km
        kext_ref[g, BLOCK + MIX_TILE:] = akn_ref[0] * km
    vext_ref[0] = avp_ref[0, 0]
    for c in range(MIX_CHUNKS):
        vext_ref[1 + c] = zt_ref[0, c, T_AV:T_AV + ATTN_KV_WIDTH, :]
    vext_ref[MIX_CHUNKS + 1] = avn_ref[0, 0]

    kk = lax.broadcasted_iota(jnp.int32, (BLOCK, LANES), 0)
    qq = lax.broadcasted_iota(jnp.int32, (BLOCK, LANES), 1)
    neg = jnp.full((BLOCK, LANES), -jnp.inf, F32)
    zero = jnp.zeros((BLOCK, LANES), F32)

    def scores_into(slot, c):
        q = jnp.concatenate([zt_ref[0, c, T_AQ + j * LANES:T_AQ + (j + 1) * LANES, :]
                             for j in range(ATTN_GROUP)], axis=1)
        r0 = c * BLOCK
        for g in range(ATTN_KV_HEADS):
            sc_ref[slot, g] = _dot(kext_ref[g, pl.ds(r0, 3 * BLOCK), :], q)

    def chunk(c, has_next):
        r0 = c * BLOCK
        blk = tile * MIX_CHUNKS + c
        slot = c % 2
        if has_next:
            scores_into(1 - slot, c + 1)

        vw = jnp.concatenate([vext_ref[c], vext_ref[c + 1], vext_ref[c + 2]], axis=1)
        no_prev = jnp.where(blk > 0, 0, BLOCK)
        no_next = jnp.where(blk < n_blocks - 1, 0, BLOCK)
        bias_prev = jnp.where(kk >= qq + no_prev, zero, neg)
        bias_next = jnp.where(kk <= qq - no_next, zero, neg)
        bias_prev = jnp.concatenate([bias_prev] * ATTN_GROUP, axis=1)
        bias_next = jnp.concatenate([bias_next] * ATTN_GROUP, axis=1)
        heads = []
        for h in range(RET_HEADS):
            qt = zt_ref[0, c, T_RQ + h * LANES:T_RQ + (h + 1) * LANES, :]
            kh = zk_ref[0, pl.ds(r0, BLOCK), K_RK + h * LANES:K_RK + (h + 1) * LANES]
            heads.append((qt, _dot(kh, qt)))

        ones = jnp.ones((BF16_ROWS, 3 * BLOCK), BF16)
        outs = []
        for g in range(ATTN_KV_HEADS):
            s = sc_ref[slot, g]
            s = jnp.concatenate([s[0:BLOCK] + bias_prev, s[BLOCK:2 * BLOCK],
                                 s[2 * BLOCK:] + bias_next], axis=0)
            sk = jnp.concatenate(
                [jnp.full((1, LANES), sink_ref[g * ATTN_GROUP + j] * LOG2E, F32)
                 for j in range(ATTN_GROUP)], axis=1)
            m = jnp.maximum(jnp.max(s, axis=0, keepdims=True), sk)
            p = jnp.exp2(s - m).astype(BF16)
            vg = jnp.concatenate([vw[g * ATTN_HEAD_DIM:(g + 1) * ATTN_HEAD_DIM, :], ones], axis=0)
            ov = _dot(vg, p)
            den = ov[ATTN_HEAD_DIM:ATTN_HEAD_DIM + 1, :] + jnp.exp2(sk - m)
            outs.append(ov[0:ATTN_HEAD_DIM, :] * (1.0 / den))
        ys = []
        for h in range(RET_HEADS):
            qt, pt = heads[h]
            y = _dot(zt_ref[0, c, T_RV + h * LANES:T_RV + (h + 1) * LANES, :],
                     (pt * dmat_ref[h]).astype(BF16))
            y = y + _dot(pf_ref[0, c, h], qt) * qdec_ref[h:h + 1, :]
            y = y + _dot(pb_ref[0, c, h], qt) * qdec_ref[RET_HEADS + h:RET_HEADS + h + 1, :]
            ys.append(y)

        a = jnp.concatenate([outs[g][:, j * LANES:(j + 1) * LANES]
                             for j in range(ATTN_GROUP) for g in range(ATTN_KV_HEADS)], axis=0)
        ms = jnp.sum(a * a, axis=0, keepdims=True) * (1.0 / ATTN_WIDTH)
        mixt_ref[c, 0:ATTN_WIDTH, :] = (a * lax.rsqrt(ms + EPS) * an_ref[...]).astype(BF16)
        for h in range(RET_HEADS):
            sl = slice(h * LANES, (h + 1) * LANES)
            y = ys[h]
            mu = jnp.mean(y, axis=0, keepdims=True)
            yc = y - mu
            var = jnp.mean(yc * yc, axis=0, keepdims=True)
            yn = yc * lax.rsqrt(var + EPS) * gn_ref[sl, :]
            gate = zt_ref[0, c, T_RG + h * LANES:T_RG + (h + 1) * LANES, :].astype(F32)
            r = gate * jax.nn.sigmoid(gate) * yn
            mixt_ref[c, ATTN_WIDTH + h * LANES:ATTN_WIDTH + (h + 1) * LANES, :] = r.astype(BF16)

    scores_into(0, 0)
    for c in range(MIX_CHUNKS):
        chunk(c, c + 1 < MIX_CHUNKS)

    mixt = jnp.concatenate([mixt_ref[c] for c in range(MIX_CHUNKS)], axis=1)
    mix = lax.dot_general(mixt, wout_ref[...], TN_DIMS, preferred_element_type=F32)
    o_ref[0] = x_ref[0] + _rms(mix, npost_ref[...])


def _mixer(x, zt, zk, states_f, states_b, dmat, qdec, sink, attn_norm, ret_gn, w_out, norm_post):
    B, S, _ = x.shape
    n = S // BLOCK
    cpt = MIX_CHUNKS
    edge_t = (1, 1, ATTN_KV_WIDTH, BLOCK)
    edge_k = (1, BLOCK, ATTN_KV_WIDTH)
    v_row = T_AV // ATTN_KV_WIDTH
    k_col = K_AK // ATTN_KV_WIDTH
    st_blk = (1, cpt, RET_HEADS, RET_HEAD_DIM, RET_HEAD_DIM)

    def prev(i):
        return jnp.maximum(i * cpt - 1, 0)

    def nxt(i):
        return jnp.minimum((i + 1) * cpt, n - 1)

    return pl.pallas_call(
        _mixer_kernel,
        grid=(B, S // MIX_TILE),
        in_specs=[pl.BlockSpec(memory_space=pltpu.SMEM),
                  pl.BlockSpec((1, MIX_TILE, D_MODEL), lambda b, i: (b, i, 0)),
                  pl.BlockSpec((1, cpt, T_ROWS, BLOCK), lambda b, i: (b, i, 0, 0)),
                  pl.BlockSpec(edge_t, lambda b, i: (b, prev(i), v_row, 0)),
                  pl.BlockSpec(edge_t, lambda b, i: (b, nxt(i), v_row, 0)),
                  pl.BlockSpec((1, MIX_TILE, K_COLS), lambda b, i: (b, i, 0)),
                  pl.BlockSpec(edge_k, lambda b, i: (b, prev(i), k_col)),
                  pl.BlockSpec(edge_k, lambda b, i: (b, nxt(i), k_col)),
                  pl.BlockSpec(st_blk, lambda b, i: (b, i, 0, 0, 0)),
                  pl.BlockSpec(st_blk, lambda b, i: (b, i, 0, 0, 0)),
                  _const_spec((RET_HEADS, BLOCK, LANES)),
                  _const_spec((2 * RET_HEADS, LANES)),
                  _const_spec((ATTN_WIDTH, LANES)),
                  _const_spec((RET_WIDTH, LANES)),
                  _const_spec((D_MODEL, D_MODEL)),
                  _const_spec((1, D_MODEL))],
        out_specs=pl.BlockSpec((1, MIX_TILE, D_MODEL), lambda b, i: (b, i, 0)),
        out_shape=jax.ShapeDtypeStruct((B, S, D_MODEL), F32),
        scratch_shapes=[pltpu.VMEM((ATTN_KV_HEADS, MIX_TILE + 2 * BLOCK, ATTN_KV_WIDTH), BF16),
                        pltpu.VMEM((cpt + 2, ATTN_KV_WIDTH, BLOCK), BF16),
                        pltpu.VMEM((cpt, D_MODEL, BLOCK), BF16),
                        pltpu.VMEM((2, ATTN_KV_HEADS, 3 * BLOCK, ATTN_GROUP * LANES), F32)],
        compiler_params=_params("parallel", "parallel"),
        name="token_mixer",
    )(sink, x, zt, zt, zt, zk, zk, zk, states_f, states_b, dmat, qdec,
      attn_norm, ret_gn, w_out, norm_post)


def _memkv_kernel(m_ref, g_ref, wkt_ref, wv_ref, kt_ref, v_ref):
    mn = _rms(m_ref[0], g_ref[...]).astype(BF16)
    kt_ref[0] = lax.dot_general(wkt_ref[...], mn, NT_DIMS,
                                preferred_element_type=F32).astype(BF16)
    v_ref[0] = _dot(mn, wv_ref[...]).astype(BF16)


def _memory_kv(mem, gain, w_kt, w_v):
    B, M, _ = mem.shape
    return pl.pallas_call(
        _memkv_kernel,
        grid=(B,),
        in_specs=[pl.BlockSpec((1, M, D_MODEL), lambda b: (b, 0, 0)),
                  _const_spec((1, D_MODEL)),
                  _const_spec((D_MODEL, D_MODEL)),
                  _const_spec((D_MODEL, D_MODEL))],
        out_specs=(pl.BlockSpec((1, D_MODEL, M), lambda b: (b, 0, 0)),
                   pl.BlockSpec((1, M, D_MODEL), lambda b: (b, 0, 0))),
        out_shape=(jax.ShapeDtypeStruct((B, D_MODEL, M), BF16),
                   jax.ShapeDtypeStruct((B, M, D_MODEL), BF16)),
        compiler_params=_params("parallel"),
        name="memory_kv",
    )(mem, gain, w_kt, w_v)


def _xattn_kernel(x_ref, gpre_ref, wq_ref, kmt_ref, vm_ref, wo_ref, gpost_ref,
                  o_ref, cat_ref, sc_ref, h_ref, q_ref, r_ref):
    pieces = [slice(r * ROW_PIECE, (r + 1) * ROW_PIECE) for r in range(TOK_TILE // ROW_PIECE)]
    for rows in pieces:
        h_ref[rows, :] = _rms(x_ref[0, rows, :], gpre_ref[...]).astype(BF16)
    for rows in pieces:
        q_ref[rows, :] = _dot(h_ref[rows, :], wq_ref[...]).astype(BF16)
    for hd in range(XA_HEADS):
        sl = slice(hd * XA_HEAD_DIM, (hd + 1) * XA_HEAD_DIM)
        sc_ref[hd] = _dot(q_ref[:, sl], kmt_ref[0, sl, :])
    for hd in range(XA_HEADS):
        sl = slice(hd * XA_HEAD_DIM, (hd + 1) * XA_HEAD_DIM)
        s = sc_ref[hd]
        p = jnp.exp(s - jnp.max(s, axis=-1, keepdims=True))
        den = jnp.sum(p, axis=-1, keepdims=True)
        cat_ref[:, sl] = (_dot(p.astype(BF16), vm_ref[0, :, sl]) * (1.0 / den)).astype(BF16)
    for rows in pieces:
        r_ref[rows, :] = _dot(cat_ref[rows, :], wo_ref[...])
    for rows in pieces:
        o_ref[0, rows, :] = x_ref[0, rows, :] + _rms(r_ref[rows, :], gpost_ref[...])


def _cross_attention(x, kt_mem, v_mem, gain_pre, w_q, w_o, gain_post):
    B, S, _ = x.shape
    M = v_mem.shape[1]
    tok = pl.BlockSpec((1, TOK_TILE, D_MODEL), lambda b, i: (b, i, 0))
    return pl.pallas_call(
        _xattn_kernel,
        grid=(B, S // TOK_TILE),
        in_specs=[tok,
                  _const_spec((1, D_MODEL)),
                  _const_spec((D_MODEL, D_MODEL)),
                  pl.BlockSpec((1, D_MODEL, M), lambda b, i: (b, 0, 0)),
                  pl.BlockSpec((1, M, D_MODEL), lambda b, i: (b, 0, 0)),
                  _const_spec((D_MODEL, D_MODEL)),
                  _const_spec((1, D_MODEL))],
        out_specs=tok,
        out_shape=jax.ShapeDtypeStruct((B, S, D_MODEL), F32),
        scratch_shapes=[pltpu.VMEM((TOK_TILE, D_MODEL), BF16),
                        pltpu.VMEM((XA_HEADS, TOK_TILE, N_MEM), F32),
                        pltpu.VMEM((TOK_TILE, D_MODEL), BF16),
                        pltpu.VMEM((TOK_TILE, D_MODEL), BF16),
                        pltpu.VMEM((TOK_TILE, D_MODEL), F32)],
        compiler_params=_params("parallel", "parallel"),
        name="cross_attention",
    )(x, gain_pre, w_q, kt_mem, v_mem, w_o, gain_post)


FF_CHUNK = 2 * MXU_TILE
FF_CHUNKS = [(c0, min(FF_CHUNK, D_FF - c0)) for c0 in range(0, D_FF, FF_CHUNK)]
assert all(w % MXU_TILE == 0 for _, w in FF_CHUNKS)


def _ffn_kernel(x_ref, gpre_ref, wgu_ref, wd_ref, gpost_ref, o_ref, gu_ref, act_ref,
                h_ref, f_ref):
    pieces = [slice(r * ROW_PIECE, (r + 1) * ROW_PIECE) for r in range(TOK_TILE // ROW_PIECE)]

    def gate_up_into(slot, j, rows=slice(None)):
        c0, w = FF_CHUNKS[j]
        gu_ref[slot, 0, rows, 0:w] = _dot(h_ref[rows, :], wgu_ref[:, c0:c0 + w])
        gu_ref[slot, 1, rows, 0:w] = _dot(h_ref[rows, :], wgu_ref[:, D_FF + c0:D_FF + c0 + w])

    for rows in pieces:
        h_ref[rows, :] = _rms(x_ref[0, rows, :], gpre_ref[...]).astype(BF16)
    for rows in pieces:
        gate_up_into(0, 0, rows)
    for j, (c0, w) in enumerate(FF_CHUNKS):
        if j + 1 < len(FF_CHUNKS):
            gate_up_into((j + 1) % 2, j + 1)
        gate = gu_ref[j % 2, 0, :, 0:w]
        up = gu_ref[j % 2, 1, :, 0:w]
        act_ref[:, c0:c0 + w] = (gate * jax.nn.sigmoid(gate) * up).astype(BF16)
    for rows in pieces:
        f_ref[rows, :] = _dot(act_ref[rows, :], wd_ref[...])
    for rows in pieces:
        o_ref[0, rows, :] = x_ref[0, rows, :] + _rms(f_ref[rows, :], gpost_ref[...])


def _ffn(x, gain_pre, w_gu, w_down, gain_post):
    B, S, _ = x.shape
    tok = pl.BlockSpec((1, TOK_TILE, D_MODEL), lambda b, i: (b, i, 0))
    return pl.pallas_call(
        _ffn_kernel,
        grid=(B, S // TOK_TILE),
        in_specs=[tok,
                  _const_spec((1, D_MODEL)),
                  _const_spec((D_MODEL, 2 * D_FF)),
                  _const_spec((D_FF, D_MODEL)),
                  _const_spec((1, D_MODEL))],
        out_specs=tok,
        out_shape=jax.ShapeDtypeStruct((B, S, D_MODEL), F32),
        scratch_shapes=[pltpu.VMEM((2, 2, TOK_TILE, FF_CHUNK), F32),
                        pltpu.VMEM((TOK_TILE, D_FF), BF16),
                        pltpu.VMEM((TOK_TILE, D_MODEL), BF16),
                        pltpu.VMEM((TOK_TILE, D_MODEL), F32)],
        compiler_params=_params("parallel", "parallel"),
        name="swiglu_ffn",
    )(x, gain_pre, w_gu, w_down, gain_post)


def _pair_heads(a, axis):
    shape = a.shape
    a = a.reshape(shape[:axis] + (ATTN_KV_HEADS, ATTN_GROUP, ATTN_HEAD_DIM) + shape[axis + 1:])
    return jnp.swapaxes(a, axis, axis + 1).reshape(shape)


def kernel(x, mem, norm_mix_pre, norm_mix_post, w_in, attn_sink, attn_out_norm,
           ret_decay_fwd, ret_decay_bwd, ret_gn, w_out, norm_xa_pre, norm_xa_post,
           norm_mem, xa_wq, xa_wkv, xa_wo, norm_ffn_pre, norm_ffn_post,
           ffn_w_gu, ffn_w_down):
    B, S, D = x.shape
    assert D == D_MODEL and S % max(TOK_TILE, MIX_TILE, STATE_CHUNKS * BLOCK) == 0
    assert mem.shape[1] == N_MEM
    depth = w_in.shape[0]
    o_k = ATTN_WIDTH
    o_v = o_k + ATTN_KV_WIDTH
    o_r = o_v + ATTN_KV_WIDTH
    rw = RET_WIDTH

    rope = _rope_tables(S // TOK_TILE)

    def row(v):
        return v.reshape(1, -1).astype(F32)

    def col(v):
        return jnp.broadcast_to(v.astype(F32)[:, None], (v.shape[0], LANES))

    for l in range(depth):
        w = w_in[l]
        w_t = jnp.concatenate([_pair_heads(w[:, :o_k], 1), w[:, o_r:o_r + rw], w[:, o_r + 2 * rw:],
                               w[:, o_v:o_r], w[:, o_k:o_v]], axis=1).T.astype(BF16)
        w_k = w[:, o_r + rw:o_r + 2 * rw].astype(BF16)
        w_o = jnp.concatenate([_pair_heads(w_out[l][:ATTN_WIDTH], 0), w_out[l][ATTN_WIDTH:]], axis=0)

        dmat, kdec, qdec, cdec = _decay_tables(ret_decay_fwd[l], ret_decay_bwd[l])
        zt, zk = _in_projection(x, row(norm_mix_pre[l]), w_t, w_k, rope)
        states_f, states_b = _retention_states(zt, zk, kdec, cdec)
        x = _mixer(x, zt, zk, states_f, states_b, dmat, qdec, attn_sink[l].astype(F32),
                   col(_pair_heads(attn_out_norm[l], 0)), col(ret_gn[l]),
                   w_o.astype(BF16), row(norm_mix_post[l]))
        kt_mem, v_mem = _memory_kv(mem, row(norm_mem[l]), xa_wkv[l][:, :D_MODEL].T.astype(BF16),
                                   xa_wkv[l][:, D_MODEL:].astype(BF16))
        x = _cross_attention(x, kt_mem, v_mem, row(norm_xa_pre[l]),
                             (xa_wq[l] * (XA_HEAD_DIM ** -0.5)).astype(BF16),
                             xa_wo[l].astype(BF16), row(norm_xa_post[l]))
        x = _ffn(x, row(norm_ffn_pre[l]), ffn_w_gu[l].astype(BF16),
                 ffn_w_down[l].astype(BF16), row(norm_ffn_post[l]))
    return x
```

```python
import math

import numpy as np
import jax
import jax.numpy as jnp
from jax import lax
from jax.experimental import pallas as pl
from jax.experimental.pallas import tpu as pltpu

D_MODEL = 1024
N_MEM = 256
ATTN_HEADS = 8
ATTN_KV_HEADS = 2
ATTN_GROUP = ATTN_HEADS // ATTN_KV_HEADS
ATTN_HEAD_DIM = 64
ATTN_WIDTH = ATTN_HEADS * ATTN_HEAD_DIM
ATTN_KV_WIDTH = ATTN_KV_HEADS * ATTN_HEAD_DIM
BLOCK = 128
RET_HEADS = 4
RET_HEAD_DIM = 128
RET_WIDTH = RET_HEADS * RET_HEAD_DIM
IN_COLS = ATTN_WIDTH + 2 * ATTN_KV_WIDTH + 4 * RET_WIDTH
XA_HEADS = 4
XA_HEAD_DIM = D_MODEL // XA_HEADS
D_FF = -(-(8 * D_MODEL) // (3 * 256)) * 256
ROPE_THETA = 10000.0
EPS = 1e-6
LOG2E = math.log2(math.e)

LANES = 128
BF16_ROWS = 16
MXU_TILE = 256
VMEM_LIMIT_BYTES = 56 * 1024 * 1024

TOK_TILE = 512
CHUNKS_PER_TILE = TOK_TILE // BLOCK
STATE_CHUNKS = 16
MIX_TILE = 1024
MIX_CHUNKS = MIX_TILE // BLOCK
ROW_PIECE = 256

T_AQ, T_RQ, T_RV, T_RG, T_AV = 0, 512, 1024, 1536, 2048
T_ROWS = T_AV + ATTN_KV_WIDTH
K_RK, K_AK = 0, RET_WIDTH
K_COLS = RET_WIDTH + ATTN_KV_WIDTH

F32 = jnp.float32
BF16 = jnp.bfloat16
NT_DIMS = (((1,), (1,)), ((), ()))
TN_DIMS = (((0,), (0,)), ((), ()))


def _const_spec(shape):
    nd = len(shape)
    return pl.BlockSpec(shape, lambda *_: (0,) * nd, pipeline_mode=pl.Buffered(1))


def _params(*sem, flags=None):
    return pltpu.CompilerParams(dimension_semantics=sem,
                                vmem_limit_bytes=VMEM_LIMIT_BYTES, flags=flags)


def _rms(x, w):
    ms = jnp.mean(x * x, axis=-1, keepdims=True)
    return x * lax.rsqrt(ms + EPS) * w


def _dot(a, b):
    return jnp.dot(a, b, preferred_element_type=F32)


def _rope_tables(n_tiles):
    def parts(dim):
        inv_freq = ROPE_THETA ** (-np.arange(0, dim, 2, dtype=np.float64) / dim)
        base = (np.arange(n_tiles, dtype=np.float64) * TOK_TILE)[:, None] * inv_freq[None, :]
        off = np.arange(TOK_TILE, dtype=np.float64)[:, None] * inv_freq[None, :]
        return np.cos(base), np.sin(base), np.cos(off), np.sin(off)

    def tok(c, s):
        return (np.concatenate([c, c], axis=1), np.concatenate([s, s], axis=1),
                np.concatenate([-s, s], axis=1))

    cb_a, sb_a, co_a, so_a = parts(ATTN_HEAD_DIM)
    cb_r, sb_r, co_r, so_r = parts(RET_HEAD_DIM)
    tok_base = np.stack(tok(cb_r, sb_r) + (np.zeros((n_tiles, LANES)),) * 5, axis=1)
    tok_off = np.stack(tok(co_r, so_r), axis=0)
    feat_base = np.concatenate([cb_a, sb_a, cb_r, sb_r], axis=1)
    feat_base = np.broadcast_to(feat_base[:, :, None], feat_base.shape + (BLOCK,))
    feat_off = np.concatenate([co_a, so_a, co_r, so_r], axis=1).T
    return tuple(jnp.asarray(t, F32) for t in (tok_base, tok_off, feat_base, feat_off))


def _decay_kernel(logit_ref, dmat_ref, kdec_ref, qdec_ref, cdec_ref):
    x = logit_ref[...]
    lg = jnp.minimum(x, 0.0) - jnp.log1p(jnp.exp(-jnp.abs(x)))
    ri = lax.broadcasted_iota(jnp.int32, (BLOCK, LANES), 0).astype(F32)
    ci = lax.broadcasted_iota(jnp.int32, (BLOCK, LANES), 1).astype(F32)
    c1 = ci[0:1, :]
    diff = ci - ri
    cdec_ref[...] = jnp.exp(BLOCK * lg)
    for h in range(RET_HEADS):
        lf = lg[h:h + 1, :]
        lb = lg[RET_HEADS + h:RET_HEADS + h + 1, :]
        dmat_ref[h] = jnp.exp(jnp.abs(diff) * jnp.where(diff >= 0, lf, lb))
        sl = slice(h * LANES, (h + 1) * LANES)
        kdec_ref[0, :, sl] = jnp.exp((BLOCK - 1 - ri) * lf)
        kdec_ref[1, :, sl] = jnp.exp(ri * lb)
        qdec_ref[h:h + 1, :] = jnp.exp((c1 + 1) * lf)
        qdec_ref[RET_HEADS + h:RET_HEADS + h + 1, :] = jnp.exp((BLOCK - c1) * lb)


def _decay_tables(decay_fwd, decay_bwd):
    logits = jnp.concatenate([decay_fwd, decay_bwd]).astype(F32)
    logits = jnp.broadcast_to(logits[:, None], (2 * RET_HEADS, LANES))
    return pl.pallas_call(
        _decay_kernel,
        out_shape=(jax.ShapeDtypeStruct((RET_HEADS, BLOCK, LANES), F32),
                   jax.ShapeDtypeStruct((2, BLOCK, RET_WIDTH), F32),
                   jax.ShapeDtypeStruct((2 * RET_HEADS, LANES), F32),
                   jax.ShapeDtypeStruct((2 * RET_HEADS, LANES), F32)),
        name="decay_tables",
    )(logits)


def _inproj_kernel(x_ref, g_ref, wt_ref, wk_ref, tb_ref, to_ref, fb_ref, fo_ref,
                   zt_ref, zk_ref):
    h = _rms(x_ref[0], g_ref[...]).astype(BF16)

    tb = tb_ref[0]

    ck = tb[0:1] * to_ref[0] - tb[1:2] * to_ref[1]
    sk = tb[2:3] * to_ref[0] + tb[0:1] * to_ref[2]

    fb = jnp.concatenate([fb_ref[0]] * CHUNKS_PER_TILE, axis=1)
    fo = fo_ref[...]

    def feat_tables(row, n):
        cb, sb = fb[row:row + n], fb[row + n:row + 2 * n]
        co, so = fo[row:row + n], fo[row + n:row + 2 * n]
        return cb * co - sb * so, sb * co + cb * so

    zk = _dot(h, wk_ref[...])
    for g in range(RET_HEADS):
        zg = zk[:, g * LANES:(g + 1) * LANES]
        r = (zg * ck + pltpu.roll(zg, LANES // 2, 1) * sk) * (RET_HEAD_DIM ** -0.5)
        zk_ref[0, :, K_RK + g * LANES:K_RK + (g + 1) * LANES] = r.astype(BF16)

    def proj_t(row, n_rows):
        return lax.dot_general(wt_ref[row:row + n_rows, :], h, NT_DIMS,
                               preferred_element_type=F32)

    def store_t(row, val):
        for c in range(CHUNKS_PER_TILE):
            zt_ref[0, c, row:row + val.shape[0], :] = val[:, c * BLOCK:(c + 1) * BLOCK].astype(BF16)

    def rope_t(z, row, head_dim, cos, sin):
        hd2 = head_dim // 2
        for k in range(z.shape[0] // head_dim):
            x1 = z[k * head_dim:k * head_dim + hd2]
            x2 = z[k * head_dim + hd2:(k + 1) * head_dim]
            store_t(row + k * head_dim, x1 * cos - x2 * sin)
            store_t(row + k * head_dim + hd2, x2 * cos + x1 * sin)

    q_scale = (ATTN_HEAD_DIM ** -0.5) * LOG2E
    cat, sat = feat_tables(0, ATTN_HEAD_DIM // 2)
    rope_t(proj_t(T_AQ, ATTN_WIDTH), T_AQ, ATTN_HEAD_DIM, cat * q_scale, sat * q_scale)
    crt, srt = feat_tables(ATTN_HEAD_DIM, RET_HEAD_DIM // 2)
    rope_t(proj_t(T_RQ, RET_WIDTH), T_RQ, RET_HEAD_DIM, crt, srt)
    store_t(T_RV, proj_t(T_RV, RET_WIDTH))
    store_t(T_RG, proj_t(T_RG, RET_WIDTH))
    kv = proj_t(T_AV, 2 * ATTN_KV_WIDTH)
    store_t(T_AV, kv[0:ATTN_KV_WIDTH])
    hd2 = ATTN_HEAD_DIM // 2
    rows = []
    for k in range(ATTN_KV_HEADS):
        r0 = ATTN_KV_WIDTH + k * ATTN_HEAD_DIM
        x1, x2 = kv[r0:r0 + hd2], kv[r0 + hd2:r0 + ATTN_HEAD_DIM]
        rows += [x1 * cat - x2 * sat, x2 * cat + x1 * sat]
    zk_ref[0, :, K_AK:K_AK + ATTN_KV_WIDTH] = jnp.concatenate(rows, axis=0).T.astype(BF16)


def _in_projection(x, gain, w_t, w_k, rope):
    B, S, _ = x.shape
    n = S // BLOCK
    tok_base, tok_off, feat_base, feat_off = rope
    return pl.pallas_call(
        _inproj_kernel,
        grid=(S // TOK_TILE, B),
        in_specs=[pl.BlockSpec((1, TOK_TILE, D_MODEL), lambda i, b: (b, i, 0)),
                  _const_spec((1, D_MODEL)),
                  _const_spec((T_ROWS + ATTN_KV_WIDTH, D_MODEL)),
                  _const_spec((D_MODEL, RET_WIDTH)),
                  pl.BlockSpec((1,) + tok_base.shape[1:], lambda i, b: (i, 0, 0)),
                  _const_spec(tok_off.shape),
                  pl.BlockSpec((1,) + feat_base.shape[1:], lambda i, b: (i, 0, 0)),
                  _const_spec(feat_off.shape)],
        out_specs=(pl.BlockSpec((1, CHUNKS_PER_TILE, T_ROWS, BLOCK), lambda i, b: (b, i, 0, 0)),
                   pl.BlockSpec((1, TOK_TILE, K_COLS), lambda i, b: (b, i, 0))),
        out_shape=(jax.ShapeDtypeStruct((B, n, T_ROWS, BLOCK), BF16),
                   jax.ShapeDtypeStruct((B, S, K_COLS), BF16)),
        compiler_params=_params("parallel", "parallel"),
        name="in_projection",
    )(x, gain, w_t, w_k, tok_base, tok_off, feat_base, feat_off)


def _state_kernel(kf_ref, vf_ref, kb_ref, vb_ref, kdec_ref, cdec_ref,
                  pf_ref, pb_ref, st_ref):
    @pl.when(pl.program_id(1) == 0)
    def _():
        st_ref[...] = jnp.zeros_like(st_ref)

    for i in range(STATE_CHUNKS):
        for d, (k_ref, v_ref, p_ref, c) in enumerate(((kf_ref, vf_ref, pf_ref, i),
                                                      (kb_ref, vb_ref, pb_ref, STATE_CHUNKS - 1 - i))):
            kd = (k_ref[0, c * BLOCK:(c + 1) * BLOCK, :].astype(F32) * kdec_ref[d]).astype(BF16)
            for h in range(RET_HEADS):
                sl = slice(h * LANES, (h + 1) * LANES)
                st = st_ref[d, h]
                p_ref[0, c, h] = st.astype(BF16)
                kv = _dot(v_ref[0, c, sl, :], kd[:, sl])
                row = d * RET_HEADS + h
                st_ref[d, h] = st * cdec_ref[row:row + 1, :] + kv


def _retention_states(zt, zk, kdec, cdec):
    B, n = zt.shape[0], zt.shape[1]
    ng = n // STATE_CHUNKS
    k_blk = (1, STATE_CHUNKS * BLOCK, RET_WIDTH)
    v_blk = (1, STATE_CHUNKS, RET_WIDTH, BLOCK)
    vrow = T_RV // RET_WIDTH
    st_blk = (1, STATE_CHUNKS, RET_HEADS, RET_HEAD_DIM, RET_HEAD_DIM)
    st_shape = jax.ShapeDtypeStruct((B, n, RET_HEADS, RET_HEAD_DIM, RET_HEAD_DIM), BF16)
    return pl.pallas_call(
        _state_kernel,
        grid=(B, ng),
        in_specs=[pl.BlockSpec(k_blk, lambda b, t: (b, t, 0)),
                  pl.BlockSpec(v_blk, lambda b, t: (b, t, vrow, 0)),
                  pl.BlockSpec(k_blk, lambda b, t: (b, ng - 1 - t, 0)),
                  pl.BlockSpec(v_blk, lambda b, t: (b, ng - 1 - t, vrow, 0)),
                  _const_spec((2, BLOCK, RET_WIDTH)),
                  _const_spec((2 * RET_HEADS, LANES))],
        out_specs=(pl.BlockSpec(st_blk, lambda b, t: (b, t, 0, 0, 0)),
                   pl.BlockSpec(st_blk, lambda b, t: (b, ng - 1 - t, 0, 0, 0))),
        out_shape=(st_shape, st_shape),
        scratch_shapes=[pltpu.VMEM((2, RET_HEADS, RET_HEAD_DIM, RET_HEAD_DIM), F32)],
        compiler_params=_params("parallel", "arbitrary"),
        name="retention_states",
    )(zk, zt, zk, zt, kdec, cdec)


def _mixer_kernel(sink_ref, x_ref, zt_ref, avp_ref, avn_ref, zk_ref, akp_ref, akn_ref,
                  pf_ref, pb_ref, dmat_ref, qdec_ref,
                  an_ref, gn_ref, wout_ref, npost_ref, o_ref, kext_ref, vext_ref, mixt_ref,
                  sc_ref):
    tile = pl.program_id(1)
    n_blocks = pl.num_programs(1) * MIX_CHUNKS

    lane = lax.broadcasted_iota(jnp.int32, (1, LANES), 1)
    for g in range(ATTN_KV_HEADS):
        km = jnp.where((lane // ATTN_HEAD_DIM) == g, 1.0, 0.0).astype(BF16)
        kext_ref[g, 0:BLOCK] = akp_ref[0] * km
        kext_ref[g, BLOCK:BLOCK + MIX_TILE] = zk_ref[0, :, K_AK:K_AK + ATTN_KV_WIDTH] * km
        kext_ref[g, BLOCK + MIX_TILE:] = akn_ref[0] * km
    vext_ref[0] = avp_ref[0, 0]
    for c in range(MIX_CHUNKS):
        vext_ref[1 + c] = zt_ref[0, c, T_AV:T_AV + ATTN_KV_WIDTH, :]
    vext_ref[MIX_CHUNKS + 1] = avn_ref[0, 0]

    kk = lax.broadcasted_iota(jnp.int32, (BLOCK, LANES), 0)
    qq = lax.broadcasted_iota(jnp.int32, (BLOCK, LANES), 1)
    neg = jnp.full((BLOCK, LANES), -jnp.inf, F32)
    zero = jnp.zeros((BLOCK, LANES), F32)

    def scores_into(slot, c):
        q = jnp.concatenate([zt_ref[0, c, T_AQ + j * LANES:T_AQ + (j + 1) * LANES, :]
                             for j in range(ATTN_GROUP)], axis=1)
        r0 = c * BLOCK
        for g in range(ATTN_KV_HEADS):
            sc_ref[slot, g] = _dot(kext_ref[g, pl.ds(r0, 3 * BLOCK), :], q)

    def chunk(c, has_next):
        r0 = c * BLOCK
        blk = tile * MIX_CHUNKS + c
        slot = c % 2
        if has_next:
            scores_into(1 - slot, c + 1)

        vw = jnp.concatenate([vext_ref[c], vext_ref[c + 1], vext_ref[c + 2]], axis=1)
        no_prev = jnp.where(blk > 0, 0, BLOCK)
        no_next = jnp.where(blk < n_blocks - 1, 0, BLOCK)
        bias_prev = jnp.where(kk >= qq + no_prev, zero, neg)
        bias_next = jnp.where(kk <= qq - no_next, zero, neg)
        bias_prev = jnp.concatenate([bias_prev] * ATTN_GROUP, axis=1)
        bias_next = jnp.concatenate([bias_next] * ATTN_GROUP, axis=1)
        heads = []
        for h in range(RET_HEADS):
            qt = zt_ref[0, c, T_RQ + h * LANES:T_RQ + (h + 1) * LANES, :]
            kh = zk_ref[0, pl.ds(r0, BLOCK), K_RK + h * LANES:K_RK + (h + 1) * LANES]
            heads.append((qt, _dot(kh, qt)))

        ones = jnp.ones((BF16_ROWS, 3 * BLOCK), BF16)
        outs = []
        for g in range(ATTN_KV_HEADS):
            s = sc_ref[slot, g]
            s = jnp.concatenate([s[0:BLOCK] + bias_prev, s[BLOCK:2 * BLOCK],
                                 s[2 * BLOCK:] + bias_next], axis=0)
            sk = jnp.concatenate(
                [jnp.full((1, LANES), sink_ref[g * ATTN_GROUP + j] * LOG2E, F32)
                 for j in range(ATTN_GROUP)], axis=1)
            m = jnp.maximum(jnp.max(s, axis=0, keepdims=True), sk)
            p = jnp.exp2(s - m).astype(BF16)
            vg = jnp.concatenate([vw[g * ATTN_HEAD_DIM:(g + 1) * ATTN_HEAD_DIM, :], ones], axis=0)
            ov = _dot(vg, p)
            den = ov[ATTN_HEAD_DIM:ATTN_HEAD_DIM + 1, :] + jnp.exp2(sk - m)
            outs.append(ov[0:ATTN_HEAD_DIM, :] * (1.0 / den))
        ys = []
        for h in range(RET_HEADS):
            qt, pt = heads[h]
            y = _dot(zt_ref[0, c, T_RV + h * LANES:T_RV + (h + 1) * LANES, :],
                     (pt * dmat_ref[h]).astype(BF16))
            y = y + _dot(pf_ref[0, c, h], qt) * qdec_ref[h:h + 1, :]
            y = y + _dot(pb_ref[0, c, h], qt) * qdec_ref[RET_HEADS + h:RET_HEADS + h + 1, :]
            ys.append(y)

        a = jnp.concatenate([outs[g][:, j * LANES:(j + 1) * LANES]
                             for j in range(ATTN_GROUP) for g in range(ATTN_KV_HEADS)], axis=0)
        ms = jnp.sum(a * a, axis=0, keepdims=True) * (1.0 / ATTN_WIDTH)
        mixt_ref[c, 0:ATTN_WIDTH, :] = (a * lax.rsqrt(ms + EPS) * an_ref[...]).astype(BF16)
        for h in range(RET_HEADS):
            sl = slice(h * LANES, (h + 1) * LANES)
            y = ys[h]
            mu = jnp.mean(y, axis=0, keepdims=True)
            yc = y - mu
            var = jnp.mean(yc * yc, axis=0, keepdims=True)
            yn = yc * lax.rsqrt(var + EPS) * gn_ref[sl, :]
            gate = zt_ref[0, c, T_RG + h * LANES:T_RG + (h + 1) * LANES, :].astype(F32)
            r = gate * jax.nn.sigmoid(gate) * yn
            mixt_ref[c, ATTN_WIDTH + h * LANES:ATTN_WIDTH + (h + 1) * LANES, :] = r.astype(BF16)

    scores_into(0, 0)
    for c in range(MIX_CHUNKS):
        chunk(c, c + 1 < MIX_CHUNKS)

    mixt = jnp.concatenate([mixt_ref[c] for c in range(MIX_CHUNKS)], axis=1)
    mix = lax.dot_general(mixt, wout_ref[...], TN_DIMS, preferred_element_type=F32)
    o_ref[0] = x_ref[0] + _rms(mix, npost_ref[...])


def _mixer(x, zt, zk, states_f, states_b, dmat, qdec, sink, attn_norm, ret_gn, w_out, norm_post):
    B, S, _ = x.shape
    n = S // BLOCK
    cpt = MIX_CHUNKS
    edge_t = (1, 1, ATTN_KV_WIDTH, BLOCK)
    edge_k = (1, BLOCK, ATTN_KV_WIDTH)
    v_row = T_AV // ATTN_KV_WIDTH
    k_col = K_AK // ATTN_KV_WIDTH
    st_blk = (1, cpt, RET_HEADS, RET_HEAD_DIM, RET_HEAD_DIM)

    def prev(i):
        return jnp.maximum(i * cpt - 1, 0)

    def nxt(i):
        return jnp.minimum((i + 1) * cpt, n - 1)

    return pl.pallas_call(
        _mixer_kernel,
        grid=(B, S // MIX_TILE),
        in_specs=[pl.BlockSpec(memory_space=pltpu.SMEM),
                  pl.BlockSpec((1, MIX_TILE, D_MODEL), lambda b, i: (b, i, 0)),
                  pl.BlockSpec((1, cpt, T_ROWS, BLOCK), lambda b, i: (b, i, 0, 0)),
                  pl.BlockSpec(edge_t, lambda b, i: (b, prev(i), v_row, 0)),
                  pl.BlockSpec(edge_t, lambda b, i: (b, nxt(i), v_row, 0)),
                  pl.BlockSpec((1, MIX_TILE, K_COLS), lambda b, i: (b, i, 0)),
                  pl.BlockSpec(edge_k, lambda b, i: (b, prev(i), k_col)),
                  pl.BlockSpec(edge_k, lambda b, i: (b, nxt(i), k_col)),
                  pl.BlockSpec(st_blk, lambda b, i: (b, i, 0, 0, 0)),
                  pl.BlockSpec(st_blk, lambda b, i: (b, i, 0, 0, 0)),
                  _const_spec((RET_HEADS, BLOCK, LANES)),
                  _const_spec((2 * RET_HEADS, LANES)),
                  _const_spec((ATTN_WIDTH, LANES)),
                  _const_spec((RET_WIDTH, LANES)),
                  _const_spec((D_MODEL, D_MODEL)),
                  _const_spec((1, D_MODEL))],
        out_specs=pl.BlockSpec((1, MIX_TILE, D_MODEL), lambda b, i: (b, i, 0)),
        out_shape=jax.ShapeDtypeStruct((B, S, D_MODEL), F32),
        scratch_shapes=[pltpu.VMEM((ATTN_KV_HEADS, MIX_TILE + 2 * BLOCK, ATTN_KV_WIDTH), BF16),
                        pltpu.VMEM((cpt + 2, ATTN_KV_WIDTH, BLOCK), BF16),
                        pltpu.VMEM((cpt, D_MODEL, BLOCK), BF16),
                        pltpu.VMEM((2, ATTN_KV_HEADS, 3 * BLOCK, ATTN_GROUP * LANES), F32)],
        compiler_params=_params("parallel", "parallel"),
        name="token_mixer",
    )(sink, x, zt, zt, zt, zk, zk, zk, states_f, states_b, dmat, qdec,
      attn_norm, ret_gn, w_out, norm_post)


def _memkv_kernel(m_ref, g_ref, wkt_ref, wv_ref, kt_ref, v_ref):
    mn = _rms(m_ref[0], g_ref[...]).astype(BF16)
    kt_ref[0] = lax.dot_general(wkt_ref[...], mn, NT_DIMS,
                                preferred_element_type=F32).astype(BF16)
    v_ref[0] = _dot(mn, wv_ref[...]).astype(BF16)


def _memory_kv(mem, gain, w_kt, w_v):
    B, M, _ = mem.shape
    return pl.pallas_call(
        _memkv_kernel,
        grid=(B,),
        in_specs=[pl.BlockSpec((1, M, D_MODEL), lambda b: (b, 0, 0)),
                  _const_spec((1, D_MODEL)),
                  _const_spec((D_MODEL, D_MODEL)),
                  _const_spec((D_MODEL, D_MODEL))],
        out_specs=(pl.BlockSpec((1, D_MODEL, M), lambda b: (b, 0, 0)),
                   pl.BlockSpec((1, M, D_MODEL), lambda b: (b, 0, 0))),
        out_shape=(jax.ShapeDtypeStruct((B, D_MODEL, M), BF16),
                   jax.ShapeDtypeStruct((B, M, D_MODEL), BF16)),
        compiler_params=_params("parallel"),
        name="memory_kv",
    )(mem, gain, w_kt, w_v)


FF_CHUNK = 2 * MXU_TILE
FF_CHUNKS = [(c0, min(FF_CHUNK, D_FF - c0)) for c0 in range(0, D_FF, FF_CHUNK)]
assert all(w % MXU_TILE == 0 for _, w in FF_CHUNKS)
assert XA_HEADS <= 4 and N_MEM <= FF_CHUNK


def _xattn_ffn_kernel(x_ref, gxa_ref, wq_ref, kmt_ref, vm_ref, wo_ref, gxp_ref,
                      gfp_ref, wgu_ref, wd_ref, gfo_ref, o_ref,
                      cat_ref, q_ref, h_ref, res_ref, x2_ref, gu_ref, act_ref):
    pieces = [slice(r * ROW_PIECE, (r + 1) * ROW_PIECE) for r in range(TOK_TILE // ROW_PIECE)]

    for rows in pieces:
        h_ref[rows, :] = _rms(x_ref[0, rows, :], gxa_ref[...]).astype(BF16)
    for rows in pieces:
        q_ref[rows, :] = _dot(h_ref[rows, :], wq_ref[...]).astype(BF16)
    for hd in range(XA_HEADS):
        sl = slice(hd * XA_HEAD_DIM, (hd + 1) * XA_HEAD_DIM)
        gu_ref[hd // 2, hd % 2, :, 0:N_MEM] = _dot(q_ref[:, sl], kmt_ref[0, sl, :])
    for hd in range(XA_HEADS):
        sl = slice(hd * XA_HEAD_DIM, (hd + 1) * XA_HEAD_DIM)
        s = gu_ref[hd // 2, hd % 2, :, 0:N_MEM]
        p = jnp.exp(s - jnp.max(s, axis=-1, keepdims=True))
        den = jnp.sum(p, axis=-1, keepdims=True)
        cat_ref[:, sl] = (_dot(p.astype(BF16), vm_ref[0, :, sl]) * (1.0 / den)).astype(BF16)
    for rows in pieces:
        res_ref[rows, :] = _dot(cat_ref[rows, :], wo_ref[...])

    def gate_up_into(slot, j, rows=slice(None)):
        c0, w = FF_CHUNKS[j]
        gu_ref[slot, 0, rows, 0:w] = _dot(h_ref[rows, :], wgu_ref[:, c0:c0 + w])
        gu_ref[slot, 1, rows, 0:w] = _dot(h_ref[rows, :], wgu_ref[:, D_FF + c0:D_FF + c0 + w])

    for rows in pieces:
        x2 = x_ref[0, rows, :] + _rms(res_ref[rows, :], gxp_ref[...])
        x2_ref[rows, :] = x2
        h_ref[rows, :] = _rms(x2, gfp_ref[...]).astype(BF16)
    for rows in pieces:
        gate_up_into(0, 0, rows)

    for j, (c0, w) in enumerate(FF_CHUNKS):
        if j + 1 < len(FF_CHUNKS):
            gate_up_into((j + 1) % 2, j + 1)
        gate = gu_ref[j % 2, 0, :, 0:w]
        up = gu_ref[j % 2, 1, :, 0:w]
        act_ref[:, c0:c0 + w] = (gate * jax.nn.sigmoid(gate) * up).astype(BF16)
    for rows in pieces:
        res_ref[rows, :] = _dot(act_ref[rows, :], wd_ref[...])
    for rows in pieces:
        o_ref[0, rows, :] = x2_ref[rows, :] + _rms(res_ref[rows, :], gfo_ref[...])


def _cross_attention_ffn(x, kt_mem, v_mem, gain_xa_pre, w_q, w_o, gain_xa_post,
                         gain_ffn_pre, w_gu, w_down, gain_ffn_post):
    B, S, _ = x.shape
    M = v_mem.shape[1]
    tok = pl.BlockSpec((1, TOK_TILE, D_MODEL), lambda b, i: (b, i, 0))
    gain = _const_spec((1, D_MODEL))
    return pl.pallas_call(
        _xattn_ffn_kernel,
        grid=(B, S // TOK_TILE),
        in_specs=[tok, gain,
                  _const_spec((D_MODEL, D_MODEL)),
                  pl.BlockSpec((1, D_MODEL, M), lambda b, i: (b, 0, 0)),
                  pl.BlockSpec((1, M, D_MODEL), lambda b, i: (b, 0, 0)),
                  _const_spec((D_MODEL, D_MODEL)),
                  gain, gain,
                  _const_spec((D_MODEL, 2 * D_FF)),
                  _const_spec((D_FF, D_MODEL)),
                  gain],
        out_specs=tok,
        out_shape=jax.ShapeDtypeStruct((B, S, D_MODEL), F32),
        scratch_shapes=[pltpu.VMEM((TOK_TILE, D_MODEL), BF16),
                        pltpu.VMEM((TOK_TILE, D_MODEL), BF16),
                        pltpu.VMEM((TOK_TILE, D_MODEL), BF16),
                        pltpu.VMEM((TOK_TILE, D_MODEL), F32),
                        pltpu.VMEM((TOK_TILE, D_MODEL), F32),
                        pltpu.VMEM((2, 2, TOK_TILE, FF_CHUNK), F32),
                        pltpu.VMEM((TOK_TILE, D_FF), BF16)],
        compiler_params=_params("parallel", "parallel"),
        name="cross_attention_ffn",
    )(x, gain_xa_pre, w_q, kt_mem, v_mem, w_o, gain_xa_post,
      gain_ffn_pre, w_gu, w_down, gain_ffn_post)


def _pair_heads(a, axis):
    shape = a.shape
    a = a.reshape(shape[:axis] + (ATTN_KV_HEADS, ATTN_GROUP, ATTN_HEAD_DIM) + shape[axis + 1:])
    return jnp.swapaxes(a, axis, axis + 1).reshape(shape)


def kernel(x, mem, norm_mix_pre, norm_mix_post, w_in, attn_sink, attn_out_norm,
           ret_decay_fwd, ret_decay_bwd, ret_gn, w_out, norm_xa_pre, norm_xa_post,
           norm_mem, xa_wq, xa_wkv, xa_wo, norm_ffn_pre, norm_ffn_post,
           ffn_w_gu, ffn_w_down):
    B, S, D = x.shape
    assert D == D_MODEL and S % max(TOK_TILE, MIX_TILE, STATE_CHUNKS * BLOCK) == 0
    assert mem.shape[1] == N_MEM
    depth = w_in.shape[0]
    o_k = ATTN_WIDTH
    o_v = o_k + ATTN_KV_WIDTH
    o_r = o_v + ATTN_KV_WIDTH
    rw = RET_WIDTH

    rope = _rope_tables(S // TOK_TILE)

    def row(v):
        return v.reshape(1, -1).astype(F32)

    def col(v):
        return jnp.broadcast_to(v.astype(F32)[:, None], (v.shape[0], LANES))

    for l in range(depth):
        w = w_in[l]
        w_t = jnp.concatenate([_pair_heads(w[:, :o_k], 1), w[:, o_r:o_r + rw], w[:, o_r + 2 * rw:],
                               w[:, o_v:o_r], w[:, o_k:o_v]], axis=1).T.astype(BF16)
        w_k = w[:, o_r + rw:o_r + 2 * rw].astype(BF16)
        w_o = jnp.concatenate([_pair_heads(w_out[l][:ATTN_WIDTH], 0), w_out[l][ATTN_WIDTH:]], axis=0)

        dmat, kdec, qdec, cdec = _decay_tables(ret_decay_fwd[l], ret_decay_bwd[l])
        zt, zk = _in_projection(x, row(norm_mix_pre[l]), w_t, w_k, rope)
        states_f, states_b = _retention_states(zt, zk, kdec, cdec)
        x = _mixer(x, zt, zk, states_f, states_b, dmat, qdec, attn_sink[l].astype(F32),
                   col(_pair_heads(attn_out_norm[l], 0)), col(ret_gn[l]),
                   w_o.astype(BF16), row(norm_mix_post[l]))
        kt_mem, v_mem = _memory_kv(mem, row(norm_mem[l]), xa_wkv[l][:, :D_MODEL].T.astype(BF16),
                                   xa_wkv[l][:, D_MODEL:].astype(BF16))
        x = _cross_attention_ffn(x, kt_mem, v_mem, row(norm_xa_pre[l]),
                                 (xa_wq[l] * (XA_HEAD_DIM ** -0.5)).astype(BF16),
                                 xa_wo[l].astype(BF16), row(norm_xa_post[l]),
                                 row(norm_ffn_pre[l]), ffn_w_gu[l].astype(BF16),
                                 ffn_w_down[l].astype(BF16), row(norm_ffn_post[l]))
    return x
```

```python
import math

import numpy as np
import jax
import jax.numpy as jnp
from jax import lax
from jax.experimental import pallas as pl
from jax.experimental.pallas import tpu as pltpu

D_MODEL = 1024
N_MEM = 256
ATTN_HEADS = 8
ATTN_KV_HEADS = 2
ATTN_GROUP = ATTN_HEADS // ATTN_KV_HEADS
ATTN_HEAD_DIM = 64
ATTN_WIDTH = ATTN_HEADS * ATTN_HEAD_DIM
ATTN_KV_WIDTH = ATTN_KV_HEADS * ATTN_HEAD_DIM
BLOCK = 128
RET_HEADS = 4
RET_HEAD_DIM = 128
RET_WIDTH = RET_HEADS * RET_HEAD_DIM
MIX_WIDTH = ATTN_WIDTH + RET_WIDTH
IN_COLS = ATTN_WIDTH + 2 * ATTN_KV_WIDTH + 4 * RET_WIDTH
XA_HEADS = 4
XA_HEAD_DIM = D_MODEL // XA_HEADS
D_FF = -(-(8 * D_MODEL) // (3 * 256)) * 256
ROPE_THETA = 10000.0
EPS = 1e-6
LOG2E = math.log2(math.e)

LANES = 128
BF16_ROWS = 16
MXU_TILE = 256
VMEM_LIMIT_BYTES = 56 * 1024 * 1024

TOK_TILE = 512
CHUNKS_PER_TILE = TOK_TILE // BLOCK
STATE_CHUNKS = 16
MIX_TILE = 1024
MIX_CHUNKS = MIX_TILE // BLOCK
ROW_PIECE = 256

T_AQ, T_RQ, T_RV, T_RG, T_AV = 0, 512, 1024, 1536, 2048
T_ROWS = T_AV + ATTN_KV_WIDTH
K_RK, K_AK = 0, RET_WIDTH
K_COLS = RET_WIDTH + ATTN_KV_WIDTH

F32 = jnp.float32
BF16 = jnp.bfloat16
NT_DIMS = (((1,), (1,)), ((), ()))
TN_DIMS = (((0,), (0,)), ((), ()))


def _const_spec(shape):
    nd = len(shape)
    return pl.BlockSpec(shape, lambda *_: (0,) * nd, pipeline_mode=pl.Buffered(1))


def _params(*sem, flags=None):
    return pltpu.CompilerParams(dimension_semantics=sem,
                                vmem_limit_bytes=VMEM_LIMIT_BYTES, flags=flags)


def _rms(x, w):
    ms = jnp.mean(x * x, axis=-1, keepdims=True)
    return x * lax.rsqrt(ms + EPS) * w


def _dot(a, b):
    return jnp.dot(a, b, preferred_element_type=F32)


def _rope_tables(n_tiles):
    def parts(dim):
        inv_freq = ROPE_THETA ** (-np.arange(0, dim, 2, dtype=np.float64) / dim)
        base = (np.arange(n_tiles, dtype=np.float64) * TOK_TILE)[:, None] * inv_freq[None, :]
        off = np.arange(TOK_TILE, dtype=np.float64)[:, None] * inv_freq[None, :]
        return np.cos(base), np.sin(base), np.cos(off), np.sin(off)

    def tok(c, s):
        return (np.concatenate([c, c], axis=1), np.concatenate([s, s], axis=1),
                np.concatenate([-s, s], axis=1))

    cb_a, sb_a, co_a, so_a = parts(ATTN_HEAD_DIM)
    cb_r, sb_r, co_r, so_r = parts(RET_HEAD_DIM)
    tok_base = np.stack(tok(cb_r, sb_r) + (np.zeros((n_tiles, LANES)),) * 5, axis=1)
    tok_off = np.stack(tok(co_r, so_r), axis=0)
    feat_base = np.concatenate([cb_a, sb_a, cb_r, sb_r], axis=1)
    feat_base = np.broadcast_to(feat_base[:, :, None], feat_base.shape + (BLOCK,))
    feat_off = np.concatenate([co_a, so_a, co_r, so_r], axis=1).T
    return tuple(jnp.asarray(t, F32) for t in (tok_base, tok_off, feat_base, feat_off))


def _decay_kernel(logit_ref, dmat_ref, kdec_ref, qdec_ref, cdec_ref):
    x = logit_ref[...]
    lg = jnp.minimum(x, 0.0) - jnp.log1p(jnp.exp(-jnp.abs(x)))
    ri = lax.broadcasted_iota(jnp.int32, (BLOCK, LANES), 0).astype(F32)
    ci = lax.broadcasted_iota(jnp.int32, (BLOCK, LANES), 1).astype(F32)
    c1 = ci[0:1, :]
    diff = ci - ri
    cdec_ref[...] = jnp.exp(BLOCK * lg)
    for h in range(RET_HEADS):
        lf = lg[h:h + 1, :]
        lb = lg[RET_HEADS + h:RET_HEADS + h + 1, :]
        dmat_ref[h] = jnp.exp(jnp.abs(diff) * jnp.where(diff >= 0, lf, lb))
        sl = slice(h * LANES, (h + 1) * LANES)
        kdec_ref[0, :, sl] = jnp.exp((BLOCK - 1 - ri) * lf)
        kdec_ref[1, :, sl] = jnp.exp(ri * lb)
        qdec_ref[h:h + 1, :] = jnp.exp((c1 + 1) * lf)
        qdec_ref[RET_HEADS + h:RET_HEADS + h + 1, :] = jnp.exp((BLOCK - c1) * lb)


def _decay_tables(decay_fwd, decay_bwd):
    logits = jnp.concatenate([decay_fwd, decay_bwd]).astype(F32)
    logits = jnp.broadcast_to(logits[:, None], (2 * RET_HEADS, LANES))
    return pl.pallas_call(
        _decay_kernel,
        out_shape=(jax.ShapeDtypeStruct((RET_HEADS, BLOCK, LANES), F32),
                   jax.ShapeDtypeStruct((2, BLOCK, RET_WIDTH), F32),
                   jax.ShapeDtypeStruct((2 * RET_HEADS, LANES), F32),
                   jax.ShapeDtypeStruct((2 * RET_HEADS, LANES), F32)),
        name="decay_tables",
    )(logits)


def _inproj_kernel(x_ref, g_ref, wt_ref, wk_ref, tb_ref, to_ref, fb_ref, fo_ref,
                   zt_ref, zk_ref):
    h = _rms(x_ref[0], g_ref[...]).astype(BF16)

    tb = tb_ref[0]

    ck = tb[0:1] * to_ref[0] - tb[1:2] * to_ref[1]
    sk = tb[2:3] * to_ref[0] + tb[0:1] * to_ref[2]

    fb = jnp.concatenate([fb_ref[0]] * CHUNKS_PER_TILE, axis=1)
    fo = fo_ref[...]

    def feat_tables(row, n):
        cb, sb = fb[row:row + n], fb[row + n:row + 2 * n]
        co, so = fo[row:row + n], fo[row + n:row + 2 * n]
        return cb * co - sb * so, sb * co + cb * so

    zk = _dot(h, wk_ref[...])
    for g in range(RET_HEADS):
        zg = zk[:, g * LANES:(g + 1) * LANES]
        r = (zg * ck + pltpu.roll(zg, LANES // 2, 1) * sk) * (RET_HEAD_DIM ** -0.5)
        zk_ref[0, :, K_RK + g * LANES:K_RK + (g + 1) * LANES] = r.astype(BF16)

    def proj_t(row, n_rows):
        return lax.dot_general(wt_ref[row:row + n_rows, :], h, NT_DIMS,
                               preferred_element_type=F32)

    def store_t(row, val):
        for c in range(CHUNKS_PER_TILE):
            zt_ref[0, c, row:row + val.shape[0], :] = val[:, c * BLOCK:(c + 1) * BLOCK].astype(BF16)

    def rope_t(z, row, head_dim, cos, sin):
        hd2 = head_dim // 2
        for k in range(z.shape[0] // head_dim):
            x1 = z[k * head_dim:k * head_dim + hd2]
            x2 = z[k * head_dim + hd2:(k + 1) * head_dim]
            store_t(row + k * head_dim, x1 * cos - x2 * sin)
            store_t(row + k * head_dim + hd2, x2 * cos + x1 * sin)

    q_scale = (ATTN_HEAD_DIM ** -0.5) * LOG2E
    cat, sat = feat_tables(0, ATTN_HEAD_DIM // 2)
    rope_t(proj_t(T_AQ, ATTN_WIDTH), T_AQ, ATTN_HEAD_DIM, cat * q_scale, sat * q_scale)
    crt, srt = feat_tables(ATTN_HEAD_DIM, RET_HEAD_DIM // 2)
    rope_t(proj_t(T_RQ, RET_WIDTH), T_RQ, RET_HEAD_DIM, crt, srt)
    store_t(T_RV, proj_t(T_RV, RET_WIDTH))
    store_t(T_RG, proj_t(T_RG, RET_WIDTH))
    kv = proj_t(T_AV, 2 * ATTN_KV_WIDTH)
    store_t(T_AV, kv[0:ATTN_KV_WIDTH])
    hd2 = ATTN_HEAD_DIM // 2
    rows = []
    for k in range(ATTN_KV_HEADS):
        r0 = ATTN_KV_WIDTH + k * ATTN_HEAD_DIM
        x1, x2 = kv[r0:r0 + hd2], kv[r0 + hd2:r0 + ATTN_HEAD_DIM]
        rows += [x1 * cat - x2 * sat, x2 * cat + x1 * sat]
    zk_ref[0, :, K_AK:K_AK + ATTN_KV_WIDTH] = jnp.concatenate(rows, axis=0).T.astype(BF16)


def _in_projection(x, gain, w_t, w_k, rope):
    B, S, _ = x.shape
    n = S // BLOCK
    tok_base, tok_off, feat_base, feat_off = rope
    return pl.pallas_call(
        _inproj_kernel,
        grid=(S // TOK_TILE, B),
        in_specs=[pl.BlockSpec((1, TOK_TILE, D_MODEL), lambda i, b: (b, i, 0)),
                  _const_spec((1, D_MODEL)),
                  _const_spec((T_ROWS + ATTN_KV_WIDTH, D_MODEL)),
                  _const_spec((D_MODEL, RET_WIDTH)),
                  pl.BlockSpec((1,) + tok_base.shape[1:], lambda i, b: (i, 0, 0)),
                  _const_spec(tok_off.shape),
                  pl.BlockSpec((1,) + feat_base.shape[1:], lambda i, b: (i, 0, 0)),
                  _const_spec(feat_off.shape)],
        out_specs=(pl.BlockSpec((1, CHUNKS_PER_TILE, T_ROWS, BLOCK), lambda i, b: (b, i, 0, 0)),
                   pl.BlockSpec((1, TOK_TILE, K_COLS), lambda i, b: (b, i, 0))),
        out_shape=(jax.ShapeDtypeStruct((B, n, T_ROWS, BLOCK), BF16),
                   jax.ShapeDtypeStruct((B, S, K_COLS), BF16)),
        compiler_params=_params("parallel", "parallel"),
        name="in_projection",
    )(x, gain, w_t, w_k, tok_base, tok_off, feat_base, feat_off)


def _state_kernel(kf_ref, vf_ref, kb_ref, vb_ref, kdec_ref, cdec_ref,
                  pf_ref, pb_ref, st_ref):
    @pl.when(pl.program_id(1) == 0)
    def _():
        st_ref[...] = jnp.zeros_like(st_ref)

    for i in range(STATE_CHUNKS):
        for d, (k_ref, v_ref, p_ref, c) in enumerate(((kf_ref, vf_ref, pf_ref, i),
                                                      (kb_ref, vb_ref, pb_ref, STATE_CHUNKS - 1 - i))):
            kd = (k_ref[0, c * BLOCK:(c + 1) * BLOCK, :].astype(F32) * kdec_ref[d]).astype(BF16)
            for h in range(RET_HEADS):
                sl = slice(h * LANES, (h + 1) * LANES)
                st = st_ref[d, h]
                p_ref[0, c, h] = st.astype(BF16)
                kv = _dot(v_ref[0, c, sl, :], kd[:, sl])
                row = d * RET_HEADS + h
                st_ref[d, h] = st * cdec_ref[row:row + 1, :] + kv


def _retention_states(zt, zk, kdec, cdec):
    B, n = zt.shape[0], zt.shape[1]
    ng = n // STATE_CHUNKS
    k_blk = (1, STATE_CHUNKS * BLOCK, RET_WIDTH)
    v_blk = (1, STATE_CHUNKS, RET_WIDTH, BLOCK)
    vrow = T_RV // RET_WIDTH
    st_blk = (1, STATE_CHUNKS, RET_HEADS, RET_HEAD_DIM, RET_HEAD_DIM)
    st_shape = jax.ShapeDtypeStruct((B, n, RET_HEADS, RET_HEAD_DIM, RET_HEAD_DIM), BF16)
    return pl.pallas_call(
        _state_kernel,
        grid=(B, ng),
        in_specs=[pl.BlockSpec(k_blk, lambda b, t: (b, t, 0)),
                  pl.BlockSpec(v_blk, lambda b, t: (b, t, vrow, 0)),
                  pl.BlockSpec(k_blk, lambda b, t: (b, ng - 1 - t, 0)),
                  pl.BlockSpec(v_blk, lambda b, t: (b, ng - 1 - t, vrow, 0)),
                  _const_spec((2, BLOCK, RET_WIDTH)),
                  _const_spec((2 * RET_HEADS, LANES))],
        out_specs=(pl.BlockSpec(st_blk, lambda b, t: (b, t, 0, 0, 0)),
                   pl.BlockSpec(st_blk, lambda b, t: (b, ng - 1 - t, 0, 0, 0))),
        out_shape=(st_shape, st_shape),
        scratch_shapes=[pltpu.VMEM((2, RET_HEADS, RET_HEAD_DIM, RET_HEAD_DIM), F32)],
        compiler_params=_params("parallel", "arbitrary"),
        name="retention_states",
    )(zk, zt, zk, zt, kdec, cdec)


def _mixer_kernel(sink_ref, zt_ref, avp_ref, avn_ref, zk_ref, akp_ref, akn_ref,
                  pf_ref, pb_ref, dmat_ref, qdec_ref,
                  an_ref, gn_ref, mixt_ref, kext_ref, vext_ref, sc_ref):
    tile = pl.program_id(1)
    n_blocks = pl.num_programs(1) * MIX_CHUNKS

    lane = lax.broadcasted_iota(jnp.int32, (1, LANES), 1)
    for g in range(ATTN_KV_HEADS):
        km = jnp.where((lane // ATTN_HEAD_DIM) == g, 1.0, 0.0).astype(BF16)
        kext_ref[g, 0:BLOCK] = akp_ref[0] * km
        kext_ref[g, BLOCK:BLOCK + MIX_TILE] = zk_ref[0, :, K_AK:K_AK + ATTN_KV_WIDTH] * km
        kext_ref[g, BLOCK + MIX_TILE:] = akn_ref[0] * km
    vext_ref[0] = avp_ref[0, 0]
    for c in range(MIX_CHUNKS):
        vext_ref[1 + c] = zt_ref[0, c, T_AV:T_AV + ATTN_KV_WIDTH, :]
    vext_ref[MIX_CHUNKS + 1] = avn_ref[0, 0]

    kk = lax.broadcasted_iota(jnp.int32, (BLOCK, LANES), 0)
    qq = lax.broadcasted_iota(jnp.int32, (BLOCK, LANES), 1)
    neg = jnp.full((BLOCK, LANES), -jnp.inf, F32)
    zero = jnp.zeros((BLOCK, LANES), F32)

    def scores_into(slot, c):
        q = jnp.concatenate([zt_ref[0, c, T_AQ + j * LANES:T_AQ + (j + 1) * LANES, :]
                             for j in range(ATTN_GROUP)], axis=1)
        r0 = c * BLOCK
        for g in range(ATTN_KV_HEADS):
            sc_ref[slot, g] = _dot(kext_ref[g, pl.ds(r0, 3 * BLOCK), :], q)

    def chunk(c, has_next):
        r0 = c * BLOCK
        blk = tile * MIX_CHUNKS + c
        slot = c % 2
        if has_next:
            scores_into(1 - slot, c + 1)

        vw = jnp.concatenate([vext_ref[c], vext_ref[c + 1], vext_ref[c + 2]], axis=1)
        no_prev = jnp.where(blk > 0, 0, BLOCK)
        no_next = jnp.where(blk < n_blocks - 1, 0, BLOCK)
        bias_prev = jnp.where(kk >= qq + no_prev, zero, neg)
        bias_next = jnp.where(kk <= qq - no_next, zero, neg)
        bias_prev = jnp.concatenate([bias_prev] * ATTN_GROUP, axis=1)
        bias_next = jnp.concatenate([bias_next] * ATTN_GROUP, axis=1)
        heads = []
        for h in range(RET_HEADS):
            qt = zt_ref[0, c, T_RQ + h * LANES:T_RQ + (h + 1) * LANES, :]
            kh = zk_ref[0, pl.ds(r0, BLOCK), K_RK + h * LANES:K_RK + (h + 1) * LANES]
            heads.append((qt, _dot(kh, qt)))

        ones = jnp.ones((BF16_ROWS, 3 * BLOCK), BF16)
        outs = []
        for g in range(ATTN_KV_HEADS):
            s = sc_ref[slot, g]
            s = jnp.concatenate([s[0:BLOCK] + bias_prev, s[BLOCK:2 * BLOCK],
                                 s[2 * BLOCK:] + bias_next], axis=0)
            sk = jnp.concatenate(
                [jnp.full((1, LANES), sink_ref[g * ATTN_GROUP + j] * LOG2E, F32)
                 for j in range(ATTN_GROUP)], axis=1)
            m = jnp.maximum(jnp.max(s, axis=0, keepdims=True), sk)
            p = jnp.exp2(s - m).astype(BF16)
            vg = jnp.concatenate([vw[g * ATTN_HEAD_DIM:(g + 1) * ATTN_HEAD_DIM, :], ones], axis=0)
            ov = _dot(vg, p)
            den = ov[ATTN_HEAD_DIM:ATTN_HEAD_DIM + 1, :] + jnp.exp2(sk - m)
            outs.append(ov[0:ATTN_HEAD_DIM, :] * (1.0 / den))
        ys = []
        for h in range(RET_HEADS):
            qt, pt = heads[h]
            y = _dot(zt_ref[0, c, T_RV + h * LANES:T_RV + (h + 1) * LANES, :],
                     (pt * dmat_ref[h]).astype(BF16))
            y = y + _dot(pf_ref[0, c, h], qt) * qdec_ref[h:h + 1, :]
            y = y + _dot(pb_ref[0, c, h], qt) * qdec_ref[RET_HEADS + h:RET_HEADS + h + 1, :]
            ys.append(y)

        a = jnp.concatenate([outs[g][:, j * LANES:(j + 1) * LANES]
                             for j in range(ATTN_GROUP) for g in range(ATTN_KV_HEADS)], axis=0)
        ms = jnp.sum(a * a, axis=0, keepdims=True) * (1.0 / ATTN_WIDTH)
        mixt_ref[0, c, 0:ATTN_WIDTH, :] = (a * lax.rsqrt(ms + EPS) * an_ref[...]).astype(BF16)
        for h in range(RET_HEADS):
            sl = slice(h * LANES, (h + 1) * LANES)
            y = ys[h]
            mu = jnp.mean(y, axis=0, keepdims=True)
            yc = y - mu
            var = jnp.mean(yc * yc, axis=0, keepdims=True)
            yn = yc * lax.rsqrt(var + EPS) * gn_ref[sl, :]
            gate = zt_ref[0, c, T_RG + h * LANES:T_RG + (h + 1) * LANES, :].astype(F32)
            r = gate * jax.nn.sigmoid(gate) * yn
            mixt_ref[0, c, ATTN_WIDTH + h * LANES:ATTN_WIDTH + (h + 1) * LANES, :] = r.astype(BF16)

    scores_into(0, 0)
    for c in range(MIX_CHUNKS):
        chunk(c, c + 1 < MIX_CHUNKS)


def _mixer(zt, zk, states_f, states_b, dmat, qdec, sink, attn_norm, ret_gn):
    B, n = zt.shape[0], zt.shape[1]
    S = n * BLOCK
    cpt = MIX_CHUNKS
    edge_t = (1, 1, ATTN_KV_WIDTH, BLOCK)
    edge_k = (1, BLOCK, ATTN_KV_WIDTH)
    v_row = T_AV // ATTN_KV_WIDTH
    k_col = K_AK // ATTN_KV_WIDTH
    st_blk = (1, cpt, RET_HEADS, RET_HEAD_DIM, RET_HEAD_DIM)

    def prev(i):
        return jnp.maximum(i * cpt - 1, 0)

    def nxt(i):
        return jnp.minimum((i + 1) * cpt, n - 1)

    return pl.pallas_call(
        _mixer_kernel,
        grid=(B, S // MIX_TILE),
        in_specs=[pl.BlockSpec(memory_space=pltpu.SMEM),
                  pl.BlockSpec((1, cpt, T_ROWS, BLOCK), lambda b, i: (b, i, 0, 0)),
                  pl.BlockSpec(edge_t, lambda b, i: (b, prev(i), v_row, 0)),
                  pl.BlockSpec(edge_t, lambda b, i: (b, nxt(i), v_row, 0)),
                  pl.BlockSpec((1, MIX_TILE, K_COLS), lambda b, i: (b, i, 0)),
                  pl.BlockSpec(edge_k, lambda b, i: (b, prev(i), k_col)),
                  pl.BlockSpec(edge_k, lambda b, i: (b, nxt(i), k_col)),
                  pl.BlockSpec(st_blk, lambda b, i: (b, i, 0, 0, 0)),
                  pl.BlockSpec(st_blk, lambda b, i: (b, i, 0, 0, 0)),
                  _const_spec((RET_HEADS, BLOCK, LANES)),
                  _const_spec((2 * RET_HEADS, LANES)),
                  _const_spec((ATTN_WIDTH, LANES)),
                  _const_spec((RET_WIDTH, LANES))],
        out_specs=pl.BlockSpec((1, cpt, MIX_WIDTH, BLOCK), lambda b, i: (b, i, 0, 0)),
        out_shape=jax.ShapeDtypeStruct((B, n, MIX_WIDTH, BLOCK), BF16),
        scratch_shapes=[pltpu.VMEM((ATTN_KV_HEADS, MIX_TILE + 2 * BLOCK, ATTN_KV_WIDTH), BF16),
                        pltpu.VMEM((cpt + 2, ATTN_KV_WIDTH, BLOCK), BF16),
                        pltpu.VMEM((2, ATTN_KV_HEADS, 3 * BLOCK, ATTN_GROUP * LANES), F32)],
        compiler_params=_params("parallel", "parallel"),
        name="token_mixer",
    )(sink, zt, zt, zt, zk, zk, zk, states_f, states_b, dmat, qdec, attn_norm, ret_gn)


def _memkv_kernel(m_ref, g_ref, wkt_ref, wv_ref, kt_ref, v_ref):
    mn = _rms(m_ref[0], g_ref[...]).astype(BF16)
    kt_ref[0] = lax.dot_general(wkt_ref[...], mn, NT_DIMS,
                                preferred_element_type=F32).astype(BF16)
    v_ref[0] = _dot(mn, wv_ref[...]).astype(BF16)


def _memory_kv(mem, gain, w_kt, w_v):
    B, M, _ = mem.shape
    return pl.pallas_call(
        _memkv_kernel,
        grid=(B,),
        in_specs=[pl.BlockSpec((1, M, D_MODEL), lambda b: (b, 0, 0)),
                  _const_spec((1, D_MODEL)),
                  _const_spec((D_MODEL, D_MODEL)),
                  _const_spec((D_MODEL, D_MODEL))],
        out_specs=(pl.BlockSpec((1, D_MODEL, M), lambda b: (b, 0, 0)),
                   pl.BlockSpec((1, M, D_MODEL), lambda b: (b, 0, 0))),
        out_shape=(jax.ShapeDtypeStruct((B, D_MODEL, M), BF16),
                   jax.ShapeDtypeStruct((B, M, D_MODEL), BF16)),
        compiler_params=_params("parallel"),
        name="memory_kv",
    )(mem, gain, w_kt, w_v)


FF_CHUNK = 2 * MXU_TILE
FF_CHUNKS = [(c0, min(FF_CHUNK, D_FF - c0)) for c0 in range(0, D_FF, FF_CHUNK)]
assert all(w % MXU_TILE == 0 for _, w in FF_CHUNKS)
assert XA_HEADS <= 4 and N_MEM <= FF_CHUNK


def _tail_kernel(x_ref, mixt_ref, wmo_ref, gmp_ref, gxa_ref, wq_ref, kmt_ref, vm_ref, wo_ref,
                 gxp_ref, gfp_ref, wgu_ref, wd_ref, gfo_ref, o_ref,
                 cat_ref, q_ref, h_ref, res_ref, xs_ref, gu_ref, act_ref):
    pieces = [slice(r * ROW_PIECE, (r + 1) * ROW_PIECE) for r in range(TOK_TILE // ROW_PIECE)]
    per_piece = ROW_PIECE // BLOCK

    for r, rows in enumerate(pieces):
        mixt = jnp.concatenate([mixt_ref[0, r * per_piece + i] for i in range(per_piece)], axis=1)
        res_ref[rows, :] = lax.dot_general(mixt, wmo_ref[...], TN_DIMS,
                                           preferred_element_type=F32)
    for rows in pieces:
        x1 = x_ref[0, rows, :] + _rms(res_ref[rows, :], gmp_ref[...])
        xs_ref[rows, :] = x1
        h_ref[rows, :] = _rms(x1, gxa_ref[...]).astype(BF16)

    for rows in pieces:
        q_ref[rows, :] = _dot(h_ref[rows, :], wq_ref[...]).astype(BF16)
    for hd in range(XA_HEADS):
        sl = slice(hd * XA_HEAD_DIM, (hd + 1) * XA_HEAD_DIM)
        gu_ref[hd // 2, hd % 2, :, 0:N_MEM] = _dot(q_ref[:, sl], kmt_ref[0, sl, :])
    for hd in range(XA_HEADS):
        sl = slice(hd * XA_HEAD_DIM, (hd + 1) * XA_HEAD_DIM)
        s = gu_ref[hd // 2, hd % 2, :, 0:N_MEM]
        p = jnp.exp(s - jnp.max(s, axis=-1, keepdims=True))
        den = jnp.sum(p, axis=-1, keepdims=True)
        cat_ref[:, sl] = (_dot(p.astype(BF16), vm_ref[0, :, sl]) * (1.0 / den)).astype(BF16)
    for rows in pieces:
        res_ref[rows, :] = _dot(cat_ref[rows, :], wo_ref[...])

    def gate_up_into(slot, j, rows=slice(None)):
        c0, w = FF_CHUNKS[j]
        gu_ref[slot, 0, rows, 0:w] = _dot(h_ref[rows, :], wgu_ref[:, c0:c0 + w])
        gu_ref[slot, 1, rows, 0:w] = _dot(h_ref[rows, :], wgu_ref[:, D_FF + c0:D_FF + c0 + w])

    for rows in pieces:
        x2 = xs_ref[rows, :] + _rms(res_ref[rows, :], gxp_ref[...])
        xs_ref[rows, :] = x2
        h_ref[rows, :] = _rms(x2, gfp_ref[...]).astype(BF16)
    for rows in pieces:
        gate_up_into(0, 0, rows)

    for j, (c0, w) in enumerate(FF_CHUNKS):
        if j + 1 < len(FF_CHUNKS):
            gate_up_into((j + 1) % 2, j + 1)
        gate = gu_ref[j % 2, 0, :, 0:w]
        up = gu_ref[j % 2, 1, :, 0:w]
        act_ref[:, c0:c0 + w] = (gate * jax.nn.sigmoid(gate) * up).astype(BF16)
    for rows in pieces:
        res_ref[rows, :] = _dot(act_ref[rows, :], wd_ref[...])
    for rows in pieces:
        o_ref[0, rows, :] = xs_ref[rows, :] + _rms(res_ref[rows, :], gfo_ref[...])


def _block_tail(x, mixt, w_mix_out, gain_mix_post, kt_mem, v_mem, gain_xa_pre, w_q, w_o,
                gain_xa_post, gain_ffn_pre, w_gu, w_down, gain_ffn_post):
    B, S, _ = x.shape
    M = v_mem.shape[1]
    tok = pl.BlockSpec((1, TOK_TILE, D_MODEL), lambda b, i: (b, i, 0))
    gain = _const_spec((1, D_MODEL))
    return pl.pallas_call(
        _tail_kernel,
        grid=(B, S // TOK_TILE),
        in_specs=[tok,
                  pl.BlockSpec((1, CHUNKS_PER_TILE, MIX_WIDTH, BLOCK), lambda b, i: (b, i, 0, 0)),
                  _const_spec((MIX_WIDTH, D_MODEL)),
                  gain, gain,
                  _const_spec((D_MODEL, D_MODEL)),
                  pl.BlockSpec((1, D_MODEL, M), lambda b, i: (b, 0, 0)),
                  pl.BlockSpec((1, M, D_MODEL), lambda b, i: (b, 0, 0)),
                  _const_spec((D_MODEL, D_MODEL)),
                  gain, gain,
                  _const_spec((D_MODEL, 2 * D_FF)),
                  _const_spec((D_FF, D_MODEL)),
                  gain],
        out_specs=tok,
        out_shape=jax.ShapeDtypeStruct((B, S, D_MODEL), F32),
        scratch_shapes=[pltpu.VMEM((TOK_TILE, D_MODEL), BF16),
                        pltpu.VMEM((TOK_TILE, D_MODEL), BF16),
                        pltpu.VMEM((TOK_TILE, D_MODEL), BF16),
                        pltpu.VMEM((TOK_TILE, D_MODEL), F32),
                        pltpu.VMEM((TOK_TILE, D_MODEL), F32),
                        pltpu.VMEM((2, 2, TOK_TILE, FF_CHUNK), F32),
                        pltpu.VMEM((TOK_TILE, D_FF), BF16)],
        compiler_params=_params("parallel", "parallel"),
        name="block_tail",
    )(x, mixt, w_mix_out, gain_mix_post, gain_xa_pre, w_q, kt_mem, v_mem, w_o, gain_xa_post,
      gain_ffn_pre, w_gu, w_down, gain_ffn_post)


def _pair_heads(a, axis):
    shape = a.shape
    a = a.reshape(shape[:axis] + (ATTN_KV_HEADS, ATTN_GROUP, ATTN_HEAD_DIM) + shape[axis + 1:])
    return jnp.swapaxes(a, axis, axis + 1).reshape(shape)


def kernel(x, mem, norm_mix_pre, norm_mix_post, w_in, attn_sink, attn_out_norm,
           ret_decay_fwd, ret_decay_bwd, ret_gn, w_out, norm_xa_pre, norm_xa_post,
           norm_mem, xa_wq, xa_wkv, xa_wo, norm_ffn_pre, norm_ffn_post,
           ffn_w_gu, ffn_w_down):
    B, S, D = x.shape
    assert D == D_MODEL and S % max(TOK_TILE, MIX_TILE, STATE_CHUNKS * BLOCK) == 0
    assert mem.shape[1] == N_MEM
    depth = w_in.shape[0]
    o_k = ATTN_WIDTH
    o_v = o_k + ATTN_KV_WIDTH
    o_r = o_v + ATTN_KV_WIDTH
    rw = RET_WIDTH

    rope = _rope_tables(S // TOK_TILE)

    def row(v):
        return v.reshape(1, -1).astype(F32)

    def col(v):
        return jnp.broadcast_to(v.astype(F32)[:, None], (v.shape[0], LANES))

    for l in range(depth):
        w = w_in[l]
        w_t = jnp.concatenate([_pair_heads(w[:, :o_k], 1), w[:, o_r:o_r + rw], w[:, o_r + 2 * rw:],
                               w[:, o_v:o_r], w[:, o_k:o_v]], axis=1).T.astype(BF16)
        w_k = w[:, o_r + rw:o_r + 2 * rw].astype(BF16)
        w_o = jnp.concatenate([_pair_heads(w_out[l][:ATTN_WIDTH], 0), w_out[l][ATTN_WIDTH:]], axis=0)

        dmat, kdec, qdec, cdec = _decay_tables(ret_decay_fwd[l], ret_decay_bwd[l])
        zt, zk = _in_projection(x, row(norm_mix_pre[l]), w_t, w_k, rope)
        states_f, states_b = _retention_states(zt, zk, kdec, cdec)
        mixt = _mixer(zt, zk, states_f, states_b, dmat, qdec, attn_sink[l].astype(F32),
                      col(_pair_heads(attn_out_norm[l], 0)), col(ret_gn[l]))
        kt_mem, v_mem = _memory_kv(mem, row(norm_mem[l]), xa_wkv[l][:, :D_MODEL].T.astype(BF16),
                                   xa_wkv[l][:, D_MODEL:].astype(BF16))
        x = _block_tail(x, mixt, w_o.astype(BF16), row(norm_mix_post[l]),
                        kt_mem, v_mem, row(norm_xa_pre[l]),
                        (xa_wq[l] * (XA_HEAD_DIM ** -0.5)).astype(BF16),
                        xa_wo[l].astype(BF16), row(norm_xa_post[l]),
                        row(norm_ffn_pre[l]), ffn_w_gu[l].astype(BF16),
                        ffn_w_down[l].astype(BF16), row(norm_ffn_post[l]))
    return x
```

```python
import math

import numpy as np
import jax
import jax.numpy as jnp
from jax import lax
from jax.experimental import pallas as pl
from jax.experimental.pallas import tpu as pltpu

D_MODEL = 1024
N_MEM = 256
ATTN_HEADS = 8
ATTN_KV_HEADS = 2
ATTN_GROUP = ATTN_HEADS // ATTN_KV_HEADS
ATTN_HEAD_DIM = 64
ATTN_WIDTH = ATTN_HEADS * ATTN_HEAD_DIM
ATTN_KV_WIDTH = ATTN_KV_HEADS * ATTN_HEAD_DIM
BLOCK = 128
RET_HEADS = 4
RET_HEAD_DIM = 128
RET_WIDTH = RET_HEADS * RET_HEAD_DIM
IN_COLS = ATTN_WIDTH + 2 * ATTN_KV_WIDTH + 4 * RET_WIDTH
XA_HEADS = 4
XA_HEAD_DIM = D_MODEL // XA_HEADS
D_FF = -(-(8 * D_MODEL) // (3 * 256)) * 256
ROPE_THETA = 10000.0
EPS = 1e-6
LOG2E = math.log2(math.e)

LANES = 128
BF16_ROWS = 16
MXU_TILE = 256
VMEM_LIMIT_BYTES = 56 * 1024 * 1024

TOK_TILE = 512
CHUNKS_PER_TILE = TOK_TILE // BLOCK
STATE_CHUNKS = 16
MIX_TILE = 1024
MIX_CHUNKS = MIX_TILE // BLOCK
ROW_PIECE = 256

T_AQ, T_RQ, T_RV, T_RG, T_AV = 0, 512, 1024, 1536, 2048
T_ROWS = T_AV + ATTN_KV_WIDTH
K_RK, K_AK = 0, RET_WIDTH
K_COLS = RET_WIDTH + ATTN_KV_WIDTH

F32 = jnp.float32
BF16 = jnp.bfloat16
NT_DIMS = (((1,), (1,)), ((), ()))
TN_DIMS = (((0,), (0,)), ((), ()))


def _const_spec(shape):
    nd = len(shape)
    return pl.BlockSpec(shape, lambda *_: (0,) * nd, pipeline_mode=pl.Buffered(1))


def _params(*sem, flags=None):
    return pltpu.CompilerParams(dimension_semantics=sem,
                                vmem_limit_bytes=VMEM_LIMIT_BYTES, flags=flags)


def _rms(x, w):
    ms = jnp.mean(x * x, axis=-1, keepdims=True)
    return x * lax.rsqrt(ms + EPS) * w


def _dot(a, b):
    return jnp.dot(a, b, preferred_element_type=F32)


def _rope_tables(n_tiles):
    def parts(dim):
        inv_freq = ROPE_THETA ** (-np.arange(0, dim, 2, dtype=np.float64) / dim)
        base = (np.arange(n_tiles, dtype=np.float64) * TOK_TILE)[:, None] * inv_freq[None, :]
        off = np.arange(TOK_TILE, dtype=np.float64)[:, None] * inv_freq[None, :]
        return np.cos(base), np.sin(base), np.cos(off), np.sin(off)

    def tok(c, s):
        return (np.concatenate([c, c], axis=1), np.concatenate([s, s], axis=1),
                np.concatenate([-s, s], axis=1))

    cb_a, sb_a, co_a, so_a = parts(ATTN_HEAD_DIM)
    cb_r, sb_r, co_r, so_r = parts(RET_HEAD_DIM)
    tok_base = np.stack(tok(cb_r, sb_r) + (np.zeros((n_tiles, LANES)),) * 5, axis=1)
    tok_off = np.stack(tok(co_r, so_r), axis=0)
    feat_base = np.concatenate([cb_a, sb_a, cb_r, sb_r], axis=1)
    feat_base = np.broadcast_to(feat_base[:, :, None], feat_base.shape + (BLOCK,))
    feat_off = np.concatenate([co_a, so_a, co_r, so_r], axis=1).T
    return tuple(jnp.asarray(t, F32) for t in (tok_base, tok_off, feat_base, feat_off))


def _decay_kernel(logit_ref, dmat_ref, kdec_ref, qdec_ref, cdec_ref):
    x = logit_ref[...]
    lg = jnp.minimum(x, 0.0) - jnp.log1p(jnp.exp(-jnp.abs(x)))
    ri = lax.broadcasted_iota(jnp.int32, (BLOCK, LANES), 0).astype(F32)
    ci = lax.broadcasted_iota(jnp.int32, (BLOCK, LANES), 1).astype(F32)
    c1 = ci[0:1, :]
    diff = ci - ri
    cdec_ref[...] = jnp.exp(BLOCK * lg)
    for h in range(RET_HEADS):
        lf = lg[h:h + 1, :]
        lb = lg[RET_HEADS + h:RET_HEADS + h + 1, :]
        dmat_ref[h] = jnp.exp(jnp.abs(diff) * jnp.where(diff >= 0, lf, lb))
        sl = slice(h * LANES, (h + 1) * LANES)
        kdec_ref[0, :, sl] = jnp.exp((BLOCK - 1 - ri) * lf)
        kdec_ref[1, :, sl] = jnp.exp(ri * lb)
        qdec_ref[h:h + 1, :] = jnp.exp((c1 + 1) * lf)
        qdec_ref[RET_HEADS + h:RET_HEADS + h + 1, :] = jnp.exp((BLOCK - c1) * lb)


def _decay_tables(decay_fwd, decay_bwd):
    logits = jnp.concatenate([decay_fwd, decay_bwd]).astype(F32)
    logits = jnp.broadcast_to(logits[:, None], (2 * RET_HEADS, LANES))
    return pl.pallas_call(
        _decay_kernel,
        out_shape=(jax.ShapeDtypeStruct((RET_HEADS, BLOCK, LANES), F32),
                   jax.ShapeDtypeStruct((2, BLOCK, RET_WIDTH), F32),
                   jax.ShapeDtypeStruct((2 * RET_HEADS, LANES), F32),
                   jax.ShapeDtypeStruct((2 * RET_HEADS, LANES), F32)),
        name="decay_tables",
    )(logits)


def _inproj_kernel(x_ref, g_ref, wt_ref, wk_ref, tb_ref, to_ref, fb_ref, fo_ref,
                   zt_ref, zk_ref):
    h = _rms(x_ref[0], g_ref[...]).astype(BF16)

    tb = tb_ref[0]

    ck = tb[0:1] * to_ref[0] - tb[1:2] * to_ref[1]
    sk = tb[2:3] * to_ref[0] + tb[0:1] * to_ref[2]

    fb = jnp.concatenate([fb_ref[0]] * CHUNKS_PER_TILE, axis=1)
    fo = fo_ref[...]

    def feat_tables(row, n):
        cb, sb = fb[row:row + n], fb[row + n:row + 2 * n]
        co, so = fo[row:row + n], fo[row + n:row + 2 * n]
        return cb * co - sb * so, sb * co + cb * so

    zk = _dot(h, wk_ref[...])
    for g in range(RET_HEADS):
        zg = zk[:, g * LANES:(g + 1) * LANES]
        r = (zg * ck + pltpu.roll(zg, LANES // 2, 1) * sk) * (RET_HEAD_DIM ** -0.5)
        zk_ref[0, :, K_RK + g * LANES:K_RK + (g + 1) * LANES] = r.astype(BF16)

    def proj_t(row, n_rows):
        return lax.dot_general(wt_ref[row:row + n_rows, :], h, NT_DIMS,
                               preferred_element_type=F32)

    def store_t(row, val):
        for c in range(CHUNKS_PER_TILE):
            zt_ref[0, c, row:row + val.shape[0], :] = val[:, c * BLOCK:(c + 1) * BLOCK].astype(BF16)

    def rope_t(z, row, head_dim, cos, sin):
        hd2 = head_dim // 2
        for k in range(z.shape[0] // head_dim):
            x1 = z[k * head_dim:k * head_dim + hd2]
            x2 = z[k * head_dim + hd2:(k + 1) * head_dim]
            store_t(row + k * head_dim, x1 * cos - x2 * sin)
            store_t(row + k * head_dim + hd2, x2 * cos + x1 * sin)

    q_scale = (ATTN_HEAD_DIM ** -0.5) * LOG2E
    cat, sat = feat_tables(0, ATTN_HEAD_DIM // 2)
    rope_t(proj_t(T_AQ, ATTN_WIDTH), T_AQ, ATTN_HEAD_DIM, cat * q_scale, sat * q_scale)
    crt, srt = feat_tables(ATTN_HEAD_DIM, RET_HEAD_DIM // 2)
    rope_t(proj_t(T_RQ, RET_WIDTH), T_RQ, RET_HEAD_DIM, crt, srt)
    store_t(T_RV, proj_t(T_RV, RET_WIDTH))
    store_t(T_RG, proj_t(T_RG, RET_WIDTH))
    kv = proj_t(T_AV, 2 * ATTN_KV_WIDTH)
    store_t(T_AV, kv[0:ATTN_KV_WIDTH])
    hd2 = ATTN_HEAD_DIM // 2
    rows = []
    for k in range(ATTN_KV_HEADS):
        r0 = ATTN_KV_WIDTH + k * ATTN_HEAD_DIM
        x1, x2 = kv[r0:r0 + hd2], kv[r0 + hd2:r0 + ATTN_HEAD_DIM]
        rows += [x1 * cat - x2 * sat, x2 * cat + x1 * sat]
    zk_ref[0, :, K_AK:K_AK + ATTN_KV_WIDTH] = jnp.concatenate(rows, axis=0).T.astype(BF16)


def _in_projection(x, gain, w_t, w_k, rope):
    B, S, _ = x.shape
    n = S // BLOCK
    tok_base, tok_off, feat_base, feat_off = rope
    return pl.pallas_call(
        _inproj_kernel,
        grid=(S // TOK_TILE, B),
        in_specs=[pl.BlockSpec((1, TOK_TILE, D_MODEL), lambda i, b: (b, i, 0)),
                  _const_spec((1, D_MODEL)),
                  _const_spec((T_ROWS + ATTN_KV_WIDTH, D_MODEL)),
                  _const_spec((D_MODEL, RET_WIDTH)),
                  pl.BlockSpec((1,) + tok_base.shape[1:], lambda i, b: (i, 0, 0)),
                  _const_spec(tok_off.shape),
                  pl.BlockSpec((1,) + feat_base.shape[1:], lambda i, b: (i, 0, 0)),
                  _const_spec(feat_off.shape)],
        out_specs=(pl.BlockSpec((1, CHUNKS_PER_TILE, T_ROWS, BLOCK), lambda i, b: (b, i, 0, 0)),
                   pl.BlockSpec((1, TOK_TILE, K_COLS), lambda i, b: (b, i, 0))),
        out_shape=(jax.ShapeDtypeStruct((B, n, T_ROWS, BLOCK), BF16),
                   jax.ShapeDtypeStruct((B, S, K_COLS), BF16)),
        compiler_params=_params("parallel", "parallel"),
        name="in_projection",
    )(x, gain, w_t, w_k, tok_base, tok_off, feat_base, feat_off)


def _state_kernel(kf_ref, vf_ref, kb_ref, vb_ref, kdec_ref, cdec_ref,
                  pf_ref, pb_ref, st_ref):
    @pl.when(pl.program_id(1) == 0)
    def _():
        st_ref[...] = jnp.zeros_like(st_ref)

    for i in range(STATE_CHUNKS):
        for d, (k_ref, v_ref, p_ref, c) in enumerate(((kf_ref, vf_ref, pf_ref, i),
                                                      (kb_ref, vb_ref, pb_ref, STATE_CHUNKS - 1 - i))):
            kd = (k_ref[0, c * BLOCK:(c + 1) * BLOCK, :].astype(F32) * kdec_ref[d]).astype(BF16)
            for h in range(RET_HEADS):
                sl = slice(h * LANES, (h + 1) * LANES)
                st = st_ref[d, h]
                p_ref[0, c, h] = st.astype(BF16)
                kv = _dot(v_ref[0, c, sl, :], kd[:, sl])
                row = d * RET_HEADS + h
                st_ref[d, h] = st * cdec_ref[row:row + 1, :] + kv


def _retention_states(zt, zk, kdec, cdec):
    B, n = zt.shape[0], zt.shape[1]
    ng = n // STATE_CHUNKS
    k_blk = (1, STATE_CHUNKS * BLOCK, RET_WIDTH)
    v_blk = (1, STATE_CHUNKS, RET_WIDTH, BLOCK)
    vrow = T_RV // RET_WIDTH
    st_blk = (1, STATE_CHUNKS, RET_HEADS, RET_HEAD_DIM, RET_HEAD_DIM)
    st_shape = jax.ShapeDtypeStruct((B, n, RET_HEADS, RET_HEAD_DIM, RET_HEAD_DIM), BF16)
    return pl.pallas_call(
        _state_kernel,
        grid=(B, ng),
        in_specs=[pl.BlockSpec(k_blk, lambda b, t: (b, t, 0)),
                  pl.BlockSpec(v_blk, lambda b, t: (b, t, vrow, 0)),
                  pl.BlockSpec(k_blk, lambda b, t: (b, ng - 1 - t, 0)),
                  pl.BlockSpec(v_blk, lambda b, t: (b, ng - 1 - t, vrow, 0)),
                  _const_spec((2, BLOCK, RET_WIDTH)),
                  _const_spec((2 * RET_HEADS, LANES))],
        out_specs=(pl.BlockSpec(st_blk, lambda b, t: (b, t, 0, 0, 0)),
                   pl.BlockSpec(st_blk, lambda b, t: (b, ng - 1 - t, 0, 0, 0))),
        out_shape=(st_shape, st_shape),
        scratch_shapes=[pltpu.VMEM((2, RET_HEADS, RET_HEAD_DIM, RET_HEAD_DIM), F32)],
        compiler_params=_params("parallel", "arbitrary"),
        name="retention_states",
    )(zk, zt, zk, zt, kdec, cdec)


def _mixer_kernel(sink_ref, zt_ref, avp_ref, avn_ref, zk_ref, akp_ref, akn_ref,
                  pf_ref, pb_ref, dmat_ref, qdec_ref,
                  an_ref, gn_ref, wout_ref, o_ref, kext_ref, vext_ref, mixt_ref,
                  sc_ref):
    tile = pl.program_id(1)
    n_blocks = pl.num_programs(1) * MIX_CHUNKS

    lane = lax.broadcasted_iota(jnp.int32, (1, LANES), 1)
    for g in range(ATTN_KV_HEADS):
        km = jnp.where((lane // ATTN_HEAD_DIM) == g, 1.0, 0.0).astype(BF16)
        kext_ref[g, 0:BLOCK] = akp_ref[0] * km
        kext_ref[g, BLOCK:BLOCK + MIX_TILE] = zk_ref[0, :, K_AK:K_AK + ATTN_KV_WIDTH] * km
        kext_ref[g, BLOCK + MIX_TILE:] = akn_ref[0] * km
    vext_ref[0] = avp_ref[0, 0]
    for c in range(MIX_CHUNKS):
        vext_ref[1 + c] = zt_ref[0, c, T_AV:T_AV + ATTN_KV_WIDTH, :]
    vext_ref[MIX_CHUNKS + 1] = avn_ref[0, 0]

    kk = lax.broadcasted_iota(jnp.int32, (BLOCK, LANES), 0)
    qq = lax.broadcasted_iota(jnp.int32, (BLOCK, LANES), 1)
    neg = jnp.full((BLOCK, LANES), -jnp.inf, F32)
    zero = jnp.zeros((BLOCK, LANES), F32)

    def scores_into(slot, c):
        q = jnp.concatenate([zt_ref[0, c, T_AQ + j * LANES:T_AQ + (j + 1) * LANES, :]
                             for j in range(ATTN_GROUP)], axis=1)
        r0 = c * BLOCK
        for g in range(ATTN_KV_HEADS):
            sc_ref[slot, g] = _dot(kext_ref[g, pl.ds(r0, 3 * BLOCK), :], q)

    def chunk(c, has_next):
        r0 = c * BLOCK
        blk = tile * MIX_CHUNKS + c
        slot = c % 2
        if has_next:
            scores_into(1 - slot, c + 1)

        vw = jnp.concatenate([vext_ref[c], vext_ref[c + 1], vext_ref[c + 2]], axis=1)
        no_prev = jnp.where(blk > 0, 0, BLOCK)
        no_next = jnp.where(blk < n_blocks - 1, 0, BLOCK)
        bias_prev = jnp.where(kk >= qq + no_prev, zero, neg)
        bias_next = jnp.where(kk <= qq - no_next, zero, neg)
        bias_prev = jnp.concatenate([bias_prev] * ATTN_GROUP, axis=1)
        bias_next = jnp.concatenate([bias_next] * ATTN_GROUP, axis=1)
        heads = []
        for h in range(RET_HEADS):
            qt = zt_ref[0, c, T_RQ + h * LANES:T_RQ + (h + 1) * LANES, :]
            kh = zk_ref[0, pl.ds(r0, BLOCK), K_RK + h * LANES:K_RK + (h + 1) * LANES]
            heads.append((qt, _dot(kh, qt)))

        ones = jnp.ones((BF16_ROWS, 3 * BLOCK), BF16)
        outs = []
        for g in range(ATTN_KV_HEADS):
            s = sc_ref[slot, g]
            s = jnp.concatenate([s[0:BLOCK] + bias_prev, s[BLOCK:2 * BLOCK],
                                 s[2 * BLOCK:] + bias_next], axis=0)
            sk = jnp.concatenate(
                [jnp.full((1, LANES), sink_ref[g * ATTN_GROUP + j] * LOG2E, F32)
                 for j in range(ATTN_GROUP)], axis=1)
            m = jnp.maximum(jnp.max(s, axis=0, keepdims=True), sk)
            p = jnp.exp2(s - m).astype(BF16)
            vg = jnp.concatenate([vw[g * ATTN_HEAD_DIM:(g + 1) * ATTN_HEAD_DIM, :], ones], axis=0)
            ov = _dot(vg, p)
            den = ov[ATTN_HEAD_DIM:ATTN_HEAD_DIM + 1, :] + jnp.exp2(sk - m)
            outs.append(ov[0:ATTN_HEAD_DIM, :] * (1.0 / den))
        ys = []
        for h in range(RET_HEADS):
            qt, pt = heads[h]
            y = _dot(zt_ref[0, c, T_RV + h * LANES:T_RV + (h + 1) * LANES, :],
                     (pt * dmat_ref[h]).astype(BF16))
            y = y + _dot(pf_ref[0, c, h], qt) * qdec_ref[h:h + 1, :]
            y = y + _dot(pb_ref[0, c, h], qt) * qdec_ref[RET_HEADS + h:RET_HEADS + h + 1, :]
            ys.append(y)

        a = jnp.concatenate([outs[g][:, j * LANES:(j + 1) * LANES]
                             for j in range(ATTN_GROUP) for g in range(ATTN_KV_HEADS)], axis=0)
        ms = jnp.sum(a * a, axis=0, keepdims=True) * (1.0 / ATTN_WIDTH)
        mixt_ref[c, 0:ATTN_WIDTH, :] = (a * lax.rsqrt(ms + EPS) * an_ref[...]).astype(BF16)
        for h in range(RET_HEADS):
            sl = slice(h * LANES, (h + 1) * LANES)
            y = ys[h]
            mu = jnp.mean(y, axis=0, keepdims=True)
            yc = y - mu
            var = jnp.mean(yc * yc, axis=0, keepdims=True)
            yn = yc * lax.rsqrt(var + EPS) * gn_ref[sl, :]
            gate = zt_ref[0, c, T_RG + h * LANES:T_RG + (h + 1) * LANES, :].astype(F32)
            r = gate * jax.nn.sigmoid(gate) * yn
            mixt_ref[c, ATTN_WIDTH + h * LANES:ATTN_WIDTH + (h + 1) * LANES, :] = r.astype(BF16)

    scores_into(0, 0)
    for c in range(MIX_CHUNKS):
        chunk(c, c + 1 < MIX_CHUNKS)

    mixt = jnp.concatenate([mixt_ref[c] for c in range(MIX_CHUNKS)], axis=1)
    o_ref[0] = lax.dot_general(mixt, wout_ref[...], TN_DIMS, preferred_element_type=F32)


def _mixer(zt, zk, states_f, states_b, dmat, qdec, sink, attn_norm, ret_gn, w_out):
    B, n = zt.shape[0], zt.shape[1]
    S = n * BLOCK
    cpt = MIX_CHUNKS
    edge_t = (1, 1, ATTN_KV_WIDTH, BLOCK)
    edge_k = (1, BLOCK, ATTN_KV_WIDTH)
    v_row = T_AV // ATTN_KV_WIDTH
    k_col = K_AK // ATTN_KV_WIDTH
    st_blk = (1, cpt, RET_HEADS, RET_HEAD_DIM, RET_HEAD_DIM)

    def prev(i):
        return jnp.maximum(i * cpt - 1, 0)

    def nxt(i):
        return jnp.minimum((i + 1) * cpt, n - 1)

    return pl.pallas_call(
        _mixer_kernel,
        grid=(B, S // MIX_TILE),
        in_specs=[pl.BlockSpec(memory_space=pltpu.SMEM),
                  pl.BlockSpec((1, cpt, T_ROWS, BLOCK), lambda b, i: (b, i, 0, 0)),
                  pl.BlockSpec(edge_t, lambda b, i: (b, prev(i), v_row, 0)),
                  pl.BlockSpec(edge_t, lambda b, i: (b, nxt(i), v_row, 0)),
                  pl.BlockSpec((1, MIX_TILE, K_COLS), lambda b, i: (b, i, 0)),
                  pl.BlockSpec(edge_k, lambda b, i: (b, prev(i), k_col)),
                  pl.BlockSpec(edge_k, lambda b, i: (b, nxt(i), k_col)),
                  pl.BlockSpec(st_blk, lambda b, i: (b, i, 0, 0, 0)),
                  pl.BlockSpec(st_blk, lambda b, i: (b, i, 0, 0, 0)),
                  _const_spec((RET_HEADS, BLOCK, LANES)),
                  _const_spec((2 * RET_HEADS, LANES)),
                  _const_spec((ATTN_WIDTH, LANES)),
                  _const_spec((RET_WIDTH, LANES)),
                  _const_spec((D_MODEL, D_MODEL))],
        out_specs=pl.BlockSpec((1, MIX_TILE, D_MODEL), lambda b, i: (b, i, 0)),
        out_shape=jax.ShapeDtypeStruct((B, S, D_MODEL), F32),
        scratch_shapes=[pltpu.VMEM((ATTN_KV_HEADS, MIX_TILE + 2 * BLOCK, ATTN_KV_WIDTH), BF16),
                        pltpu.VMEM((cpt + 2, ATTN_KV_WIDTH, BLOCK), BF16),
                        pltpu.VMEM((cpt, D_MODEL, BLOCK), BF16),
                        pltpu.VMEM((2, ATTN_KV_HEADS, 3 * BLOCK, ATTN_GROUP * LANES), F32)],
        compiler_params=_params("parallel", "parallel"),
        name="token_mixer",
    )(sink, zt, zt, zt, zk, zk, zk, states_f, states_b, dmat, qdec, attn_norm, ret_gn, w_out)


def _memkv_kernel(m_ref, g_ref, wkt_ref, wv_ref, kt_ref, v_ref):
    mn = _rms(m_ref[0], g_ref[...]).astype(BF16)
    kt_ref[0] = lax.dot_general(wkt_ref[...], mn, NT_DIMS,
                                preferred_element_type=F32).astype(BF16)
    v_ref[0] = _dot(mn, wv_ref[...]).astype(BF16)


def _memory_kv(mem, gain, w_kt, w_v):
    B, M, _ = mem.shape
    return pl.pallas_call(
        _memkv_kernel,
        grid=(B,),
        in_specs=[pl.BlockSpec((1, M, D_MODEL), lambda b: (b, 0, 0)),
                  _const_spec((1, D_MODEL)),
                  _const_spec((D_MODEL, D_MODEL)),
                  _const_spec((D_MODEL, D_MODEL))],
        out_specs=(pl.BlockSpec((1, D_MODEL, M), lambda b: (b, 0, 0)),
                   pl.BlockSpec((1, M, D_MODEL), lambda b: (b, 0, 0))),
        out_shape=(jax.ShapeDtypeStruct((B, D_MODEL, M), BF16),
                   jax.ShapeDtypeStruct((B, M, D_MODEL), BF16)),
        compiler_params=_params("parallel"),
        name="memory_kv",
    )(mem, gain, w_kt, w_v)


FF_CHUNK = 2 * MXU_TILE
FF_CHUNKS = [(c0, min(FF_CHUNK, D_FF - c0)) for c0 in range(0, D_FF, FF_CHUNK)]
assert all(w % MXU_TILE == 0 for _, w in FF_CHUNKS)
assert XA_HEADS <= 4 and N_MEM <= FF_CHUNK


def _xattn_ffn_kernel(x_ref, mix_ref, gmp_ref, gxa_ref, wq_ref, kmt_ref, vm_ref, wo_ref, gxp_ref,
                      gfp_ref, wgu_ref, wd_ref, gfo_ref, o_ref,
                      cat_ref, q_ref, h_ref, res_ref, xs_ref, gu_ref, act_ref):
    pieces = [slice(r * ROW_PIECE, (r + 1) * ROW_PIECE) for r in range(TOK_TILE // ROW_PIECE)]

    for rows in pieces:
        x1 = x_ref[0, rows, :] + _rms(mix_ref[0, rows, :], gmp_ref[...])
        xs_ref[rows, :] = x1
        h_ref[rows, :] = _rms(x1, gxa_ref[...]).astype(BF16)
    for rows in pieces:
        q_ref[rows, :] = _dot(h_ref[rows, :], wq_ref[...]).astype(BF16)
    for hd in range(XA_HEADS):
        sl = slice(hd * XA_HEAD_DIM, (hd + 1) * XA_HEAD_DIM)
        gu_ref[hd // 2, hd % 2, :, 0:N_MEM] = _dot(q_ref[:, sl], kmt_ref[0, sl, :])
    for hd in range(XA_HEADS):
        sl = slice(hd * XA_HEAD_DIM, (hd + 1) * XA_HEAD_DIM)
        s = gu_ref[hd // 2, hd % 2, :, 0:N_MEM]
        p = jnp.exp(s - jnp.max(s, axis=-1, keepdims=True))
        den = jnp.sum(p, axis=-1, keepdims=True)
        cat_ref[:, sl] = (_dot(p.astype(BF16), vm_ref[0, :, sl]) * (1.0 / den)).astype(BF16)
    for rows in pieces:
        res_ref[rows, :] = _dot(cat_ref[rows, :], wo_ref[...])

    def gate_up_into(slot, j, rows=slice(None)):
        c0, w = FF_CHUNKS[j]
        gu_ref[slot, 0, rows, 0:w] = _dot(h_ref[rows, :], wgu_ref[:, c0:c0 + w])
        gu_ref[slot, 1, rows, 0:w] = _dot(h_ref[rows, :], wgu_ref[:, D_FF + c0:D_FF + c0 + w])

    for rows in pieces:
        x2 = xs_ref[rows, :] + _rms(res_ref[rows, :], gxp_ref[...])
        xs_ref[rows, :] = x2
        h_ref[rows, :] = _rms(x2, gfp_ref[...]).astype(BF16)
    for rows in pieces:
        gate_up_into(0, 0, rows)

    for j, (c0, w) in enumerate(FF_CHUNKS):
        if j + 1 < len(FF_CHUNKS):
            gate_up_into((j + 1) % 2, j + 1)
        gate = gu_ref[j % 2, 0, :, 0:w]
        up = gu_ref[j % 2, 1, :, 0:w]
        act_ref[:, c0:c0 + w] = (gate * jax.nn.sigmoid(gate) * up).astype(BF16)
    for rows in pieces:
        res_ref[rows, :] = _dot(act_ref[rows, :], wd_ref[...])
    for rows in pieces:
        o_ref[0, rows, :] = xs_ref[rows, :] + _rms(res_ref[rows, :], gfo_ref[...])


def _cross_attention_ffn(x, mix, gain_mix_post, kt_mem, v_mem, gain_xa_pre, w_q, w_o,
                         gain_xa_post, gain_ffn_pre, w_gu, w_down, gain_ffn_post):
    B, S, _ = x.shape
    M = v_mem.shape[1]
    tok = pl.BlockSpec((1, TOK_TILE, D_MODEL), lambda b, i: (b, i, 0))
    gain = _const_spec((1, D_MODEL))
    return pl.pallas_call(
        _xattn_ffn_kernel,
        grid=(B, S // TOK_TILE),
        in_specs=[tok, tok, gain, gain,
                  _const_spec((D_MODEL, D_MODEL)),
                  pl.BlockSpec((1, D_MODEL, M), lambda b, i: (b, 0, 0)),
                  pl.BlockSpec((1, M, D_MODEL), lambda b, i: (b, 0, 0)),
                  _const_spec((D_MODEL, D_MODEL)),
                  gain, gain,
                  _const_spec((D_MODEL, 2 * D_FF)),
                  _const_spec((D_FF, D_MODEL)),
                  gain],
        out_specs=tok,
        out_shape=jax.ShapeDtypeStruct((B, S, D_MODEL), F32),
        scratch_shapes=[pltpu.VMEM((TOK_TILE, D_MODEL), BF16),
                        pltpu.VMEM((TOK_TILE, D_MODEL), BF16),
                        pltpu.VMEM((TOK_TILE, D_MODEL), BF16),
                        pltpu.VMEM((TOK_TILE, D_MODEL), F32),
                        pltpu.VMEM((TOK_TILE, D_MODEL), F32),
                        pltpu.VMEM((2, 2, TOK_TILE, FF_CHUNK), F32),
                        pltpu.VMEM((TOK_TILE, D_FF), BF16)],
        compiler_params=_params("parallel", "parallel"),
        name="cross_attention_ffn",
    )(x, mix, gain_mix_post, gain_xa_pre, w_q, kt_mem, v_mem, w_o, gain_xa_post,
      gain_ffn_pre, w_gu, w_down, gain_ffn_post)


def _pair_heads(a, axis):
    shape = a.shape
    a = a.reshape(shape[:axis] + (ATTN_KV_HEADS, ATTN_GROUP, ATTN_HEAD_DIM) + shape[axis + 1:])
    return jnp.swapaxes(a, axis, axis + 1).reshape(shape)


def kernel(x, mem, norm_mix_pre, norm_mix_post, w_in, attn_sink, attn_out_norm,
           ret_decay_fwd, ret_decay_bwd, ret_gn, w_out, norm_xa_pre, norm_xa_post,
           norm_mem, xa_wq, xa_wkv, xa_wo, norm_ffn_pre, norm_ffn_post,
           ffn_w_gu, ffn_w_down):
    B, S, D = x.shape
    assert D == D_MODEL and S % max(TOK_TILE, MIX_TILE, STATE_CHUNKS * BLOCK) == 0
    assert mem.shape[1] == N_MEM
    depth = w_in.shape[0]
    o_k = ATTN_WIDTH
    o_v = o_k + ATTN_KV_WIDTH
    o_r = o_v + ATTN_KV_WIDTH
    rw = RET_WIDTH

    rope = _rope_tables(S // TOK_TILE)

    def row(v):
        return v.reshape(1, -1).astype(F32)

    def col(v):
        return jnp.broadcast_to(v.astype(F32)[:, None], (v.shape[0], LANES))

    for l in range(depth):
        w = w_in[l]
        w_t = jnp.concatenate([_pair_heads(w[:, :o_k], 1), w[:, o_r:o_r + rw], w[:, o_r + 2 * rw:],
                               w[:, o_v:o_r], w[:, o_k:o_v]], axis=1).T.astype(BF16)
        w_k = w[:, o_r + rw:o_r + 2 * rw].astype(BF16)
        w_o = jnp.concatenate([_pair_heads(w_out[l][:ATTN_WIDTH], 0), w_out[l][ATTN_WIDTH:]], axis=0)

        dmat, kdec, qdec, cdec = _decay_tables(ret_decay_fwd[l], ret_decay_bwd[l])
        zt, zk = _in_projection(x, row(norm_mix_pre[l]), w_t, w_k, rope)
        states_f, states_b = _retention_states(zt, zk, kdec, cdec)
        mix = _mixer(zt, zk, states_f, states_b, dmat, qdec, attn_sink[l].astype(F32),
                     col(_pair_heads(attn_out_norm[l], 0)), col(ret_gn[l]), w_o.astype(BF16))
        kt_mem, v_mem = _memory_kv(mem, row(norm_mem[l]), xa_wkv[l][:, :D_MODEL].T.astype(BF16),
                                   xa_wkv[l][:, D_MODEL:].astype(BF16))
        x = _cross_attention_ffn(x, mix, row(norm_mix_post[l]), kt_mem, v_mem, row(norm_xa_pre[l]),
                                 (xa_wq[l] * (XA_HEAD_DIM ** -0.5)).astype(BF16),
                                 xa_wo[l].astype(BF16), row(norm_xa_post[l]),
                                 row(norm_ffn_pre[l]), ffn_w_gu[l].astype(BF16),
                                 ffn_w_down[l].astype(BF16), row(norm_ffn_post[l]))
    return x
```

```python
import math

import numpy as np
import jax
import jax.numpy as jnp
from jax import lax
from jax.experimental import pallas as pl
from jax.experimental.pallas import tpu as pltpu

D_MODEL = 1024
N_MEM = 256
ATTN_HEADS = 8
ATTN_KV_HEADS = 2
ATTN_GROUP = ATTN_HEADS // ATTN_KV_HEADS
ATTN_HEAD_DIM = 64
ATTN_WIDTH = ATTN_HEADS * ATTN_HEAD_DIM
ATTN_KV_WIDTH = ATTN_KV_HEADS * ATTN_HEAD_DIM
BLOCK = 128
RET_HEADS = 4
RET_HEAD_DIM = 128
RET_WIDTH = RET_HEADS * RET_HEAD_DIM
IN_COLS = ATTN_WIDTH + 2 * ATTN_KV_WIDTH + 4 * RET_WIDTH
XA_HEADS = 4
XA_HEAD_DIM = D_MODEL // XA_HEADS
D_FF = -(-(8 * D_MODEL) // (3 * 256)) * 256
ROPE_THETA = 10000.0
EPS = 1e-6
LOG2E = math.log2(math.e)

LANES = 128
BF16_ROWS = 16
MXU_TILE = 256
VMEM_LIMIT_BYTES = 56 * 1024 * 1024

TOK_TILE = 512
CHUNKS_PER_TILE = TOK_TILE // BLOCK
STATE_CHUNKS = 16
MIX_TILE = 1024
MIX_CHUNKS = MIX_TILE // BLOCK
ROW_PIECE = 256

T_AQ, T_RQ, T_RV, T_RG, T_AV = 0, 512, 1024, 1536, 2048
T_ROWS = T_AV + ATTN_KV_WIDTH
K_RK, K_AK = 0, RET_WIDTH
K_COLS = RET_WIDTH + ATTN_KV_WIDTH

F32 = jnp.float32
BF16 = jnp.bfloat16
NT_DIMS = (((1,), (1,)), ((), ()))
TN_DIMS = (((0,), (0,)), ((), ()))


def _const_spec(shape):
    nd = len(shape)
    return pl.BlockSpec(shape, lambda *_: (0,) * nd, pipeline_mode=pl.Buffered(1))


def _params(*sem, flags=None):
    return pltpu.CompilerParams(dimension_semantics=sem,
                                vmem_limit_bytes=VMEM_LIMIT_BYTES, flags=flags)


def _rms(x, w):
    ms = jnp.mean(x * x, axis=-1, keepdims=True)
    return x * lax.rsqrt(ms + EPS) * w


def _dot(a, b):
    return jnp.dot(a, b, preferred_element_type=F32)


def _rope_tables(n_tiles):
    def parts(dim):
        inv_freq = ROPE_THETA ** (-np.arange(0, dim, 2, dtype=np.float64) / dim)
        base = (np.arange(n_tiles, dtype=np.float64) * TOK_TILE)[:, None] * inv_freq[None, :]
        off = np.arange(TOK_TILE, dtype=np.float64)[:, None] * inv_freq[None, :]
        return np.cos(base), np.sin(base), np.cos(off), np.sin(off)

    def tok(c, s):
        return (np.concatenate([c, c], axis=1), np.concatenate([s, s], axis=1),
                np.concatenate([-s, s], axis=1))

    cb_a, sb_a, co_a, so_a = parts(ATTN_HEAD_DIM)
    cb_r, sb_r, co_r, so_r = parts(RET_HEAD_DIM)
    tok_base = np.stack(tok(cb_r, sb_r) + (np.zeros((n_tiles, LANES)),) * 5, axis=1)
    tok_off = np.stack(tok(co_r, so_r), axis=0)
    feat_base = np.concatenate([cb_a, sb_a, cb_r, sb_r], axis=1)
    feat_base = np.broadcast_to(feat_base[:, :, None], feat_base.shape + (BLOCK,))
    feat_off = np.concatenate([co_a, so_a, co_r, so_r], axis=1).T
    return tuple(jnp.asarray(t, F32) for t in (tok_base, tok_off, feat_base, feat_off))


def _decay_kernel(logit_ref, dmat_ref, kdec_ref, qdec_ref, cdec_ref):
    x = logit_ref[...]
    lg = jnp.minimum(x, 0.0) - jnp.log1p(jnp.exp(-jnp.abs(x)))
    ri = lax.broadcasted_iota(jnp.int32, (BLOCK, LANES), 0).astype(F32)
    ci = lax.broadcasted_iota(jnp.int32, (BLOCK, LANES), 1).astype(F32)
    c1 = ci[0:1, :]
    diff = ci - ri
    cdec_ref[...] = jnp.exp(BLOCK * lg)
    for h in range(RET_HEADS):
        lf = lg[h:h + 1, :]
        lb = lg[RET_HEADS + h:RET_HEADS + h + 1, :]
        dmat_ref[h] = jnp.exp(jnp.abs(diff) * jnp.where(diff >= 0, lf, lb))
        sl = slice(h * LANES, (h + 1) * LANES)
        kdec_ref[0, :, sl] = jnp.exp((BLOCK - 1 - ri) * lf)
        kdec_ref[1, :, sl] = jnp.exp(ri * lb)
        qdec_ref[h:h + 1, :] = jnp.exp((c1 + 1) * lf)
        qdec_ref[RET_HEADS + h:RET_HEADS + h + 1, :] = jnp.exp((BLOCK - c1) * lb)


def _decay_tables(decay_fwd, decay_bwd):
    logits = jnp.concatenate([decay_fwd, decay_bwd]).astype(F32)
    logits = jnp.broadcast_to(logits[:, None], (2 * RET_HEADS, LANES))
    return pl.pallas_call(
        _decay_kernel,
        out_shape=(jax.ShapeDtypeStruct((RET_HEADS, BLOCK, LANES), F32),
                   jax.ShapeDtypeStruct((2, BLOCK, RET_WIDTH), F32),
                   jax.ShapeDtypeStruct((2 * RET_HEADS, LANES), F32),
                   jax.ShapeDtypeStruct((2 * RET_HEADS, LANES), F32)),
        name="decay_tables",
    )(logits)


def _inproj_kernel(x_ref, g_ref, wt_ref, wk_ref, tb_ref, to_ref, fb_ref, fo_ref,
                   zt_ref, zk_ref):
    h = _rms(x_ref[0], g_ref[...]).astype(BF16)

    tb = tb_ref[0]

    ck = tb[0:1] * to_ref[0] - tb[1:2] * to_ref[1]
    sk = tb[2:3] * to_ref[0] + tb[0:1] * to_ref[2]

    fb = jnp.concatenate([fb_ref[0]] * CHUNKS_PER_TILE, axis=1)
    fo = fo_ref[...]

    def feat_tables(row, n):
        cb, sb = fb[row:row + n], fb[row + n:row + 2 * n]
        co, so = fo[row:row + n], fo[row + n:row + 2 * n]
        return cb * co - sb * so, sb * co + cb * so

    zk = _dot(h, wk_ref[...])
    for g in range(RET_HEADS):
        zg = zk[:, g * LANES:(g + 1) * LANES]
        r = (zg * ck + pltpu.roll(zg, LANES // 2, 1) * sk) * (RET_HEAD_DIM ** -0.5)
        zk_ref[0, :, K_RK + g * LANES:K_RK + (g + 1) * LANES] = r.astype(BF16)

    def proj_t(row, n_rows):
        return lax.dot_general(wt_ref[row:row + n_rows, :], h, NT_DIMS,
                               preferred_element_type=F32)

    def store_t(row, val):
        for c in range(CHUNKS_PER_TILE):
            zt_ref[0, c, row:row + val.shape[0], :] = val[:, c * BLOCK:(c + 1) * BLOCK].astype(BF16)

    def rope_t(z, row, head_dim, cos, sin):
        hd2 = head_dim // 2
        for k in range(z.shape[0] // head_dim):
            x1 = z[k * head_dim:k * head_dim + hd2]
            x2 = z[k * head_dim + hd2:(k + 1) * head_dim]
            store_t(row + k * head_dim, x1 * cos - x2 * sin)
            store_t(row + k * head_dim + hd2, x2 * cos + x1 * sin)

    q_scale = (ATTN_HEAD_DIM ** -0.5) * LOG2E
    cat, sat = feat_tables(0, ATTN_HEAD_DIM // 2)
    rope_t(proj_t(T_AQ, ATTN_WIDTH), T_AQ, ATTN_HEAD_DIM, cat * q_scale, sat * q_scale)
    crt, srt = feat_tables(ATTN_HEAD_DIM, RET_HEAD_DIM // 2)
    rope_t(proj_t(T_RQ, RET_WIDTH), T_RQ, RET_HEAD_DIM, crt, srt)
    store_t(T_RV, proj_t(T_RV, RET_WIDTH))
    store_t(T_RG, proj_t(T_RG, RET_WIDTH))
    kv = proj_t(T_AV, 2 * ATTN_KV_WIDTH)
    store_t(T_AV, kv[0:ATTN_KV_WIDTH])
    hd2 = ATTN_HEAD_DIM // 2
    rows = []
    for k in range(ATTN_KV_HEADS):
        r0 = ATTN_KV_WIDTH + k * ATTN_HEAD_DIM
        x1, x2 = kv[r0:r0 + hd2], kv[r0 + hd2:r0 + ATTN_HEAD_DIM]
        rows += [x1 * cat - x2 * sat, x2 * cat + x1 * sat]
    zk_ref[0, :, K_AK:K_AK + ATTN_KV_WIDTH] = jnp.concatenate(rows, axis=0).T.astype(BF16)


def _in_projection(x, gain, w_t, w_k, rope):
    B, S, _ = x.shape
    n = S // BLOCK
    tok_base, tok_off, feat_base, feat_off = rope
    return pl.pallas_call(
        _inproj_kernel,
        grid=(S // TOK_TILE, B),
        in_specs=[pl.BlockSpec((1, TOK_TILE, D_MODEL), lambda i, b: (b, i, 0)),
                  _const_spec((1, D_MODEL)),
                  _const_spec((T_ROWS + ATTN_KV_WIDTH, D_MODEL)),
                  _const_spec((D_MODEL, RET_WIDTH)),
                  pl.BlockSpec((1,) + tok_base.shape[1:], lambda i, b: (i, 0, 0)),
                  _const_spec(tok_off.shape),
                  pl.BlockSpec((1,) + feat_base.shape[1:], lambda i, b: (i, 0, 0)),
                  _const_spec(feat_off.shape)],
        out_specs=(pl.BlockSpec((1, CHUNKS_PER_TILE, T_ROWS, BLOCK), lambda i, b: (b, i, 0, 0)),
                   pl.BlockSpec((1, TOK_TILE, K_COLS), lambda i, b: (b, i, 0))),
        out_shape=(jax.ShapeDtypeStruct((B, n, T_ROWS, BLOCK), BF16),
                   jax.ShapeDtypeStruct((B, S, K_COLS), BF16)),
        compiler_params=_params("parallel", "parallel"),
        name="in_projection",
    )(x, gain, w_t, w_k, tok_base, tok_off, feat_base, feat_off)


def _state_kernel(kf_ref, vf_ref, kb_ref, vb_ref, kdec_ref, cdec_ref,
                  pf_ref, pb_ref, st_ref):
    @pl.when(pl.program_id(1) == 0)
    def _():
        st_ref[...] = jnp.zeros_like(st_ref)

    for i in range(STATE_CHUNKS):
        for d, (k_ref, v_ref, p_ref, c) in enumerate(((kf_ref, vf_ref, pf_ref, i),
                                                      (kb_ref, vb_ref, pb_ref, STATE_CHUNKS - 1 - i))):
            kd = (k_ref[0, c * BLOCK:(c + 1) * BLOCK, :].astype(F32) * kdec_ref[d]).astype(BF16)
            for h in range(RET_HEADS):
                sl = slice(h * LANES, (h + 1) * LANES)
                st = st_ref[d, h]
                p_ref[0, c, h] = st.astype(BF16)
                kv = _dot(v_ref[0, c, sl, :], kd[:, sl])
                row = d * RET_HEADS + h
                st_ref[d, h] = st * cdec_ref[row:row + 1, :] + kv


def _retention_states(zt, zk, kdec, cdec):
    B, n = zt.shape[0], zt.shape[1]
    ng = n // STATE_CHUNKS
    k_blk = (1, STATE_CHUNKS * BLOCK, RET_WIDTH)
    v_blk = (1, STATE_CHUNKS, RET_WIDTH, BLOCK)
    vrow = T_RV // RET_WIDTH
    st_blk = (1, STATE_CHUNKS, RET_HEADS, RET_HEAD_DIM, RET_HEAD_DIM)
    st_shape = jax.ShapeDtypeStruct((B, n, RET_HEADS, RET_HEAD_DIM, RET_HEAD_DIM), BF16)
    return pl.pallas_call(
        _state_kernel,
        grid=(B, ng),
        in_specs=[pl.BlockSpec(k_blk, lambda b, t: (b, t, 0)),
                  pl.BlockSpec(v_blk, lambda b, t: (b, t, vrow, 0)),
                  pl.BlockSpec(k_blk, lambda b, t: (b, ng - 1 - t, 0)),
                  pl.BlockSpec(v_blk, lambda b, t: (b, ng - 1 - t, vrow, 0)),
                  _const_spec((2, BLOCK, RET_WIDTH)),
                  _const_spec((2 * RET_HEADS, LANES))],
        out_specs=(pl.BlockSpec(st_blk, lambda b, t: (b, t, 0, 0, 0)),
                   pl.BlockSpec(st_blk, lambda b, t: (b, ng - 1 - t, 0, 0, 0))),
        out_shape=(st_shape, st_shape),
        scratch_shapes=[pltpu.VMEM((2, RET_HEADS, RET_HEAD_DIM, RET_HEAD_DIM), F32)],
        compiler_params=_params("parallel", "arbitrary"),
        name="retention_states",
    )(zk, zt, zk, zt, kdec, cdec)


def _mixer_kernel(sink_ref, x_ref, zt_ref, avp_ref, avn_ref, zk_ref, akp_ref, akn_ref,
                  pf_ref, pb_ref, dmat_ref, qdec_ref,
                  an_ref, gn_ref, wout_ref, npost_ref, o_ref, kext_ref, vext_ref, mixt_ref,
                  sc_ref):
    tile = pl.program_id(1)
    n_blocks = pl.num_programs(1) * MIX_CHUNKS

    lane = lax.broadcasted_iota(jnp.int32, (1, LANES), 1)
    for g in range(ATTN_KV_HEADS):
        km = jnp.where((lane // ATTN_HEAD_DIM) == g, 1.0, 0.0).astype(BF16)
        kext_ref[g, 0:BLOCK] = akp_ref[0] * km
        kext_ref[g, BLOCK:BLOCK + MIX_TILE] = zk_ref[0, :, K_AK:K_AK + ATTN_KV_WIDTH] * km
        kext_ref[g, BLOCK + MIX_TILE:] = akn_ref[0] * km
    vext_ref[0] = avp_ref[0, 0]
    for c in range(MIX_CHUNKS):
        vext_ref[1 + c] = zt_ref[0, c, T_AV:T_AV + ATTN_KV_WIDTH, :]
    vext_ref[MIX_CHUNKS + 1] = avn_ref[0, 0]

    kk = lax.broadcasted_iota(jnp.int32, (BLOCK, LANES), 0)
    qq = lax.broadcasted_iota(jnp.int32, (BLOCK, LANES), 1)
    neg = jnp.full((BLOCK, LANES), -jnp.inf, F32)
    zero = jnp.zeros((BLOCK, LANES), F32)

    def scores_into(slot, c):
        q = jnp.concatenate([zt_ref[0, c, T_AQ + j * LANES:T_AQ + (j + 1) * LANES, :]
                             for j in range(ATTN_GROUP)], axis=1)
        r0 = c * BLOCK
        for g in range(ATTN_KV_HEADS):
            sc_ref[slot, g] = _dot(kext_ref[g, pl.ds(r0, 3 * BLOCK), :], q)

    def chunk(c, has_next):
        r0 = c * BLOCK
        blk = tile * MIX_CHUNKS + c
        slot = c % 2
        if has_next:
            scores_into(1 - slot, c + 1)

        vw = jnp.concatenate([vext_ref[c], vext_ref[c + 1], vext_ref[c + 2]], axis=1)
        no_prev = jnp.where(blk > 0, 0, BLOCK)
        no_next = jnp.where(blk < n_blocks - 1, 0, BLOCK)
        bias_prev = jnp.where(kk >= qq + no_prev, zero, neg)
        bias_next = jnp.where(kk <= qq - no_next, zero, neg)
        bias_prev = jnp.concatenate([bias_prev] * ATTN_GROUP, axis=1)
        bias_next = jnp.concatenate([bias_next] * ATTN_GROUP, axis=1)
        heads = []
        for h in range(RET_HEADS):
            qt = zt_ref[0, c, T_RQ + h * LANES:T_RQ + (h + 1) * LANES, :]
            kh = zk_ref[0, pl.ds(r0, BLOCK), K_RK + h * LANES:K_RK + (h + 1) * LANES]
            heads.append((qt, _dot(kh, qt)))

        ones = jnp.ones((BF16_ROWS, 3 * BLOCK), BF16)
        outs = []
        for g in range(ATTN_KV_HEADS):
            s = sc_ref[slot, g]
            s = jnp.concatenate([s[0:BLOCK] + bias_prev, s[BLOCK:2 * BLOCK],
                                 s[2 * BLOCK:] + bias_next], axis=0)
            sk = jnp.concatenate(
                [jnp.full((1, LANES), sink_ref[g * ATTN_GROUP + j] * LOG2E, F32)
                 for j in range(ATTN_GROUP)], axis=1)
            m = jnp.maximum(jnp.max(s, axis=0, keepdims=True), sk)
            p = jnp.exp2(s - m).astype(BF16)
            vg = jnp.concatenate([vw[g * ATTN_HEAD_DIM:(g + 1) * ATTN_HEAD_DIM, :], ones], axis=0)
            ov = _dot(vg, p)
            den = ov[ATTN_HEAD_DIM:ATTN_HEAD_DIM + 1, :] + jnp.exp2(sk - m)
            outs.append(ov[0:ATTN_HEAD_DIM, :] * (1.0 / den))
        ys = []
        for h in range(RET_HEADS):
            qt, pt = heads[h]
            y = _dot(zt_ref[0, c, T_RV + h * LANES:T_RV + (h + 1) * LANES, :],
                     (pt * dmat_ref[h]).astype(BF16))
            y = y + _dot(pf_ref[0, c, h], qt) * qdec_ref[h:h + 1, :]
            y = y + _dot(pb_ref[0, c, h], qt) * qdec_ref[RET_HEADS + h:RET_HEADS + h + 1, :]
            ys.append(y)

        a = jnp.concatenate([outs[g][:, j * LANES:(j + 1) * LANES]
                             for j in range(ATTN_GROUP) for g in range(ATTN_KV_HEADS)], axis=0)
        ms = jnp.sum(a * a, axis=0, keepdims=True) * (1.0 / ATTN_WIDTH)
        mixt_ref[c, 0:ATTN_WIDTH, :] = (a * lax.rsqrt(ms + EPS) * an_ref[...]).astype(BF16)
        for h in range(RET_HEADS):
            sl = slice(h * LANES, (h + 1) * LANES)
            y = ys[h]
            mu = jnp.mean(y, axis=0, keepdims=True)
            yc = y - mu
            var = jnp.mean(yc * yc, axis=0, keepdims=True)
            yn = yc * lax.rsqrt(var + EPS) * gn_ref[sl, :]
            gate = zt_ref[0, c, T_RG + h * LANES:T_RG + (h + 1) * LANES, :].astype(F32)
            r = gate * jax.nn.sigmoid(gate) * yn
            mixt_ref[c, ATTN_WIDTH + h * LANES:ATTN_WIDTH + (h + 1) * LANES, :] = r.astype(BF16)

    scores_into(0, 0)
    for c in range(MIX_CHUNKS):
        chunk(c, c + 1 < MIX_CHUNKS)

    mixt = jnp.concatenate([mixt_ref[c] for c in range(MIX_CHUNKS)], axis=1)
    mix = lax.dot_general(mixt, wout_ref[...], TN_DIMS, preferred_element_type=F32)
    o_ref[0] = x_ref[0] + _rms(mix, npost_ref[...])


def _mixer(x, zt, zk, states_f, states_b, dmat, qdec, sink, attn_norm, ret_gn, w_out, norm_post):
    B, S, _ = x.shape
    n = S // BLOCK
    cpt = MIX_CHUNKS
    edge_t = (1, 1, ATTN_KV_WIDTH, BLOCK)
    edge_k = (1, BLOCK, ATTN_KV_WIDTH)
    v_row = T_AV // ATTN_KV_WIDTH
    k_col = K_AK // ATTN_KV_WIDTH
    st_blk = (1, cpt, RET_HEADS, RET_HEAD_DIM, RET_HEAD_DIM)

    def prev(i):
        return jnp.maximum(i * cpt - 1, 0)

    def nxt(i):
        return jnp.minimum((i + 1) * cpt, n - 1)

    return pl.pallas_call(
        _mixer_kernel,
        grid=(B, S // MIX_TILE),
        in_specs=[pl.BlockSpec(memory_space=pltpu.SMEM),
                  pl.BlockSpec((1, MIX_TILE, D_MODEL), lambda b, i: (b, i, 0)),
                  pl.BlockSpec((1, cpt, T_ROWS, BLOCK), lambda b, i: (b, i, 0, 0)),
                  pl.BlockSpec(edge_t, lambda b, i: (b, prev(i), v_row, 0)),
                  pl.BlockSpec(edge_t, lambda b, i: (b, nxt(i), v_row, 0)),
                  pl.BlockSpec((1, MIX_TILE, K_COLS), lambda b, i: (b, i, 0)),
                  pl.BlockSpec(edge_k, lambda b, i: (b, prev(i), k_col)),
                  pl.BlockSpec(edge_k, lambda b, i: (b, nxt(i), k_col)),
                  pl.BlockSpec(st_blk, lambda b, i: (b, i, 0, 0, 0)),
                  pl.BlockSpec(st_blk, lambda b, i: (b, i, 0, 0, 0)),
                  _const_spec((RET_HEADS, BLOCK, LANES)),
                  _const_spec((2 * RET_HEADS, LANES)),
                  _const_spec((ATTN_WIDTH, LANES)),
                  _const_spec((RET_WIDTH, LANES)),
                  _const_spec((D_MODEL, D_MODEL)),
                  _const_spec((1, D_MODEL))],
        out_specs=pl.BlockSpec((1, MIX_TILE, D_MODEL), lambda b, i: (b, i, 0)),
        out_shape=jax.ShapeDtypeStruct((B, S, D_MODEL), F32),
        scratch_shapes=[pltpu.VMEM((ATTN_KV_HEADS, MIX_TILE + 2 * BLOCK, ATTN_KV_WIDTH), BF16),
                        pltpu.VMEM((cpt + 2, ATTN_KV_WIDTH, BLOCK), BF16),
                        pltpu.VMEM((cpt, D_MODEL, BLOCK), BF16),
                        pltpu.VMEM((2, ATTN_KV_HEADS, 3 * BLOCK, ATTN_GROUP * LANES), F32)],
        compiler_params=_params("parallel", "parallel"),
        name="token_mixer",
    )(sink, x, zt, zt, zt, zk, zk, zk, states_f, states_b, dmat, qdec,
      attn_norm, ret_gn, w_out, norm_post)


FF_CHUNK = 2 * MXU_TILE
FF_CHUNKS = [(c0, min(FF_CHUNK, D_FF - c0)) for c0 in range(0, D_FF, FF_CHUNK)]
assert all(w % MXU_TILE == 0 for _, w in FF_CHUNKS)
assert XA_HEADS <= 4 and N_MEM <= FF_CHUNK


def _xattn_ffn_kernel(x_ref, mem_ref, gmem_ref, wkt_ref, wv_ref, gxa_ref, wq_ref, wo_ref, gxp_ref,
                      gfp_ref, wgu_ref, wd_ref, gfo_ref, o_ref,
                      kmt_ref, vm_ref, cat_ref, q_ref, h_ref, res_ref, x2_ref, gu_ref, act_ref):
    pieces = [slice(r * ROW_PIECE, (r + 1) * ROW_PIECE) for r in range(TOK_TILE // ROW_PIECE)]

    @pl.when(pl.program_id(1) == 0)
    def _():
        mn = _rms(mem_ref[0], gmem_ref[...]).astype(BF16)
        kmt_ref[...] = lax.dot_general(wkt_ref[...], mn, NT_DIMS,
                                       preferred_element_type=F32).astype(BF16)
        vm_ref[...] = _dot(mn, wv_ref[...]).astype(BF16)

    for rows in pieces:
        h_ref[rows, :] = _rms(x_ref[0, rows, :], gxa_ref[...]).astype(BF16)
    for rows in pieces:
        q_ref[rows, :] = _dot(h_ref[rows, :], wq_ref[...]).astype(BF16)
    for hd in range(XA_HEADS):
        sl = slice(hd * XA_HEAD_DIM, (hd + 1) * XA_HEAD_DIM)
        gu_ref[hd // 2, hd % 2, :, 0:N_MEM] = _dot(q_ref[:, sl], kmt_ref[sl, :])
    for hd in range(XA_HEADS):
        sl = slice(hd * XA_HEAD_DIM, (hd + 1) * XA_HEAD_DIM)
        s = gu_ref[hd // 2, hd % 2, :, 0:N_MEM]
        p = jnp.exp(s - jnp.max(s, axis=-1, keepdims=True))
        den = jnp.sum(p, axis=-1, keepdims=True)
        cat_ref[:, sl] = (_dot(p.astype(BF16), vm_ref[:, sl]) * (1.0 / den)).astype(BF16)
    for rows in pieces:
        res_ref[rows, :] = _dot(cat_ref[rows, :], wo_ref[...])

    def gate_up_into(slot, j, rows=slice(None)):
        c0, w = FF_CHUNKS[j]
        gu_ref[slot, 0, rows, 0:w] = _dot(h_ref[rows, :], wgu_ref[:, c0:c0 + w])
        gu_ref[slot, 1, rows, 0:w] = _dot(h_ref[rows, :], wgu_ref[:, D_FF + c0:D_FF + c0 + w])

    for rows in pieces:
        x2 = x_ref[0, rows, :] + _rms(res_ref[rows, :], gxp_ref[...])
        x2_ref[rows, :] = x2
        h_ref[rows, :] = _rms(x2, gfp_ref[...]).astype(BF16)
    for rows in pieces:
        gate_up_into(0, 0, rows)

    for j, (c0, w) in enumerate(FF_CHUNKS):
        if j + 1 < len(FF_CHUNKS):
            gate_up_into((j + 1) % 2, j + 1)
        gate = gu_ref[j % 2, 0, :, 0:w]
        up = gu_ref[j % 2, 1, :, 0:w]
        act_ref[:, c0:c0 + w] = (gate * jax.nn.sigmoid(gate) * up).astype(BF16)
    for rows in pieces:
        res_ref[rows, :] = _dot(act_ref[rows, :], wd_ref[...])
    for rows in pieces:
        o_ref[0, rows, :] = x2_ref[rows, :] + _rms(res_ref[rows, :], gfo_ref[...])


def _cross_attention_ffn(x, mem, gain_mem, w_kt, w_v, gain_xa_pre, w_q, w_o, gain_xa_post,
                         gain_ffn_pre, w_gu, w_down, gain_ffn_post):
    B, S, _ = x.shape
    M = mem.shape[1]
    tok = pl.BlockSpec((1, TOK_TILE, D_MODEL), lambda b, i: (b, i, 0))
    gain = _const_spec((1, D_MODEL))
    return pl.pallas_call(
        _xattn_ffn_kernel,
        grid=(B, S // TOK_TILE),
        in_specs=[tok,
                  pl.BlockSpec((1, M, D_MODEL), lambda b, i: (b, 0, 0)),
                  gain,
                  _const_spec((D_MODEL, D_MODEL)),
                  _const_spec((D_MODEL, D_MODEL)),
                  gain,
                  _const_spec((D_MODEL, D_MODEL)),
                  _const_spec((D_MODEL, D_MODEL)),
                  gain, gain,
                  _const_spec((D_MODEL, 2 * D_FF)),
                  _const_spec((D_FF, D_MODEL)),
                  gain],
        out_specs=tok,
        out_shape=jax.ShapeDtypeStruct((B, S, D_MODEL), F32),
        scratch_shapes=[pltpu.VMEM((D_MODEL, M), BF16),
                        pltpu.VMEM((M, D_MODEL), BF16),
                        pltpu.VMEM((TOK_TILE, D_MODEL), BF16),
                        pltpu.VMEM((TOK_TILE, D_MODEL), BF16),
                        pltpu.VMEM((TOK_TILE, D_MODEL), BF16),
                        pltpu.VMEM((TOK_TILE, D_MODEL), F32),
                        pltpu.VMEM((TOK_TILE, D_MODEL), F32),
                        pltpu.VMEM((2, 2, TOK_TILE, FF_CHUNK), F32),
                        pltpu.VMEM((TOK_TILE, D_FF), BF16)],
        compiler_params=_params("parallel", "arbitrary"),
        name="cross_attention_ffn",
    )(x, mem, gain_mem, w_kt, w_v, gain_xa_pre, w_q, w_o, gain_xa_post,
      gain_ffn_pre, w_gu, w_down, gain_ffn_post)


def _pair_heads(a, axis):
    shape = a.shape
    a = a.reshape(shape[:axis] + (ATTN_KV_HEADS, ATTN_GROUP, ATTN_HEAD_DIM) + shape[axis + 1:])
    return jnp.swapaxes(a, axis, axis + 1).reshape(shape)


def kernel(x, mem, norm_mix_pre, norm_mix_post, w_in, attn_sink, attn_out_norm,
           ret_decay_fwd, ret_decay_bwd, ret_gn, w_out, norm_xa_pre, norm_xa_post,
           norm_mem, xa_wq, xa_wkv, xa_wo, norm_ffn_pre, norm_ffn_post,
           ffn_w_gu, ffn_w_down):
    B, S, D = x.shape
    assert D == D_MODEL and S % max(TOK_TILE, MIX_TILE, STATE_CHUNKS * BLOCK) == 0
    assert mem.shape[1] == N_MEM
    depth = w_in.shape[0]
    o_k = ATTN_WIDTH
    o_v = o_k + ATTN_KV_WIDTH
    o_r = o_v + ATTN_KV_WIDTH
    rw = RET_WIDTH

    rope = _rope_tables(S // TOK_TILE)

    def row(v):
        return v.reshape(1, -1).astype(F32)

    def col(v):
        return jnp.broadcast_to(v.astype(F32)[:, None], (v.shape[0], LANES))

    for l in range(depth):
        w = w_in[l]
        w_t = jnp.concatenate([_pair_heads(w[:, :o_k], 1), w[:, o_r:o_r + rw], w[:, o_r + 2 * rw:],
                               w[:, o_v:o_r], w[:, o_k:o_v]], axis=1).T.astype(BF16)
        w_k = w[:, o_r + rw:o_r + 2 * rw].astype(BF16)
        w_o = jnp.concatenate([_pair_heads(w_out[l][:ATTN_WIDTH], 0), w_out[l][ATTN_WIDTH:]], axis=0)

        dmat, kdec, qdec, cdec = _decay_tables(ret_decay_fwd[l], ret_decay_bwd[l])
        zt, zk = _in_projection(x, row(norm_mix_pre[l]), w_t, w_k, rope)
        states_f, states_b = _retention_states(zt, zk, kdec, cdec)
        x = _mixer(x, zt, zk, states_f, states_b, dmat, qdec, attn_sink[l].astype(F32),
                   col(_pair_heads(attn_out_norm[l], 0)), col(ret_gn[l]),
                   w_o.astype(BF16), row(norm_mix_post[l]))
        x = _cross_attention_ffn(x, mem, row(norm_mem[l]), xa_wkv[l][:, :D_MODEL].T.astype(BF16),
                                 xa_wkv[l][:, D_MODEL:].astype(BF16), row(norm_xa_pre[l]),
                                 (xa_wq[l] * (XA_HEAD_DIM ** -0.5)).astype(BF16),
                                 xa_wo[l].astype(BF16), row(norm_xa_post[l]),
                                 row(norm_ffn_pre[l]), ffn_w_gu[l].astype(BF16),
                                 ffn_w_down[l].astype(BF16), row(norm_ffn_post[l]))
    return x
```

```python
import math

import numpy as np
import jax
import jax.numpy as jnp
from jax import lax
from jax.experimental import pallas as pl
from jax.experimental.pallas import tpu as pltpu

D_MODEL = 1024
N_MEM = 256
ATTN_HEADS = 8
ATTN_KV_HEADS = 2
ATTN_GROUP = ATTN_HEADS // ATTN_KV_HEADS
ATTN_HEAD_DIM = 64
ATTN_WIDTH = ATTN_HEADS * ATTN_HEAD_DIM
ATTN_KV_WIDTH = ATTN_KV_HEADS * ATTN_HEAD_DIM
BLOCK = 128
RET_HEADS = 4
RET_HEAD_DIM = 128
RET_WIDTH = RET_HEADS * RET_HEAD_DIM
XA_HEADS = 4
XA_HEAD_DIM = D_MODEL // XA_HEADS
assert math.log2(XA_HEAD_DIM) % 2 == 0
D_FF = -(-(8 * D_MODEL) // (3 * 256)) * 256
ROPE_THETA = 10000.0
EPS = 1e-6
LOG2E = math.log2(math.e)

LANES = 128
BF16_ROWS = 16
MXU_TILE = 256
VMEM_LIMIT_BYTES = 56 * 1024 * 1024

TOK_TILE = 512
CHUNKS_PER_TILE = TOK_TILE // BLOCK
STATE_CHUNKS = 16
MIX_TILE = 1024
MIX_CHUNKS = MIX_TILE // BLOCK
ROW_PIECE = 256

T_AQ, T_RQ, T_RV, T_RG, T_AV = 0, 512, 1024, 1536, 2048
T_ROWS = T_AV + ATTN_KV_WIDTH
K_RK, K_AK = 0, RET_WIDTH
K_COLS = RET_WIDTH + ATTN_KV_WIDTH

F32 = jnp.float32
BF16 = jnp.bfloat16
NT_DIMS = (((1,), (1,)), ((), ()))
TN_DIMS = (((0,), (0,)), ((), ()))


def _const_spec(shape):
    nd = len(shape)
    return pl.BlockSpec(shape, lambda *_: (0,) * nd, pipeline_mode=pl.Buffered(1))


def _params(*sem):
    return pltpu.CompilerParams(dimension_semantics=sem,
                                vmem_limit_bytes=VMEM_LIMIT_BYTES)


def _rms(x, w):
    ms = jnp.mean(x * x, axis=-1, keepdims=True)
    return x * lax.rsqrt(ms + EPS) * w


def _dot(a, b):
    return jnp.dot(a, b, preferred_element_type=F32)


def _rope_tables(n_tiles):
    def parts(dim):
        inv_freq = ROPE_THETA ** (-np.arange(0, dim, 2, dtype=np.float64) / dim)
        base = (np.arange(n_tiles, dtype=np.float64) * TOK_TILE)[:, None] * inv_freq[None, :]
        off = np.arange(TOK_TILE, dtype=np.float64)[:, None] * inv_freq[None, :]
        return np.cos(base), np.sin(base), np.cos(off), np.sin(off)

    def tok(c, s):
        return (np.concatenate([c, c], axis=1), np.concatenate([s, s], axis=1),
                np.concatenate([-s, s], axis=1))

    cb_a, sb_a, co_a, so_a = parts(ATTN_HEAD_DIM)
    cb_r, sb_r, co_r, so_r = parts(RET_HEAD_DIM)
    tok_base = np.stack(tok(cb_r, sb_r) + (np.zeros((n_tiles, LANES)),) * 5, axis=1)
    tok_off = np.stack(tok(co_r, so_r), axis=0)
    feat_base = np.concatenate([cb_a, sb_a, cb_r, sb_r], axis=1)
    feat_base = np.broadcast_to(feat_base[:, :, None], feat_base.shape + (BLOCK,))
    feat_off = np.concatenate([co_a, so_a, co_r, so_r], axis=1).T
    return tuple(jnp.asarray(t, F32) for t in (tok_base, tok_off, feat_base, feat_off))


def _decay_kernel(logit_ref, dmat_ref, kdec_ref, qdec_ref, cdec_ref):
    x = logit_ref[...]
    lg = jnp.minimum(x, 0.0) - jnp.log1p(jnp.exp(-jnp.abs(x)))
    ri = lax.broadcasted_iota(jnp.int32, (BLOCK, LANES), 0).astype(F32)
    ci = lax.broadcasted_iota(jnp.int32, (BLOCK, LANES), 1).astype(F32)
    c1 = ci[0:1, :]
    diff = ci - ri
    cdec_ref[...] = jnp.exp(BLOCK * lg)
    for h in range(RET_HEADS):
        lf = lg[h:h + 1, :]
        lb = lg[RET_HEADS + h:RET_HEADS + h + 1, :]
        dmat_ref[h] = jnp.exp(jnp.abs(diff) * jnp.where(diff >= 0, lf, lb))
        sl = slice(h * LANES, (h + 1) * LANES)
        kdec_ref[0, :, sl] = jnp.exp((BLOCK - 1 - ri) * lf)
        kdec_ref[1, :, sl] = jnp.exp(ri * lb)
        qdec_ref[h:h + 1, :] = jnp.exp((c1 + 1) * lf)
        qdec_ref[RET_HEADS + h:RET_HEADS + h + 1, :] = jnp.exp((BLOCK - c1) * lb)


def _decay_tables(decay_fwd, decay_bwd):
    logits = jnp.concatenate([decay_fwd, decay_bwd]).astype(F32)
    logits = jnp.broadcast_to(logits[:, None], (2 * RET_HEADS, LANES))
    return pl.pallas_call(
        _decay_kernel,
        out_shape=(jax.ShapeDtypeStruct((RET_HEADS, BLOCK, LANES), F32),
                   jax.ShapeDtypeStruct((2, BLOCK, RET_WIDTH), F32),
                   jax.ShapeDtypeStruct((2 * RET_HEADS, LANES), F32),
                   jax.ShapeDtypeStruct((2 * RET_HEADS, LANES), F32)),
        name="decay_tables",
    )(logits)


def _inproj_kernel(x_ref, g_ref, wt_ref, wk_ref, tb_ref, to_ref, fb_ref, fo_ref,
                   zt_ref, zk_ref):
    h = _rms(x_ref[0], g_ref[...]).astype(BF16)

    tb = tb_ref[0]

    ck = tb[0:1] * to_ref[0] - tb[1:2] * to_ref[1]
    sk = tb[2:3] * to_ref[0] + tb[0:1] * to_ref[2]

    fb = jnp.concatenate([fb_ref[0]] * CHUNKS_PER_TILE, axis=1)
    fo = fo_ref[...]

    def feat_tables(row, n):
        cb, sb = fb[row:row + n], fb[row + n:row + 2 * n]
        co, so = fo[row:row + n], fo[row + n:row + 2 * n]
        return cb * co - sb * so, sb * co + cb * so

    zk = _dot(h, wk_ref[...])
    for g in range(RET_HEADS):
        zg = zk[:, g * LANES:(g + 1) * LANES]
        r = (zg * ck + pltpu.roll(zg, LANES // 2, 1) * sk) * (RET_HEAD_DIM ** -0.5)
        zk_ref[0, :, K_RK + g * LANES:K_RK + (g + 1) * LANES] = r.astype(BF16)

    def proj_t(row, n_rows):
        return lax.dot_general(wt_ref[row:row + n_rows, :], h, NT_DIMS,
                               preferred_element_type=F32)

    def store_t(row, val):
        for c in range(CHUNKS_PER_TILE):
            zt_ref[0, c, row:row + val.shape[0], :] = val[:, c * BLOCK:(c + 1) * BLOCK].astype(BF16)

    def rope_t(z, row, head_dim, cos, sin):
        hd2 = head_dim // 2
        for k in range(z.shape[0] // head_dim):
            x1 = z[k * head_dim:k * head_dim + hd2]
            x2 = z[k * head_dim + hd2:(k + 1) * head_dim]
            store_t(row + k * head_dim, x1 * cos - x2 * sin)
            store_t(row + k * head_dim + hd2, x2 * cos + x1 * sin)

    q_scale = (ATTN_HEAD_DIM ** -0.5) * LOG2E
    cat, sat = feat_tables(0, ATTN_HEAD_DIM // 2)
    rope_t(proj_t(T_AQ, ATTN_WIDTH), T_AQ, ATTN_HEAD_DIM, cat * q_scale, sat * q_scale)
    crt, srt = feat_tables(ATTN_HEAD_DIM, RET_HEAD_DIM // 2)
    rope_t(proj_t(T_RQ, RET_WIDTH), T_RQ, RET_HEAD_DIM, crt, srt)
    store_t(T_RV, proj_t(T_RV, RET_WIDTH))
    store_t(T_RG, proj_t(T_RG, RET_WIDTH))
    kv = proj_t(T_AV, 2 * ATTN_KV_WIDTH)
    store_t(T_AV, kv[0:ATTN_KV_WIDTH])
    hd2 = ATTN_HEAD_DIM // 2
    rows = []
    for k in range(ATTN_KV_HEADS):
        r0 = ATTN_KV_WIDTH + k * ATTN_HEAD_DIM
        x1, x2 = kv[r0:r0 + hd2], kv[r0 + hd2:r0 + ATTN_HEAD_DIM]
        rows += [x1 * cat - x2 * sat, x2 * cat + x1 * sat]
    zk_ref[0, :, K_AK:K_AK + ATTN_KV_WIDTH] = jnp.concatenate(rows, axis=0).T.astype(BF16)


def _in_projection(x, gain, w_t, w_k, rope):
    B, S, _ = x.shape
    n = S // BLOCK
    tok_base, tok_off, feat_base, feat_off = rope
    return pl.pallas_call(
        _inproj_kernel,
        grid=(S // TOK_TILE, B),
        in_specs=[pl.BlockSpec((1, TOK_TILE, D_MODEL), lambda i, b: (b, i, 0)),
                  _const_spec((1, D_MODEL)),
                  _const_spec((T_ROWS + ATTN_KV_WIDTH, D_MODEL)),
                  _const_spec((D_MODEL, RET_WIDTH)),
                  pl.BlockSpec((1,) + tok_base.shape[1:], lambda i, b: (i, 0, 0)),
                  _const_spec(tok_off.shape),
                  pl.BlockSpec((1,) + feat_base.shape[1:], lambda i, b: (i, 0, 0)),
                  _const_spec(feat_off.shape)],
        out_specs=(pl.BlockSpec((1, CHUNKS_PER_TILE, T_ROWS, BLOCK), lambda i, b: (b, i, 0, 0)),
                   pl.BlockSpec((1, TOK_TILE, K_COLS), lambda i, b: (b, i, 0))),
        out_shape=(jax.ShapeDtypeStruct((B, n, T_ROWS, BLOCK), BF16),
                   jax.ShapeDtypeStruct((B, S, K_COLS), BF16)),
        compiler_params=_params("parallel", "parallel"),
        name="in_projection",
    )(x, gain, w_t, w_k, tok_base, tok_off, feat_base, feat_off)


def _state_kernel(kf_ref, vf_ref, kb_ref, vb_ref, kdec_ref, cdec_ref,
                  pf_ref, pb_ref, st_ref):
    @pl.when(pl.program_id(1) == 0)
    def _():
        st_ref[...] = jnp.zeros_like(st_ref)

    for i in range(STATE_CHUNKS):
        for d, (k_ref, v_ref, p_ref, c) in enumerate(((kf_ref, vf_ref, pf_ref, i),
                                                      (kb_ref, vb_ref, pb_ref, STATE_CHUNKS - 1 - i))):
            kd = (k_ref[0, c * BLOCK:(c + 1) * BLOCK, :].astype(F32) * kdec_ref[d]).astype(BF16)
            for h in range(RET_HEADS):
                sl = slice(h * LANES, (h + 1) * LANES)
                st = st_ref[d, h]
                p_ref[0, c, h] = st.astype(BF16)
                kv = _dot(v_ref[0, c, sl, :], kd[:, sl])
                row = d * RET_HEADS + h
                st_ref[d, h] = st * cdec_ref[row:row + 1, :] + kv


def _retention_states(zt, zk, kdec, cdec):
    B, n = zt.shape[0], zt.shape[1]
    ng = n // STATE_CHUNKS
    k_blk = (1, STATE_CHUNKS * BLOCK, RET_WIDTH)
    v_blk = (1, STATE_CHUNKS, RET_WIDTH, BLOCK)
    vrow = T_RV // RET_WIDTH
    st_blk = (1, STATE_CHUNKS, RET_HEADS, RET_HEAD_DIM, RET_HEAD_DIM)
    st_shape = jax.ShapeDtypeStruct((B, n, RET_HEADS, RET_HEAD_DIM, RET_HEAD_DIM), BF16)
    return pl.pallas_call(
        _state_kernel,
        grid=(B, ng),
        in_specs=[pl.BlockSpec(k_blk, lambda b, t: (b, t, 0)),
                  pl.BlockSpec(v_blk, lambda b, t: (b, t, vrow, 0)),
                  pl.BlockSpec(k_blk, lambda b, t: (b, ng - 1 - t, 0)),
                  pl.BlockSpec(v_blk, lambda b, t: (b, ng - 1 - t, vrow, 0)),
                  _const_spec((2, BLOCK, RET_WIDTH)),
                  _const_spec((2 * RET_HEADS, LANES))],
        out_specs=(pl.BlockSpec(st_blk, lambda b, t: (b, t, 0, 0, 0)),
                   pl.BlockSpec(st_blk, lambda b, t: (b, ng - 1 - t, 0, 0, 0))),
        out_shape=(st_shape, st_shape),
        scratch_shapes=[pltpu.VMEM((2, RET_HEADS, RET_HEAD_DIM, RET_HEAD_DIM), F32)],
        compiler_params=_params("parallel", "arbitrary"),
        name="retention_states",
    )(zk, zt, zk, zt, kdec, cdec)


def _mixer_kernel(sink_ref, x_ref, zt_ref, avp_ref, avn_ref, zk_ref, akp_ref, akn_ref,
                  pf_ref, pb_ref, dmat_ref, qdec_ref,
                  an_ref, gn_ref, wout_ref, npost_ref, o_ref, kext_ref, vext_ref, mixt_ref,
                  sc_ref):
    tile = pl.program_id(1)
    n_blocks = pl.num_programs(1) * MIX_CHUNKS

    lane = lax.broadcasted_iota(jnp.int32, (1, LANES), 1)
    for g in range(ATTN_KV_HEADS):
        km = jnp.where((lane // ATTN_HEAD_DIM) == g, 1.0, 0.0).astype(BF16)
        kext_ref[g, 0:BLOCK] = akp_ref[0] * km
        kext_ref[g, BLOCK:BLOCK + MIX_TILE] = zk_ref[0, :, K_AK:K_AK + ATTN_KV_WIDTH] * km
        kext_ref[g, BLOCK + MIX_TILE:] = akn_ref[0] * km
    vext_ref[0] = avp_ref[0, 0]
    for c in range(MIX_CHUNKS):
        vext_ref[1 + c] = zt_ref[0, c, T_AV:T_AV + ATTN_KV_WIDTH, :]
    vext_ref[MIX_CHUNKS + 1] = avn_ref[0, 0]

    kk = lax.broadcasted_iota(jnp.int32, (BLOCK, LANES), 0)
    qq = lax.broadcasted_iota(jnp.int32, (BLOCK, LANES), 1)
    neg = jnp.full((BLOCK, LANES), -jnp.inf, F32)
    zero = jnp.zeros((BLOCK, LANES), F32)

    def scores_into(slot, c):
        q = jnp.concatenate([zt_ref[0, c, T_AQ + j * LANES:T_AQ + (j + 1) * LANES, :]
                             for j in range(ATTN_GROUP)], axis=1)
        r0 = c * BLOCK
        for g in range(ATTN_KV_HEADS):
            sc_ref[slot, g] = _dot(kext_ref[g, pl.ds(r0, 3 * BLOCK), :], q)

    def chunk(c, has_next):
        r0 = c * BLOCK
        blk = tile * MIX_CHUNKS + c
        slot = c % 2
        if has_next:
            scores_into(1 - slot, c + 1)

        vw = jnp.concatenate([vext_ref[c], vext_ref[c + 1], vext_ref[c + 2]], axis=1)
        no_prev = jnp.where(blk > 0, 0, BLOCK)
        no_next = jnp.where(blk < n_blocks - 1, 0, BLOCK)
        bias_prev = jnp.where(kk >= qq + no_prev, zero, neg)
        bias_next = jnp.where(kk <= qq - no_next, zero, neg)
        bias_prev = jnp.concatenate([bias_prev] * ATTN_GROUP, axis=1)
        bias_next = jnp.concatenate([bias_next] * ATTN_GROUP, axis=1)
        heads = []
        for h in range(RET_HEADS):
            qt = zt_ref[0, c, T_RQ + h * LANES:T_RQ + (h + 1) * LANES, :]
            kh = zk_ref[0, pl.ds(r0, BLOCK), K_RK + h * LANES:K_RK + (h + 1) * LANES]
            heads.append((qt, _dot(kh, qt)))

        ones = jnp.ones((BF16_ROWS, 3 * BLOCK), BF16)
        outs = []
        for g in range(ATTN_KV_HEADS):
            s = sc_ref[slot, g]
            s = jnp.concatenate([s[0:BLOCK] + bias_prev, s[BLOCK:2 * BLOCK],
                                 s[2 * BLOCK:] + bias_next], axis=0)
            sk = jnp.concatenate(
                [jnp.full((1, LANES), sink_ref[g * ATTN_GROUP + j] * LOG2E, F32)
                 for j in range(ATTN_GROUP)], axis=1)
            m = jnp.maximum(jnp.max(s, axis=0, keepdims=True), sk)
            p = jnp.exp2(s - m).astype(BF16)
            vg = jnp.concatenate([vw[g * ATTN_HEAD_DIM:(g + 1) * ATTN_HEAD_DIM, :], ones], axis=0)
            ov = _dot(vg, p)
            den = ov[ATTN_HEAD_DIM:ATTN_HEAD_DIM + 1, :] + jnp.exp2(sk - m)
            outs.append(ov[0:ATTN_HEAD_DIM, :] * (1.0 / den))
        ys = []
        for h in range(RET_HEADS):
            qt, pt = heads[h]
            y = _dot(zt_ref[0, c, T_RV + h * LANES:T_RV + (h + 1) * LANES, :],
                     (pt * dmat_ref[h]).astype(BF16))
            y = y + _dot(pf_ref[0, c, h], qt) * qdec_ref[h:h + 1, :]
            y = y + _dot(pb_ref[0, c, h], qt) * qdec_ref[RET_HEADS + h:RET_HEADS + h + 1, :]
            ys.append(y)

        a = jnp.concatenate([outs[g][:, j * LANES:(j + 1) * LANES]
                             for j in range(ATTN_GROUP) for g in range(ATTN_KV_HEADS)], axis=0)
        ms = jnp.sum(a * a, axis=0, keepdims=True) * (1.0 / ATTN_WIDTH)
        mixt_ref[c, 0:ATTN_WIDTH, :] = (a * lax.rsqrt(ms + EPS) * an_ref[...]).astype(BF16)
        for h in range(RET_HEADS):
            sl = slice(h * LANES, (h + 1) * LANES)
            y = ys[h]
            mu = jnp.mean(y, axis=0, keepdims=True)
            yc = y - mu
            var = jnp.mean(yc * yc, axis=0, keepdims=True)
            yn = yc * lax.rsqrt(var + EPS) * gn_ref[sl, :]
            gate = zt_ref[0, c, T_RG + h * LANES:T_RG + (h + 1) * LANES, :].astype(F32)
            r = gate * jax.nn.sigmoid(gate) * yn
            mixt_ref[c, ATTN_WIDTH + h * LANES:ATTN_WIDTH + (h + 1) * LANES, :] = r.astype(BF16)

    scores_into(0, 0)
    for c in range(MIX_CHUNKS):
        chunk(c, c + 1 < MIX_CHUNKS)

    mixt = jnp.concatenate([mixt_ref[c] for c in range(MIX_CHUNKS)], axis=1)
    mix = lax.dot_general(mixt, wout_ref[...], TN_DIMS, preferred_element_type=F32)
    o_ref[0] = x_ref[0] + _rms(mix, npost_ref[...])


def _mixer(x, zt, zk, states_f, states_b, dmat, qdec, sink, attn_norm, ret_gn, w_out, norm_post):
    B, S, _ = x.shape
    n = S // BLOCK
    cpt = MIX_CHUNKS
    edge_t = (1, 1, ATTN_KV_WIDTH, BLOCK)
    edge_k = (1, BLOCK, ATTN_KV_WIDTH)
    v_row = T_AV // ATTN_KV_WIDTH
    k_col = K_AK // ATTN_KV_WIDTH
    st_blk = (1, cpt, RET_HEADS, RET_HEAD_DIM, RET_HEAD_DIM)

    def prev(i):
        return jnp.maximum(i * cpt - 1, 0)

    def nxt(i):
        return jnp.minimum((i + 1) * cpt, n - 1)

    return pl.pallas_call(
        _mixer_kernel,
        grid=(B, S // MIX_TILE),
        in_specs=[pl.BlockSpec(memory_space=pltpu.SMEM),
                  pl.BlockSpec((1, MIX_TILE, D_MODEL), lambda b, i: (b, i, 0)),
                  pl.BlockSpec((1, cpt, T_ROWS, BLOCK), lambda b, i: (b, i, 0, 0)),
                  pl.BlockSpec(edge_t, lambda b, i: (b, prev(i), v_row, 0)),
                  pl.BlockSpec(edge_t, lambda b, i: (b, nxt(i), v_row, 0)),
                  pl.BlockSpec((1, MIX_TILE, K_COLS), lambda b, i: (b, i, 0)),
                  pl.BlockSpec(edge_k, lambda b, i: (b, prev(i), k_col)),
                  pl.BlockSpec(edge_k, lambda b, i: (b, nxt(i), k_col)),
                  pl.BlockSpec(st_blk, lambda b, i: (b, i, 0, 0, 0)),
                  pl.BlockSpec(st_blk, lambda b, i: (b, i, 0, 0, 0)),
                  _const_spec((RET_HEADS, BLOCK, LANES)),
                  _const_spec((2 * RET_HEADS, LANES)),
                  _const_spec((ATTN_WIDTH, LANES)),
                  _const_spec((RET_WIDTH, LANES)),
                  _const_spec((D_MODEL, D_MODEL)),
                  _const_spec((1, D_MODEL))],
        out_specs=pl.BlockSpec((1, MIX_TILE, D_MODEL), lambda b, i: (b, i, 0)),
        out_shape=jax.ShapeDtypeStruct((B, S, D_MODEL), F32),
        scratch_shapes=[pltpu.VMEM((ATTN_KV_HEADS, MIX_TILE + 2 * BLOCK, ATTN_KV_WIDTH), BF16),
                        pltpu.VMEM((cpt + 2, ATTN_KV_WIDTH, BLOCK), BF16),
                        pltpu.VMEM((cpt, D_MODEL, BLOCK), BF16),
                        pltpu.VMEM((2, ATTN_KV_HEADS, 3 * BLOCK, ATTN_GROUP * LANES), F32)],
        compiler_params=_params("parallel", "parallel"),
        name="token_mixer",
    )(sink, x, zt, zt, zt, zk, zk, zk, states_f, states_b, dmat, qdec,
      attn_norm, ret_gn, w_out, norm_post)


def _memkv_kernel(m_ref, g_ref, wkt_ref, wv_ref, kt_ref, v_ref):
    mn = _rms(m_ref[0], g_ref[...]).astype(BF16)
    kt_ref[0] = lax.dot_general(wkt_ref[...], mn, NT_DIMS,
                                preferred_element_type=F32).astype(BF16)
    v_ref[0] = _dot(mn, wv_ref[...]).astype(BF16)


def _memory_kv(mem, gain, w_kt, w_v):
    B, M, _ = mem.shape
    return pl.pallas_call(
        _memkv_kernel,
        grid=(B,),
        in_specs=[pl.BlockSpec((1, M, D_MODEL), lambda b: (b, 0, 0)),
                  _const_spec((1, D_MODEL)),
                  _const_spec((D_MODEL, D_MODEL)),
                  _const_spec((D_MODEL, D_MODEL))],
        out_specs=(pl.BlockSpec((1, D_MODEL, M), lambda b: (b, 0, 0)),
                   pl.BlockSpec((1, M, D_MODEL), lambda b: (b, 0, 0))),
        out_shape=(jax.ShapeDtypeStruct((B, D_MODEL, M), BF16),
                   jax.ShapeDtypeStruct((B, M, D_MODEL), BF16)),
        compiler_params=_params("parallel"),
        name="memory_kv",
    )(mem, gain, w_kt, w_v)


FF_CHUNK = 2 * MXU_TILE
FF_CHUNKS = [(c0, min(FF_CHUNK, D_FF - c0)) for c0 in range(0, D_FF, FF_CHUNK)]
assert all(w % MXU_TILE == 0 for _, w in FF_CHUNKS)
assert XA_HEADS <= 4 and N_MEM <= FF_CHUNK


def _xattn_ffn_kernel(x_ref, gxa_ref, wq_ref, kmt_ref, vm_ref, wo_ref, gxp_ref,
                      gfp_ref, wgu_ref, wd_ref, gfo_ref, o_ref,
                      cat_ref, q_ref, h_ref, res_ref, x2_ref, gu_ref, act_ref):
    pieces = [slice(r * ROW_PIECE, (r + 1) * ROW_PIECE) for r in range(TOK_TILE // ROW_PIECE)]

    for rows in pieces:
        h_ref[rows, :] = _rms(x_ref[0, rows, :], gxa_ref[...]).astype(BF16)
    for rows in pieces:
        q_ref[rows, :] = _dot(h_ref[rows, :], wq_ref[...]).astype(BF16)
    for hd in range(XA_HEADS):
        sl = slice(hd * XA_HEAD_DIM, (hd + 1) * XA_HEAD_DIM)
        gu_ref[hd // 2, hd % 2, :, 0:N_MEM] = _dot(q_ref[:, sl], kmt_ref[0, sl, :])
    for hd in range(XA_HEADS):
        sl = slice(hd * XA_HEAD_DIM, (hd + 1) * XA_HEAD_DIM)
        s = gu_ref[hd // 2, hd % 2, :, 0:N_MEM]
        p = jnp.exp(s - jnp.max(s, axis=-1, keepdims=True))
        den = jnp.sum(p, axis=-1, keepdims=True)
        cat_ref[:, sl] = (_dot(p.astype(BF16), vm_ref[0, :, sl]) * (1.0 / den)).astype(BF16)
    for rows in pieces:
        res_ref[rows, :] = _dot(cat_ref[rows, :], wo_ref[...])

    def gate_up_into(slot, j, rows=slice(None)):
        c0, w = FF_CHUNKS[j]
        gu_ref[slot, 0, rows, 0:w] = _dot(h_ref[rows, :], wgu_ref[:, c0:c0 + w])
        gu_ref[slot, 1, rows, 0:w] = _dot(h_ref[rows, :], wgu_ref[:, D_FF + c0:D_FF + c0 + w])

    for rows in pieces:
        x2 = x_ref[0, rows, :] + _rms(res_ref[rows, :], gxp_ref[...])
        x2_ref[rows, :] = x2
        h_ref[rows, :] = _rms(x2, gfp_ref[...]).astype(BF16)
    for rows in pieces:
        gate_up_into(0, 0, rows)

    for j, (c0, w) in enumerate(FF_CHUNKS):
        if j + 1 < len(FF_CHUNKS):
            gate_up_into((j + 1) % 2, j + 1)
        gate = gu_ref[j % 2, 0, :, 0:w]
        up = gu_ref[j % 2, 1, :, 0:w]
        act_ref[:, c0:c0 + w] = (gate * jax.nn.sigmoid(gate) * up).astype(BF16)
    for rows in pieces:
        res_ref[rows, :] = _dot(act_ref[rows, :], wd_ref[...])
    for rows in pieces:
        o_ref[0, rows, :] = x2_ref[rows, :] + _rms(res_ref[rows, :], gfo_ref[...])


def _cross_attention_ffn(x, kt_mem, v_mem, gain_xa_pre, w_q, w_o, gain_xa_post,
                         gain_ffn_pre, w_gu, w_down, gain_ffn_post):
    B, S, _ = x.shape
    M = v_mem.shape[1]
    tok = pl.BlockSpec((1, TOK_TILE, D_MODEL), lambda b, i: (b, i, 0))
    gain = _const_spec((1, D_MODEL))
    return pl.pallas_call(
        _xattn_ffn_kernel,
        grid=(B, S // TOK_TILE),
        in_specs=[tok, gain,
                  _const_spec((D_MODEL, D_MODEL)),
                  pl.BlockSpec((1, D_MODEL, M), lambda b, i: (b, 0, 0)),
                  pl.BlockSpec((1, M, D_MODEL), lambda b, i: (b, 0, 0)),
                  _const_spec((D_MODEL, D_MODEL)),
                  gain, gain,
                  _const_spec((D_MODEL, 2 * D_FF)),
                  _const_spec((D_FF, D_MODEL)),
                  gain],
        out_specs=tok,
        out_shape=jax.ShapeDtypeStruct((B, S, D_MODEL), F32),
        scratch_shapes=[pltpu.VMEM((TOK_TILE, D_MODEL), BF16),
                        pltpu.VMEM((TOK_TILE, D_MODEL), BF16),
                        pltpu.VMEM((TOK_TILE, D_MODEL), BF16),
                        pltpu.VMEM((TOK_TILE, D_MODEL), F32),
                        pltpu.VMEM((TOK_TILE, D_MODEL), F32),
                        pltpu.VMEM((2, 2, TOK_TILE, FF_CHUNK), F32),
                        pltpu.VMEM((TOK_TILE, D_FF), BF16)],
        compiler_params=_params("parallel", "parallel"),
        name="cross_attention_ffn",
    )(x, gain_xa_pre, w_q, kt_mem, v_mem, w_o, gain_xa_post,
      gain_ffn_pre, w_gu, w_down, gain_ffn_post)


def _pair_heads(a, axis):
    shape = a.shape
    a = a.reshape(shape[:axis] + (ATTN_KV_HEADS, ATTN_GROUP, ATTN_HEAD_DIM) + shape[axis + 1:])
    return jnp.swapaxes(a, axis, axis + 1).reshape(shape)


def kernel(x, mem, norm_mix_pre, norm_mix_post, w_in, attn_sink, attn_out_norm,
           ret_decay_fwd, ret_decay_bwd, ret_gn, w_out, norm_xa_pre, norm_xa_post,
           norm_mem, xa_wq, xa_wkv, xa_wo, norm_ffn_pre, norm_ffn_post,
           ffn_w_gu, ffn_w_down):
    B, S, D = x.shape
    assert D == D_MODEL and S % max(TOK_TILE, MIX_TILE, STATE_CHUNKS * BLOCK) == 0
    assert mem.shape[1] == N_MEM
    depth = w_in.shape[0]
    o_k = ATTN_WIDTH
    o_v = o_k + ATTN_KV_WIDTH
    o_r = o_v + ATTN_KV_WIDTH
    rw = RET_WIDTH

    rope = _rope_tables(S // TOK_TILE)

    def row(v):
        return v.reshape(1, -1).astype(F32)

    def col(v):
        return jnp.broadcast_to(v.astype(F32)[:, None], (v.shape[0], LANES))

    for l in range(depth):
        w = w_in[l]
        w_t = jnp.concatenate([_pair_heads(w[:, :o_k], 1), w[:, o_r:o_r + rw], w[:, o_r + 2 * rw:],
                               w[:, o_v:o_r], w[:, o_k:o_v]], axis=1).T.astype(BF16)
        w_k = w[:, o_r + rw:o_r + 2 * rw].astype(BF16)
        w_o = jnp.concatenate([_pair_heads(w_out[l][:ATTN_WIDTH], 0), w_out[l][ATTN_WIDTH:]], axis=0)

        dmat, kdec, qdec, cdec = _decay_tables(ret_decay_fwd[l], ret_decay_bwd[l])
        zt, zk = _in_projection(x, row(norm_mix_pre[l]), w_t, w_k, rope)
        states_f, states_b = _retention_states(zt, zk, kdec, cdec)
        x = _mixer(x, zt, zk, states_f, states_b, dmat, qdec, attn_sink[l].astype(F32),
                   col(_pair_heads(attn_out_norm[l], 0)), col(ret_gn[l]),
                   w_o.astype(BF16), row(norm_mix_post[l]))
        kt_mem, v_mem = _memory_kv(mem, row(norm_mem[l]), xa_wkv[l][:, :D_MODEL].T.astype(BF16),
                                   xa_wkv[l][:, D_MODEL:].astype(BF16))
        x = _cross_attention_ffn(x, kt_mem, v_mem, row(norm_xa_pre[l]),
                                 (xa_wq[l] * (XA_HEAD_DIM ** -0.5)).astype(BF16),
                                 xa_wo[l].astype(BF16), row(norm_xa_post[l]),
                                 row(norm_ffn_pre[l]), ffn_w_gu[l].astype(BF16),
                                 ffn_w_down[l].astype(BF16), row(norm_ffn_post[l]))
    return x
```

```python
import math

import numpy as np
import jax
import jax.numpy as jnp
from jax import lax
from jax.experimental import pallas as pl
from jax.experimental.pallas import tpu as pltpu

D_MODEL = 1024
N_MEM = 256
ATTN_HEADS = 8
ATTN_KV_HEADS = 2
ATTN_GROUP = ATTN_HEADS // ATTN_KV_HEADS
ATTN_HEAD_DIM = 64
ATTN_WIDTH = ATTN_HEADS * ATTN_HEAD_DIM
ATTN_KV_WIDTH = ATTN_KV_HEADS * ATTN_HEAD_DIM
BLOCK = 128
RET_HEADS = 4
RET_HEAD_DIM = 128
RET_WIDTH = RET_HEADS * RET_HEAD_DIM
XA_HEADS = 4
XA_HEAD_DIM = D_MODEL // XA_HEADS
assert math.log2(XA_HEAD_DIM) % 2 == 0
D_FF = -(-(8 * D_MODEL) // (3 * 256)) * 256
ROPE_THETA = 10000.0
EPS = 1e-6
LOG2E = math.log2(math.e)

LANES = 128
BF16_ROWS = 16
MXU_TILE = 256
VMEM_LIMIT_BYTES = 56 * 1024 * 1024

TOK_TILE = 512
CHUNKS_PER_TILE = TOK_TILE // BLOCK
STATE_CHUNKS = 16
MIX_TILE = 1024
MIX_CHUNKS = MIX_TILE // BLOCK
ROW_PIECE = 256

T_AQ, T_RQ, T_RV, T_RG, T_AV = 0, 512, 1024, 1536, 2048
T_ROWS = T_AV + ATTN_KV_WIDTH
K_RK, K_AK = 0, RET_WIDTH
K_COLS = RET_WIDTH + ATTN_KV_WIDTH

F32 = jnp.float32
BF16 = jnp.bfloat16
NT_DIMS = (((1,), (1,)), ((), ()))
TN_DIMS = (((0,), (0,)), ((), ()))


def _const_spec(shape):
    nd = len(shape)
    return pl.BlockSpec(shape, lambda *_: (0,) * nd, pipeline_mode=pl.Buffered(1))


def _params(*sem):
    return pltpu.CompilerParams(dimension_semantics=sem,
                                vmem_limit_bytes=VMEM_LIMIT_BYTES)


def _rms(x, w):
    ms = jnp.mean(x * x, axis=-1, keepdims=True)
    return x * lax.rsqrt(ms + EPS) * w


def _dot(a, b):
    return jnp.dot(a, b, preferred_element_type=F32)


def _rope_tables(n_tiles):
    def parts(dim):
        inv_freq = ROPE_THETA ** (-np.arange(0, dim, 2, dtype=np.float64) / dim)
        base = (np.arange(n_tiles, dtype=np.float64) * TOK_TILE)[:, None] * inv_freq[None, :]
        off = np.arange(TOK_TILE, dtype=np.float64)[:, None] * inv_freq[None, :]
        return np.cos(base), np.sin(base), np.cos(off), np.sin(off)

    def tok(c, s):
        return (np.concatenate([c, c], axis=1), np.concatenate([s, s], axis=1),
                np.concatenate([-s, s], axis=1))

    cb_a, sb_a, co_a, so_a = parts(ATTN_HEAD_DIM)
    cb_r, sb_r, co_r, so_r = parts(RET_HEAD_DIM)
    tok_base = np.stack(tok(cb_r, sb_r) + (np.zeros((n_tiles, LANES)),) * 5, axis=1)
    tok_off = np.stack(tok(co_r, so_r), axis=0)
    feat_base = np.concatenate([cb_a, sb_a, cb_r, sb_r], axis=1)
    feat_base = np.broadcast_to(feat_base[:, :, None], feat_base.shape + (BLOCK,))
    feat_off = np.concatenate([co_a, so_a, co_r, so_r], axis=1).T
    return tuple(jnp.asarray(t, F32) for t in (tok_base, tok_off, feat_base, feat_off))


def _decay_kernel(logit_ref, dmat_ref, kdec_ref, qdec_ref, cdec_ref):
    x = logit_ref[...]
    lg = jnp.minimum(x, 0.0) - jnp.log1p(jnp.exp(-jnp.abs(x)))
    ri = lax.broadcasted_iota(jnp.int32, (BLOCK, LANES), 0).astype(F32)
    ci = lax.broadcasted_iota(jnp.int32, (BLOCK, LANES), 1).astype(F32)
    c1 = ci[0:1, :]
    diff = ci - ri
    cdec_ref[...] = jnp.exp(BLOCK * lg)
    for h in range(RET_HEADS):
        lf = lg[h:h + 1, :]
        lb = lg[RET_HEADS + h:RET_HEADS + h + 1, :]
        dmat_ref[h] = jnp.exp(jnp.abs(diff) * jnp.where(diff >= 0, lf, lb))
        sl = slice(h * LANES, (h + 1) * LANES)
        kdec_ref[0, :, sl] = jnp.exp((BLOCK - 1 - ri) * lf)
        kdec_ref[1, :, sl] = jnp.exp(ri * lb)
        qdec_ref[h:h + 1, :] = jnp.exp((c1 + 1) * lf)
        qdec_ref[RET_HEADS + h:RET_HEADS + h + 1, :] = jnp.exp((BLOCK - c1) * lb)


def _decay_tables(decay_fwd, decay_bwd):
    logits = jnp.concatenate([decay_fwd, decay_bwd]).astype(F32)
    logits = jnp.broadcast_to(logits[:, None], (2 * RET_HEADS, LANES))
    return pl.pallas_call(
        _decay_kernel,
        out_shape=(jax.ShapeDtypeStruct((RET_HEADS, BLOCK, LANES), F32),
                   jax.ShapeDtypeStruct((2, BLOCK, RET_WIDTH), F32),
                   jax.ShapeDtypeStruct((2 * RET_HEADS, LANES), F32),
                   jax.ShapeDtypeStruct((2 * RET_HEADS, LANES), F32)),
        name="decay_tables",
    )(logits)


def _inproj_kernel(x_ref, g_ref, wt_ref, wk_ref, tb_ref, to_ref, fb_ref, fo_ref,
                   zt_ref, zk_ref):
    h = _rms(x_ref[0], g_ref[...]).astype(BF16)

    tb = tb_ref[0]

    ck = tb[0:1] * to_ref[0] - tb[1:2] * to_ref[1]
    sk = tb[2:3] * to_ref[0] + tb[0:1] * to_ref[2]

    fb = jnp.concatenate([fb_ref[0]] * CHUNKS_PER_TILE, axis=1)
    fo = fo_ref[...]

    def feat_tables(row, n):
        cb, sb = fb[row:row + n], fb[row + n:row + 2 * n]
        co, so = fo[row:row + n], fo[row + n:row + 2 * n]
        return cb * co - sb * so, sb * co + cb * so

    zk = _dot(h, wk_ref[...])
    for g in range(RET_HEADS):
        zg = zk[:, g * LANES:(g + 1) * LANES]
        r = (zg * ck + pltpu.roll(zg, LANES // 2, 1) * sk) * (RET_HEAD_DIM ** -0.5)
        zk_ref[0, :, K_RK + g * LANES:K_RK + (g + 1) * LANES] = r.astype(BF16)

    def proj_t(row, n_rows):
        return lax.dot_general(wt_ref[row:row + n_rows, :], h, NT_DIMS,
                               preferred_element_type=F32)

    def store_t(row, val):
        for c in range(CHUNKS_PER_TILE):
            zt_ref[0, c, row:row + val.shape[0], :] = val[:, c * BLOCK:(c + 1) * BLOCK].astype(BF16)

    def rope_t(z, row, head_dim, cos, sin):
        hd2 = head_dim // 2
        for k in range(z.shape[0] // head_dim):
            x1 = z[k * head_dim:k * head_dim + hd2]
            x2 = z[k * head_dim + hd2:(k + 1) * head_dim]
            store_t(row + k * head_dim, x1 * cos - x2 * sin)
            store_t(row + k * head_dim + hd2, x2 * cos + x1 * sin)

    q_scale = (ATTN_HEAD_DIM ** -0.5) * LOG2E
    cat, sat = feat_tables(0, ATTN_HEAD_DIM // 2)
    rope_t(proj_t(T_AQ, ATTN_WIDTH), T_AQ, ATTN_HEAD_DIM, cat * q_scale, sat * q_scale)
    crt, srt = feat_tables(ATTN_HEAD_DIM, RET_HEAD_DIM // 2)
    rope_t(proj_t(T_RQ, RET_WIDTH), T_RQ, RET_HEAD_DIM, crt, srt)
    store_t(T_RV, proj_t(T_RV, RET_WIDTH))
    store_t(T_RG, proj_t(T_RG, RET_WIDTH))
    kv = proj_t(T_AV, 2 * ATTN_KV_WIDTH)
    store_t(T_AV, kv[0:ATTN_KV_WIDTH])
    hd2 = ATTN_HEAD_DIM // 2
    rows = []
    for k in range(ATTN_KV_HEADS):
        r0 = ATTN_KV_WIDTH + k * ATTN_HEAD_DIM
        x1, x2 = kv[r0:r0 + hd2], kv[r0 + hd2:r0 + ATTN_HEAD_DIM]
        rows += [x1 * cat - x2 * sat, x2 * cat + x1 * sat]
    zk_ref[0, :, K_AK:K_AK + ATTN_KV_WIDTH] = jnp.concatenate(rows, axis=0).T.astype(BF16)


def _in_projection(x, gain, w_t, w_k, rope):
    B, S, _ = x.shape
    n = S // BLOCK
    tok_base, tok_off, feat_base, feat_off = rope
    return pl.pallas_call(
        _inproj_kernel,
        grid=(S // TOK_TILE, B),
        in_specs=[pl.BlockSpec((1, TOK_TILE, D_MODEL), lambda i, b: (b, i, 0)),
                  _const_spec((1, D_MODEL)),
                  _const_spec((T_ROWS + ATTN_KV_WIDTH, D_MODEL)),
                  _const_spec((D_MODEL, RET_WIDTH)),
                  pl.BlockSpec((1,) + tok_base.shape[1:], lambda i, b: (i, 0, 0)),
                  _const_spec(tok_off.shape),
                  pl.BlockSpec((1,) + feat_base.shape[1:], lambda i, b: (i, 0, 0)),
                  _const_spec(feat_off.shape)],
        out_specs=(pl.BlockSpec((1, CHUNKS_PER_TILE, T_ROWS, BLOCK), lambda i, b: (b, i, 0, 0)),
                   pl.BlockSpec((1, TOK_TILE, K_COLS), lambda i, b: (b, i, 0))),
        out_shape=(jax.ShapeDtypeStruct((B, n, T_ROWS, BLOCK), BF16),
                   jax.ShapeDtypeStruct((B, S, K_COLS), BF16)),
        compiler_params=_params("parallel", "parallel"),
        name="in_projection",
    )(x, gain, w_t, w_k, tok_base, tok_off, feat_base, feat_off)


def _state_kernel(kf_ref, vf_ref, kb_ref, vb_ref, kdec_ref, cdec_ref,
                  pf_ref, pb_ref, st_ref):
    @pl.when(pl.program_id(1) == 0)
    def _():
        st_ref[...] = jnp.zeros_like(st_ref)

    for i in range(STATE_CHUNKS):
        for d, (k_ref, v_ref, p_ref, c) in enumerate(((kf_ref, vf_ref, pf_ref, i),
                                                      (kb_ref, vb_ref, pb_ref, STATE_CHUNKS - 1 - i))):
            kd = (k_ref[0, c * BLOCK:(c + 1) * BLOCK, :].astype(F32) * kdec_ref[d]).astype(BF16)
            for h in range(RET_HEADS):
                sl = slice(h * LANES, (h + 1) * LANES)
                st = st_ref[d, h]
                p_ref[0, c, h] = st.astype(BF16)
                kv = _dot(v_ref[0, c, sl, :], kd[:, sl])
                row = d * RET_HEADS + h
                st_ref[d, h] = st * cdec_ref[row:row + 1, :] + kv


def _retention_states(zt, zk, kdec, cdec):
    B, n = zt.shape[0], zt.shape[1]
    ng = n // STATE_CHUNKS
    k_blk = (1, STATE_CHUNKS * BLOCK, RET_WIDTH)
    v_blk = (1, STATE_CHUNKS, RET_WIDTH, BLOCK)
    vrow = T_RV // RET_WIDTH
    st_blk = (1, STATE_CHUNKS, RET_HEADS, RET_HEAD_DIM, RET_HEAD_DIM)
    st_shape = jax.ShapeDtypeStruct((B, n, RET_HEADS, RET_HEAD_DIM, RET_HEAD_DIM), BF16)
    return pl.pallas_call(
        _state_kernel,
        grid=(B, ng),
        in_specs=[pl.BlockSpec(k_blk, lambda b, t: (b, t, 0)),
                  pl.BlockSpec(v_blk, lambda b, t: (b, t, vrow, 0)),
                  pl.BlockSpec(k_blk, lambda b, t: (b, ng - 1 - t, 0)),
                  pl.BlockSpec(v_blk, lambda b, t: (b, ng - 1 - t, vrow, 0)),
                  _const_spec((2, BLOCK, RET_WIDTH)),
                  _const_spec((2 * RET_HEADS, LANES))],
        out_specs=(pl.BlockSpec(st_blk, lambda b, t: (b, t, 0, 0, 0)),
                   pl.BlockSpec(st_blk, lambda b, t: (b, ng - 1 - t, 0, 0, 0))),
        out_shape=(st_shape, st_shape),
        scratch_shapes=[pltpu.VMEM((2, RET_HEADS, RET_HEAD_DIM, RET_HEAD_DIM), F32)],
        compiler_params=_params("parallel", "arbitrary"),
        name="retention_states",
    )(zk, zt, zk, zt, kdec, cdec)


def _mixer_kernel(sink_ref, x_ref, zt_ref, avp_ref, avn_ref, zk_ref, akp_ref, akn_ref,
                  pf_ref, pb_ref, dmat_ref, qdec_ref,
                  an_ref, gn_ref, wout_ref, npost_ref, o_ref, kext_ref, vext_ref, mixt_ref,
                  sc_ref):
    tile = pl.program_id(1)
    n_blocks = pl.num_programs(1) * MIX_CHUNKS

    lane = lax.broadcasted_iota(jnp.int32, (1, LANES), 1)
    for g in range(ATTN_KV_HEADS):
        km = jnp.where((lane // ATTN_HEAD_DIM) == g, 1.0, 0.0).astype(BF16)
        kext_ref[g, 0:BLOCK] = akp_ref[0] * km
        kext_ref[g, BLOCK:BLOCK + MIX_TILE] = zk_ref[0, :, K_AK:K_AK + ATTN_KV_WIDTH] * km
        kext_ref[g, BLOCK + MIX_TILE:] = akn_ref[0] * km
    vext_ref[0] = avp_ref[0, 0]
    for c in range(MIX_CHUNKS):
        vext_ref[1 + c] = zt_ref[0, c, T_AV:T_AV + ATTN_KV_WIDTH, :]
    vext_ref[MIX_CHUNKS + 1] = avn_ref[0, 0]

    kk = lax.broadcasted_iota(jnp.int32, (BLOCK, LANES), 0)
    qq = lax.broadcasted_iota(jnp.int32, (BLOCK, LANES), 1)
    neg = jnp.full((BLOCK, LANES), -jnp.inf, F32)
    zero = jnp.zeros((BLOCK, LANES), F32)

    def scores_into(slot, c):
        q = jnp.concatenate([zt_ref[0, c, T_AQ + j * LANES:T_AQ + (j + 1) * LANES, :]
                             for j in range(ATTN_GROUP)], axis=1)
        r0 = c * BLOCK
        for g in range(ATTN_KV_HEADS):
            sc_ref[slot, g] = _dot(kext_ref[g, pl.ds(r0, 3 * BLOCK), :], q)

    def chunk(c, has_next):
        r0 = c * BLOCK
        blk = tile * MIX_CHUNKS + c
        slot = c % 2
        if has_next:
            scores_into(1 - slot, c + 1)

        vw = jnp.concatenate([vext_ref[c], vext_ref[c + 1], vext_ref[c + 2]], axis=1)
        no_prev = jnp.where(blk > 0, 0, BLOCK) if c == 0 else 0
        no_next = jnp.where(blk < n_blocks - 1, 0, BLOCK) if c == MIX_CHUNKS - 1 else 0
        bias_prev = jnp.where(kk >= qq + no_prev, zero, neg)
        bias_next = jnp.where(kk <= qq - no_next, zero, neg)
        bias_prev = jnp.concatenate([bias_prev] * ATTN_GROUP, axis=1)
        bias_next = jnp.concatenate([bias_next] * ATTN_GROUP, axis=1)
        heads = []
        for h in range(RET_HEADS):
            qt = zt_ref[0, c, T_RQ + h * LANES:T_RQ + (h + 1) * LANES, :]
            kh = zk_ref[0, pl.ds(r0, BLOCK), K_RK + h * LANES:K_RK + (h + 1) * LANES]
            heads.append((qt, _dot(kh, qt)))

        ones = jnp.ones((BF16_ROWS, 3 * BLOCK), BF16)
        outs = []
        for g in range(ATTN_KV_HEADS):
            s = sc_ref[slot, g]
            s = jnp.concatenate([s[0:BLOCK] + bias_prev, s[BLOCK:2 * BLOCK],
                                 s[2 * BLOCK:] + bias_next], axis=0)
            sk = jnp.concatenate(
                [jnp.full((1, LANES), sink_ref[g * ATTN_GROUP + j] * LOG2E, F32)
                 for j in range(ATTN_GROUP)], axis=1)
            m = jnp.maximum(jnp.max(s, axis=0, keepdims=True), sk)
            p = jnp.exp2(s - m).astype(BF16)
            vg = jnp.concatenate([vw[g * ATTN_HEAD_DIM:(g + 1) * ATTN_HEAD_DIM, :], ones], axis=0)
            ov = _dot(vg, p)
            den = ov[ATTN_HEAD_DIM:ATTN_HEAD_DIM + 1, :] + jnp.exp2(sk - m)
            outs.append(ov[0:ATTN_HEAD_DIM, :] * (1.0 / den))
        ys = []
        for h in range(RET_HEADS):
            qt, pt = heads[h]
            y = _dot(zt_ref[0, c, T_RV + h * LANES:T_RV + (h + 1) * LANES, :],
                     (pt * dmat_ref[h]).astype(BF16))
            y = y + _dot(pf_ref[0, c, h], qt) * qdec_ref[h:h + 1, :]
            y = y + _dot(pb_ref[0, c, h], qt) * qdec_ref[RET_HEADS + h:RET_HEADS + h + 1, :]
            ys.append(y)

        a = jnp.concatenate([outs[g][:, j * LANES:(j + 1) * LANES]
                             for j in range(ATTN_GROUP) for g in range(ATTN_KV_HEADS)], axis=0)
        ms = jnp.sum(a * a, axis=0, keepdims=True) * (1.0 / ATTN_WIDTH)
        mixt_ref[c, 0:ATTN_WIDTH, :] = (a * lax.rsqrt(ms + EPS) * an_ref[...]).astype(BF16)
        for h in range(RET_HEADS):
            sl = slice(h * LANES, (h + 1) * LANES)
            y = ys[h]
            mu = jnp.mean(y, axis=0, keepdims=True)
            yc = y - mu
            var = jnp.mean(yc * yc, axis=0, keepdims=True)
            yn = yc * lax.rsqrt(var + EPS) * gn_ref[sl, :]
            gate = zt_ref[0, c, T_RG + h * LANES:T_RG + (h + 1) * LANES, :].astype(F32)
            r = gate * jax.nn.sigmoid(gate) * yn
            mixt_ref[c, ATTN_WIDTH + h * LANES:ATTN_WIDTH + (h + 1) * LANES, :] = r.astype(BF16)

    scores_into(0, 0)
    for c in range(MIX_CHUNKS):
        chunk(c, c + 1 < MIX_CHUNKS)

    mixt = jnp.concatenate([mixt_ref[c] for c in range(MIX_CHUNKS)], axis=1)
    mix = lax.dot_general(mixt, wout_ref[...], TN_DIMS, preferred_element_type=F32)
    o_ref[0] = x_ref[0] + _rms(mix, npost_ref[...])


def _mixer(x, zt, zk, states_f, states_b, dmat, qdec, sink, attn_norm, ret_gn, w_out, norm_post):
    B, S, _ = x.shape
    n = S // BLOCK
    cpt = MIX_CHUNKS
    edge_t = (1, 1, ATTN_KV_WIDTH, BLOCK)
    edge_k = (1, BLOCK, ATTN_KV_WIDTH)
    v_row = T_AV // ATTN_KV_WIDTH
    k_col = K_AK // ATTN_KV_WIDTH
    st_blk = (1, cpt, RET_HEADS, RET_HEAD_DIM, RET_HEAD_DIM)

    def prev(i):
        return jnp.maximum(i * cpt - 1, 0)

    def nxt(i):
        return jnp.minimum((i + 1) * cpt, n - 1)

    return pl.pallas_call(
        _mixer_kernel,
        grid=(B, S // MIX_TILE),
        in_specs=[pl.BlockSpec(memory_space=pltpu.SMEM),
                  pl.BlockSpec((1, MIX_TILE, D_MODEL), lambda b, i: (b, i, 0)),
                  pl.BlockSpec((1, cpt, T_ROWS, BLOCK), lambda b, i: (b, i, 0, 0)),
                  pl.BlockSpec(edge_t, lambda b, i: (b, prev(i), v_row, 0)),
                  pl.BlockSpec(edge_t, lambda b, i: (b, nxt(i), v_row, 0)),
                  pl.BlockSpec((1, MIX_TILE, K_COLS), lambda b, i: (b, i, 0)),
                  pl.BlockSpec(edge_k, lambda b, i: (b, prev(i), k_col)),
                  pl.BlockSpec(edge_k, lambda b, i: (b, nxt(i), k_col)),
                  pl.BlockSpec(st_blk, lambda b, i: (b, i, 0, 0, 0)),
                  pl.BlockSpec(st_blk, lambda b, i: (b, i, 0, 0, 0)),
                  _const_spec((RET_HEADS, BLOCK, LANES)),
                  _const_spec((2 * RET_HEADS, LANES)),
                  _const_spec((ATTN_WIDTH, LANES)),
                  _const_spec((RET_WIDTH, LANES)),
                  _const_spec((D_MODEL, D_MODEL)),
                  _const_spec((1, D_MODEL))],
        out_specs=pl.BlockSpec((1, MIX_TILE, D_MODEL), lambda b, i: (b, i, 0)),
        out_shape=jax.ShapeDtypeStruct((B, S, D_MODEL), F32),
        scratch_shapes=[pltpu.VMEM((ATTN_KV_HEADS, MIX_TILE + 2 * BLOCK, ATTN_KV_WIDTH), BF16),
                        pltpu.VMEM((cpt + 2, ATTN_KV_WIDTH, BLOCK), BF16),
                        pltpu.VMEM((cpt, D_MODEL, BLOCK), BF16),
                        pltpu.VMEM((2, ATTN_KV_HEADS, 3 * BLOCK, ATTN_GROUP * LANES), F32)],
        compiler_params=_params("parallel", "parallel"),
        name="token_mixer",
    )(sink, x, zt, zt, zt, zk, zk, zk, states_f, states_b, dmat, qdec,
      attn_norm, ret_gn, w_out, norm_post)


def _memkv_kernel(m_ref, g_ref, wkt_ref, wv_ref, kt_ref, v_ref):
    mn = _rms(m_ref[0], g_ref[...]).astype(BF16)
    kt_ref[0] = lax.dot_general(wkt_ref[...], mn, NT_DIMS,
                                preferred_element_type=F32).astype(BF16)
    v_ref[0] = _dot(mn, wv_ref[...]).astype(BF16)


def _memory_kv(mem, gain, w_kt, w_v):
    B, M, _ = mem.shape
    return pl.pallas_call(
        _memkv_kernel,
        grid=(B,),
        in_specs=[pl.BlockSpec((1, M, D_MODEL), lambda b: (b, 0, 0)),
                  _const_spec((1, D_MODEL)),
                  _const_spec((D_MODEL, D_MODEL)),
                  _const_spec((D_MODEL, D_MODEL))],
        out_specs=(pl.BlockSpec((1, D_MODEL, M), lambda b: (b, 0, 0)),
                   pl.BlockSpec((1, M, D_MODEL), lambda b: (b, 0, 0))),
        out_shape=(jax.ShapeDtypeStruct((B, D_MODEL, M), BF16),
                   jax.ShapeDtypeStruct((B, M, D_MODEL), BF16)),
        compiler_params=_params("parallel"),
        name="memory_kv",
    )(mem, gain, w_kt, w_v)


FF_CHUNK = 2 * MXU_TILE
FF_CHUNKS = [(c0, min(FF_CHUNK, D_FF - c0)) for c0 in range(0, D_FF, FF_CHUNK)]
assert all(w % MXU_TILE == 0 for _, w in FF_CHUNKS)
assert XA_HEADS <= 4 and N_MEM <= FF_CHUNK


def _xattn_ffn_kernel(x_ref, gxa_ref, wq_ref, kmt_ref, vm_ref, wo_ref, gxp_ref,
                      gfp_ref, wgu_ref, wd_ref, gfo_ref, o_ref,
                      cat_ref, q_ref, h_ref, res_ref, x2_ref, gu_ref, act_ref):
    pieces = [slice(r * ROW_PIECE, (r + 1) * ROW_PIECE) for r in range(TOK_TILE // ROW_PIECE)]

    for rows in pieces:
        h_ref[rows, :] = _rms(x_ref[0, rows, :], gxa_ref[...]).astype(BF16)
    for rows in pieces:
        q_ref[rows, :] = _dot(h_ref[rows, :], wq_ref[...]).astype(BF16)
    for hd in range(XA_HEADS):
        sl = slice(hd * XA_HEAD_DIM, (hd + 1) * XA_HEAD_DIM)
        gu_ref[hd // 2, hd % 2, :, 0:N_MEM] = _dot(q_ref[:, sl], kmt_ref[0, sl, :])
    for hd in range(XA_HEADS):
        sl = slice(hd * XA_HEAD_DIM, (hd + 1) * XA_HEAD_DIM)
        s = gu_ref[hd // 2, hd % 2, :, 0:N_MEM]
        p = jnp.exp(s - jnp.max(s, axis=-1, keepdims=True))
        den = jnp.sum(p, axis=-1, keepdims=True)
        cat_ref[:, sl] = (_dot(p.astype(BF16), vm_ref[0, :, sl]) * (1.0 / den)).astype(BF16)
    for rows in pieces:
        res_ref[rows, :] = _dot(cat_ref[rows, :], wo_ref[...])

    def gate_up_into(slot, j, rows=slice(None)):
        c0, w = FF_CHUNKS[j]
        gu_ref[slot, 0, rows, 0:w] = _dot(h_ref[rows, :], wgu_ref[:, c0:c0 + w])
        gu_ref[slot, 1, rows, 0:w] = _dot(h_ref[rows, :], wgu_ref[:, D_FF + c0:D_FF + c0 + w])

    for rows in pieces:
        x2 = x_ref[0, rows, :] + _rms(res_ref[rows, :], gxp_ref[...])
        x2_ref[rows, :] = x2
        h_ref[rows, :] = _rms(x2, gfp_ref[...]).astype(BF16)
    for rows in pieces:
        gate_up_into(0, 0, rows)

    for j, (c0, w) in enumerate(FF_CHUNKS):
        if j + 1 < len(FF_CHUNKS):
            gate_up_into((j + 1) % 2, j + 1)
        gate = gu_ref[j % 2, 0, :, 0:w]
        up = gu_ref[j % 2, 1, :, 0:w]
        act_ref[:, c0:c0 + w] = (gate * jax.nn.sigmoid(gate) * up).astype(BF16)
    for rows in pieces:
        res_ref[rows, :] = _dot(act_ref[rows, :], wd_ref[...])
    for rows in pieces:
        o_ref[0, rows, :] = x2_ref[rows, :] + _rms(res_ref[rows, :], gfo_ref[...])


def _cross_attention_ffn(x, kt_mem, v_mem, gain_xa_pre, w_q, w_o, gain_xa_post,
                         gain_ffn_pre, w_gu, w_down, gain_ffn_post):
    B, S, _ = x.shape
    M = v_mem.shape[1]
    tok = pl.BlockSpec((1, TOK_TILE, D_MODEL), lambda b, i: (b, i, 0))
    gain = _const_spec((1, D_MODEL))
    return pl.pallas_call(
        _xattn_ffn_kernel,
        grid=(B, S // TOK_TILE),
        in_specs=[tok, gain,
                  _const_spec((D_MODEL, D_MODEL)),
                  pl.BlockSpec((1, D_MODEL, M), lambda b, i: (b, 0, 0)),
                  pl.BlockSpec((1, M, D_MODEL), lambda b, i: (b, 0, 0)),
                  _const_spec((D_MODEL, D_MODEL)),
                  gain, gain,
                  _const_spec((D_MODEL, 2 * D_FF)),
                  _const_spec((D_FF, D_MODEL)),
                  gain],
        out_specs=tok,
        out_shape=jax.ShapeDtypeStruct((B, S, D_MODEL), F32),
        scratch_shapes=[pltpu.VMEM((TOK_TILE, D_MODEL), BF16),
                        pltpu.VMEM((TOK_TILE, D_MODEL), BF16),
                        pltpu.VMEM((TOK_TILE, D_MODEL), BF16),
                        pltpu.VMEM((TOK_TILE, D_MODEL), F32),
                        pltpu.VMEM((TOK_TILE, D_MODEL), F32),
                        pltpu.VMEM((2, 2, TOK_TILE, FF_CHUNK), F32),
                        pltpu.VMEM((TOK_TILE, D_FF), BF16)],
        compiler_params=_params("parallel", "parallel"),
        name="cross_attention_ffn",
    )(x, gain_xa_pre, w_q, kt_mem, v_mem, w_o, gain_xa_post,
      gain_ffn_pre, w_gu, w_down, gain_ffn_post)


def _pair_heads(a, axis):
    shape = a.shape
    a = a.reshape(shape[:axis] + (ATTN_KV_HEADS, ATTN_GROUP, ATTN_HEAD_DIM) + shape[axis + 1:])
    return jnp.swapaxes(a, axis, axis + 1).reshape(shape)


def kernel(x, mem, norm_mix_pre, norm_mix_post, w_in, attn_sink, attn_out_norm,
           ret_decay_fwd, ret_decay_bwd, ret_gn, w_out, norm_xa_pre, norm_xa_post,
           norm_mem, xa_wq, xa_wkv, xa_wo, norm_ffn_pre, norm_ffn_post,
           ffn_w_gu, ffn_w_down):
    B, S, D = x.shape
    assert D == D_MODEL and S % max(TOK_TILE, MIX_TILE, STATE_CHUNKS * BLOCK) == 0
    assert mem.shape[1] == N_MEM
    depth = w_in.shape[0]
    o_k = ATTN_WIDTH
    o_v = o_k + ATTN_KV_WIDTH
    o_r = o_v + ATTN_KV_WIDTH
    rw = RET_WIDTH

    rope = _rope_tables(S // TOK_TILE)

    def row(v):
        return v.reshape(1, -1).astype(F32)

    def col(v):
        return jnp.broadcast_to(v.astype(F32)[:, None], (v.shape[0], LANES))

    for l in range(depth):
        w = w_in[l]
        w_t = jnp.concatenate([_pair_heads(w[:, :o_k], 1), w[:, o_r:o_r + rw], w[:, o_r + 2 * rw:],
                               w[:, o_v:o_r], w[:, o_k:o_v]], axis=1).T.astype(BF16)
        w_k = w[:, o_r + rw:o_r + 2 * rw].astype(BF16)
        w_o = jnp.concatenate([_pair_heads(w_out[l][:ATTN_WIDTH], 0), w_out[l][ATTN_WIDTH:]], axis=0)

        dmat, kdec, qdec, cdec = _decay_tables(ret_decay_fwd[l], ret_decay_bwd[l])
        zt, zk = _in_projection(x, row(norm_mix_pre[l]), w_t, w_k, rope)
        states_f, states_b = _retention_states(zt, zk, kdec, cdec)
        x = _mixer(x, zt, zk, states_f, states_b, dmat, qdec, attn_sink[l].astype(F32),
                   col(_pair_heads(attn_out_norm[l], 0)), col(ret_gn[l]),
                   w_o.astype(BF16), row(norm_mix_post[l]))
        kt_mem, v_mem = _memory_kv(mem, row(norm_mem[l]), xa_wkv[l][:, :D_MODEL].T.astype(BF16),
                                   xa_wkv[l][:, D_MODEL:].astype(BF16))
        x = _cross_attention_ffn(x, kt_mem, v_mem, row(norm_xa_pre[l]),
                                 (xa_wq[l] * (XA_HEAD_DIM ** -0.5)).astype(BF16),
                                 xa_wo[l].astype(BF16), row(norm_xa_post[l]),
                                 row(norm_ffn_pre[l]), ffn_w_gu[l].astype(BF16),
                                 ffn_w_down[l].astype(BF16), row(norm_ffn_post[l]))
    return x
```

```python
import math

import numpy as np
import jax
import jax.numpy as jnp
from jax import lax
from jax.experimental import pallas as pl
from jax.experimental.pallas import tpu as pltpu

D_MODEL = 1024
N_MEM = 256
ATTN_HEADS = 8
ATTN_KV_HEADS = 2
ATTN_GROUP = ATTN_HEADS // ATTN_KV_HEADS
ATTN_HEAD_DIM = 64
ATTN_WIDTH = ATTN_HEADS * ATTN_HEAD_DIM
ATTN_KV_WIDTH = ATTN_KV_HEADS * ATTN_HEAD_DIM
BLOCK = 128
RET_HEADS = 4
RET_HEAD_DIM = 128
RET_WIDTH = RET_HEADS * RET_HEAD_DIM
XA_HEADS = 4
XA_HEAD_DIM = D_MODEL // XA_HEADS
assert math.log2(XA_HEAD_DIM) % 2 == 0
D_FF = -(-(8 * D_MODEL) // (3 * 256)) * 256
ROPE_THETA = 10000.0
EPS = 1e-6
LOG2E = math.log2(math.e)

LANES = 128
BF16_ROWS = 16
MXU_TILE = 256
VMEM_LIMIT_BYTES = 56 * 1024 * 1024

TOK_TILE = 512
CHUNKS_PER_TILE = TOK_TILE // BLOCK
STATE_CHUNKS = 16
MIX_TILE = 1024
MIX_CHUNKS = MIX_TILE // BLOCK
ROW_PIECE = 256

T_AQ, T_RQ, T_RV, T_RG, T_AV = 0, 512, 1024, 1536, 2048
T_ROWS = T_AV + ATTN_KV_WIDTH
K_RK, K_AK = 0, RET_WIDTH
K_COLS = RET_WIDTH + ATTN_KV_WIDTH

F32 = jnp.float32
BF16 = jnp.bfloat16
NT_DIMS = (((1,), (1,)), ((), ()))
TN_DIMS = (((0,), (0,)), ((), ()))


def _const_spec(shape):
    nd = len(shape)
    return pl.BlockSpec(shape, lambda *_: (0,) * nd, pipeline_mode=pl.Buffered(1))


def _params(*sem):
    return pltpu.CompilerParams(dimension_semantics=sem,
                                vmem_limit_bytes=VMEM_LIMIT_BYTES)


def _rms(x, w):
    ms = jnp.mean(x * x, axis=-1, keepdims=True)
    return x * lax.rsqrt(ms + EPS) * w


def _dot(a, b):
    return jnp.dot(a, b, preferred_element_type=F32)


def _rope_tables(n_tiles):
    def parts(dim):
        inv_freq = ROPE_THETA ** (-np.arange(0, dim, 2, dtype=np.float64) / dim)
        base = (np.arange(n_tiles, dtype=np.float64) * TOK_TILE)[:, None] * inv_freq[None, :]
        off = np.arange(TOK_TILE, dtype=np.float64)[:, None] * inv_freq[None, :]
        return np.cos(base), np.sin(base), np.cos(off), np.sin(off)

    def tok(c, s):
        return (np.concatenate([c, c], axis=1), np.concatenate([s, s], axis=1),
                np.concatenate([-s, s], axis=1))

    cb_a, sb_a, co_a, so_a = parts(ATTN_HEAD_DIM)
    cb_r, sb_r, co_r, so_r = parts(RET_HEAD_DIM)
    tok_base = np.stack(tok(cb_r, sb_r) + (np.zeros((n_tiles, LANES)),) * 5, axis=1)
    tok_off = np.stack(tok(co_r, so_r), axis=0)
    feat_base = np.concatenate([cb_a, sb_a, cb_r, sb_r], axis=1)
    feat_base = np.broadcast_to(feat_base[:, :, None], feat_base.shape + (BLOCK,))
    feat_off = np.concatenate([co_a, so_a, co_r, so_r], axis=1).T
    return tuple(jnp.asarray(t, F32) for t in (tok_base, tok_off, feat_base, feat_off))


def _decay_kernel(logit_ref, dmat_ref, kdec_ref, qdec_ref, cdec_ref):
    x = logit_ref[...]
    lg = jnp.minimum(x, 0.0) - jnp.log1p(jnp.exp(-jnp.abs(x)))
    ri = lax.broadcasted_iota(jnp.int32, (BLOCK, LANES), 0).astype(F32)
    ci = lax.broadcasted_iota(jnp.int32, (BLOCK, LANES), 1).astype(F32)
    c1 = ci[0:1, :]
    diff = ci - ri
    cdec_ref[...] = jnp.exp(BLOCK * lg)
    for h in range(RET_HEADS):
        lf = lg[h:h + 1, :]
        lb = lg[RET_HEADS + h:RET_HEADS + h + 1, :]
        dmat_ref[h] = jnp.exp(jnp.abs(diff) * jnp.where(diff >= 0, lf, lb))
        sl = slice(h * LANES, (h + 1) * LANES)
        kdec_ref[0, :, sl] = jnp.exp((BLOCK - 1 - ri) * lf)
        kdec_ref[1, :, sl] = jnp.exp(ri * lb)
        qdec_ref[h:h + 1, :] = jnp.exp((c1 + 1) * lf)
        qdec_ref[RET_HEADS + h:RET_HEADS + h + 1, :] = jnp.exp((BLOCK - c1) * lb)


def _decay_tables(decay_fwd, decay_bwd):
    logits = jnp.concatenate([decay_fwd, decay_bwd]).astype(F32)
    logits = jnp.broadcast_to(logits[:, None], (2 * RET_HEADS, LANES))
    return pl.pallas_call(
        _decay_kernel,
        out_shape=(jax.ShapeDtypeStruct((RET_HEADS, BLOCK, LANES), F32),
                   jax.ShapeDtypeStruct((2, BLOCK, RET_WIDTH), F32),
                   jax.ShapeDtypeStruct((2 * RET_HEADS, LANES), F32),
                   jax.ShapeDtypeStruct((2 * RET_HEADS, LANES), F32)),
        name="decay_tables",
    )(logits)


def _inproj_kernel(x_ref, g_ref, wt_ref, wk_ref, tb_ref, to_ref, fb_ref, fo_ref,
                   zt_ref, zk_ref):
    h = _rms(x_ref[0], g_ref[...]).astype(BF16)

    tb = tb_ref[0]

    ck = tb[0:1] * to_ref[0] - tb[1:2] * to_ref[1]
    sk = tb[2:3] * to_ref[0] + tb[0:1] * to_ref[2]

    fb = jnp.concatenate([fb_ref[0]] * CHUNKS_PER_TILE, axis=1)
    fo = fo_ref[...]

    def feat_tables(row, n):
        cb, sb = fb[row:row + n], fb[row + n:row + 2 * n]
        co, so = fo[row:row + n], fo[row + n:row + 2 * n]
        return cb * co - sb * so, sb * co + cb * so

    zk = _dot(h, wk_ref[...])
    for g in range(RET_HEADS):
        zg = zk[:, g * LANES:(g + 1) * LANES]
        r = (zg * ck + pltpu.roll(zg, LANES // 2, 1) * sk) * (RET_HEAD_DIM ** -0.5)
        zk_ref[0, :, K_RK + g * LANES:K_RK + (g + 1) * LANES] = r.astype(BF16)

    def proj_t(row, n_rows):
        return lax.dot_general(wt_ref[row:row + n_rows, :], h, NT_DIMS,
                               preferred_element_type=F32)

    def store_t(row, val):
        for c in range(CHUNKS_PER_TILE):
            zt_ref[0, c, row:row + val.shape[0], :] = val[:, c * BLOCK:(c + 1) * BLOCK].astype(BF16)

    def rope_t(z, row, head_dim, cos, sin):
        hd2 = head_dim // 2
        for k in range(z.shape[0] // head_dim):
            x1 = z[k * head_dim:k * head_dim + hd2]
            x2 = z[k * head_dim + hd2:(k + 1) * head_dim]
            store_t(row + k * head_dim, x1 * cos - x2 * sin)
            store_t(row + k * head_dim + hd2, x2 * cos + x1 * sin)

    q_scale = (ATTN_HEAD_DIM ** -0.5) * LOG2E
    cat, sat = feat_tables(0, ATTN_HEAD_DIM // 2)
    rope_t(proj_t(T_AQ, ATTN_WIDTH), T_AQ, ATTN_HEAD_DIM, cat * q_scale, sat * q_scale)
    crt, srt = feat_tables(ATTN_HEAD_DIM, RET_HEAD_DIM // 2)
    rope_t(proj_t(T_RQ, RET_WIDTH), T_RQ, RET_HEAD_DIM, crt, srt)
    store_t(T_RV, proj_t(T_RV, RET_WIDTH))
    store_t(T_RG, proj_t(T_RG, RET_WIDTH))
    kv = proj_t(T_AV, 2 * ATTN_KV_WIDTH)
    store_t(T_AV, kv[0:ATTN_KV_WIDTH])
    hd2 = ATTN_HEAD_DIM // 2
    rows = []
    for k in range(ATTN_KV_HEADS):
        r0 = ATTN_KV_WIDTH + k * ATTN_HEAD_DIM
        x1, x2 = kv[r0:r0 + hd2], kv[r0 + hd2:r0 + ATTN_HEAD_DIM]
        rows += [x1 * cat - x2 * sat, x2 * cat + x1 * sat]
    zk_ref[0, :, K_AK:K_AK + ATTN_KV_WIDTH] = jnp.concatenate(rows, axis=0).T.astype(BF16)


def _in_projection(x, gain, w_t, w_k, rope):
    B, S, _ = x.shape
    n = S // BLOCK
    tok_base, tok_off, feat_base, feat_off = rope
    return pl.pallas_call(
        _inproj_kernel,
        grid=(S // TOK_TILE, B),
        in_specs=[pl.BlockSpec((1, TOK_TILE, D_MODEL), lambda i, b: (b, i, 0)),
                  _const_spec((1, D_MODEL)),
                  _const_spec((T_ROWS + ATTN_KV_WIDTH, D_MODEL)),
                  _const_spec((D_MODEL, RET_WIDTH)),
                  pl.BlockSpec((1,) + tok_base.shape[1:], lambda i, b: (i, 0, 0)),
                  _const_spec(tok_off.shape),
                  pl.BlockSpec((1,) + feat_base.shape[1:], lambda i, b: (i, 0, 0)),
                  _const_spec(feat_off.shape)],
        out_specs=(pl.BlockSpec((1, CHUNKS_PER_TILE, T_ROWS, BLOCK), lambda i, b: (b, i, 0, 0)),
                   pl.BlockSpec((1, TOK_TILE, K_COLS), lambda i, b: (b, i, 0))),
        out_shape=(jax.ShapeDtypeStruct((B, n, T_ROWS, BLOCK), BF16),
                   jax.ShapeDtypeStruct((B, S, K_COLS), BF16)),
        compiler_params=_params("parallel", "parallel"),
        name="in_projection",
    )(x, gain, w_t, w_k, tok_base, tok_off, feat_base, feat_off)


def _state_kernel(kf_ref, vf_ref, kb_ref, vb_ref, kdec_ref, cdec_ref,
                  pf_ref, pb_ref, st_ref):
    @pl.when(pl.program_id(1) == 0)
    def _():
        st_ref[...] = jnp.zeros_like(st_ref)

    for i in range(STATE_CHUNKS):
        for d, (k_ref, v_ref, p_ref, c) in enumerate(((kf_ref, vf_ref, pf_ref, i),
                                                      (kb_ref, vb_ref, pb_ref, STATE_CHUNKS - 1 - i))):
            kd = (k_ref[0, c * BLOCK:(c + 1) * BLOCK, :].astype(F32) * kdec_ref[d]).astype(BF16)
            for h in range(RET_HEADS):
                sl = slice(h * LANES, (h + 1) * LANES)
                st = st_ref[d, h]
                p_ref[0, c, h] = st.astype(BF16)
                kv = _dot(v_ref[0, c, sl, :], kd[:, sl])
                row = d * RET_HEADS + h
                st_ref[d, h] = st * cdec_ref[row:row + 1, :] + kv


def _retention_states(zt, zk, kdec, cdec):
    B, n = zt.shape[0], zt.shape[1]
    ng = n // STATE_CHUNKS
    k_blk = (1, STATE_CHUNKS * BLOCK, RET_WIDTH)
    v_blk = (1, STATE_CHUNKS, RET_WIDTH, BLOCK)
    vrow = T_RV // RET_WIDTH
    st_blk = (1, STATE_CHUNKS, RET_HEADS, RET_HEAD_DIM, RET_HEAD_DIM)
    st_shape = jax.ShapeDtypeStruct((B, n, RET_HEADS, RET_HEAD_DIM, RET_HEAD_DIM), BF16)
    return pl.pallas_call(
        _state_kernel,
        grid=(B, ng),
        in_specs=[pl.BlockSpec(k_blk, lambda b, t: (b, t, 0)),
                  pl.BlockSpec(v_blk, lambda b, t: (b, t, vrow, 0)),
                  pl.BlockSpec(k_blk, lambda b, t: (b, ng - 1 - t, 0)),
                  pl.BlockSpec(v_blk, lambda b, t: (b, ng - 1 - t, vrow, 0)),
                  _const_spec((2, BLOCK, RET_WIDTH)),
                  _const_spec((2 * RET_HEADS, LANES))],
        out_specs=(pl.BlockSpec(st_blk, lambda b, t: (b, t, 0, 0, 0)),
                   pl.BlockSpec(st_blk, lambda b, t: (b, ng - 1 - t, 0, 0, 0))),
        out_shape=(st_shape, st_shape),
        scratch_shapes=[pltpu.VMEM((2, RET_HEADS, RET_HEAD_DIM, RET_HEAD_DIM), F32)],
        compiler_params=_params("parallel", "arbitrary"),
        name="retention_states",
    )(zk, zt, zk, zt, kdec, cdec)


def _mixer_kernel(sink_ref, x_ref, zt_ref, avp_ref, avn_ref, zk_ref, akp_ref, akn_ref,
                  pf_ref, pb_ref, dmat_ref, qdec_ref,
                  an_ref, gn_ref, wout_ref, npost_ref, o_ref, kext_ref, vext_ref, mixt_ref,
                  sc_ref):
    tile = pl.program_id(1)
    n_blocks = pl.num_programs(1) * MIX_CHUNKS

    lane = lax.broadcasted_iota(jnp.int32, (1, LANES), 1)
    for g in range(ATTN_KV_HEADS):
        km = jnp.where((lane // ATTN_HEAD_DIM) == g, 1.0, 0.0).astype(BF16)
        kext_ref[g, 0:BLOCK] = akp_ref[0] * km
        kext_ref[g, BLOCK:BLOCK + MIX_TILE] = zk_ref[0, :, K_AK:K_AK + ATTN_KV_WIDTH] * km
        kext_ref[g, BLOCK + MIX_TILE:] = akn_ref[0] * km
    vext_ref[0] = avp_ref[0, 0]
    for c in range(MIX_CHUNKS):
        vext_ref[1 + c] = zt_ref[0, c, T_AV:T_AV + ATTN_KV_WIDTH, :]
    vext_ref[MIX_CHUNKS + 1] = avn_ref[0, 0]

    kk = lax.broadcasted_iota(jnp.int32, (BLOCK, LANES), 0)
    qq = lax.broadcasted_iota(jnp.int32, (BLOCK, LANES), 1)
    neg = jnp.full((BLOCK, LANES), -jnp.inf, F32)
    zero = jnp.zeros((BLOCK, LANES), F32)

    def scores_into(slot, c):
        q = jnp.concatenate([zt_ref[0, c, T_AQ + j * LANES:T_AQ + (j + 1) * LANES, :]
                             for j in range(ATTN_GROUP)], axis=1)
        r0 = c * BLOCK
        for g in range(ATTN_KV_HEADS):
            sc_ref[slot, g] = _dot(kext_ref[g, pl.ds(r0, 3 * BLOCK), :], q)

    def chunk(c, has_next):
        r0 = c * BLOCK
        blk = tile * MIX_CHUNKS + c
        slot = c % 2
        if has_next:
            scores_into(1 - slot, c + 1)

        vw = jnp.concatenate([vext_ref[c], vext_ref[c + 1], vext_ref[c + 2]], axis=1)
        no_prev = jnp.where(blk > 0, 0, BLOCK)
        no_next = jnp.where(blk < n_blocks - 1, 0, BLOCK)
        bias_prev = jnp.where(kk >= qq + no_prev, zero, neg)
        bias_next = jnp.where(kk <= qq - no_next, zero, neg)
        bias_prev = jnp.concatenate([bias_prev] * ATTN_GROUP, axis=1)
        bias_next = jnp.concatenate([bias_next] * ATTN_GROUP, axis=1)
        heads = []
        for h in range(RET_HEADS):
            qt = zt_ref[0, c, T_RQ + h * LANES:T_RQ + (h + 1) * LANES, :]
            kh = zk_ref[0, pl.ds(r0, BLOCK), K_RK + h * LANES:K_RK + (h + 1) * LANES]
            heads.append((qt, _dot(kh, qt)))

        ones = jnp.ones((BF16_ROWS, 3 * BLOCK), BF16)
        outs = []
        for g in range(ATTN_KV_HEADS):
            s = sc_ref[slot, g]
            s = jnp.concatenate([s[0:BLOCK] + bias_prev, s[BLOCK:2 * BLOCK],
                                 s[2 * BLOCK:] + bias_next], axis=0)
            sk = jnp.concatenate(
                [jnp.full((1, LANES), sink_ref[g * ATTN_GROUP + j] * LOG2E, F32)
                 for j in range(ATTN_GROUP)], axis=1)
            m = jnp.maximum(jnp.max(s, axis=0, keepdims=True), sk)
            p = jnp.exp2(s - m).astype(BF16)
            vg = jnp.concatenate([vw[g * ATTN_HEAD_DIM:(g + 1) * ATTN_HEAD_DIM, :], ones], axis=0)
            ov = _dot(vg, p)
            den = ov[ATTN_HEAD_DIM:ATTN_HEAD_DIM + 1, :] + jnp.exp2(sk - m)
            outs.append(ov[0:ATTN_HEAD_DIM, :] * (1.0 / den))
        ys = []
        for h in range(RET_HEADS):
            qt, pt = heads[h]
            y = _dot(zt_ref[0, c, T_RV + h * LANES:T_RV + (h + 1) * LANES, :],
                     (pt * dmat_ref[h]).astype(BF16))
            y = y + _dot(pf_ref[0, c, h], qt) * qdec_ref[h:h + 1, :]
            y = y + _dot(pb_ref[0, c, h], qt) * qdec_ref[RET_HEADS + h:RET_HEADS + h + 1, :]
            ys.append(y)

        a = jnp.concatenate([outs[g][:, j * LANES:(j + 1) * LANES]
                             for j in range(ATTN_GROUP) for g in range(ATTN_KV_HEADS)], axis=0)
        ms = jnp.sum(a * a, axis=0, keepdims=True) * (1.0 / ATTN_WIDTH)
        mixt_ref[c, 0:ATTN_WIDTH, :] = (a * lax.rsqrt(ms + EPS) * an_ref[...]).astype(BF16)
        for h in range(RET_HEADS):
            sl = slice(h * LANES, (h + 1) * LANES)
            y = ys[h]
            mu = jnp.mean(y, axis=0, keepdims=True)
            yc = y - mu
            var = jnp.mean(yc * yc, axis=0, keepdims=True)
            yn = yc * lax.rsqrt(var + EPS) * gn_ref[sl, :]
            gate = zt_ref[0, c, T_RG + h * LANES:T_RG + (h + 1) * LANES, :].astype(F32)
            r = gate * jax.nn.sigmoid(gate) * yn
            mixt_ref[c, ATTN_WIDTH + h * LANES:ATTN_WIDTH + (h + 1) * LANES, :] = r.astype(BF16)

    scores_into(0, 0)
    for c in range(MIX_CHUNKS):
        chunk(c, c + 1 < MIX_CHUNKS)

    mixt = jnp.concatenate([mixt_ref[c] for c in range(MIX_CHUNKS)], axis=1)
    mix = lax.dot_general(mixt, wout_ref[...], TN_DIMS, preferred_element_type=F32)
    o_ref[0] = x_ref[0] + _rms(mix, npost_ref[...])


def _mixer(x, zt, zk, states_f, states_b, dmat, qdec, sink, attn_norm, ret_gn, w_out, norm_post):
    B, S, _ = x.shape
    n = S // BLOCK
    cpt = MIX_CHUNKS
    edge_t = (1, 1, ATTN_KV_WIDTH, BLOCK)
    edge_k = (1, BLOCK, ATTN_KV_WIDTH)
    v_row = T_AV // ATTN_KV_WIDTH
    k_col = K_AK // ATTN_KV_WIDTH
    st_blk = (1, cpt, RET_HEADS, RET_HEAD_DIM, RET_HEAD_DIM)

    def prev(i):
        return jnp.maximum(i * cpt - 1, 0)

    def nxt(i):
        return jnp.minimum((i + 1) * cpt, n - 1)

    return pl.pallas_call(
        _mixer_kernel,
        grid=(B, S // MIX_TILE),
        in_specs=[pl.BlockSpec(memory_space=pltpu.SMEM),
                  pl.BlockSpec((1, MIX_TILE, D_MODEL), lambda b, i: (b, i, 0)),
                  pl.BlockSpec((1, cpt, T_ROWS, BLOCK), lambda b, i: (b, i, 0, 0)),
                  pl.BlockSpec(edge_t, lambda b, i: (b, prev(i), v_row, 0)),
                  pl.BlockSpec(edge_t, lambda b, i: (b, nxt(i), v_row, 0)),
                  pl.BlockSpec((1, MIX_TILE, K_COLS), lambda b, i: (b, i, 0)),
                  pl.BlockSpec(edge_k, lambda b, i: (b, prev(i), k_col)),
                  pl.BlockSpec(edge_k, lambda b, i: (b, nxt(i), k_col)),
                  pl.BlockSpec(st_blk, lambda b, i: (b, i, 0, 0, 0)),
                  pl.BlockSpec(st_blk, lambda b, i: (b, i, 0, 0, 0)),
                  _const_spec((RET_HEADS, BLOCK, LANES)),
                  _const_spec((2 * RET_HEADS, LANES)),
                  _const_spec((ATTN_WIDTH, LANES)),
                  _const_spec((RET_WIDTH, LANES)),
                  _const_spec((D_MODEL, D_MODEL)),
                  _const_spec((1, D_MODEL))],
        out_specs=pl.BlockSpec((1, MIX_TILE, D_MODEL), lambda b, i: (b, i, 0)),
        out_shape=jax.ShapeDtypeStruct((B, S, D_MODEL), F32),
        scratch_shapes=[pltpu.VMEM((ATTN_KV_HEADS, MIX_TILE + 2 * BLOCK, ATTN_KV_WIDTH), BF16),
                        pltpu.VMEM((cpt + 2, ATTN_KV_WIDTH, BLOCK), BF16),
                        pltpu.VMEM((cpt, D_MODEL, BLOCK), BF16),
                        pltpu.VMEM((2, ATTN_KV_HEADS, 3 * BLOCK, ATTN_GROUP * LANES), F32)],
        compiler_params=_params("parallel", "parallel"),
        name="token_mixer",
    )(sink, x, zt, zt, zt, zk, zk, zk, states_f, states_b, dmat, qdec,
      attn_norm, ret_gn, w_out, norm_post)


def _memkv_kernel(m_ref, g_ref, wkt_ref, wv_ref, kt_ref, v_ref):
    mn = _rms(m_ref[0], g_ref[...]).astype(BF16)
    kt_ref[0] = lax.dot_general(wkt_ref[...], mn, NT_DIMS,
                                preferred_element_type=F32).astype(BF16)
    v_ref[0] = _dot(mn, wv_ref[...]).astype(BF16)


def _memory_kv(mem, gain, w_kt, w_v):
    B, M, _ = mem.shape
    return pl.pallas_call(
        _memkv_kernel,
        grid=(B,),
        in_specs=[pl.BlockSpec((1, M, D_MODEL), lambda b: (b, 0, 0)),
                  _const_spec((1, D_MODEL)),
                  _const_spec((D_MODEL, D_MODEL)),
                  _const_spec((D_MODEL, D_MODEL))],
        out_specs=(pl.BlockSpec((1, D_MODEL, M), lambda b: (b, 0, 0)),
                   pl.BlockSpec((1, M, D_MODEL), lambda b: (b, 0, 0))),
        out_shape=(jax.ShapeDtypeStruct((B, D_MODEL, M), BF16),
                   jax.ShapeDtypeStruct((B, M, D_MODEL), BF16)),
        compiler_params=_params("parallel"),
        name="memory_kv",
    )(mem, gain, w_kt, w_v)


FF_CHUNK = 2 * MXU_TILE
FF_CHUNKS = [(c0, min(FF_CHUNK, D_FF - c0)) for c0 in range(0, D_FF, FF_CHUNK)]
assert all(w % MXU_TILE == 0 for _, w in FF_CHUNKS)
assert XA_HEADS <= 4 and N_MEM <= FF_CHUNK


def _xattn_ffn_kernel(x_ref, gxa_ref, wq_ref, kmt_ref, vm_ref, wo_ref, gxp_ref,
                      gfp_ref, wgu_ref, wd_ref, gfo_ref, o_ref,
                      cat_ref, q_ref, h_ref, res_ref, x2_ref, gu_ref, act_ref):
    pieces = [slice(r * ROW_PIECE, (r + 1) * ROW_PIECE) for r in range(TOK_TILE // ROW_PIECE)]

    for rows in pieces:
        h_ref[rows, :] = _rms(x_ref[0, rows, :], gxa_ref[...]).astype(BF16)
    for rows in pieces:
        q_ref[rows, :] = _dot(h_ref[rows, :], wq_ref[...]).astype(BF16)
    for hd in range(XA_HEADS):
        sl = slice(hd * XA_HEAD_DIM, (hd + 1) * XA_HEAD_DIM)
        gu_ref[hd // 2, hd % 2, :, 0:N_MEM] = _dot(q_ref[:, sl], kmt_ref[0, sl, :])
    for hd in range(XA_HEADS):
        sl = slice(hd * XA_HEAD_DIM, (hd + 1) * XA_HEAD_DIM)
        s = gu_ref[hd // 2, hd % 2, :, 0:N_MEM]
        p = jnp.exp(s - jnp.max(s, axis=-1, keepdims=True))
        den = jnp.sum(p, axis=-1, keepdims=True)
        cat_ref[:, sl] = (_dot(p.astype(BF16), vm_ref[0, :, sl]) * (1.0 / den)).astype(BF16)
    for rows in pieces:
        res_ref[rows, :] = _dot(cat_ref[rows, :], wo_ref[...])

    def gate_up_into(slot, j, rows=slice(None)):
        c0, w = FF_CHUNKS[j]
        gu_ref[slot, 0, rows, 0:w] = _dot(h_ref[rows, :], wgu_ref[:, c0:c0 + w])
        gu_ref[slot, 1, rows, 0:w] = _dot(h_ref[rows, :], wgu_ref[:, D_FF + c0:D_FF + c0 + w])

    for rows in pieces:
        x2 = x_ref[0, rows, :] + _rms(res_ref[rows, :], gxp_ref[...])
        x2_ref[rows, :] = x2
        h_ref[rows, :] = _rms(x2, gfp_ref[...]).astype(BF16)
    for rows in pieces:
        gate_up_into(0, 0, rows)

    for j, (c0, w) in enumerate(FF_CHUNKS):
        if j + 1 < len(FF_CHUNKS):
            gate_up_into((j + 1) % 2, j + 1)
        for rows in pieces:
            gate = gu_ref[j % 2, 0, rows, 0:w]
            up = gu_ref[j % 2, 1, rows, 0:w]
            act_ref[rows, c0:c0 + w] = (gate * jax.nn.sigmoid(gate) * up).astype(BF16)
    for rows in pieces:
        res_ref[rows, :] = _dot(act_ref[rows, :], wd_ref[...])
    for rows in pieces:
        o_ref[0, rows, :] = x2_ref[rows, :] + _rms(res_ref[rows, :], gfo_ref[...])


def _cross_attention_ffn(x, kt_mem, v_mem, gain_xa_pre, w_q, w_o, gain_xa_post,
                         gain_ffn_pre, w_gu, w_down, gain_ffn_post):
    B, S, _ = x.shape
    M = v_mem.shape[1]
    tok = pl.BlockSpec((1, TOK_TILE, D_MODEL), lambda b, i: (b, i, 0))
    gain = _const_spec((1, D_MODEL))
    return pl.pallas_call(
        _xattn_ffn_kernel,
        grid=(B, S // TOK_TILE),
        in_specs=[tok, gain,
                  _const_spec((D_MODEL, D_MODEL)),
                  pl.BlockSpec((1, D_MODEL, M), lambda b, i: (b, 0, 0)),
                  pl.BlockSpec((1, M, D_MODEL), lambda b, i: (b, 0, 0)),
                  _const_spec((D_MODEL, D_MODEL)),
                  gain, gain,
                  _const_spec((D_MODEL, 2 * D_FF)),
                  _const_spec((D_FF, D_MODEL)),
                  gain],
        out_specs=tok,
        out_shape=jax.ShapeDtypeStruct((B, S, D_MODEL), F32),
        scratch_shapes=[pltpu.VMEM((TOK_TILE, D_MODEL), BF16),
                        pltpu.VMEM((TOK_TILE, D_MODEL), BF16),
                        pltpu.VMEM((TOK_TILE, D_MODEL), BF16),
                        pltpu.VMEM((TOK_TILE, D_MODEL), F32),
                        pltpu.VMEM((TOK_TILE, D_MODEL), F32),
                        pltpu.VMEM((2, 2, TOK_TILE, FF_CHUNK), F32),
                        pltpu.VMEM((TOK_TILE, D_FF), BF16)],
        compiler_params=_params("parallel", "parallel"),
        name="cross_attention_ffn",
    )(x, gain_xa_pre, w_q, kt_mem, v_mem, w_o, gain_xa_post,
      gain_ffn_pre, w_gu, w_down, gain_ffn_post)


def _pair_heads(a, axis):
    shape = a.shape
    a = a.reshape(shape[:axis] + (ATTN_KV_HEADS, ATTN_GROUP, ATTN_HEAD_DIM) + shape[axis + 1:])
    return jnp.swapaxes(a, axis, axis + 1).reshape(shape)


def kernel(x, mem, norm_mix_pre, norm_mix_post, w_in, attn_sink, attn_out_norm,
           ret_decay_fwd, ret_decay_bwd, ret_gn, w_out, norm_xa_pre, norm_xa_post,
           norm_mem, xa_wq, xa_wkv, xa_wo, norm_ffn_pre, norm_ffn_post,
           ffn_w_gu, ffn_w_down):
    B, S, D = x.shape
    assert D == D_MODEL and S % max(TOK_TILE, MIX_TILE, STATE_CHUNKS * BLOCK) == 0
    assert mem.shape[1] == N_MEM
    depth = w_in.shape[0]
    o_k = ATTN_WIDTH
    o_v = o_k + ATTN_KV_WIDTH
    o_r = o_v + ATTN_KV_WIDTH
    rw = RET_WIDTH

    rope = _rope_tables(S // TOK_TILE)

    def row(v):
        return v.reshape(1, -1).astype(F32)

    def col(v):
        return jnp.broadcast_to(v.astype(F32)[:, None], (v.shape[0], LANES))

    for l in range(depth):
        w = w_in[l]
        w_t = jnp.concatenate([_pair_heads(w[:, :o_k], 1), w[:, o_r:o_r + rw], w[:, o_r + 2 * rw:],
                               w[:, o_v:o_r], w[:, o_k:o_v]], axis=1).T.astype(BF16)
        w_k = w[:, o_r + rw:o_r + 2 * rw].astype(BF16)
        w_o = jnp.concatenate([_pair_heads(w_out[l][:ATTN_WIDTH], 0), w_out[l][ATTN_WIDTH:]], axis=0)

        dmat, kdec, qdec, cdec = _decay_tables(ret_decay_fwd[l], ret_decay_bwd[l])
        zt, zk = _in_projection(x, row(norm_mix_pre[l]), w_t, w_k, rope)
        states_f, states_b = _retention_states(zt, zk, kdec, cdec)
        x = _mixer(x, zt, zk, states_f, states_b, dmat, qdec, attn_sink[l].astype(F32),
                   col(_pair_heads(attn_out_norm[l], 0)), col(ret_gn[l]),
                   w_o.astype(BF16), row(norm_mix_post[l]))
        kt_mem, v_mem = _memory_kv(mem, row(norm_mem[l]), xa_wkv[l][:, :D_MODEL].T.astype(BF16),
                                   xa_wkv[l][:, D_MODEL:].astype(BF16))
        x = _cross_attention_ffn(x, kt_mem, v_mem, row(norm_xa_pre[l]),
                                 (xa_wq[l] * (XA_HEAD_DIM ** -0.5)).astype(BF16),
                                 xa_wo[l].astype(BF16), row(norm_xa_post[l]),
                                 row(norm_ffn_pre[l]), ffn_w_gu[l].astype(BF16),
                                 ffn_w_down[l].astype(BF16), row(norm_ffn_post[l]))
    return x
```
